```python
import jax, jax.numpy as jnp
from jax import lax
import numpy as np

D_MODEL = 1024
BATCH = 8
SEQ = 2048
DEPTH = 1
DEC_BATCH = 32
DEC_SEQ = 1
PAST_LEN = 8192
PAGE_SIZE = 128

MIX_WIDTH = D_MODEL
ATT_WIDTH = MIX_WIDTH // 2
CONV_CH = MIX_WIDTH - ATT_WIDTH
HEAD_DIM = 64
N_HEADS = ATT_WIDTH // HEAD_DIM
ROPE_DIM = HEAD_DIM // 4
ROPE_THETA = 500000.0
MOBA_BLOCK = 256
MOBA_TOPK = 3
Q_CHUNK = 32
ATT_SCALE = HEAD_DIM ** -0.5
CONV_K = 3
PROJ_WIDTH = 3 * ATT_WIDTH + 3 * CONV_CH
SPLIT_POINTS = (ATT_WIDTH, 2 * ATT_WIDTH, 3 * ATT_WIDTH,
                3 * ATT_WIDTH + CONV_CH, 3 * ATT_WIDTH + 2 * CONV_CH)
N_EXPERT_GROUPS = 4
EXPERTS_PER_GROUP = 8
N_EXPERTS = N_EXPERT_GROUPS * EXPERTS_PER_GROUP
EXPERT_TOPK = 2
EXPERT_FF = D_MODEL // 4
RMS_EPS = 1e-6

kernel_name = "hymba_moba_shortconv_hiermoe_step"


def rmsnorm(x, g):
    xf = x.astype(jnp.float32)
    y = xf * lax.rsqrt(jnp.mean(xf * xf, axis=-1, keepdims=True) + RMS_EPS)
    return (y * g.astype(jnp.float32)).astype(x.dtype)


def apply_partial_rope(x, pos):
    half = ROPE_DIM // 2
    inv = ROPE_THETA ** (-jnp.arange(0, ROPE_DIM, 2, dtype=jnp.float32) / ROPE_DIM)
    ang = pos.astype(jnp.float32)[:, None] * inv[None, :]
    cos = jnp.cos(ang)[None, :, None, :]
    sin = jnp.sin(ang)[None, :, None, :]
    xf = x.astype(jnp.float32)
    x1 = xf[..., :half]
    x2 = xf[..., half:ROPE_DIM]
    out = jnp.concatenate([x1 * cos - x2 * sin, x2 * cos + x1 * sin, xf[..., ROPE_DIM:]], axis=-1)
    return out.astype(x.dtype)


def moba_attend(q, k, v, qpos):
    B, H, L, dh = k.shape
    Q = q.shape[2]
    nb = -(-L // MOBA_BLOCK)
    pad = nb * MOBA_BLOCK - L
    k = jnp.pad(k, ((0, 0), (0, 0), (0, pad), (0, 0)))
    v = jnp.pad(v, ((0, 0), (0, 0), (0, pad), (0, 0)))
    kb = k.reshape(B, H, nb, MOBA_BLOCK, dh)
    vb = v.reshape(B, H, nb, MOBA_BLOCK, dh)
    means = jnp.mean(kb.astype(jnp.float32), axis=3)
    n_sel = min(MOBA_TOPK, nb)
    bi = jnp.arange(B)[:, None, None, None]
    hi = jnp.arange(H)[None, :, None, None]
    blk_ids = jnp.arange(nb)
    offs = jnp.arange(MOBA_BLOCK)

    def attend_chunk(q_c, pos_c):
        C = q_c.shape[2]
        own = pos_c // MOBA_BLOCK
        gate = jnp.einsum('bhcd,bhnd->bhcn', q_c.astype(jnp.float32), means)
        fully_past = blk_ids[None, :] < own[:, None]
        gate = jnp.where(fully_past[None, None], gate, -jnp.inf)
        _, sel = lax.top_k(gate, n_sel)
        own_b = jnp.broadcast_to(own[None, None, :, None], (B, H, C, 1)).astype(sel.dtype)
        sel_ok = sel < own_b
        idx = jnp.concatenate([sel, own_b], axis=-1)
        slot_ok = jnp.concatenate([sel_ok, jnp.ones_like(sel_ok[..., :1])], axis=-1)
        kg = kb[bi, hi, idx]
        vg = vb[bi, hi, idx]
        kpos = idx[..., None] * MOBA_BLOCK + offs
        mask = slot_ok[..., None] & (kpos <= pos_c[None, None, :, None, None])
        s = jnp.einsum('bhcd,bhcnkd->bhcnk', q_c, kg).astype(jnp.float32) * ATT_SCALE
        s = jnp.where(mask, s, -jnp.inf)
        n_slot = idx.shape[-1]
        p = jax.nn.softmax(s.reshape(B, H, C, n_slot * MOBA_BLOCK), axis=-1)
        p = p.reshape(B, H, C, n_slot, MOBA_BLOCK).astype(vg.dtype)
        return jnp.einsum('bhcnk,bhcnkd->bhcd', p, vg)

    if Q > Q_CHUNK and Q % Q_CHUNK == 0:
        nc = Q // Q_CHUNK
        qc = jnp.moveaxis(q.reshape(B, H, nc, Q_CHUNK, dh), 2, 0)
        pc = qpos.reshape(nc, Q_CHUNK)
        out = lax.map(lambda a: attend_chunk(a[0], a[1]), (qc, pc))
        return jnp.moveaxis(out, 0, 2).reshape(B, H, Q, dh)
    return attend_chunk(q, qpos)


def short_conv(u, prev, w):
    T = u.shape[1]
    up = jnp.concatenate([prev, u], axis=1)
    out = w[0] * up[:, 0:T]
    for i in range(1, CONV_K):
        out = out + w[i] * up[:, i:i + T]
    return out, up[:, -(CONV_K - 1):]


def hier_moe(x, w_rg, b_rg, w_re, b_re, w_gate, w_up, w_down):
    T = x.shape[0]
    lg = (x @ w_rg).astype(jnp.float32) + b_rg.astype(jnp.float32)
    pg_all = jax.nn.softmax(lg, axis=-1)
    g = jnp.argmax(lg, axis=-1)
    pg = jnp.take_along_axis(pg_all, g[:, None], axis=1)
    le = ((x @ w_re).astype(jnp.float32) + b_re.astype(jnp.float32)).reshape(
        T, N_EXPERT_GROUPS, EXPERTS_PER_GROUP)
    le_g = jnp.take_along_axis(le, g[:, None, None], axis=1)[:, 0]
    top_l, top_i = lax.top_k(le_g, EXPERT_TOPK)
    w_top = jax.nn.softmax(top_l, axis=-1) * pg
    eid = g[:, None] * EXPERTS_PER_GROUP + top_i
    gates = jnp.sum(jax.nn.one_hot(eid, N_EXPERTS, dtype=jnp.float32) * w_top[..., None], axis=1)
    hg = jnp.einsum('td,edf->tef', x, w_gate)
    hu = jnp.einsum('td,edf->tef', x, w_up)
    h = jax.nn.silu(hg) * hu * gates[:, :, None].astype(x.dtype)
    return jnp.einsum('tef,efd->td', h, w_down)


def decoder_layer(x, pos, k_past, v_past, conv_prev, g_mix, w_in, conv_w, w_out, g_ffn,
                  w_rg, b_rg, w_re, b_re, w_gate, w_up, w_down):
    B, T, D = x.shape
    xn = rmsnorm(x, g_mix)
    proj = xn @ w_in
    q, k, v, b_gate, c_gate, h_in = jnp.split(proj, SPLIT_POINTS, axis=-1)
    q = apply_partial_rope(q.reshape(B, T, N_HEADS, HEAD_DIM), pos)
    k = apply_partial_rope(k.reshape(B, T, N_HEADS, HEAD_DIM), pos)
    v = v.reshape(B, T, N_HEADS, HEAD_DIM)
    if k_past is None:
        k_all, v_all = k, v
    else:
        k_all = jnp.concatenate([k_past, k], axis=1)
        v_all = jnp.concatenate([v_past, v], axis=1)
    att = moba_attend(q.transpose(0, 2, 1, 3), k_all.transpose(0, 2, 1, 3),
                      v_all.transpose(0, 2, 1, 3), pos)
    att = att.transpose(0, 2, 1, 3).reshape(B, T, ATT_WIDTH)
    conv_out, conv_new = short_conv(c_gate * h_in, conv_prev, conv_w)
    conv_y = b_gate * conv_out
    hres = x + jnp.concatenate([att, conv_y], axis=-1) @ w_out
    hn = rmsnorm(hres, g_ffn)
    ffn = hier_moe(hn.reshape(B * T, D), w_rg, b_rg, w_re, b_re, w_gate, w_up, w_down)
    return hres + ffn.reshape(B, T, D), k, v, conv_new


def setup_inputs(seed: int = 0) -> dict:
    key = jax.random.key(seed)
    ks = jax.random.split(key, 20)
    f32 = jnp.float32
    n_pages = PAST_LEN // PAGE_SIZE
    n_used = DEC_BATCH * n_pages
    n_phys = n_used + max(1, n_used // 4)
    page_table = jax.random.permutation(ks[0], n_phys)[:n_used].reshape(DEC_BATCH, n_pages).astype(jnp.int32)
    nrm = lambda k, shape, s: jax.random.normal(k, shape, f32) * s
    return {
        "x_prompt": nrm(ks[1], (BATCH, SEQ, D_MODEL), 1.0),
        "x_sample": nrm(ks[2], (DEC_BATCH, DEC_SEQ, D_MODEL), 1.0),
        "cache_k": nrm(ks[3], (DEPTH, n_phys, PAGE_SIZE, N_HEADS, HEAD_DIM), 1.0),
        "cache_v": nrm(ks[4], (DEPTH, n_phys, PAGE_SIZE, N_HEADS, HEAD_DIM), 1.0),
        "state_conv": nrm(ks[5], (DEPTH, DEC_BATCH, CONV_K - 1, CONV_CH), 1.0),
        "page_table": page_table,
        "g_mix": 1.0 + nrm(ks[6], (DEPTH, D_MODEL), 0.02),
        "w_in": nrm(ks[7], (DEPTH, D_MODEL, PROJ_WIDTH), D_MODEL ** -0.5),
        "conv_w": nrm(ks[8], (DEPTH, CONV_K, CONV_CH), CONV_K ** -0.5),
        "w_out": nrm(ks[9], (DEPTH, MIX_WIDTH, D_MODEL), MIX_WIDTH ** -0.5),
        "g_ffn": 1.0 + nrm(ks[10], (DEPTH, D_MODEL), 0.02),
        "w_router_group": nrm(ks[11], (DEPTH, D_MODEL, N_EXPERT_GROUPS), D_MODEL ** -0.5),
        "b_router_group": nrm(ks[12], (DEPTH, N_EXPERT_GROUPS), 0.01),
        "w_router_expert": nrm(ks[13], (DEPTH, D_MODEL, N_EXPERTS), D_MODEL ** -0.5),
        "b_router_expert": nrm(ks[14], (DEPTH, N_EXPERTS), 0.01),
        "w_gate": nrm(ks[15], (DEPTH, N_EXPERTS, D_MODEL, EXPERT_FF), D_MODEL ** -0.5),
        "w_up": nrm(ks[16], (DEPTH, N_EXPERTS, D_MODEL, EXPERT_FF), D_MODEL ** -0.5),
        "w_down": nrm(ks[17], (DEPTH, N_EXPERTS, EXPERT_FF, D_MODEL), EXPERT_FF ** -0.5),
        "g_final": 1.0 + nrm(ks[18], (D_MODEL,), 0.02),
    }


def reference(x_prompt, x_sample, cache_k, cache_v, state_conv, page_table, g_mix, w_in, conv_w,
              w_out, g_ffn, w_router_group, b_router_group, w_router_expert, b_router_expert,
              w_gate, w_up, w_down, g_final):
    n_dec, dec_len = x_sample.shape[0], x_sample.shape[1]
    past_len = page_table.shape[1] * cache_k.shape[2]
    pos_p = jnp.arange(x_prompt.shape[1], dtype=jnp.int32)
    pos_s = past_len + jnp.arange(dec_len, dtype=jnp.int32)
    y_p, y_s = x_prompt, x_sample
    kp_l, vp_l, cp_l, ksl, vsl, csl = [], [], [], [], [], []
    for l in range(DEPTH):
        lw = (g_mix[l], w_in[l], conv_w[l], w_out[l], g_ffn[l], w_router_group[l],
              b_router_group[l], w_router_expert[l], b_router_expert[l], w_gate[l], w_up[l], w_down[l])
        conv0 = jnp.zeros((y_p.shape[0], CONV_K - 1, CONV_CH), y_p.dtype)
        y_p, kp, vp, cp = decoder_layer(y_p, pos_p, None, None, conv0, *lw)
        k_past = cache_k[l][page_table].reshape(n_dec, past_len, N_HEADS, HEAD_DIM)
        v_past = cache_v[l][page_table].reshape(n_dec, past_len, N_HEADS, HEAD_DIM)
        y_s, ks_, vs_, cs_ = decoder_layer(y_s, pos_s, k_past, v_past, state_conv[l], *lw)
        kp_l.append(kp); vp_l.append(vp); cp_l.append(cp)
        ksl.append(ks_); vsl.append(vs_); csl.append(cs_)
    y_prompt = rmsnorm(y_p, g_final)
    y_sample = rmsnorm(y_s, g_final)
    return (y_prompt, y_sample, jnp.stack(kp_l), jnp.stack(vp_l), jnp.stack(cp_l),
            jnp.stack(ksl), jnp.stack(vsl), jnp.stack(csl))
```

```python
import functools

import jax
import jax.numpy as jnp
from jax import lax
from jax.experimental import pallas as pl
from jax.experimental.pallas import tpu as pltpu

F32 = jnp.float32
BF16 = jnp.bfloat16
HIGHEST = lax.Precision.HIGHEST

D_MODEL = 1024
HEAD_DIM = 64
N_HEADS = 8
ATT_WIDTH = N_HEADS * HEAD_DIM
CONV_CH = D_MODEL - ATT_WIDTH
ROPE_DIM = HEAD_DIM // 4
ROPE_HALF = ROPE_DIM // 2
ROPE_THETA = 500000.0
MOBA_BLOCK = 256
MOBA_TOPK = 3
ATT_SCALE = HEAD_DIM ** -0.5
CONV_K = 3
N_EXPERT_GROUPS = 4
EXPERTS_PER_GROUP = 8
N_EXPERTS = N_EXPERT_GROUPS * EXPERTS_PER_GROUP
EXPERT_FF = D_MODEL // 4
RMS_EPS = 1e-6

LANES = 128
SUBLANES = 8
HEADS_PER_VREG = LANES // HEAD_DIM
VMEM_LIMIT = 56 * 1024 * 1024

NEG_INF = float("-inf")
NT_DIMS = (((1,), (1,)), ((), ()))


def _rms(x, g):
    ms = jnp.mean(x * x, axis=-1, keepdims=True)
    return x * lax.rsqrt(ms + RMS_EPS) * g


def _mm(a, w, precise):
    if precise:
        return jnp.dot(a, w, precision=HIGHEST, preferred_element_type=F32)
    return jnp.dot(a.astype(BF16), w, preferred_element_type=F32)


def _rope_rows(a, cos, sin_lo, sin_hi):
    n = a.shape[-1]
    return (a * cos + pltpu.roll(a, n - ROPE_HALF, 1) * sin_lo
            + pltpu.roll(a, ROPE_HALF, 1) * sin_hi)


def _tile_lanes(t, reps):
    return jnp.concatenate([t] * reps, axis=-1)


def _rope_angles(pos):
    inv = ROPE_THETA ** (-jnp.arange(0, ROPE_DIM, 2, dtype=F32) / ROPE_DIM)
    ang = pos.astype(F32)[:, None] * inv[None, :]
    return jnp.cos(ang), jnp.sin(ang)


def _rope_row_tables(cos, sin):
    rows = cos.shape[0]
    ones = jnp.ones((rows, HEAD_DIM - ROPE_DIM), F32)
    zeros = jnp.zeros((rows, HEAD_DIM - ROPE_HALF), F32)
    c = jnp.concatenate([cos, cos, ones], axis=1)
    s_lo = jnp.concatenate([-sin, zeros], axis=1)
    s_hi = jnp.concatenate([jnp.zeros((rows, ROPE_HALF), F32), sin,
                            jnp.zeros((rows, HEAD_DIM - ROPE_DIM), F32)], axis=1)
    rep = lambda t: jnp.concatenate([t] * HEADS_PER_VREG, axis=1)
    return rep(c), rep(s_lo), rep(s_hi)


def _prompt_inproj_kernel(x_ref, g_ref, wq_ref, wkt_ref, wvt_ref, wc_ref, cos_ref, slo_ref, shi_ref,
                          cost_ref, sint_ref, cw_ref, prev_ref,
                          q_ref, kt_ref, vt_ref, cy_ref, cn_ref, ubuf, halo, *, tt):
    t = pl.program_id(1)
    xn = _rms(x_ref[...], g_ref[...]).astype(BF16)
    reps = ATT_WIDTH // LANES
    q = jnp.dot(xn, wq_ref[...], preferred_element_type=F32)
    q_ref[...] = _rope_rows(q, _tile_lanes(cos_ref[...], reps), _tile_lanes(slo_ref[...], reps),
                            _tile_lanes(shi_ref[...], reps))

    vt_ref[...] = lax.dot_general(wvt_ref[...], xn, NT_DIMS, preferred_element_type=F32)
    kt_ref[...] = lax.dot_general(wkt_ref[...], xn, NT_DIMS, preferred_element_type=F32)
    cos_t, sin_t = cost_ref[...], sint_ref[...]
    for h in range(N_HEADS):
        r = h * HEAD_DIM
        x1 = kt_ref[r:r + ROPE_HALF, :]
        x2 = kt_ref[r + ROPE_HALF:r + ROPE_DIM, :]
        kt_ref[r:r + ROPE_HALF, :] = x1 * cos_t - x2 * sin_t
        kt_ref[r + ROPE_HALF:r + ROPE_DIM, :] = x2 * cos_t + x1 * sin_t

    def proj(i):
        return jnp.dot(xn, wc_ref[:, i * CONV_CH:(i + 1) * CONV_CH], preferred_element_type=F32)

    b_gate = proj(0)
    u = proj(1) * proj(2)

    @pl.when(t == 0)
    def _():
        ubuf[SUBLANES - (CONV_K - 1):SUBLANES, :] = prev_ref[...]

    @pl.when(t > 0)
    def _():
        ubuf[0:SUBLANES, :] = halo[...]

    ubuf[SUBLANES:SUBLANES + tt, :] = u
    cw = cw_ref[...]
    conv = (cw[0:1, :] * ubuf[SUBLANES - 2:SUBLANES - 2 + tt, :]
            + cw[1:2, :] * ubuf[SUBLANES - 1:SUBLANES - 1 + tt, :]
            + cw[2:3, :] * u)
    cy_ref[...] = (b_gate * conv).astype(cy_ref.dtype)
    halo[...] = ubuf[tt:tt + SUBLANES, :]
    cn_ref[...] = ubuf[tt + SUBLANES - (CONV_K - 1):tt + SUBLANES, :]


def _prompt_inproj(x, g_mix, w_in, row_tables, col_tables, conv_w, conv_prev, tt=512):
    bsz, seq, d = x.shape
    nt = seq // tt
    w_bf = w_in.astype(BF16)
    wq = w_bf[:, :ATT_WIDTH]
    wkt = w_bf[:, ATT_WIDTH:2 * ATT_WIDTH].T
    wvt = w_bf[:, 2 * ATT_WIDTH:3 * ATT_WIDTH].T
    wc = w_bf[:, 3 * ATT_WIDTH:]
    row_spec = lambda width: pl.BlockSpec((None, tt, width), lambda b, t: (b, t, 0))
    col_spec = pl.BlockSpec((None, ATT_WIDTH, tt), lambda b, t: (b, 0, t))
    tab_spec = pl.BlockSpec((tt, LANES), lambda b, t: (t, 0))
    tabt_spec = pl.BlockSpec((ROPE_HALF, tt), lambda b, t: (0, t))
    full = lambda shape: pl.BlockSpec(shape, lambda b, t: (0,) * len(shape))
    state_spec = pl.BlockSpec((None, CONV_K - 1, CONV_CH), lambda b, t: (b, 0, 0))
    return pl.pallas_call(
        functools.partial(_prompt_inproj_kernel, tt=tt),
        grid=(bsz, nt),
        in_specs=[row_spec(d), full((1, d)), full(wq.shape), full(wkt.shape), full(wvt.shape),
                  full(wc.shape), tab_spec, tab_spec, tab_spec, tabt_spec, tabt_spec,
                  full((CONV_K, CONV_CH)), state_spec],
        out_specs=[row_spec(ATT_WIDTH), col_spec, col_spec, row_spec(CONV_CH), state_spec],
        out_shape=[jax.ShapeDtypeStruct((bsz, seq, ATT_WIDTH), F32),
                   jax.ShapeDtypeStruct((bsz, ATT_WIDTH, seq), F32),
                   jax.ShapeDtypeStruct((bsz, ATT_WIDTH, seq), F32),
                   jax.ShapeDtypeStruct((bsz, seq, CONV_CH), BF16),
                   jax.ShapeDtypeStruct((bsz, CONV_K - 1, CONV_CH), F32)],
        scratch_shapes=[pltpu.VMEM((tt + SUBLANES, CONV_CH), F32),
                        pltpu.VMEM((SUBLANES, CONV_CH), F32)],
        compiler_params=pltpu.CompilerParams(
            dimension_semantics=("arbitrary", "arbitrary"), vmem_limit_bytes=VMEM_LIMIT),
        name="prompt_inproj",
    )(x, g_mix, wq, wkt, wvt, wc, *row_tables, *col_tables, conv_w, conv_prev)


def _prompt_attn_kernel(q_ref, kt_ref, vt_ref, o_ref, means_ref, kb_ref, vb_ref, *, nb):
    qi = pl.program_id(2)
    blk = MOBA_BLOCK
    seq = nb * blk

    @pl.when(qi == 0)
    def _():
        kt = kt_ref[...]
        kid = lax.broadcasted_iota(jnp.int32, (nb, seq), 1)
        jid = lax.broadcasted_iota(jnp.int32, (nb, seq), 0)
        avg = jnp.where((kid >= jid * blk) & (kid < (jid + 1) * blk), 1.0 / blk, 0.0)
        means_ref[...] = lax.dot_general(avg, kt, NT_DIMS, precision=HIGHEST,
                                         preferred_element_type=F32)
        for j in range(nb):
            kb_ref[j] = kt[:, j * blk:(j + 1) * blk].astype(BF16)
        vb_ref[...] = vt_ref[...].T.astype(BF16)

    q = q_ref[...]
    lane = lax.broadcasted_iota(jnp.int32, (1, LANES), 1)
    bid = lax.broadcasted_iota(jnp.int32, (1, nb), 1)
    row = lax.broadcasted_iota(jnp.int32, (blk, blk), 0)
    col = lax.broadcasted_iota(jnp.int32, (blk, blk), 1)
    out = jnp.zeros((blk, LANES), F32)
    for h in range(HEADS_PER_VREG):
        hmask = (lane >= h * HEAD_DIM) & (lane < (h + 1) * HEAD_DIM)
        qh = jnp.where(hmask, q, 0.0)
        gates = lax.dot_general(qh, means_ref[...], NT_DIMS, precision=HIGHEST,
                                preferred_element_type=F32)
        gates = jnp.where(bid < qi, gates, NEG_INF)
        sel = jnp.zeros((blk, nb), F32)
        for j in range(nb):
            gj = gates[:, j:j + 1]
            beats = (gates > gj) | ((gates == gj) & (bid < j))
            cnt = jnp.sum(beats.astype(F32), axis=1, keepdims=True)
            sel = jnp.where((bid == j) & (cnt < MOBA_TOPK), 1.0, sel)
        sel = jnp.where(bid < qi, sel, 0.0)

        qs = (qh * ATT_SCALE).astype(BF16)
        off = pl.multiple_of(qi * blk, blk)
        s = jnp.dot(qs, kb_ref[qi], preferred_element_type=F32)
        s = jnp.where(row >= col, s, NEG_INF)
        m = jnp.max(s, axis=1, keepdims=True)
        p = jnp.exp(s - m)
        l = jnp.sum(p, axis=1, keepdims=True)
        acc = jnp.dot(p.astype(BF16), vb_ref[pl.ds(off, blk), :], preferred_element_type=F32)

        def body(j, carry):
            m, l, acc = carry
            sel_j = jnp.sum(jnp.where(bid == j, sel, 0.0), axis=1, keepdims=True) > 0.5
            o = pl.multiple_of(j * blk, blk)
            s = jnp.dot(qs, kb_ref[j], preferred_element_type=F32)
            s = jnp.where(sel_j, s, NEG_INF)
            m_new = jnp.maximum(m, jnp.max(s, axis=1, keepdims=True))
            alpha = jnp.exp(m - m_new)
            p = jnp.exp(s - m_new)
            l = alpha * l + jnp.sum(p, axis=1, keepdims=True)
            acc = alpha * acc + jnp.dot(p.astype(BF16), vb_ref[pl.ds(o, blk), :],
                                        preferred_element_type=F32)
            return m_new, l, acc

        m, l, acc = lax.fori_loop(0, qi, body, (m, l, acc))
        out = jnp.where(hmask, acc / l, out)
    o_ref[...] = out.astype(o_ref.dtype)


def _prompt_attn(q, kt, vt):
    bsz, seq, _ = q.shape
    nb = seq // MOBA_BLOCK
    n_pairs = ATT_WIDTH // LANES
    q_spec = pl.BlockSpec((None, MOBA_BLOCK, LANES), lambda b, hp, qi: (b, qi, hp))
    kv_spec = pl.BlockSpec((None, LANES, seq), lambda b, hp, qi: (b, hp, 0))
    return pl.pallas_call(
        functools.partial(_prompt_attn_kernel, nb=nb),
        grid=(bsz, n_pairs, nb),
        in_specs=[q_spec, kv_spec, kv_spec],
        out_specs=q_spec,
        out_shape=jax.ShapeDtypeStruct((bsz, seq, ATT_WIDTH), BF16),
        scratch_shapes=[pltpu.VMEM((nb, LANES), F32),
                        pltpu.VMEM((nb, LANES, MOBA_BLOCK), BF16),
                        pltpu.VMEM((seq, LANES), BF16)],
        compiler_params=pltpu.CompilerParams(
            dimension_semantics=("arbitrary", "arbitrary", "arbitrary"),
            vmem_limit_bytes=VMEM_LIMIT),
        name="prompt_attn",
    )(q, kt, vt)


def _outproj_router_kernel(x_ref, att_ref, cy_ref, wo_ref, g_ref, wr_ref, br_ref,
                           hres_ref, hn_ref, gates_ref, *, precise):
    hres = (x_ref[...] + _mm(att_ref[...], wo_ref[0:ATT_WIDTH, :], precise)
            + _mm(cy_ref[...], wo_ref[ATT_WIDTH:, :], precise))
    hres_ref[...] = hres
    hn = _rms(hres, g_ref[...])
    hn_ref[...] = hn.astype(hn_ref.dtype)

    lo = jnp.dot(hn, wr_ref[...], precision=HIGHEST, preferred_element_type=F32) + br_ref[...]
    lane = lax.broadcasted_iota(jnp.int32, lo.shape, 1)
    is_group = (lane >= N_EXPERTS) & (lane < N_EXPERTS + N_EXPERT_GROUPS)
    lg = jnp.where(is_group, lo, NEG_INF)
    mg = jnp.max(lg, axis=1, keepdims=True)
    g_lane = jnp.min(jnp.where(lg == mg, lane, LANES), axis=1, keepdims=True)
    pg = 1.0 / jnp.sum(jnp.exp(lg - mg), axis=1, keepdims=True)
    e_lo = (g_lane - N_EXPERTS) * EXPERTS_PER_GROUP
    in_g = (lane >= e_lo) & (lane < e_lo + EXPERTS_PER_GROUP)
    le = jnp.where(in_g, lo, NEG_INF)
    m1 = jnp.max(le, axis=1, keepdims=True)
    i1 = jnp.min(jnp.where(le == m1, lane, LANES), axis=1, keepdims=True)
    le2 = jnp.where(lane == i1, NEG_INF, le)
    m2 = jnp.max(le2, axis=1, keepdims=True)
    i2 = jnp.min(jnp.where(le2 == m2, lane, LANES), axis=1, keepdims=True)
    e2 = jnp.exp(m2 - m1)
    w1 = pg / (1.0 + e2)
    w2 = pg * e2 / (1.0 + e2)
    gates_ref[...] = jnp.where(lane == i1, w1, 0.0) + jnp.where(lane == i2, w2, 0.0)


def _outproj_router(x, att, cy, w_out, g_ffn, w_r, b_r, tm, precise):
    n, d = x.shape
    rows = lambda width: pl.BlockSpec((tm, width), lambda i: (i, 0))
    full = lambda shape: pl.BlockSpec(shape, lambda i: (0,) * len(shape))
    return pl.pallas_call(
        functools.partial(_outproj_router_kernel, precise=precise),
        grid=(n // tm,),
        in_specs=[rows(d), rows(ATT_WIDTH), rows(CONV_CH), full(w_out.shape), full((1, d)),
                  full(w_r.shape), full((1, LANES))],
        out_specs=[rows(d), rows(d), rows(LANES)],
        out_shape=[jax.ShapeDtypeStruct((n, d), F32), jax.ShapeDtypeStruct((n, d), BF16),
                   jax.ShapeDtypeStruct((n, LANES), F32)],
        compiler_params=pltpu.CompilerParams(
            dimension_semantics=("arbitrary",), vmem_limit_bytes=VMEM_LIMIT),
        name="outproj_router_precise" if precise else "outproj_router",
    )(x, att, cy, w_out, g_ffn, w_r, b_r)


def _moe_dense_kernel(hn_ref, gates_ref, hres_ref, wg_ref, wu_ref, wd_ref, gf_ref, y_ref, acc_ref):
    e = pl.program_id(1)

    @pl.when(e == 0)
    def _():
        acc_ref[...] = jnp.zeros_like(acc_ref)

    x = hn_ref[...]
    gates = gates_ref[...]
    lane = lax.broadcasted_iota(jnp.int32, gates.shape, 1)
    ge = jnp.sum(jnp.where(lane == e, gates, 0.0), axis=1, keepdims=True)
    hg = jnp.dot(x, wg_ref[...], preferred_element_type=F32)
    hu = jnp.dot(x, wu_ref[...], preferred_element_type=F32)
    h = hg * (1.0 / (1.0 + jnp.exp(-hg))) * hu * ge
    acc_ref[...] += jnp.dot(h.astype(BF16), wd_ref[...], preferred_element_type=F32)

    @pl.when(e == pl.num_programs(1) - 1)
    def _():
        y_ref[...] = _rms(hres_ref[...] + acc_ref[...], gf_ref[...])


def _moe_dense(hn, gates, hres, wg, wu, wd, g_final, tm):
    n, d = hn.shape
    rows = lambda width: pl.BlockSpec((tm, width), lambda i, e: (i, 0))
    return pl.pallas_call(
        _moe_dense_kernel,
        grid=(n // tm, N_EXPERTS),
        in_specs=[rows(d), rows(LANES), rows(d),
                  pl.BlockSpec((None, d, EXPERT_FF), lambda i, e: (e, 0, 0)),
                  pl.BlockSpec((None, d, EXPERT_FF), lambda i, e: (e, 0, 0)),
                  pl.BlockSpec((None, EXPERT_FF, d), lambda i, e: (e, 0, 0)),
                  pl.BlockSpec((1, d), lambda i, e: (0, 0))],
        out_specs=rows(d),
        out_shape=jax.ShapeDtypeStruct((n, d), F32),
        scratch_shapes=[pltpu.VMEM((tm, d), F32)],
        compiler_params=pltpu.CompilerParams(
            dimension_semantics=("arbitrary", "arbitrary"), vmem_limit_bytes=VMEM_LIMIT),
        name="moe_dense",
    )(hn, gates, hres, wg, wu, wd, g_final)


def _sample_inproj_kernel(x_ref, g_ref, w_ref, cos_ref, slo_ref, shi_ref, cw_ref, p0_ref, p1_ref,
                          q_ref, k_ref, v_ref, cy_ref, u_ref):
    xn = _rms(x_ref[...], g_ref[...])
    reps = ATT_WIDTH // LANES
    cos = _tile_lanes(cos_ref[...], reps)
    slo = _tile_lanes(slo_ref[...], reps)
    shi = _tile_lanes(shi_ref[...], reps)

    def proj(i):
        return jnp.dot(xn, w_ref[:, i * ATT_WIDTH:(i + 1) * ATT_WIDTH], precision=HIGHEST,
                       preferred_element_type=F32)

    q_ref[...] = _rope_rows(proj(0), cos, slo, shi)
    k_ref[...] = _rope_rows(proj(1), cos, slo, shi)
    v_ref[...] = proj(2)
    b_gate = proj(3)
    u = proj(4) * proj(5)
    u_ref[...] = u
    cw = cw_ref[...]
    cy_ref[...] = b_gate * (cw[0:1, :] * p0_ref[...] + cw[1:2, :] * p1_ref[...] + cw[2:3, :] * u)


def _sample_inproj(x, g_mix, w_in, tables, conv_w, prev0, prev1):
    n = x.shape[0]
    out = jax.ShapeDtypeStruct((n, ATT_WIDTH), F32)
    return pl.pallas_call(
        _sample_inproj_kernel,
        out_shape=[out] * 5,
        compiler_params=pltpu.CompilerParams(vmem_limit_bytes=VMEM_LIMIT),
        name="sample_inproj",
    )(x, g_mix, w_in, *tables, conv_w, prev0, prev1)


PAGES_PER_STEP = 16


def _sample_select_kernel(pt_ref, *refs, page_size, n_blocks):
    q_ref = refs[0]
    page_refs = refs[1:1 + PAGES_PER_STEP]
    sel_ref, gates = refs[1 + PAGES_PER_STEP], refs[2 + PAGES_PER_STEP]
    g = pl.program_id(1)
    pages_per_block = MOBA_BLOCK // page_size
    blocks_per_step = PAGES_PER_STEP // pages_per_block
    lane = lax.broadcasted_iota(jnp.int32, (N_HEADS, LANES), 1)

    @pl.when(g == 0)
    def _():
        gates[...] = jnp.where(lane < n_blocks, 0.0, NEG_INF)

    qb = q_ref[...]
    acc = gates[...]
    for i in range(blocks_per_step):
        tot = page_refs[i * pages_per_block][...] * qb
        for p in range(1, pages_per_block):
            tot = tot + page_refs[i * pages_per_block + p][...] * qb
        per_head = jnp.sum(jnp.sum(tot, axis=1), axis=1, keepdims=True) * (1.0 / MOBA_BLOCK)
        acc = jnp.where(lane == g * blocks_per_step + i, per_head, acc)
    gates[...] = acc

    @pl.when(g == pl.num_programs(1) - 1)
    def _():
        out = jnp.zeros((N_HEADS, LANES), jnp.int32)
        for j in range(n_blocks):
            gj = acc[:, j:j + 1]
            beats = (acc > gj) | ((acc == gj) & (lane < j))
            cnt = jnp.sum(beats.astype(jnp.int32), axis=1, keepdims=True)
            out = jnp.where(lane == cnt, j, out)
        sel_ref[...] = out


def _sample_select(page_table, q_bcast, cache_kt):
    n_dec, n_pages = page_table.shape
    _, n_heads, head_dim, page_size = cache_kt.shape
    n_blocks = n_pages * page_size // MOBA_BLOCK

    def page_spec(i):
        return pl.BlockSpec((None, n_heads, head_dim, page_size),
                            lambda b, g, pt: (pt[b, g * PAGES_PER_STEP + i], 0, 0, 0))

    per_seq = lambda shape: pl.BlockSpec((None,) + shape, lambda b, g, pt: (b,) + (0,) * len(shape))
    return pl.pallas_call(
        functools.partial(_sample_select_kernel, page_size=page_size, n_blocks=n_blocks),
        grid_spec=pltpu.PrefetchScalarGridSpec(
            num_scalar_prefetch=1,
            grid=(n_dec, n_pages // PAGES_PER_STEP),
            in_specs=[per_seq((n_heads, head_dim, page_size))]
                     + [page_spec(i) for i in range(PAGES_PER_STEP)],
            out_specs=per_seq((n_heads, LANES)),
            scratch_shapes=[pltpu.VMEM((n_heads, LANES), F32)],
        ),
        out_shape=jax.ShapeDtypeStruct((n_dec, n_heads, LANES), jnp.int32),
        compiler_params=pltpu.CompilerParams(
            dimension_semantics=("arbitrary", "arbitrary"), vmem_limit_bytes=VMEM_LIMIT),
        name="sample_select",
    )(page_table, q_bcast, *([cache_kt] * PAGES_PER_STEP))


def _sample_attn_kernel(sel_ref, pt_ref, q_ref, kn_ref, vn_ref, ck_ref, cv_ref, o_ref,
                        kbuf, vbuf, sem, *, pages_per_block, page_size):
    b = pl.program_id(0)
    nb = pl.num_programs(0)

    def copies(bb, slot):
        out = []
        for h in range(N_HEADS):
            for r in range(MOBA_TOPK):
                block = sel_ref[(bb * N_HEADS + h) * MOBA_TOPK + r]
                for p in range(pages_per_block):
                    page = pt_ref[bb, block * pages_per_block + p]
                    dst = pl.ds((r * pages_per_block + p) * page_size, page_size)
                    out.append(pltpu.make_async_copy(ck_ref.at[page, h], kbuf.at[slot, h, :, dst],
                                                     sem.at[slot, 0]))
                    out.append(pltpu.make_async_copy(cv_ref.at[page, h], vbuf.at[slot, h, :, dst],
                                                     sem.at[slot, 1]))
        return out

    @pl.when(b == 0)
    def _():
        for c in copies(0, 0):
            c.start()

    @pl.when(b + 1 < nb)
    def _():
        for c in copies(b + 1, (b + 1) % 2):
            c.start()

    slot = b % 2
    for c in copies(b, slot):
        c.wait()

    q = q_ref[...]
    kn = kn_ref[...]
    vn = vn_ref[...]
    outs = []
    for h in range(N_HEADS):
        hs = slice(h * HEAD_DIM, (h + 1) * HEAD_DIM)
        qh = q[:, hs] * ATT_SCALE
        q8 = jnp.broadcast_to(qh, (SUBLANES, HEAD_DIM))
        s = jnp.dot(q8, kbuf[slot, h], precision=HIGHEST,
                    preferred_element_type=F32)
        s_self = jnp.sum(qh * kn[:, hs], axis=1, keepdims=True)
        m = jnp.maximum(jnp.max(s, axis=1, keepdims=True), s_self)
        p = jnp.exp(s - m)
        p_self = jnp.exp(s_self - m)
        l = jnp.sum(p, axis=1, keepdims=True) + p_self
        o = lax.dot_general(p, vbuf[slot, h], NT_DIMS, precision=HIGHEST,
                            preferred_element_type=F32)
        o = (o + p_self * vn[:, hs]) / l
        outs.append(o[0:1, :])
    o_ref[...] = jnp.concatenate(outs, axis=1)


def _sample_attn(sel_flat, page_table, q3, k3, v3, cache_kt, cache_vt):
    n_dec = q3.shape[0]
    page_size = cache_kt.shape[3]
    pages_per_block = MOBA_BLOCK // page_size
    n_keys = MOBA_TOPK * MOBA_BLOCK
    row = pl.BlockSpec((None, 1, ATT_WIDTH), lambda b, sel, pt: (b, 0, 0))
    hbm = pl.BlockSpec(memory_space=pl.ANY)
    return pl.pallas_call(
        functools.partial(_sample_attn_kernel, pages_per_block=pages_per_block,
                          page_size=page_size),
        grid_spec=pltpu.PrefetchScalarGridSpec(
            num_scalar_prefetch=2,
            grid=(n_dec,),
            in_specs=[row, row, row, hbm, hbm],
            out_specs=row,
            scratch_shapes=[pltpu.VMEM((2, N_HEADS, HEAD_DIM, n_keys), F32),
                            pltpu.VMEM((2, N_HEADS, HEAD_DIM, n_keys), F32),
                            pltpu.SemaphoreType.DMA((2, 2))],
        ),
        out_shape=jax.ShapeDtypeStruct((n_dec, 1, ATT_WIDTH), F32),
        compiler_params=pltpu.CompilerParams(
            dimension_semantics=("arbitrary",), vmem_limit_bytes=VMEM_LIMIT),
        name="sample_attn",
    )(sel_flat, page_table, q3, k3, v3, cache_kt, cache_vt)


def kernel(x_prompt, x_sample, cache_k, cache_v, state_conv, page_table, g_mix, w_in, conv_w, w_out,
           g_ffn, w_router_group, b_router_group, w_router_expert, b_router_expert, w_gate, w_up,
           w_down, g_final):
    depth = g_mix.shape[0]
    assert depth == 1
    bsz, seq, d = x_prompt.shape
    n_dec, dec_len, _ = x_sample.shape
    assert dec_len == 1
    page_size = cache_k.shape[2]
    past_len = page_table.shape[1] * page_size
    assert past_len % MOBA_BLOCK == 0 and MOBA_BLOCK % page_size == 0
    assert past_len // MOBA_BLOCK >= MOBA_TOPK and seq % MOBA_BLOCK == 0

    g_mix2 = g_mix[0][None, :]
    g_ffn2 = g_ffn[0][None, :]
    g_final2 = g_final[None, :]
    w_in_f, w_out_f = w_in[0], w_out[0]
    w_out_bf = w_out_f.astype(BF16)
    wg_bf, wu_bf, wd_bf = w_gate[0].astype(BF16), w_up[0].astype(BF16), w_down[0].astype(BF16)
    pad = LANES - N_EXPERTS - N_EXPERT_GROUPS
    w_r = jnp.concatenate([w_router_expert[0], w_router_group[0], jnp.zeros((d, pad), F32)], axis=1)
    b_r = jnp.concatenate([b_router_expert[0], b_router_group[0], jnp.zeros((pad,), F32)])[None, :]
    cw = conv_w[0]

    cos_p, sin_p = _rope_angles(jnp.arange(seq, dtype=jnp.int32))
    conv0 = jnp.zeros((bsz, CONV_K - 1, CONV_CH), F32)
    q_p, kt_p, vt_p, cy_p, conv_p = _prompt_inproj(
        x_prompt, g_mix2, w_in_f, _rope_row_tables(cos_p, sin_p), (cos_p.T, sin_p.T), cw, conv0)
    att_p = _prompt_attn(q_p, kt_p, vt_p)
    n_p = bsz * seq
    hres_p, hn_p, gates_p = _outproj_router(
        x_prompt.reshape(n_p, d), att_p.reshape(n_p, ATT_WIDTH), cy_p.reshape(n_p, CONV_CH),
        w_out_bf, g_ffn2, w_r, b_r, tm=512, precise=False)
    y_p = _moe_dense(hn_p, gates_p, hres_p, wg_bf, wu_bf, wd_bf, g_final2, tm=1024)

    cos_s, sin_s = _rope_angles(past_len + jnp.arange(dec_len, dtype=jnp.int32))
    x_s = x_sample.reshape(n_dec, d)
    prev0, prev1 = state_conv[0, :, 0, :], state_conv[0, :, 1, :]
    q_s, k_s, v_s, cy_s, u_s = _sample_inproj(x_s, g_mix2, w_in_f, _rope_row_tables(cos_s, sin_s),
                                              cw, prev0, prev1)
    cache_kt = jnp.transpose(cache_k[0], (0, 2, 3, 1))
    cache_vt = jnp.transpose(cache_v[0], (0, 2, 3, 1))
    q_bcast = jnp.broadcast_to(q_s.reshape(n_dec, N_HEADS, HEAD_DIM, 1),
                               (n_dec, N_HEADS, HEAD_DIM, page_size))
    sel = _sample_select(page_table, q_bcast, cache_kt)
    sel_flat = sel[:, :, :MOBA_TOPK].reshape(-1)
    q3 = q_s.reshape(n_dec, 1, ATT_WIDTH)
    att_s = _sample_attn(sel_flat, page_table, q3, k_s.reshape(n_dec, 1, ATT_WIDTH),
                         v_s.reshape(n_dec, 1, ATT_WIDTH), cache_kt, cache_vt)
    hres_s, hn_s, gates_s = _outproj_router(
        x_s, att_s.reshape(n_dec, ATT_WIDTH), cy_s, w_out_f, g_ffn2, w_r, b_r,
        tm=n_dec, precise=True)
    y_s = _moe_dense(hn_s, gates_s, hres_s, wg_bf, wu_bf, wd_bf, g_final2, tm=n_dec)

    conv_s = jnp.stack([prev1, u_s], axis=1)
    to_bthd = lambda t: jnp.transpose(t.reshape(bsz, N_HEADS, HEAD_DIM, seq), (0, 3, 1, 2))[None]
    return (y_p.reshape(bsz, seq, d), y_s.reshape(n_dec, dec_len, d),
            to_bthd(kt_p), to_bthd(vt_p), conv_p[None],
            k_s.reshape(1, n_dec, dec_len, N_HEADS, HEAD_DIM),
            v_s.reshape(1, n_dec, dec_len, N_HEADS, HEAD_DIM),
            conv_s[None])
```

```python
import functools

import jax
import jax.numpy as jnp
from jax import lax
from jax.experimental import pallas as pl
from jax.experimental.pallas import tpu as pltpu

F32 = jnp.float32
BF16 = jnp.bfloat16
HIGHEST = lax.Precision.HIGHEST

D_MODEL = 1024
HEAD_DIM = 64
N_HEADS = 8
ATT_WIDTH = N_HEADS * HEAD_DIM
CONV_CH = D_MODEL - ATT_WIDTH
ROPE_DIM = HEAD_DIM // 4
ROPE_HALF = ROPE_DIM // 2
ROPE_THETA = 500000.0
MOBA_BLOCK = 256
MOBA_TOPK = 3
ATT_SCALE = HEAD_DIM ** -0.5
CONV_K = 3
N_EXPERT_GROUPS = 4
EXPERTS_PER_GROUP = 8
N_EXPERTS = N_EXPERT_GROUPS * EXPERTS_PER_GROUP
EXPERT_FF = D_MODEL // 4
RMS_EPS = 1e-6

LANES = 128
SUBLANES = 8
HEADS_PER_VREG = LANES // HEAD_DIM
VMEM_LIMIT = 56 * 1024 * 1024

NEG_INF = float("-inf")
NT_DIMS = (((1,), (1,)), ((), ()))


def _rms(x, g):
    ms = jnp.mean(x * x, axis=-1, keepdims=True)
    return x * lax.rsqrt(ms + RMS_EPS) * g


def _mm(a, w, precise):
    if precise:
        return jnp.dot(a, w, precision=HIGHEST, preferred_element_type=F32)
    return jnp.dot(a.astype(BF16), w, preferred_element_type=F32)


def _rope_rows(a, cos, sin_lo, sin_hi):
    n = a.shape[-1]
    return (a * cos + pltpu.roll(a, n - ROPE_HALF, 1) * sin_lo
            + pltpu.roll(a, ROPE_HALF, 1) * sin_hi)


def _tile_lanes(t, reps):
    return jnp.concatenate([t] * reps, axis=-1)


def _rope_angles(pos):
    inv = ROPE_THETA ** (-jnp.arange(0, ROPE_DIM, 2, dtype=F32) / ROPE_DIM)
    ang = pos.astype(F32)[:, None] * inv[None, :]
    return jnp.cos(ang), jnp.sin(ang)


def _rope_row_tables(cos, sin):
    rows = cos.shape[0]
    ones = jnp.ones((rows, HEAD_DIM - ROPE_DIM), F32)
    zeros = jnp.zeros((rows, HEAD_DIM - ROPE_HALF), F32)
    c = jnp.concatenate([cos, cos, ones], axis=1)
    s_lo = jnp.concatenate([-sin, zeros], axis=1)
    s_hi = jnp.concatenate([jnp.zeros((rows, ROPE_HALF), F32), sin,
                            jnp.zeros((rows, HEAD_DIM - ROPE_DIM), F32)], axis=1)
    rep = lambda t: jnp.concatenate([t] * HEADS_PER_VREG, axis=1)
    return rep(c), rep(s_lo), rep(s_hi)


def _rope_cols(ref, cos_t, sin_t):
    for h in range(N_HEADS):
        r = h * HEAD_DIM
        x1 = ref[r:r + ROPE_HALF, :]
        x2 = ref[r + ROPE_HALF:r + ROPE_DIM, :]
        ref[r:r + ROPE_HALF, :] = x1 * cos_t - x2 * sin_t
        ref[r + ROPE_HALF:r + ROPE_DIM, :] = x2 * cos_t + x1 * sin_t


def _prompt_inproj_kernel(x_ref, g_ref, wqt_ref, wk_ref, wvt_ref, wc_ref, cos_ref, slo_ref, shi_ref,
                          cost_ref, sint_ref, cw_ref, prev_ref,
                          qt_ref, kt_ref, vt_ref, kb_ref, vtb_ref, km_ref, cy_ref, cn_ref,
                          ubuf, halo, km_acc, *, tt):
    t = pl.program_id(1)
    blk = MOBA_BLOCK
    xn = _rms(x_ref[...], g_ref[...]).astype(BF16)
    reps = ATT_WIDTH // LANES

    qt_ref[...] = lax.dot_general(wqt_ref[...], xn, NT_DIMS, preferred_element_type=F32)
    _rope_cols(qt_ref, cost_ref[...], sint_ref[...])
    vt = lax.dot_general(wvt_ref[...], xn, NT_DIMS, preferred_element_type=F32)
    vt_ref[...] = vt
    for i in range(tt // blk):
        vtb_ref[i] = vt[:, i * blk:(i + 1) * blk].astype(BF16)

    k = jnp.dot(xn, wk_ref[...], preferred_element_type=F32)
    k = _rope_rows(k, _tile_lanes(cos_ref[...], reps), _tile_lanes(slo_ref[...], reps),
                   _tile_lanes(shi_ref[...], reps))
    kb_ref[...] = k.astype(BF16)
    kt_ref[...] = k.T

    @pl.when(t == 0)
    def _():
        km_acc[...] = jnp.zeros_like(km_acc)

    rid = lax.broadcasted_iota(jnp.int32, km_acc.shape, 0)
    km = km_acc[...]
    for i in range(tt // blk):
        mean_i = jnp.mean(k[i * blk:(i + 1) * blk, :], axis=0, keepdims=True)
        km = jnp.where(rid == t * (tt // blk) + i, mean_i, km)
    km_acc[...] = km
    km_ref[...] = km

    def proj(i):
        return jnp.dot(xn, wc_ref[:, i * CONV_CH:(i + 1) * CONV_CH], preferred_element_type=F32)

    b_gate = proj(0)
    u = proj(1) * proj(2)

    @pl.when(t == 0)
    def _():
        ubuf[SUBLANES - (CONV_K - 1):SUBLANES, :] = prev_ref[...]

    @pl.when(t > 0)
    def _():
        ubuf[0:SUBLANES, :] = halo[...]

    ubuf[SUBLANES:SUBLANES + tt, :] = u
    cw = cw_ref[...]
    conv = (cw[0:1, :] * ubuf[SUBLANES - 2:SUBLANES - 2 + tt, :]
            + cw[1:2, :] * ubuf[SUBLANES - 1:SUBLANES - 1 + tt, :]
            + cw[2:3, :] * u)
    cy_ref[...] = (b_gate * conv).astype(cy_ref.dtype)
    halo[...] = ubuf[tt:tt + SUBLANES, :]
    cn_ref[...] = ubuf[tt + SUBLANES - (CONV_K - 1):tt + SUBLANES, :]


def _prompt_inproj(x, g_mix, w_in, row_tables, col_tables, conv_w, conv_prev, tt=512):
    bsz, seq, d = x.shape
    nt = seq // tt
    nb = seq // MOBA_BLOCK
    w_bf = w_in.astype(BF16)
    wqt = w_bf[:, :ATT_WIDTH].T
    wk = w_bf[:, ATT_WIDTH:2 * ATT_WIDTH]
    wvt = w_bf[:, 2 * ATT_WIDTH:3 * ATT_WIDTH].T
    wc = w_bf[:, 3 * ATT_WIDTH:]
    row_spec = lambda width: pl.BlockSpec((None, tt, width), lambda b, t: (b, t, 0))
    col_spec = pl.BlockSpec((None, ATT_WIDTH, tt), lambda b, t: (b, 0, t))
    tab_spec = pl.BlockSpec((tt, LANES), lambda b, t: (t, 0))
    tabt_spec = pl.BlockSpec((ROPE_HALF, tt), lambda b, t: (0, t))
    full = lambda shape: pl.BlockSpec(shape, lambda b, t: (0,) * len(shape))
    per_b = lambda shape: pl.BlockSpec((None,) + shape, lambda b, t: (b,) + (0,) * len(shape))
    vtb_spec = pl.BlockSpec((None, tt // MOBA_BLOCK, ATT_WIDTH, MOBA_BLOCK),
                            lambda b, t: (b, t, 0, 0))
    col_shape = jax.ShapeDtypeStruct((bsz, ATT_WIDTH, seq), F32)
    return pl.pallas_call(
        functools.partial(_prompt_inproj_kernel, tt=tt),
        grid=(bsz, nt),
        in_specs=[row_spec(d), full((1, d)), full(wqt.shape), full(wk.shape), full(wvt.shape),
                  full(wc.shape), tab_spec, tab_spec, tab_spec, tabt_spec, tabt_spec,
                  full((CONV_K, CONV_CH)), per_b((CONV_K - 1, CONV_CH))],
        out_specs=[col_spec, col_spec, col_spec, row_spec(ATT_WIDTH), vtb_spec,
                   per_b((nb, ATT_WIDTH)), row_spec(CONV_CH), per_b((CONV_K - 1, CONV_CH))],
        out_shape=[col_shape, col_shape, col_shape,
                   jax.ShapeDtypeStruct((bsz, seq, ATT_WIDTH), BF16),
                   jax.ShapeDtypeStruct((bsz, nb, ATT_WIDTH, MOBA_BLOCK), BF16),
                   jax.ShapeDtypeStruct((bsz, nb, ATT_WIDTH), F32),
                   jax.ShapeDtypeStruct((bsz, seq, CONV_CH), BF16),
                   jax.ShapeDtypeStruct((bsz, CONV_K - 1, CONV_CH), F32)],
        scratch_shapes=[pltpu.VMEM((tt + SUBLANES, CONV_CH), F32),
                        pltpu.VMEM((SUBLANES, CONV_CH), F32),
                        pltpu.VMEM((nb, ATT_WIDTH), F32)],
        compiler_params=pltpu.CompilerParams(
            dimension_semantics=("arbitrary", "arbitrary"), vmem_limit_bytes=VMEM_LIMIT),
        name="prompt_inproj",
    )(x, g_mix, wqt, wk, wvt, wc, *row_tables, *col_tables, conv_w, conv_prev)


def _prompt_attn_kernel(qt_ref, kb_ref, vtb_ref, km_ref, o_ref, s_ref, ot_ref, *, nb):
    blk = MOBA_BLOCK
    means = km_ref[...]
    feat = lax.broadcasted_iota(jnp.int32, (LANES, 1), 0)
    bid = lax.broadcasted_iota(jnp.int32, (nb, blk), 0)
    key = lax.broadcasted_iota(jnp.int32, (blk, blk), 0)
    qry = lax.broadcasted_iota(jnp.int32, (blk, blk), 1)

    def block_bias(qth, qi):
        if qi <= MOBA_TOPK:
            return [None] * qi
        gates = jnp.dot(means, qth, precision=HIGHEST, preferred_element_type=F32)
        gates = jnp.where(bid < qi, gates, NEG_INF)
        rows = []
        for j in range(qi):
            gj = gates[j:j + 1, :]
            beats = (gates > gj) | ((gates == gj) & (bid < j))
            cnt = jnp.sum(beats.astype(F32), axis=0, keepdims=True)
            rows.append(jnp.where(cnt < MOBA_TOPK, 0.0, NEG_INF))
        return rows

    def scores_pass(qi, h, slot, state):
        qt = qt_ref[:, qi * blk:(qi + 1) * blk]
        qth = jnp.where((feat >= h * HEAD_DIM) & (feat < (h + 1) * HEAD_DIM), qt, 0.0)
        bias = block_bias(qth, qi)
        qs = (qth * ATT_SCALE).astype(BF16)
        m = None
        for j in range(qi + 1):
            s = jnp.dot(kb_ref[j * blk:(j + 1) * blk, :], qs,
                        preferred_element_type=F32)
            if j == qi:
                s = jnp.where(key <= qry, s, NEG_INF)
            elif bias[j] is not None:
                s = s + bias[j]
            s_ref[slot, j] = s
            m_blk = jnp.max(s, axis=0, keepdims=True)
            m = m_blk if m is None else jnp.maximum(m, m_blk)
            yield
        state["m"] = m

    def values_pass(qi, h, slot, state):
        m = state["m"]
        l = acc = None
        for j in range(qi + 1):
            p = jnp.exp(s_ref[slot, j] - m)
            l_blk = jnp.sum(p, axis=0, keepdims=True)
            pv = jnp.dot(vtb_ref[j, h * HEAD_DIM:(h + 1) * HEAD_DIM, :], p.astype(BF16),
                         preferred_element_type=F32)
            l, acc = (l_blk, pv) if l is None else (l + l_blk, acc + pv)
            yield
        ot_ref[h * HEAD_DIM:(h + 1) * HEAD_DIM, qi * blk:(qi + 1) * blk] = acc / l

    def run_interleaved(*gens):
        live = [g for g in gens if g is not None]
        while live:
            for g in list(live):
                if next(g, "done") == "done":
                    live.remove(g)

    pending = None
    for i, (qi, h) in enumerate((qi, h) for qi in range(nb) for h in range(HEADS_PER_VREG)):
        state = {}
        run_interleaved(scores_pass(qi, h, i % 2, state), pending)
        pending = values_pass(qi, h, i % 2, state)
    run_interleaved(pending)
    o_ref[...] = ot_ref[...].T.astype(o_ref.dtype)


def _prompt_attn(qt, kb, vtb, kmeans):
    bsz, seq, _ = kb.shape
    nb = seq // MOBA_BLOCK
    n_pairs = ATT_WIDTH // LANES
    return pl.pallas_call(
        functools.partial(_prompt_attn_kernel, nb=nb),
        grid=(bsz, n_pairs),
        in_specs=[pl.BlockSpec((None, LANES, seq), lambda b, hp: (b, hp, 0)),
                  pl.BlockSpec((None, seq, LANES), lambda b, hp: (b, 0, hp)),
                  pl.BlockSpec((None, nb, LANES, MOBA_BLOCK), lambda b, hp: (b, 0, hp, 0)),
                  pl.BlockSpec((None, nb, LANES), lambda b, hp: (b, 0, hp))],
        out_specs=pl.BlockSpec((None, seq, LANES), lambda b, hp: (b, 0, hp)),
        out_shape=jax.ShapeDtypeStruct((bsz, seq, ATT_WIDTH), BF16),
        scratch_shapes=[pltpu.VMEM((2, nb, MOBA_BLOCK, MOBA_BLOCK), F32),
                        pltpu.VMEM((LANES, seq), F32)],
        compiler_params=pltpu.CompilerParams(
            dimension_semantics=("arbitrary", "arbitrary"), vmem_limit_bytes=VMEM_LIMIT),
        name="prompt_attn",
    )(qt, kb, vtb, kmeans)


def _outproj_router_kernel(x_ref, att_ref, cy_ref, wo_ref, g_ref, wr_ref, br_ref,
                           hres_ref, hn_ref, gates_ref, *, precise):
    hres = (x_ref[...] + _mm(att_ref[...], wo_ref[0:ATT_WIDTH, :], precise)
            + _mm(cy_ref[...], wo_ref[ATT_WIDTH:, :], precise))
    hres_ref[...] = hres
    hn = _rms(hres, g_ref[...])
    hn_ref[...] = hn.astype(hn_ref.dtype)

    lo = jnp.dot(hn, wr_ref[...], precision=HIGHEST, preferred_element_type=F32) + br_ref[...]
    lane = lax.broadcasted_iota(jnp.int32, lo.shape, 1)
    is_group = (lane >= N_EXPERTS) & (lane < N_EXPERTS + N_EXPERT_GROUPS)
    lg = jnp.where(is_group, lo, NEG_INF)
    mg = jnp.max(lg, axis=1, keepdims=True)
    g_lane = jnp.min(jnp.where(lg == mg, lane, LANES), axis=1, keepdims=True)
    pg = 1.0 / jnp.sum(jnp.exp(lg - mg), axis=1, keepdims=True)
    e_lo = (g_lane - N_EXPERTS) * EXPERTS_PER_GROUP
    in_g = (lane >= e_lo) & (lane < e_lo + EXPERTS_PER_GROUP)
    le = jnp.where(in_g, lo, NEG_INF)
    m1 = jnp.max(le, axis=1, keepdims=True)
    i1 = jnp.min(jnp.where(le == m1, lane, LANES), axis=1, keepdims=True)
    le2 = jnp.where(lane == i1, NEG_INF, le)
    m2 = jnp.max(le2, axis=1, keepdims=True)
    i2 = jnp.min(jnp.where(le2 == m2, lane, LANES), axis=1, keepdims=True)
    e2 = jnp.exp(m2 - m1)
    w1 = pg / (1.0 + e2)
    w2 = pg * e2 / (1.0 + e2)
    gates_ref[...] = jnp.where(lane == i1, w1, 0.0) + jnp.where(lane == i2, w2, 0.0)


def _outproj_router(x, att, cy, w_out, g_ffn, w_r, b_r, tm, precise):
    n, d = x.shape
    rows = lambda width: pl.BlockSpec((tm, width), lambda i: (i, 0))
    full = lambda shape: pl.BlockSpec(shape, lambda i: (0,) * len(shape))
    return pl.pallas_call(
        functools.partial(_outproj_router_kernel, precise=precise),
        grid=(n // tm,),
        in_specs=[rows(d), rows(ATT_WIDTH), rows(CONV_CH), full(w_out.shape), full((1, d)),
                  full(w_r.shape), full((1, LANES))],
        out_specs=[rows(d), rows(d), rows(LANES)],
        out_shape=[jax.ShapeDtypeStruct((n, d), F32), jax.ShapeDtypeStruct((n, d), BF16),
                   jax.ShapeDtypeStruct((n, LANES), F32)],
        compiler_params=pltpu.CompilerParams(
            dimension_semantics=("arbitrary",), vmem_limit_bytes=VMEM_LIMIT),
        name="outproj_router_precise" if precise else "outproj_router",
    )(x, att, cy, w_out, g_ffn, w_r, b_r)


def _moe_dense_kernel(hn_ref, gates_ref, hres_ref, wg_ref, wu_ref, wd_ref, gf_ref, y_ref, acc_ref):
    e = pl.program_id(1)

    @pl.when(e == 0)
    def _():
        acc_ref[...] = jnp.zeros_like(acc_ref)

    x = hn_ref[...]
    gates = gates_ref[...]
    lane = lax.broadcasted_iota(jnp.int32, gates.shape, 1)
    ge = jnp.sum(jnp.where(lane == e, gates, 0.0), axis=1, keepdims=True)
    hg = jnp.dot(x, wg_ref[...], preferred_element_type=F32)
    hu = jnp.dot(x, wu_ref[...], preferred_element_type=F32)
    h = hg * (1.0 / (1.0 + jnp.exp(-hg))) * hu * ge
    acc_ref[...] += jnp.dot(h.astype(BF16), wd_ref[...], preferred_element_type=F32)

    @pl.when(e == pl.num_programs(1) - 1)
    def _():
        y_ref[...] = _rms(hres_ref[...] + acc_ref[...], gf_ref[...])


def _moe_dense(hn, gates, hres, wg, wu, wd, g_final, tm):
    n, d = hn.shape
    rows = lambda width: pl.BlockSpec((tm, width), lambda i, e: (i, 0))
    return pl.pallas_call(
        _moe_dense_kernel,
        grid=(n // tm, N_EXPERTS),
        in_specs=[rows(d), rows(LANES), rows(d),
                  pl.BlockSpec((None, d, EXPERT_FF), lambda i, e: (e, 0, 0)),
                  pl.BlockSpec((None, d, EXPERT_FF), lambda i, e: (e, 0, 0)),
                  pl.BlockSpec((None, EXPERT_FF, d), lambda i, e: (e, 0, 0)),
                  pl.BlockSpec((1, d), lambda i, e: (0, 0))],
        out_specs=rows(d),
        out_shape=jax.ShapeDtypeStruct((n, d), F32),
        scratch_shapes=[pltpu.VMEM((tm, d), F32)],
        compiler_params=pltpu.CompilerParams(
            dimension_semantics=("arbitrary", "arbitrary"), vmem_limit_bytes=VMEM_LIMIT),
        name="moe_dense",
    )(hn, gates, hres, wg, wu, wd, g_final)


def _sample_inproj_kernel(x_ref, g_ref, w_ref, cos_ref, slo_ref, shi_ref, cw_ref, p0_ref, p1_ref,
                          q_ref, k_ref, v_ref, cy_ref, u_ref):
    xn = _rms(x_ref[...], g_ref[...])
    reps = ATT_WIDTH // LANES
    cos = _tile_lanes(cos_ref[...], reps)
    slo = _tile_lanes(slo_ref[...], reps)
    shi = _tile_lanes(shi_ref[...], reps)

    def proj(i):
        return jnp.dot(xn, w_ref[:, i * ATT_WIDTH:(i + 1) * ATT_WIDTH], precision=HIGHEST,
                       preferred_element_type=F32)

    q_ref[...] = _rope_rows(proj(0), cos, slo, shi)
    k_ref[...] = _rope_rows(proj(1), cos, slo, shi)
    v_ref[...] = proj(2)
    b_gate = proj(3)
    u = proj(4) * proj(5)
    u_ref[...] = u
    cw = cw_ref[...]
    cy_ref[...] = b_gate * (cw[0:1, :] * p0_ref[...] + cw[1:2, :] * p1_ref[...] + cw[2:3, :] * u)


def _sample_inproj(x, g_mix, w_in, tables, conv_w, prev0, prev1):
    n = x.shape[0]
    out = jax.ShapeDtypeStruct((n, ATT_WIDTH), F32)
    return pl.pallas_call(
        _sample_inproj_kernel,
        out_shape=[out] * 5,
        compiler_params=pltpu.CompilerParams(vmem_limit_bytes=VMEM_LIMIT),
        name="sample_inproj",
    )(x, g_mix, w_in, *tables, conv_w, prev0, prev1)


PAGES_PER_STEP = 16


def _sample_select_kernel(pt_ref, *refs, page_size, n_blocks):
    q_ref = refs[0]
    page_refs = refs[1:1 + PAGES_PER_STEP]
    sel_ref, gates = refs[1 + PAGES_PER_STEP], refs[2 + PAGES_PER_STEP]
    g = pl.program_id(1)
    pages_per_block = MOBA_BLOCK // page_size
    blocks_per_step = PAGES_PER_STEP // pages_per_block
    lane = lax.broadcasted_iota(jnp.int32, (N_HEADS, LANES), 1)

    @pl.when(g == 0)
    def _():
        gates[...] = jnp.where(lane < n_blocks, 0.0, NEG_INF)

    qb = q_ref[...]
    acc = gates[...]
    for i in range(blocks_per_step):
        tot = page_refs[i * pages_per_block][...] * qb
        for p in range(1, pages_per_block):
            tot = tot + page_refs[i * pages_per_block + p][...] * qb
        per_head = jnp.sum(jnp.sum(tot, axis=1), axis=1, keepdims=True) * (1.0 / MOBA_BLOCK)
        acc = jnp.where(lane == g * blocks_per_step + i, per_head, acc)
    gates[...] = acc

    @pl.when(g == pl.num_programs(1) - 1)
    def _():
        out = jnp.zeros((N_HEADS, LANES), jnp.int32)
        for j in range(n_blocks):
            gj = acc[:, j:j + 1]
            beats = (acc > gj) | ((acc == gj) & (lane < j))
            cnt = jnp.sum(beats.astype(jnp.int32), axis=1, keepdims=True)
            out = jnp.where(lane == cnt, j, out)
        sel_ref[...] = out


def _sample_select(page_table, q_bcast, cache_kt):
    n_dec, n_pages = page_table.shape
    _, n_heads, head_dim, page_size = cache_kt.shape
    n_blocks = n_pages * page_size // MOBA_BLOCK

    def page_spec(i):
        return pl.BlockSpec((None, n_heads, head_dim, page_size),
                            lambda b, g, pt: (pt[b, g * PAGES_PER_STEP + i], 0, 0, 0))

    per_seq = lambda shape: pl.BlockSpec((None,) + shape, lambda b, g, pt: (b,) + (0,) * len(shape))
    return pl.pallas_call(
        functools.partial(_sample_select_kernel, page_size=page_size, n_blocks=n_blocks),
        grid_spec=pltpu.PrefetchScalarGridSpec(
            num_scalar_prefetch=1,
            grid=(n_dec, n_pages // PAGES_PER_STEP),
            in_specs=[per_seq((n_heads, head_dim, page_size))]
                     + [page_spec(i) for i in range(PAGES_PER_STEP)],
            out_specs=per_seq((n_heads, LANES)),
            scratch_shapes=[pltpu.VMEM((n_heads, LANES), F32)],
        ),
        out_shape=jax.ShapeDtypeStruct((n_dec, n_heads, LANES), jnp.int32),
        compiler_params=pltpu.CompilerParams(
            dimension_semantics=("arbitrary", "arbitrary"), vmem_limit_bytes=VMEM_LIMIT),
        name="sample_select",
    )(page_table, q_bcast, *([cache_kt] * PAGES_PER_STEP))


def _sample_attn_kernel(sel_ref, pt_ref, q_ref, kn_ref, vn_ref, ck_ref, cv_ref, o_ref,
                        kbuf, vbuf, sem, *, pages_per_block, page_size):
    b = pl.program_id(0)
    nb = pl.num_programs(0)

    def copies(bb, slot):
        out = []
        for h in range(N_HEADS):
            for r in range(MOBA_TOPK):
                block = sel_ref[(bb * N_HEADS + h) * MOBA_TOPK + r]
                for p in range(pages_per_block):
                    page = pt_ref[bb, block * pages_per_block + p]
                    dst = pl.ds((r * pages_per_block + p) * page_size, page_size)
                    out.append(pltpu.make_async_copy(ck_ref.at[page, h], kbuf.at[slot, h, :, dst],
                                                     sem.at[slot, 0]))
                    out.append(pltpu.make_async_copy(cv_ref.at[page, h], vbuf.at[slot, h, :, dst],
                                                     sem.at[slot, 1]))
        return out

    @pl.when(b == 0)
    def _():
        for c in copies(0, 0):
            c.start()

    @pl.when(b + 1 < nb)
    def _():
        for c in copies(b + 1, (b + 1) % 2):
            c.start()

    slot = b % 2
    for c in copies(b, slot):
        c.wait()

    q = q_ref[...]
    kn = kn_ref[...]
    vn = vn_ref[...]
    outs = []
    for h in range(N_HEADS):
        hs = slice(h * HEAD_DIM, (h + 1) * HEAD_DIM)
        qh = q[:, hs] * ATT_SCALE
        q8 = jnp.broadcast_to(qh, (SUBLANES, HEAD_DIM))
        s = jnp.dot(q8, kbuf[slot, h], precision=HIGHEST,
                    preferred_element_type=F32)
        s_self = jnp.sum(qh * kn[:, hs], axis=1, keepdims=True)
        m = jnp.maximum(jnp.max(s, axis=1, keepdims=True), s_self)
        p = jnp.exp(s - m)
        p_self = jnp.exp(s_self - m)
        l = jnp.sum(p, axis=1, keepdims=True) + p_self
        o = lax.dot_general(p, vbuf[slot, h], NT_DIMS, precision=HIGHEST,
                            preferred_element_type=F32)
        o = (o + p_self * vn[:, hs]) / l
        outs.append(o[0:1, :])
    o_ref[...] = jnp.concatenate(outs, axis=1)


def _sample_attn(sel_flat, page_table, q3, k3, v3, cache_kt, cache_vt):
    n_dec = q3.shape[0]
    page_size = cache_kt.shape[3]
    pages_per_block = MOBA_BLOCK // page_size
    n_keys = MOBA_TOPK * MOBA_BLOCK
    row = pl.BlockSpec((None, 1, ATT_WIDTH), lambda b, sel, pt: (b, 0, 0))
    hbm = pl.BlockSpec(memory_space=pl.ANY)
    return pl.pallas_call(
        functools.partial(_sample_attn_kernel, pages_per_block=pages_per_block,
                          page_size=page_size),
        grid_spec=pltpu.PrefetchScalarGridSpec(
            num_scalar_prefetch=2,
            grid=(n_dec,),
            in_specs=[row, row, row, hbm, hbm],
            out_specs=row,
            scratch_shapes=[pltpu.VMEM((2, N_HEADS, HEAD_DIM, n_keys), F32),
                            pltpu.VMEM((2, N_HEADS, HEAD_DIM, n_keys), F32),
                            pltpu.SemaphoreType.DMA((2, 2))],
        ),
        out_shape=jax.ShapeDtypeStruct((n_dec, 1, ATT_WIDTH), F32),
        compiler_params=pltpu.CompilerParams(
            dimension_semantics=("arbitrary",), vmem_limit_bytes=VMEM_LIMIT),
        name="sample_attn",
    )(sel_flat, page_table, q3, k3, v3, cache_kt, cache_vt)


def kernel(x_prompt, x_sample, cache_k, cache_v, state_conv, page_table, g_mix, w_in, conv_w, w_out,
           g_ffn, w_router_group, b_router_group, w_router_expert, b_router_expert, w_gate, w_up,
           w_down, g_final):
    depth = g_mix.shape[0]
    assert depth == 1
    bsz, seq, d = x_prompt.shape
    n_dec, dec_len, _ = x_sample.shape
    assert dec_len == 1
    page_size = cache_k.shape[2]
    past_len = page_table.shape[1] * page_size
    assert past_len % MOBA_BLOCK == 0 and MOBA_BLOCK % page_size == 0
    assert past_len // MOBA_BLOCK >= MOBA_TOPK and seq % MOBA_BLOCK == 0

    g_mix2 = g_mix[0][None, :]
    g_ffn2 = g_ffn[0][None, :]
    g_final2 = g_final[None, :]
    w_in_f, w_out_f = w_in[0], w_out[0]
    w_out_bf = w_out_f.astype(BF16)
    wg_bf, wu_bf, wd_bf = w_gate[0].astype(BF16), w_up[0].astype(BF16), w_down[0].astype(BF16)
    pad = LANES - N_EXPERTS - N_EXPERT_GROUPS
    w_r = jnp.concatenate([w_router_expert[0], w_router_group[0], jnp.zeros((d, pad), F32)], axis=1)
    b_r = jnp.concatenate([b_router_expert[0], b_router_group[0], jnp.zeros((pad,), F32)])[None, :]
    cw = conv_w[0]

    cos_p, sin_p = _rope_angles(jnp.arange(seq, dtype=jnp.int32))
    conv0 = jnp.zeros((bsz, CONV_K - 1, CONV_CH), F32)
    qt_p, kt_p, vt_p, kb_p, vtb_p, kmeans_p, cy_p, conv_p = _prompt_inproj(
        x_prompt, g_mix2, w_in_f, _rope_row_tables(cos_p, sin_p), (cos_p.T, sin_p.T), cw, conv0)
    att_p = _prompt_attn(qt_p, kb_p, vtb_p, kmeans_p)
    n_p = bsz * seq
    hres_p, hn_p, gates_p = _outproj_router(
        x_prompt.reshape(n_p, d), att_p.reshape(n_p, ATT_WIDTH), cy_p.reshape(n_p, CONV_CH),
        w_out_bf, g_ffn2, w_r, b_r, tm=512, precise=False)
    y_p = _moe_dense(hn_p, gates_p, hres_p, wg_bf, wu_bf, wd_bf, g_final2, tm=1024)

    cos_s, sin_s = _rope_angles(past_len + jnp.arange(dec_len, dtype=jnp.int32))
    x_s = x_sample.reshape(n_dec, d)
    prev0, prev1 = state_conv[0, :, 0, :], state_conv[0, :, 1, :]
    q_s, k_s, v_s, cy_s, u_s = _sample_inproj(x_s, g_mix2, w_in_f, _rope_row_tables(cos_s, sin_s),
                                              cw, prev0, prev1)
    cache_kt = jnp.transpose(cache_k[0], (0, 2, 3, 1))
    cache_vt = jnp.transpose(cache_v[0], (0, 2, 3, 1))
    q_bcast = jnp.broadcast_to(q_s.reshape(n_dec, N_HEADS, HEAD_DIM, 1),
                               (n_dec, N_HEADS, HEAD_DIM, page_size))
    sel = _sample_select(page_table, q_bcast, cache_kt)
    sel_flat = sel[:, :, :MOBA_TOPK].reshape(-1)
    q3 = q_s.reshape(n_dec, 1, ATT_WIDTH)
    att_s = _sample_attn(sel_flat, page_table, q3, k_s.reshape(n_dec, 1, ATT_WIDTH),
                         v_s.reshape(n_dec, 1, ATT_WIDTH), cache_kt, cache_vt)
    hres_s, hn_s, gates_s = _outproj_router(
        x_s, att_s.reshape(n_dec, ATT_WIDTH), cy_s, w_out_f, g_ffn2, w_r, b_r,
        tm=n_dec, precise=True)
    y_s = _moe_dense(hn_s, gates_s, hres_s, wg_bf, wu_bf, wd_bf, g_final2, tm=n_dec)

    conv_s = jnp.stack([prev1, u_s], axis=1)
    to_bthd = lambda t: jnp.transpose(t.reshape(bsz, N_HEADS, HEAD_DIM, seq), (0, 3, 1, 2))[None]
    return (y_p.reshape(bsz, seq, d), y_s.reshape(n_dec, dec_len, d),
            to_bthd(kt_p), to_bthd(vt_p), conv_p[None],
            k_s.reshape(1, n_dec, dec_len, N_HEADS, HEAD_DIM),
            v_s.reshape(1, n_dec, dec_len, N_HEADS, HEAD_DIM),
            conv_s[None])
```

```python
import functools

import jax
import jax.numpy as jnp
from jax import lax
from jax.experimental import pallas as pl
from jax.experimental.pallas import tpu as pltpu

F32 = jnp.float32
BF16 = jnp.bfloat16
HIGHEST = lax.Precision.HIGHEST

D_MODEL = 1024
HEAD_DIM = 64
N_HEADS = 8
ATT_WIDTH = N_HEADS * HEAD_DIM
CONV_CH = D_MODEL - ATT_WIDTH
ROPE_DIM = HEAD_DIM // 4
ROPE_HALF = ROPE_DIM // 2
ROPE_THETA = 500000.0
MOBA_BLOCK = 256
MOBA_TOPK = 3
ATT_SCALE = HEAD_DIM ** -0.5
CONV_K = 3
N_EXPERT_GROUPS = 4
EXPERTS_PER_GROUP = 8
N_EXPERTS = N_EXPERT_GROUPS * EXPERTS_PER_GROUP
EXPERT_TOPK = 2
EXPERT_FF = D_MODEL // 4
RMS_EPS = 1e-6

LANES = 128
SUBLANES = 8
HEADS_PER_VREG = LANES // HEAD_DIM
TOKEN_ROWS = D_MODEL // LANES
MOE_TILE = 256
VMEM_LIMIT = 56 * 1024 * 1024

NEG_INF = float("-inf")
NT_DIMS = (((1,), (1,)), ((), ()))


def _rms(x, g):
    ms = jnp.mean(x * x, axis=-1, keepdims=True)
    return x * lax.rsqrt(ms + RMS_EPS) * g


def _mm(a, w, precise):
    if precise:
        return jnp.dot(a, w, precision=HIGHEST, preferred_element_type=F32)
    return jnp.dot(a.astype(BF16), w, preferred_element_type=F32)


def _rope_rows(a, cos, sin_lo, sin_hi):
    n = a.shape[-1]
    return (a * cos + pltpu.roll(a, n - ROPE_HALF, 1) * sin_lo
            + pltpu.roll(a, ROPE_HALF, 1) * sin_hi)


def _tile_lanes(t, reps):
    return jnp.concatenate([t] * reps, axis=-1)


def _rope_angles(pos):
    inv = ROPE_THETA ** (-jnp.arange(0, ROPE_DIM, 2, dtype=F32) / ROPE_DIM)
    ang = pos.astype(F32)[:, None] * inv[None, :]
    return jnp.cos(ang), jnp.sin(ang)


def _rope_row_tables(cos, sin):
    rows = cos.shape[0]
    ones = jnp.ones((rows, HEAD_DIM - ROPE_DIM), F32)
    zeros = jnp.zeros((rows, HEAD_DIM - ROPE_HALF), F32)
    c = jnp.concatenate([cos, cos, ones], axis=1)
    s_lo = jnp.concatenate([-sin, zeros], axis=1)
    s_hi = jnp.concatenate([jnp.zeros((rows, ROPE_HALF), F32), sin,
                            jnp.zeros((rows, HEAD_DIM - ROPE_DIM), F32)], axis=1)
    rep = lambda t: jnp.concatenate([t] * HEADS_PER_VREG, axis=1)
    return rep(c), rep(s_lo), rep(s_hi)


def _rope_cols(ref, cos_t, sin_t):
    for h in range(N_HEADS):
        r = h * HEAD_DIM
        x1 = ref[r:r + ROPE_HALF, :]
        x2 = ref[r + ROPE_HALF:r + ROPE_DIM, :]
        ref[r:r + ROPE_HALF, :] = x1 * cos_t - x2 * sin_t
        ref[r + ROPE_HALF:r + ROPE_DIM, :] = x2 * cos_t + x1 * sin_t


def _prompt_inproj_kernel(x_ref, g_ref, wqt_ref, wk_ref, wvt_ref, wc_ref, cos_ref, slo_ref, shi_ref,
                          cost_ref, sint_ref, cw_ref, prev_ref,
                          qt_ref, kt_ref, vt_ref, kb_ref, vtb_ref, km_ref, cy_ref, cn_ref,
                          ubuf, halo, km_acc, *, tt):
    t = pl.program_id(1)
    blk = MOBA_BLOCK
    xn = _rms(x_ref[...], g_ref[...]).astype(BF16)
    reps = ATT_WIDTH // LANES

    qt_ref[...] = lax.dot_general(wqt_ref[...], xn, NT_DIMS, preferred_element_type=F32)
    _rope_cols(qt_ref, cost_ref[...], sint_ref[...])
    vt = lax.dot_general(wvt_ref[...], xn, NT_DIMS, preferred_element_type=F32)
    vt_ref[...] = vt
    for i in range(tt // blk):
        vtb_ref[i] = vt[:, i * blk:(i + 1) * blk].astype(BF16)

    k = jnp.dot(xn, wk_ref[...], preferred_element_type=F32)
    k = _rope_rows(k, _tile_lanes(cos_ref[...], reps), _tile_lanes(slo_ref[...], reps),
                   _tile_lanes(shi_ref[...], reps))
    kb_ref[...] = k.astype(BF16)
    kt_ref[...] = k.T

    @pl.when(t == 0)
    def _():
        km_acc[...] = jnp.zeros_like(km_acc)

    rid = lax.broadcasted_iota(jnp.int32, km_acc.shape, 0)
    km = km_acc[...]
    for i in range(tt // blk):
        mean_i = jnp.mean(k[i * blk:(i + 1) * blk, :], axis=0, keepdims=True)
        km = jnp.where(rid == t * (tt // blk) + i, mean_i, km)
    km_acc[...] = km
    km_ref[...] = km

    def proj(i):
        return jnp.dot(xn, wc_ref[:, i * CONV_CH:(i + 1) * CONV_CH], preferred_element_type=F32)

    b_gate = proj(0)
    u = proj(1) * proj(2)

    @pl.when(t == 0)
    def _():
        ubuf[SUBLANES - (CONV_K - 1):SUBLANES, :] = prev_ref[...]

    @pl.when(t > 0)
    def _():
        ubuf[0:SUBLANES, :] = halo[...]

    ubuf[SUBLANES:SUBLANES + tt, :] = u
    cw = cw_ref[...]
    conv = (cw[0:1, :] * ubuf[SUBLANES - 2:SUBLANES - 2 + tt, :]
            + cw[1:2, :] * ubuf[SUBLANES - 1:SUBLANES - 1 + tt, :]
            + cw[2:3, :] * u)
    cy_ref[...] = (b_gate * conv).astype(cy_ref.dtype)
    halo[...] = ubuf[tt:tt + SUBLANES, :]
    cn_ref[...] = ubuf[tt + SUBLANES - (CONV_K - 1):tt + SUBLANES, :]


def _prompt_inproj(x, g_mix, w_in, row_tables, col_tables, conv_w, conv_prev, tt=512):
    bsz, seq, d = x.shape
    nt = seq // tt
    nb = seq // MOBA_BLOCK
    w_bf = w_in.astype(BF16)
    wqt = w_bf[:, :ATT_WIDTH].T
    wk = w_bf[:, ATT_WIDTH:2 * ATT_WIDTH]
    wvt = w_bf[:, 2 * ATT_WIDTH:3 * ATT_WIDTH].T
    wc = w_bf[:, 3 * ATT_WIDTH:]
    row_spec = lambda width: pl.BlockSpec((None, tt, width), lambda b, t: (b, t, 0))
    col_spec = pl.BlockSpec((None, ATT_WIDTH, tt), lambda b, t: (b, 0, t))
    tab_spec = pl.BlockSpec((tt, LANES), lambda b, t: (t, 0))
    tabt_spec = pl.BlockSpec((ROPE_HALF, tt), lambda b, t: (0, t))
    full = lambda shape: pl.BlockSpec(shape, lambda b, t: (0,) * len(shape))
    per_b = lambda shape: pl.BlockSpec((None,) + shape, lambda b, t: (b,) + (0,) * len(shape))
    vtb_spec = pl.BlockSpec((None, tt // MOBA_BLOCK, ATT_WIDTH, MOBA_BLOCK),
                            lambda b, t: (b, t, 0, 0))
    col_shape = jax.ShapeDtypeStruct((bsz, ATT_WIDTH, seq), F32)
    return pl.pallas_call(
        functools.partial(_prompt_inproj_kernel, tt=tt),
        grid=(bsz, nt),
        in_specs=[row_spec(d), full((1, d)), full(wqt.shape), full(wk.shape), full(wvt.shape),
                  full(wc.shape), tab_spec, tab_spec, tab_spec, tabt_spec, tabt_spec,
                  full((CONV_K, CONV_CH)), per_b((CONV_K - 1, CONV_CH))],
        out_specs=[col_spec, col_spec, col_spec, row_spec(ATT_WIDTH), vtb_spec,
                   per_b((nb, ATT_WIDTH)), row_spec(CONV_CH), per_b((CONV_K - 1, CONV_CH))],
        out_shape=[col_shape, col_shape, col_shape,
                   jax.ShapeDtypeStruct((bsz, seq, ATT_WIDTH), BF16),
                   jax.ShapeDtypeStruct((bsz, nb, ATT_WIDTH, MOBA_BLOCK), BF16),
                   jax.ShapeDtypeStruct((bsz, nb, ATT_WIDTH), F32),
                   jax.ShapeDtypeStruct((bsz, seq, CONV_CH), BF16),
                   jax.ShapeDtypeStruct((bsz, CONV_K - 1, CONV_CH), F32)],
        scratch_shapes=[pltpu.VMEM((tt + SUBLANES, CONV_CH), F32),
                        pltpu.VMEM((SUBLANES, CONV_CH), F32),
                        pltpu.VMEM((nb, ATT_WIDTH), F32)],
        compiler_params=pltpu.CompilerParams(
            dimension_semantics=("arbitrary", "arbitrary"), vmem_limit_bytes=VMEM_LIMIT),
        name="prompt_inproj",
    )(x, g_mix, wqt, wk, wvt, wc, *row_tables, *col_tables, conv_w, conv_prev)


def _prompt_attn_kernel(qt_ref, kb_ref, vtb_ref, km_ref, o_ref, s_ref, ot_ref, *, nb):
    blk = MOBA_BLOCK
    means = km_ref[...]
    feat = lax.broadcasted_iota(jnp.int32, (LANES, 1), 0)
    bid = lax.broadcasted_iota(jnp.int32, (nb, blk), 0)
    key = lax.broadcasted_iota(jnp.int32, (blk, blk), 0)
    qry = lax.broadcasted_iota(jnp.int32, (blk, blk), 1)

    def block_bias(qth, qi):
        if qi <= MOBA_TOPK:
            return [None] * qi
        gates = jnp.dot(means, qth, precision=HIGHEST, preferred_element_type=F32)
        gates = jnp.where(bid < qi, gates, NEG_INF)
        rows = []
        for j in range(qi):
            gj = gates[j:j + 1, :]
            beats = (gates > gj) | ((gates == gj) & (bid < j))
            cnt = jnp.sum(beats.astype(F32), axis=0, keepdims=True)
            rows.append(jnp.where(cnt < MOBA_TOPK, 0.0, NEG_INF))
        return rows

    def scores_pass(qi, h, slot, state):
        qt = qt_ref[:, qi * blk:(qi + 1) * blk]
        qth = jnp.where((feat >= h * HEAD_DIM) & (feat < (h + 1) * HEAD_DIM), qt, 0.0)
        bias = block_bias(qth, qi)
        qs = (qth * ATT_SCALE).astype(BF16)
        m = None
        for j in range(qi + 1):
            s = jnp.dot(kb_ref[j * blk:(j + 1) * blk, :], qs,
                        preferred_element_type=F32)
            if j == qi:
                s = jnp.where(key <= qry, s, NEG_INF)
            elif bias[j] is not None:
                s = s + bias[j]
            s_ref[slot, j] = s
            m_blk = jnp.max(s, axis=0, keepdims=True)
            m = m_blk if m is None else jnp.maximum(m, m_blk)
            yield
        state["m"] = m

    def values_pass(qi, h, slot, state):
        m = state["m"]
        l = acc = None
        for j in range(qi + 1):
            p = jnp.exp(s_ref[slot, j] - m)
            l_blk = jnp.sum(p, axis=0, keepdims=True)
            pv = jnp.dot(vtb_ref[j, h * HEAD_DIM:(h + 1) * HEAD_DIM, :], p.astype(BF16),
                         preferred_element_type=F32)
            l, acc = (l_blk, pv) if l is None else (l + l_blk, acc + pv)
            yield
        ot_ref[h * HEAD_DIM:(h + 1) * HEAD_DIM, qi * blk:(qi + 1) * blk] = acc / l

    def run_interleaved(*gens):
        live = [g for g in gens if g is not None]
        while live:
            for g in list(live):
                if next(g, "done") == "done":
                    live.remove(g)

    pending = None
    for i, (qi, h) in enumerate((qi, h) for qi in range(nb) for h in range(HEADS_PER_VREG)):
        state = {}
        run_interleaved(scores_pass(qi, h, i % 2, state), pending)
        pending = values_pass(qi, h, i % 2, state)
    run_interleaved(pending)
    o_ref[...] = ot_ref[...].T.astype(o_ref.dtype)


def _prompt_attn(qt, kb, vtb, kmeans):
    bsz, seq, _ = kb.shape
    nb = seq // MOBA_BLOCK
    n_pairs = ATT_WIDTH // LANES
    return pl.pallas_call(
        functools.partial(_prompt_attn_kernel, nb=nb),
        grid=(bsz, n_pairs),
        in_specs=[pl.BlockSpec((None, LANES, seq), lambda b, hp: (b, hp, 0)),
                  pl.BlockSpec((None, seq, LANES), lambda b, hp: (b, 0, hp)),
                  pl.BlockSpec((None, nb, LANES, MOBA_BLOCK), lambda b, hp: (b, 0, hp, 0)),
                  pl.BlockSpec((None, nb, LANES), lambda b, hp: (b, 0, hp))],
        out_specs=pl.BlockSpec((None, seq, LANES), lambda b, hp: (b, 0, hp)),
        out_shape=jax.ShapeDtypeStruct((bsz, seq, ATT_WIDTH), BF16),
        scratch_shapes=[pltpu.VMEM((2, nb, MOBA_BLOCK, MOBA_BLOCK), F32),
                        pltpu.VMEM((LANES, seq), F32)],
        compiler_params=pltpu.CompilerParams(
            dimension_semantics=("arbitrary", "arbitrary"), vmem_limit_bytes=VMEM_LIMIT),
        name="prompt_attn",
    )(qt, kb, vtb, kmeans)


def _outproj_router_kernel(x_ref, att_ref, cy_ref, wo_ref, g_ref, wr_ref, br_ref, cnt_in_ref,
                           hres_ref, hn_ref, route_ref, cnt_ref, tri_ref, *, precise):
    step = pl.program_id(0)
    tm = x_ref.shape[0]
    hres = (x_ref[...] + _mm(att_ref[...], wo_ref[0:ATT_WIDTH, :], precise)
            + _mm(cy_ref[...], wo_ref[ATT_WIDTH:, :], precise))
    hres_ref[...] = hres
    hn = _rms(hres, g_ref[...])
    for s in range(TOKEN_ROWS):
        hn_ref[pl.ds(s, tm, stride=TOKEN_ROWS), :] = hn[:, s * LANES:(s + 1) * LANES]

    @pl.when(step == 0)
    def _():
        cnt_ref[...] = cnt_in_ref[...]
        r = lax.broadcasted_iota(jnp.int32, (tm, tm), 0)
        c = lax.broadcasted_iota(jnp.int32, (tm, tm), 1)
        tri_ref[...] = jnp.where(c < r, 1.0, 0.0).astype(BF16)

    lo = jnp.dot(hn, wr_ref[...], precision=HIGHEST, preferred_element_type=F32) + br_ref[...]
    lane = lax.broadcasted_iota(jnp.int32, lo.shape, 1)
    is_group = (lane >= N_EXPERTS) & (lane < N_EXPERTS + N_EXPERT_GROUPS)
    lg = jnp.where(is_group, lo, NEG_INF)
    mg = jnp.max(lg, axis=1, keepdims=True)
    g_lane = jnp.min(jnp.where(lg == mg, lane, LANES), axis=1, keepdims=True)
    pg = 1.0 / jnp.sum(jnp.exp(lg - mg), axis=1, keepdims=True)
    e_lo = (g_lane - N_EXPERTS) * EXPERTS_PER_GROUP
    in_g = (lane >= e_lo) & (lane < e_lo + EXPERTS_PER_GROUP)
    le = jnp.where(in_g, lo, NEG_INF)
    m1 = jnp.max(le, axis=1, keepdims=True)
    i1 = jnp.min(jnp.where(le == m1, lane, LANES), axis=1, keepdims=True)
    le2 = jnp.where(lane == i1, NEG_INF, le)
    m2 = jnp.max(le2, axis=1, keepdims=True)
    i2 = jnp.min(jnp.where(le2 == m2, lane, LANES), axis=1, keepdims=True)
    e2 = jnp.exp(m2 - m1)
    w1 = pg / (1.0 + e2)
    w2 = pg * e2 / (1.0 + e2)

    oh1, oh2 = lane == i1, lane == i2
    assigned = jnp.where(oh1 | oh2, 1.0, 0.0)
    before = cnt_ref[...] + jnp.dot(tri_ref[...], assigned.astype(BF16),
                                    preferred_element_type=F32)
    r1 = jnp.sum(jnp.where(oh1, before, 0.0), axis=1, keepdims=True)
    r2 = jnp.sum(jnp.where(oh2, before, 0.0), axis=1, keepdims=True)
    cnt_ref[...] += jnp.sum(assigned, axis=0, keepdims=True)
    cols = (i1.astype(F32), i2.astype(F32), r1, r2, w1, w2)
    route = jnp.zeros(lo.shape, F32)
    for c, v in enumerate(cols):
        route = jnp.where(lane == c, v, route)
    route_ref[...] = route


def _outproj_router(x, att, cy, w_out, g_ffn, w_r, b_r, counts_in, tm, precise):
    n, d = x.shape
    rows = lambda width: pl.BlockSpec((tm, width), lambda i: (i, 0))
    full = lambda shape: pl.BlockSpec(shape, lambda i: (0,) * len(shape))
    return pl.pallas_call(
        functools.partial(_outproj_router_kernel, precise=precise),
        grid=(n // tm,),
        in_specs=[rows(d), rows(ATT_WIDTH), rows(CONV_CH), full(w_out.shape), full((1, d)),
                  full(w_r.shape), full((1, LANES)), full((1, LANES))],
        out_specs=[rows(d), pl.BlockSpec((tm * TOKEN_ROWS, LANES), lambda i: (i, 0)), rows(LANES),
                   full((1, LANES))],
        out_shape=[jax.ShapeDtypeStruct((n, d), F32),
                   jax.ShapeDtypeStruct((n * TOKEN_ROWS, LANES), F32),
                   jax.ShapeDtypeStruct((n, LANES), F32),
                   jax.ShapeDtypeStruct((1, LANES), F32)],
        scratch_shapes=[pltpu.VMEM((tm, tm), BF16)],
        compiler_params=pltpu.CompilerParams(
            dimension_semantics=("arbitrary",), vmem_limit_bytes=VMEM_LIMIT),
        name="outproj_router_precise" if precise else "outproj_router",
    )(x, att, cy, w_out, g_ffn, w_r, b_r, counts_in)


def _token_rows(ref, t):
    return ref.at[pl.ds(pl.multiple_of(t * TOKEN_ROWS, TOKEN_ROWS), TOKEN_ROWS), :]


def _moe_dispatch_kernel(fill_ref, na_ref, d1_ref, d2_ref, hn_ref, *refs, tm, n_tiles, zero_fill):
    xs_ref, zbuf, sem = refs[-3:]
    step = pl.program_id(0)

    if zero_fill:
        @pl.when(step == 0)
        def _():
            zbuf[...] = jnp.zeros_like(zbuf)

            def fill(slot):
                return pltpu.make_async_copy(
                    zbuf, xs_ref.at[pl.ds(pl.multiple_of(slot * TOKEN_ROWS, TOKEN_ROWS),
                                          MOE_TILE * TOKEN_ROWS), :], sem.at[1])

            def start_fill(i, carry):
                fill(i * MOE_TILE).start()
                return carry

            def wait_fill(i, carry):
                fill(i * MOE_TILE).wait()
                return carry

            def group_fills(act):
                for e in range(N_EXPERTS):
                    @pl.when(fill_ref[e] >= 0)
                    def _():
                        act(fill(fill_ref[e]))

            group_fills(lambda c: c.start())
            lax.fori_loop(na_ref[0], n_tiles, start_fill, 0)
            group_fills(lambda c: c.wait())
            lax.fori_loop(na_ref[0], n_tiles, wait_fill, 0)

    def copies(t):
        src = _token_rows(hn_ref, t)
        return (pltpu.make_async_copy(src, _token_rows(xs_ref, d1_ref[0, t]), sem.at[0]),
                pltpu.make_async_copy(src, _token_rows(xs_ref, d2_ref[0, t]), sem.at[0]))

    def start(t, carry):
        for c in copies(t):
            c.start()
        return carry

    def wait(t, carry):
        for c in copies(t):
            c.wait()
        return carry

    lax.fori_loop(0, tm, start, 0, unroll=8)
    lax.fori_loop(0, tm, wait, 0, unroll=8)


def _moe_dispatch(fill_starts, n_active, dest1, dest2, hn_tiles, xs_prev, n_tiles, tm):
    n = dest1.size
    d1 = dest1.reshape(n // tm, 1, tm)
    d2 = dest2.reshape(n // tm, 1, tm)
    zero_fill = xs_prev is None
    smem_row = pl.BlockSpec((None, 1, tm), lambda i, fs, na: (i, 0, 0), memory_space=pltpu.SMEM)
    hbm = pl.BlockSpec(memory_space=pl.ANY)
    in_specs = [smem_row, smem_row,
                pl.BlockSpec((tm * TOKEN_ROWS, LANES), lambda i, fs, na: (i, 0))]
    args = [fill_starts, n_active, d1, d2, hn_tiles]
    aliases = {}
    if not zero_fill:
        in_specs.append(hbm)
        args.append(xs_prev)
        aliases = {len(args) - 1: 0}
    return pl.pallas_call(
        functools.partial(_moe_dispatch_kernel, tm=tm, n_tiles=n_tiles, zero_fill=zero_fill),
        grid_spec=pltpu.PrefetchScalarGridSpec(
            num_scalar_prefetch=2,
            grid=(n // tm,),
            in_specs=in_specs,
            out_specs=hbm,
            scratch_shapes=[pltpu.VMEM((MOE_TILE * TOKEN_ROWS, LANES), F32),
                            pltpu.SemaphoreType.DMA((2,))],
        ),
        out_shape=jax.ShapeDtypeStruct((n_tiles * MOE_TILE * TOKEN_ROWS, LANES), F32),
        input_output_aliases=aliases,
        compiler_params=pltpu.CompilerParams(
            dimension_semantics=("arbitrary",), vmem_limit_bytes=VMEM_LIMIT),
        name="moe_dispatch" if zero_fill else "moe_dispatch_more",
    )(*args)


def _moe_ffn_kernel(te_ref, na_ref, xs_ref, wg_ref, wu_ref, wd_ref, ys_ref, wg_b, wu_b, wd_b):
    i = pl.program_id(0)
    tg = MOE_TILE
    active = i < na_ref[0]

    @pl.when(active)
    def _():
        @pl.when((i == 0) | (te_ref[i] != te_ref[jnp.maximum(i - 1, 0)]))
        def _():
            wg_b[...] = wg_ref[...].astype(BF16)
            wu_b[...] = wu_ref[...].astype(BF16)
            wd_b[...] = wd_ref[...].astype(BF16)

        x = jnp.concatenate(
            [xs_ref[pl.ds(s, tg, stride=TOKEN_ROWS), :].astype(BF16) for s in range(TOKEN_ROWS)],
            axis=1)
        hg = jnp.dot(x, wg_b[...], preferred_element_type=F32)
        hu = jnp.dot(x, wu_b[...], preferred_element_type=F32)
        h = hg * (1.0 / (1.0 + jnp.exp(-hg))) * hu
        y = jnp.dot(h.astype(BF16), wd_b[...], preferred_element_type=F32)
        for s in range(TOKEN_ROWS):
            ys_ref[pl.ds(s, tg, stride=TOKEN_ROWS), :] = y[:, s * LANES:(s + 1) * LANES]

    @pl.when(jnp.logical_not(active))
    def _():
        ys_ref[...] = jnp.zeros_like(ys_ref)


def _moe_ffn(tile_expert, n_active, xs, w_gate, w_up, w_down):
    n_tiles = tile_expert.shape[0]
    tile_rows = MOE_TILE * TOKEN_ROWS
    d, ff = w_gate.shape[1:]
    last = lambda na: na[0] - 1
    w_spec = lambda shape: pl.BlockSpec(
        (None,) + shape, lambda i, te, na: (te[jnp.minimum(i, last(na))], 0, 0))
    return pl.pallas_call(
        _moe_ffn_kernel,
        grid_spec=pltpu.PrefetchScalarGridSpec(
            num_scalar_prefetch=2,
            grid=(n_tiles,),
            in_specs=[pl.BlockSpec((tile_rows, LANES),
                                   lambda i, te, na: (jnp.minimum(i, last(na)), 0)),
                      w_spec((d, ff)), w_spec((d, ff)), w_spec((ff, d))],
            out_specs=pl.BlockSpec((tile_rows, LANES), lambda i, te, na: (i, 0)),
            scratch_shapes=[pltpu.VMEM((d, ff), BF16), pltpu.VMEM((d, ff), BF16),
                            pltpu.VMEM((ff, d), BF16)],
        ),
        out_shape=jax.ShapeDtypeStruct(xs.shape, F32),
        compiler_params=pltpu.CompilerParams(
            dimension_semantics=("arbitrary",), vmem_limit_bytes=VMEM_LIMIT),
        name="moe_ffn",
    )(tile_expert, n_active, xs, w_gate, w_up, w_down)


def _moe_combine_kernel(d1_ref, d2_ref, route_ref, hres_ref, gf_ref, ys_ref, y_ref, buf, sem, *, tm):
    def copies(t):
        return (pltpu.make_async_copy(_token_rows(ys_ref, d1_ref[0, t]), _token_rows(buf.at[0], t),
                                      sem.at[0]),
                pltpu.make_async_copy(_token_rows(ys_ref, d2_ref[0, t]), _token_rows(buf.at[1], t),
                                      sem.at[1]))

    def start(t, carry):
        for c in copies(t):
            c.start()
        return carry

    def wait(t, carry):
        for c in copies(t):
            c.wait()
        return carry

    lax.fori_loop(0, tm, start, 0, unroll=8)
    lax.fori_loop(0, tm, wait, 0, unroll=8)

    route = route_ref[...]
    w1, w2 = route[:, 4:5], route[:, 5:6]
    for s in range(TOKEN_ROWS):
        cols = slice(s * LANES, (s + 1) * LANES)
        y_ref[:, cols] = (hres_ref[:, cols]
                          + w1 * buf[0, pl.ds(s, tm, stride=TOKEN_ROWS), :]
                          + w2 * buf[1, pl.ds(s, tm, stride=TOKEN_ROWS), :])
    y_ref[...] = _rms(y_ref[...], gf_ref[...])


def _moe_combine(dest1, dest2, route, hres, g_final, ys, tm):
    n, d = hres.shape
    d1 = dest1.reshape(n // tm, 1, tm)
    d2 = dest2.reshape(n // tm, 1, tm)
    smem_row = pl.BlockSpec((None, 1, tm), lambda i: (i, 0, 0), memory_space=pltpu.SMEM)
    rows = lambda width: pl.BlockSpec((tm, width), lambda i: (i, 0))
    return pl.pallas_call(
        functools.partial(_moe_combine_kernel, tm=tm),
        grid=(n // tm,),
        in_specs=[smem_row, smem_row, rows(LANES), rows(d), pl.BlockSpec((1, d), lambda i: (0, 0)),
                  pl.BlockSpec(memory_space=pl.ANY)],
        out_specs=rows(d),
        out_shape=jax.ShapeDtypeStruct((n, d), F32),
        scratch_shapes=[pltpu.VMEM((2, tm * TOKEN_ROWS, LANES), F32),
                        pltpu.SemaphoreType.DMA((2,))],
        compiler_params=pltpu.CompilerParams(
            dimension_semantics=("arbitrary",), vmem_limit_bytes=VMEM_LIMIT),
        name="moe_combine",
    )(d1, d2, route, hres, g_final, ys)


def _moe_plan(counts, n_tiles):
    counts = counts[0, :N_EXPERTS].astype(jnp.int32)
    padded = (counts + MOE_TILE - 1) // MOE_TILE * MOE_TILE
    ends = jnp.cumsum(padded)
    offsets = ends - padded
    n_active = (ends[-1:] // MOE_TILE).astype(jnp.int32)
    tile_start = jnp.arange(n_tiles, dtype=jnp.int32) * MOE_TILE
    tile_expert = jnp.minimum(jnp.sum(tile_start[:, None] >= ends[None, :], axis=1), N_EXPERTS - 1)
    fill_starts = jnp.where(padded > 0, ends - MOE_TILE, -1)
    return offsets, tile_expert.astype(jnp.int32), n_active, fill_starts.astype(jnp.int32)


def _moe_dests(route, offsets):
    def dest(e_col, r_col):
        onehot = route[:, e_col:e_col + 1].astype(jnp.int32) == jnp.arange(N_EXPERTS)[None, :]
        return jnp.sum(jnp.where(onehot, offsets[None, :], 0), axis=1) + route[:, r_col].astype(jnp.int32)
    return dest(0, 2), dest(1, 3)


def _sample_inproj_kernel(x_ref, g_ref, w_ref, cos_ref, slo_ref, shi_ref, cw_ref, p0_ref, p1_ref,
                          q_ref, k_ref, v_ref, cy_ref, u_ref):
    xn = _rms(x_ref[...], g_ref[...])
    reps = ATT_WIDTH // LANES
    cos = _tile_lanes(cos_ref[...], reps)
    slo = _tile_lanes(slo_ref[...], reps)
    shi = _tile_lanes(shi_ref[...], reps)

    def proj(i):
        return jnp.dot(xn, w_ref[:, i * ATT_WIDTH:(i + 1) * ATT_WIDTH], precision=HIGHEST,
                       preferred_element_type=F32)

    q_ref[...] = _rope_rows(proj(0), cos, slo, shi)
    k_ref[...] = _rope_rows(proj(1), cos, slo, shi)
    v_ref[...] = proj(2)
    b_gate = proj(3)
    u = proj(4) * proj(5)
    u_ref[...] = u
    cw = cw_ref[...]
    cy_ref[...] = b_gate * (cw[0:1, :] * p0_ref[...] + cw[1:2, :] * p1_ref[...] + cw[2:3, :] * u)


def _sample_inproj(x, g_mix, w_in, tables, conv_w, prev0, prev1):
    n = x.shape[0]
    out = jax.ShapeDtypeStruct((n, ATT_WIDTH), F32)
    return pl.pallas_call(
        _sample_inproj_kernel,
        out_shape=[out] * 5,
        compiler_params=pltpu.CompilerParams(vmem_limit_bytes=VMEM_LIMIT),
        name="sample_inproj",
    )(x, g_mix, w_in, *tables, conv_w, prev0, prev1)


PAGES_PER_STEP = 16


def _sample_select_kernel(pt_ref, *refs, page_size, n_blocks):
    q_ref = refs[0]
    page_refs = refs[1:1 + PAGES_PER_STEP]
    sel_ref, gates = refs[1 + PAGES_PER_STEP], refs[2 + PAGES_PER_STEP]
    g = pl.program_id(1)
    pages_per_block = MOBA_BLOCK // page_size
    blocks_per_step = PAGES_PER_STEP // pages_per_block
    lane = lax.broadcasted_iota(jnp.int32, (N_HEADS, LANES), 1)

    @pl.when(g == 0)
    def _():
        gates[...] = jnp.where(lane < n_blocks, 0.0, NEG_INF)

    qb = q_ref[...]
    acc = gates[...]
    for i in range(blocks_per_step):
        tot = page_refs[i * pages_per_block][...] * qb
        for p in range(1, pages_per_block):
            tot = tot + page_refs[i * pages_per_block + p][...] * qb
        per_head = jnp.sum(jnp.sum(tot, axis=1), axis=1, keepdims=True) * (1.0 / MOBA_BLOCK)
        acc = jnp.where(lane == g * blocks_per_step + i, per_head, acc)
    gates[...] = acc

    @pl.when(g == pl.num_programs(1) - 1)
    def _():
        out = jnp.zeros((N_HEADS, LANES), jnp.int32)
        for j in range(n_blocks):
            gj = acc[:, j:j + 1]
            beats = (acc > gj) | ((acc == gj) & (lane < j))
            cnt = jnp.sum(beats.astype(jnp.int32), axis=1, keepdims=True)
            out = jnp.where(lane == cnt, j, out)
        sel_ref[...] = out


def _sample_select(page_table, q_bcast, cache_kt):
    n_dec, n_pages = page_table.shape
    _, n_heads, head_dim, page_size = cache_kt.shape
    n_blocks = n_pages * page_size // MOBA_BLOCK

    def page_spec(i):
        return pl.BlockSpec((None, n_heads, head_dim, page_size),
                            lambda b, g, pt: (pt[b, g * PAGES_PER_STEP + i], 0, 0, 0))

    per_seq = lambda shape: pl.BlockSpec((None,) + shape, lambda b, g, pt: (b,) + (0,) * len(shape))
    return pl.pallas_call(
        functools.partial(_sample_select_kernel, page_size=page_size, n_blocks=n_blocks),
        grid_spec=pltpu.PrefetchScalarGridSpec(
            num_scalar_prefetch=1,
            grid=(n_dec, n_pages // PAGES_PER_STEP),
            in_specs=[per_seq((n_heads, head_dim, page_size))]
                     + [page_spec(i) for i in range(PAGES_PER_STEP)],
            out_specs=per_seq((n_heads, LANES)),
            scratch_shapes=[pltpu.VMEM((n_heads, LANES), F32)],
        ),
        out_shape=jax.ShapeDtypeStruct((n_dec, n_heads, LANES), jnp.int32),
        compiler_params=pltpu.CompilerParams(
            dimension_semantics=("arbitrary", "arbitrary"), vmem_limit_bytes=VMEM_LIMIT),
        name="sample_select",
    )(page_table, q_bcast, *([cache_kt] * PAGES_PER_STEP))


def _sample_attn_kernel(sel_ref, pt_ref, q_ref, kn_ref, vn_ref, ck_ref, cv_ref, o_ref,
                        kbuf, vbuf, sem, *, pages_per_block, page_size):
    b = pl.program_id(0)
    nb = pl.num_programs(0)

    def copies(bb, slot):
        out = []
        for h in range(N_HEADS):
            for r in range(MOBA_TOPK):
                block = sel_ref[(bb * N_HEADS + h) * MOBA_TOPK + r]
                for p in range(pages_per_block):
                    page = pt_ref[bb, block * pages_per_block + p]
                    dst = pl.ds((r * pages_per_block + p) * page_size, page_size)
                    out.append(pltpu.make_async_copy(ck_ref.at[page, h], kbuf.at[slot, h, :, dst],
                                                     sem.at[slot, 0]))
                    out.append(pltpu.make_async_copy(cv_ref.at[page, h], vbuf.at[slot, h, :, dst],
                                                     sem.at[slot, 1]))
        return out

    @pl.when(b == 0)
    def _():
        for c in copies(0, 0):
            c.start()

    @pl.when(b + 1 < nb)
    def _():
        for c in copies(b + 1, (b + 1) % 2):
            c.start()

    slot = b % 2
    for c in copies(b, slot):
        c.wait()

    q = q_ref[...]
    kn = kn_ref[...]
    vn = vn_ref[...]
    outs = []
    for h in range(N_HEADS):
        hs = slice(h * HEAD_DIM, (h + 1) * HEAD_DIM)
        qh = q[:, hs] * ATT_SCALE
        q8 = jnp.broadcast_to(qh, (SUBLANES, HEAD_DIM))
        s = jnp.dot(q8, kbuf[slot, h], precision=HIGHEST,
                    preferred_element_type=F32)
        s_self = jnp.sum(qh * kn[:, hs], axis=1, keepdims=True)
        m = jnp.maximum(jnp.max(s, axis=1, keepdims=True), s_self)
        p = jnp.exp(s - m)
        p_self = jnp.exp(s_self - m)
        l = jnp.sum(p, axis=1, keepdims=True) + p_self
        o = lax.dot_general(p, vbuf[slot, h], NT_DIMS, precision=HIGHEST,
                            preferred_element_type=F32)
        o = (o + p_self * vn[:, hs]) / l
        outs.append(o[0:1, :])
    o_ref[...] = jnp.concatenate(outs, axis=1)


def _sample_attn(sel_flat, page_table, q3, k3, v3, cache_kt, cache_vt):
    n_dec = q3.shape[0]
    page_size = cache_kt.shape[3]
    pages_per_block = MOBA_BLOCK // page_size
    n_keys = MOBA_TOPK * MOBA_BLOCK
    row = pl.BlockSpec((None, 1, ATT_WIDTH), lambda b, sel, pt: (b, 0, 0))
    hbm = pl.BlockSpec(memory_space=pl.ANY)
    return pl.pallas_call(
        functools.partial(_sample_attn_kernel, pages_per_block=pages_per_block,
                          page_size=page_size),
        grid_spec=pltpu.PrefetchScalarGridSpec(
            num_scalar_prefetch=2,
            grid=(n_dec,),
            in_specs=[row, row, row, hbm, hbm],
            out_specs=row,
            scratch_shapes=[pltpu.VMEM((2, N_HEADS, HEAD_DIM, n_keys), F32),
                            pltpu.VMEM((2, N_HEADS, HEAD_DIM, n_keys), F32),
                            pltpu.SemaphoreType.DMA((2, 2))],
        ),
        out_shape=jax.ShapeDtypeStruct((n_dec, 1, ATT_WIDTH), F32),
        compiler_params=pltpu.CompilerParams(
            dimension_semantics=("arbitrary",), vmem_limit_bytes=VMEM_LIMIT),
        name="sample_attn",
    )(sel_flat, page_table, q3, k3, v3, cache_kt, cache_vt)


def kernel(x_prompt, x_sample, cache_k, cache_v, state_conv, page_table, g_mix, w_in, conv_w, w_out,
           g_ffn, w_router_group, b_router_group, w_router_expert, b_router_expert, w_gate, w_up,
           w_down, g_final):
    depth = g_mix.shape[0]
    assert depth == 1
    bsz, seq, d = x_prompt.shape
    n_dec, dec_len, _ = x_sample.shape
    assert dec_len == 1
    page_size = cache_k.shape[2]
    past_len = page_table.shape[1] * page_size
    assert past_len % MOBA_BLOCK == 0 and MOBA_BLOCK % page_size == 0
    assert past_len // MOBA_BLOCK >= MOBA_TOPK and seq % MOBA_BLOCK == 0

    g_mix2 = g_mix[0][None, :]
    g_ffn2 = g_ffn[0][None, :]
    g_final2 = g_final[None, :]
    w_in_f, w_out_f = w_in[0], w_out[0]
    w_out_bf = w_out_f.astype(BF16)
    pad = LANES - N_EXPERTS - N_EXPERT_GROUPS
    w_r = jnp.concatenate([w_router_expert[0], w_router_group[0], jnp.zeros((d, pad), F32)], axis=1)
    b_r = jnp.concatenate([b_router_expert[0], b_router_group[0], jnp.zeros((pad,), F32)])[None, :]
    cw = conv_w[0]

    cos_p, sin_p = _rope_angles(jnp.arange(seq, dtype=jnp.int32))
    conv0 = jnp.zeros((bsz, CONV_K - 1, CONV_CH), F32)
    qt_p, kt_p, vt_p, kb_p, vtb_p, kmeans_p, cy_p, conv_p = _prompt_inproj(
        x_prompt, g_mix2, w_in_f, _rope_row_tables(cos_p, sin_p), (cos_p.T, sin_p.T), cw, conv0)
    att_p = _prompt_attn(qt_p, kb_p, vtb_p, kmeans_p)
    n_p = bsz * seq
    hres_p, hn_p, route_p, counts_p = _outproj_router(
        x_prompt.reshape(n_p, d), att_p.reshape(n_p, ATT_WIDTH), cy_p.reshape(n_p, CONV_CH),
        w_out_bf, g_ffn2, w_r, b_r, jnp.zeros((1, LANES), F32), tm=512, precise=False)

    cos_s, sin_s = _rope_angles(past_len + jnp.arange(dec_len, dtype=jnp.int32))
    x_s = x_sample.reshape(n_dec, d)
    prev0, prev1 = state_conv[0, :, 0, :], state_conv[0, :, 1, :]
    q_s, k_s, v_s, cy_s, u_s = _sample_inproj(x_s, g_mix2, w_in_f, _rope_row_tables(cos_s, sin_s),
                                              cw, prev0, prev1)
    cache_kt = jnp.transpose(cache_k[0], (0, 2, 3, 1))
    cache_vt = jnp.transpose(cache_v[0], (0, 2, 3, 1))
    q_bcast = jnp.broadcast_to(q_s.reshape(n_dec, N_HEADS, HEAD_DIM, 1),
                               (n_dec, N_HEADS, HEAD_DIM, page_size))
    sel = _sample_select(page_table, q_bcast, cache_kt)
    sel_flat = sel[:, :, :MOBA_TOPK].reshape(-1)
    q3 = q_s.reshape(n_dec, 1, ATT_WIDTH)
    att_s = _sample_attn(sel_flat, page_table, q3, k_s.reshape(n_dec, 1, ATT_WIDTH),
                         v_s.reshape(n_dec, 1, ATT_WIDTH), cache_kt, cache_vt)
    hres_s, hn_s, route_s, counts = _outproj_router(
        x_s, att_s.reshape(n_dec, ATT_WIDTH), cy_s, w_out_f, g_ffn2, w_r, b_r, counts_p,
        tm=n_dec, precise=True)

    n_assign = EXPERT_TOPK * (n_p + n_dec)
    n_tiles = -(-(n_assign + N_EXPERTS * (MOE_TILE - 1)) // MOE_TILE)
    offsets, tile_expert, n_active, fill_starts = _moe_plan(counts, n_tiles)
    d1_p, d2_p = _moe_dests(route_p, offsets)
    d1_s, d2_s = _moe_dests(route_s, offsets)
    xs = _moe_dispatch(fill_starts, n_active, d1_p, d2_p, hn_p, None, n_tiles, tm=512)
    xs = _moe_dispatch(fill_starts, n_active, d1_s, d2_s, hn_s, xs, n_tiles, tm=n_dec)
    ys = _moe_ffn(tile_expert, n_active, xs, w_gate[0], w_up[0], w_down[0])
    y_p = _moe_combine(d1_p, d2_p, route_p, hres_p, g_final2, ys, tm=256)
    y_s = _moe_combine(d1_s, d2_s, route_s, hres_s, g_final2, ys, tm=n_dec)

    conv_s = jnp.stack([prev1, u_s], axis=1)
    to_bthd = lambda t: jnp.transpose(t.reshape(bsz, N_HEADS, HEAD_DIM, seq), (0, 3, 1, 2))[None]
    return (y_p.reshape(bsz, seq, d), y_s.reshape(n_dec, dec_len, d),
            to_bthd(kt_p), to_bthd(vt_p), conv_p[None],
            k_s.reshape(1, n_dec, dec_len, N_HEADS, HEAD_DIM),
            v_s.reshape(1, n_dec, dec_len, N_HEADS, HEAD_DIM),
            conv_s[None])
```

```python
import functools

import jax
import jax.numpy as jnp
from jax import lax
from jax.experimental import pallas as pl
from jax.experimental.pallas import tpu as pltpu

F32 = jnp.float32
BF16 = jnp.bfloat16
HIGHEST = lax.Precision.HIGHEST

D_MODEL = 1024
HEAD_DIM = 64
N_HEADS = 8
ATT_WIDTH = N_HEADS * HEAD_DIM
CONV_CH = D_MODEL - ATT_WIDTH
ROPE_DIM = HEAD_DIM // 4
ROPE_HALF = ROPE_DIM // 2
ROPE_THETA = 500000.0
MOBA_BLOCK = 256
MOBA_TOPK = 3
ATT_SCALE = HEAD_DIM ** -0.5
CONV_K = 3
N_EXPERT_GROUPS = 4
EXPERTS_PER_GROUP = 8
N_EXPERTS = N_EXPERT_GROUPS * EXPERTS_PER_GROUP
EXPERT_TOPK = 2
EXPERT_FF = D_MODEL // 4
RMS_EPS = 1e-6

LANES = 128
SUBLANES = 8
HEADS_PER_VREG = LANES // HEAD_DIM
TOKEN_ROWS = D_MODEL // LANES
MOE_TILE = 256
DMA_QUEUES = 2
ISSUE_UNROLL = 8
VMEM_LIMIT = 56 * 1024 * 1024

NEG_INF = float("-inf")
NT_DIMS = (((1,), (1,)), ((), ()))


def _rms(x, g):
    ms = jnp.mean(x * x, axis=-1, keepdims=True)
    return x * lax.rsqrt(ms + RMS_EPS) * g


def _mm(a, w, precise):
    if precise:
        return jnp.dot(a, w, precision=HIGHEST, preferred_element_type=F32)
    return jnp.dot(a.astype(BF16), w, preferred_element_type=F32)


def _mm_f32(a, w, precise):
    if precise:
        return jnp.dot(a, w, precision=HIGHEST, preferred_element_type=F32)
    a_hi, w_hi = a.astype(BF16), w.astype(BF16)
    a_lo = (a - a_hi.astype(F32)).astype(BF16)
    w_lo = (w - w_hi.astype(F32)).astype(BF16)
    dot = functools.partial(jnp.dot, preferred_element_type=F32)
    return dot(a_hi, w_hi) + (dot(a_hi, w_lo) + dot(a_lo, w_hi))


def _rope_rows(a, cos, sin_lo, sin_hi):
    n = a.shape[-1]
    return (a * cos + pltpu.roll(a, n - ROPE_HALF, 1) * sin_lo
            + pltpu.roll(a, ROPE_HALF, 1) * sin_hi)


def _tile_lanes(t, reps):
    return jnp.concatenate([t] * reps, axis=-1)


def _rope_angles(pos):
    inv = ROPE_THETA ** (-jnp.arange(0, ROPE_DIM, 2, dtype=F32) / ROPE_DIM)
    ang = pos.astype(F32)[:, None] * inv[None, :]
    return jnp.cos(ang), jnp.sin(ang)


def _rope_row_tables(cos, sin):
    rows = cos.shape[0]
    ones = jnp.ones((rows, HEAD_DIM - ROPE_DIM), F32)
    zeros = jnp.zeros((rows, HEAD_DIM - ROPE_HALF), F32)
    c = jnp.concatenate([cos, cos, ones], axis=1)
    s_lo = jnp.concatenate([-sin, zeros], axis=1)
    s_hi = jnp.concatenate([jnp.zeros((rows, ROPE_HALF), F32), sin,
                            jnp.zeros((rows, HEAD_DIM - ROPE_DIM), F32)], axis=1)
    rep = lambda t: jnp.concatenate([t] * HEADS_PER_VREG, axis=1)
    return rep(c), rep(s_lo), rep(s_hi)


def _rope_cols(ref, cos_t, sin_t):
    for h in range(N_HEADS):
        r = h * HEAD_DIM
        x1 = ref[r:r + ROPE_HALF, :]
        x2 = ref[r + ROPE_HALF:r + ROPE_DIM, :]
        ref[r:r + ROPE_HALF, :] = x1 * cos_t - x2 * sin_t
        ref[r + ROPE_HALF:r + ROPE_DIM, :] = x2 * cos_t + x1 * sin_t


def _prompt_inproj_kernel(x_ref, g_ref, wqt_ref, wk_ref, wvt_ref, wc_ref, cos_ref, slo_ref, shi_ref,
                          cost_ref, sint_ref, cw_ref, prev_ref,
                          qt_ref, kt_ref, vt_ref, kb_ref, vtb_ref, km_ref, cy_ref, cn_ref,
                          ubuf, halo, km_acc, *, tt):
    t = pl.program_id(1)
    blk = MOBA_BLOCK
    xn = _rms(x_ref[...], g_ref[...]).astype(BF16)
    reps = ATT_WIDTH // LANES

    qt_ref[...] = lax.dot_general(wqt_ref[...], xn, NT_DIMS, preferred_element_type=F32)
    _rope_cols(qt_ref, cost_ref[...], sint_ref[...])
    vt = lax.dot_general(wvt_ref[...], xn, NT_DIMS, preferred_element_type=F32)
    vt_ref[...] = vt
    for i in range(tt // blk):
        vtb_ref[i] = vt[:, i * blk:(i + 1) * blk].astype(BF16)

    k = jnp.dot(xn, wk_ref[...], preferred_element_type=F32)
    k = _rope_rows(k, _tile_lanes(cos_ref[...], reps), _tile_lanes(slo_ref[...], reps),
                   _tile_lanes(shi_ref[...], reps))
    kb_ref[...] = k.astype(BF16)
    kt_ref[...] = k.T

    @pl.when(t == 0)
    def _():
        km_acc[...] = jnp.zeros_like(km_acc)

    rid = lax.broadcasted_iota(jnp.int32, km_acc.shape, 0)
    km = km_acc[...]
    for i in range(tt // blk):
        mean_i = jnp.mean(k[i * blk:(i + 1) * blk, :], axis=0, keepdims=True)
        km = jnp.where(rid == t * (tt // blk) + i, mean_i, km)
    km_acc[...] = km
    km_ref[...] = km

    def proj(i):
        return jnp.dot(xn, wc_ref[:, i * CONV_CH:(i + 1) * CONV_CH], preferred_element_type=F32)

    b_gate = proj(0)
    u = proj(1) * proj(2)

    @pl.when(t == 0)
    def _():
        ubuf[SUBLANES - (CONV_K - 1):SUBLANES, :] = prev_ref[...]

    @pl.when(t > 0)
    def _():
        ubuf[0:SUBLANES, :] = halo[...]

    ubuf[SUBLANES:SUBLANES + tt, :] = u
    cw = cw_ref[...]
    conv = (cw[0:1, :] * ubuf[SUBLANES - 2:SUBLANES - 2 + tt, :]
            + cw[1:2, :] * ubuf[SUBLANES - 1:SUBLANES - 1 + tt, :]
            + cw[2:3, :] * u)
    cy_ref[...] = (b_gate * conv).astype(cy_ref.dtype)
    halo[...] = ubuf[tt:tt + SUBLANES, :]
    cn_ref[...] = ubuf[tt + SUBLANES - (CONV_K - 1):tt + SUBLANES, :]


def _prompt_inproj(x, g_mix, w_in, row_tables, col_tables, conv_w, conv_prev, tt=512):
    bsz, seq, d = x.shape
    nt = seq // tt
    nb = seq // MOBA_BLOCK
    w_bf = w_in.astype(BF16)
    wqt = w_bf[:, :ATT_WIDTH].T
    wk = w_bf[:, ATT_WIDTH:2 * ATT_WIDTH]
    wvt = w_bf[:, 2 * ATT_WIDTH:3 * ATT_WIDTH].T
    wc = w_bf[:, 3 * ATT_WIDTH:]
    row_spec = lambda width: pl.BlockSpec((None, tt, width), lambda b, t: (b, t, 0))
    col_spec = pl.BlockSpec((None, ATT_WIDTH, tt), lambda b, t: (b, 0, t))
    tab_spec = pl.BlockSpec((tt, LANES), lambda b, t: (t, 0))
    tabt_spec = pl.BlockSpec((ROPE_HALF, tt), lambda b, t: (0, t))
    full = lambda shape: pl.BlockSpec(shape, lambda b, t: (0,) * len(shape))
    per_b = lambda shape: pl.BlockSpec((None,) + shape, lambda b, t: (b,) + (0,) * len(shape))
    vtb_spec = pl.BlockSpec((None, tt // MOBA_BLOCK, ATT_WIDTH, MOBA_BLOCK),
                            lambda b, t: (b, t, 0, 0))
    col_shape = jax.ShapeDtypeStruct((bsz, ATT_WIDTH, seq), F32)
    return pl.pallas_call(
        functools.partial(_prompt_inproj_kernel, tt=tt),
        grid=(bsz, nt),
        in_specs=[row_spec(d), full((1, d)), full(wqt.shape), full(wk.shape), full(wvt.shape),
                  full(wc.shape), tab_spec, tab_spec, tab_spec, tabt_spec, tabt_spec,
                  full((CONV_K, CONV_CH)), per_b((CONV_K - 1, CONV_CH))],
        out_specs=[col_spec, col_spec, col_spec, row_spec(ATT_WIDTH), vtb_spec,
                   per_b((nb, ATT_WIDTH)), row_spec(CONV_CH), per_b((CONV_K - 1, CONV_CH))],
        out_shape=[col_shape, col_shape, col_shape,
                   jax.ShapeDtypeStruct((bsz, seq, ATT_WIDTH), BF16),
                   jax.ShapeDtypeStruct((bsz, nb, ATT_WIDTH, MOBA_BLOCK), BF16),
                   jax.ShapeDtypeStruct((bsz, nb, ATT_WIDTH), F32),
                   jax.ShapeDtypeStruct((bsz, seq, CONV_CH), BF16),
                   jax.ShapeDtypeStruct((bsz, CONV_K - 1, CONV_CH), F32)],
        scratch_shapes=[pltpu.VMEM((tt + SUBLANES, CONV_CH), F32),
                        pltpu.VMEM((SUBLANES, CONV_CH), F32),
                        pltpu.VMEM((nb, ATT_WIDTH), F32)],
        compiler_params=pltpu.CompilerParams(
            dimension_semantics=("arbitrary", "arbitrary"), vmem_limit_bytes=VMEM_LIMIT),
        name="prompt_inproj",
    )(x, g_mix, wqt, wk, wvt, wc, *row_tables, *col_tables, conv_w, conv_prev)


def _prompt_attn_kernel(qt_ref, kb_ref, vtb_ref, km_ref, o_ref, s_ref, ot_ref, *, nb):
    blk = MOBA_BLOCK
    means = km_ref[...]
    feat = lax.broadcasted_iota(jnp.int32, (LANES, 1), 0)
    bid = lax.broadcasted_iota(jnp.int32, (nb, blk), 0)
    key = lax.broadcasted_iota(jnp.int32, (blk, blk), 0)
    qry = lax.broadcasted_iota(jnp.int32, (blk, blk), 1)

    def block_bias(qth, qi):
        if qi <= MOBA_TOPK:
            return [None] * qi
        gates = jnp.dot(means, qth, precision=HIGHEST, preferred_element_type=F32)
        gates = jnp.where(bid < qi, gates, NEG_INF)
        rows = []
        for j in range(qi):
            gj = gates[j:j + 1, :]
            beats = (gates > gj) | ((gates == gj) & (bid < j))
            cnt = jnp.sum(beats.astype(F32), axis=0, keepdims=True)
            rows.append(jnp.where(cnt < MOBA_TOPK, 0.0, NEG_INF))
        return rows

    def scores_pass(qi, h, slot, state):
        qt = qt_ref[:, qi * blk:(qi + 1) * blk]
        qth = jnp.where((feat >= h * HEAD_DIM) & (feat < (h + 1) * HEAD_DIM), qt, 0.0)
        bias = block_bias(qth, qi)
        qs = (qth * ATT_SCALE).astype(BF16)
        m = None
        for j in range(qi + 1):
            s = jnp.dot(kb_ref[j * blk:(j + 1) * blk, :], qs,
                        preferred_element_type=F32)
            if j == qi:
                s = jnp.where(key <= qry, s, NEG_INF)
            elif bias[j] is not None:
                s = s + bias[j]
            s_ref[slot, j] = s
            m_blk = jnp.max(s, axis=0, keepdims=True)
            m = m_blk if m is None else jnp.maximum(m, m_blk)
            yield
        state["m"] = m

    def values_pass(qi, h, slot, state):
        m = state["m"]
        l = acc = None
        for j in range(qi + 1):
            p = jnp.exp(s_ref[slot, j] - m)
            l_blk = jnp.sum(p, axis=0, keepdims=True)
            pv = jnp.dot(vtb_ref[j, h * HEAD_DIM:(h + 1) * HEAD_DIM, :], p.astype(BF16),
                         preferred_element_type=F32)
            l, acc = (l_blk, pv) if l is None else (l + l_blk, acc + pv)
            yield
        ot_ref[h * HEAD_DIM:(h + 1) * HEAD_DIM, qi * blk:(qi + 1) * blk] = acc / l

    def run_interleaved(*gens):
        live = [g for g in gens if g is not None]
        while live:
            for g in list(live):
                if next(g, "done") == "done":
                    live.remove(g)

    pending = None
    for i, (qi, h) in enumerate((qi, h) for qi in range(nb) for h in range(HEADS_PER_VREG)):
        state = {}
        run_interleaved(scores_pass(qi, h, i % 2, state), pending)
        pending = values_pass(qi, h, i % 2, state)
    run_interleaved(pending)
    o_ref[...] = ot_ref[...].T.astype(o_ref.dtype)


def _prompt_attn(qt, kb, vtb, kmeans):
    bsz, seq, _ = kb.shape
    nb = seq // MOBA_BLOCK
    n_pairs = ATT_WIDTH // LANES
    return pl.pallas_call(
        functools.partial(_prompt_attn_kernel, nb=nb),
        grid=(bsz, n_pairs),
        in_specs=[pl.BlockSpec((None, LANES, seq), lambda b, hp: (b, hp, 0)),
                  pl.BlockSpec((None, seq, LANES), lambda b, hp: (b, 0, hp)),
                  pl.BlockSpec((None, nb, LANES, MOBA_BLOCK), lambda b, hp: (b, 0, hp, 0)),
                  pl.BlockSpec((None, nb, LANES), lambda b, hp: (b, 0, hp))],
        out_specs=pl.BlockSpec((None, seq, LANES), lambda b, hp: (b, 0, hp)),
        out_shape=jax.ShapeDtypeStruct((bsz, seq, ATT_WIDTH), BF16),
        scratch_shapes=[pltpu.VMEM((2, nb, MOBA_BLOCK, MOBA_BLOCK), F32),
                        pltpu.VMEM((LANES, seq), F32)],
        compiler_params=pltpu.CompilerParams(
            dimension_semantics=("arbitrary", "arbitrary"), vmem_limit_bytes=VMEM_LIMIT),
        name="prompt_attn",
    )(qt, kb, vtb, kmeans)


def _outproj_router_kernel(x_ref, att_ref, cy_ref, wo_ref, g_ref, wr_ref, br_ref, cnt_in_ref,
                           hres_ref, hn_ref, route_ref, cnt_ref, tri_ref, *, precise):
    step = pl.program_id(0)
    tm = x_ref.shape[0]
    hres = (x_ref[...] + _mm(att_ref[...], wo_ref[0:ATT_WIDTH, :], precise)
            + _mm(cy_ref[...], wo_ref[ATT_WIDTH:, :], precise))
    hres_ref[...] = hres
    hn = _rms(hres, g_ref[...])
    for s in range(TOKEN_ROWS):
        hn_ref[pl.ds(s, tm, stride=TOKEN_ROWS), :] = hn[:, s * LANES:(s + 1) * LANES]

    @pl.when(step == 0)
    def _():
        cnt_ref[...] = cnt_in_ref[...]
        r = lax.broadcasted_iota(jnp.int32, (tm, tm), 0)
        c = lax.broadcasted_iota(jnp.int32, (tm, tm), 1)
        tri_ref[...] = jnp.where(c < r, 1.0, 0.0).astype(BF16)

    lo = _mm_f32(hn, wr_ref[...], precise) + br_ref[...]
    lane = lax.broadcasted_iota(jnp.int32, lo.shape, 1)
    is_group = (lane >= N_EXPERTS) & (lane < N_EXPERTS + N_EXPERT_GROUPS)
    lg = jnp.where(is_group, lo, NEG_INF)
    mg = jnp.max(lg, axis=1, keepdims=True)
    g_lane = jnp.min(jnp.where(lg == mg, lane, LANES), axis=1, keepdims=True)
    pg = 1.0 / jnp.sum(jnp.exp(lg - mg), axis=1, keepdims=True)
    e_lo = (g_lane - N_EXPERTS) * EXPERTS_PER_GROUP
    in_g = (lane >= e_lo) & (lane < e_lo + EXPERTS_PER_GROUP)
    le = jnp.where(in_g, lo, NEG_INF)
    m1 = jnp.max(le, axis=1, keepdims=True)
    i1 = jnp.min(jnp.where(le == m1, lane, LANES), axis=1, keepdims=True)
    le2 = jnp.where(lane == i1, NEG_INF, le)
    m2 = jnp.max(le2, axis=1, keepdims=True)
    i2 = jnp.min(jnp.where(le2 == m2, lane, LANES), axis=1, keepdims=True)
    e2 = jnp.exp(m2 - m1)
    w1 = pg / (1.0 + e2)
    w2 = pg * e2 / (1.0 + e2)

    oh1, oh2 = lane == i1, lane == i2
    assigned = jnp.where(oh1 | oh2, 1.0, 0.0)
    before = cnt_ref[...] + jnp.dot(tri_ref[...], assigned.astype(BF16),
                                    preferred_element_type=F32)
    r1 = jnp.sum(jnp.where(oh1, before, 0.0), axis=1, keepdims=True)
    r2 = jnp.sum(jnp.where(oh2, before, 0.0), axis=1, keepdims=True)
    cnt_ref[...] += jnp.sum(assigned, axis=0, keepdims=True)
    cols = (i1.astype(F32), i2.astype(F32), r1, r2, w1, w2)
    route = jnp.zeros(lo.shape, F32)
    for c, v in enumerate(cols):
        route = jnp.where(lane == c, v, route)
    route_ref[...] = route


def _outproj_router(x, att, cy, w_out, g_ffn, w_r, b_r, counts_in, tm, precise):
    n, d = x.shape
    rows = lambda width: pl.BlockSpec((tm, width), lambda i: (i, 0))
    full = lambda shape: pl.BlockSpec(shape, lambda i: (0,) * len(shape))
    return pl.pallas_call(
        functools.partial(_outproj_router_kernel, precise=precise),
        grid=(n // tm,),
        in_specs=[rows(d), rows(ATT_WIDTH), rows(CONV_CH), full(w_out.shape), full((1, d)),
                  full(w_r.shape), full((1, LANES)), full((1, LANES))],
        out_specs=[rows(d), pl.BlockSpec((tm * TOKEN_ROWS, LANES), lambda i: (i, 0)), rows(LANES),
                   full((1, LANES))],
        out_shape=[jax.ShapeDtypeStruct((n, d), F32),
                   jax.ShapeDtypeStruct((n * TOKEN_ROWS, LANES), F32),
                   jax.ShapeDtypeStruct((n, LANES), F32),
                   jax.ShapeDtypeStruct((1, LANES), F32)],
        scratch_shapes=[pltpu.VMEM((tm, tm), BF16)],
        compiler_params=pltpu.CompilerParams(
            dimension_semantics=("arbitrary",), vmem_limit_bytes=VMEM_LIMIT),
        name="outproj_router_precise" if precise else "outproj_router",
    )(x, att, cy, w_out, g_ffn, w_r, b_r, counts_in)


def _token_rows(ref, t):
    return ref.at[pl.ds(pl.multiple_of(t * TOKEN_ROWS, TOKEN_ROWS), TOKEN_ROWS), :]


def _start_token_copies(copies, n):
    def start(g, carry):
        for u in range(ISSUE_UNROLL):
            for c in copies(g * ISSUE_UNROLL + u):
                c.start(priority=u % DMA_QUEUES)
        return carry

    lax.fori_loop(0, n // ISSUE_UNROLL, start, 0)


def _moe_dispatch_kernel(fill_ref, na_ref, d1_ref, d2_ref, hn_ref, *refs, tm, n_tiles, zero_fill):
    xs_ref, zbuf, sem = refs[-3:]
    step = pl.program_id(0)

    if zero_fill:
        @pl.when(step == 0)
        def _():
            zbuf[...] = jnp.zeros_like(zbuf)

            def fill(slot):
                return pltpu.make_async_copy(
                    zbuf, xs_ref.at[pl.ds(pl.multiple_of(slot * TOKEN_ROWS, TOKEN_ROWS),
                                          MOE_TILE * TOKEN_ROWS), :], sem.at[1])

            def start_fill(i, carry):
                fill(i * MOE_TILE).start()
                return carry

            def wait_fill(i, carry):
                fill(i * MOE_TILE).wait()
                return carry

            def group_fills(act):
                for e in range(N_EXPERTS):
                    @pl.when(fill_ref[e] >= 0)
                    def _():
                        act(fill(fill_ref[e]))

            group_fills(lambda c: c.start())
            lax.fori_loop(na_ref[0], n_tiles, start_fill, 0)
            group_fills(lambda c: c.wait())
            lax.fori_loop(na_ref[0], n_tiles, wait_fill, 0)

    def copies(t):
        src = _token_rows(hn_ref, t)
        return (pltpu.make_async_copy(src, _token_rows(xs_ref, d1_ref[0, t]), sem.at[0]),
                pltpu.make_async_copy(src, _token_rows(xs_ref, d2_ref[0, t]), sem.at[0]))

    _start_token_copies(copies, tm)
    for _ in range(EXPERT_TOPK):
        pltpu.make_async_copy(hn_ref, xs_ref.at[pl.ds(0, tm * TOKEN_ROWS), :], sem.at[0]).wait()


def _moe_dispatch(fill_starts, n_active, dest1, dest2, hn_tiles, xs_prev, n_tiles, tm):
    n = dest1.size
    d1 = dest1.reshape(n // tm, 1, tm)
    d2 = dest2.reshape(n // tm, 1, tm)
    zero_fill = xs_prev is None
    smem_row = pl.BlockSpec((None, 1, tm), lambda i, fs, na: (i, 0, 0), memory_space=pltpu.SMEM)
    hbm = pl.BlockSpec(memory_space=pl.ANY)
    in_specs = [smem_row, smem_row,
                pl.BlockSpec((tm * TOKEN_ROWS, LANES), lambda i, fs, na: (i, 0))]
    args = [fill_starts, n_active, d1, d2, hn_tiles]
    aliases = {}
    if not zero_fill:
        in_specs.append(hbm)
        args.append(xs_prev)
        aliases = {len(args) - 1: 0}
    return pl.pallas_call(
        functools.partial(_moe_dispatch_kernel, tm=tm, n_tiles=n_tiles, zero_fill=zero_fill),
        grid_spec=pltpu.PrefetchScalarGridSpec(
            num_scalar_prefetch=2,
            grid=(n // tm,),
            in_specs=in_specs,
            out_specs=hbm,
            scratch_shapes=[pltpu.VMEM((MOE_TILE * TOKEN_ROWS, LANES), F32),
                            pltpu.SemaphoreType.DMA((2,))],
        ),
        out_shape=jax.ShapeDtypeStruct((n_tiles * MOE_TILE * TOKEN_ROWS, LANES), F32),
        input_output_aliases=aliases,
        compiler_params=pltpu.CompilerParams(
            dimension_semantics=("arbitrary",), vmem_limit_bytes=VMEM_LIMIT),
        name="moe_dispatch" if zero_fill else "moe_dispatch_more",
    )(*args)


def _moe_ffn_kernel(te_ref, na_ref, xs_ref, wg_ref, wu_ref, wd_ref, ys_ref, wg_b, wu_b, wd_b):
    i = pl.program_id(0)
    tg = MOE_TILE
    active = i < na_ref[0]

    @pl.when(active)
    def _():
        @pl.when((i == 0) | (te_ref[i] != te_ref[jnp.maximum(i - 1, 0)]))
        def _():
            wg_b[...] = wg_ref[...].astype(BF16)
            wu_b[...] = wu_ref[...].astype(BF16)
            wd_b[...] = wd_ref[...].astype(BF16)

        x = jnp.concatenate(
            [xs_ref[pl.ds(s, tg, stride=TOKEN_ROWS), :].astype(BF16) for s in range(TOKEN_ROWS)],
            axis=1)
        hg = jnp.dot(x, wg_b[...], preferred_element_type=F32)
        hu = jnp.dot(x, wu_b[...], preferred_element_type=F32)
        h = hg * (1.0 / (1.0 + jnp.exp(-hg))) * hu
        y = jnp.dot(h.astype(BF16), wd_b[...], preferred_element_type=F32)
        for s in range(TOKEN_ROWS):
            ys_ref[pl.ds(s, tg, stride=TOKEN_ROWS), :] = y[:, s * LANES:(s + 1) * LANES]

    @pl.when(jnp.logical_not(active))
    def _():
        ys_ref[...] = jnp.zeros_like(ys_ref)


def _moe_ffn(tile_expert, n_active, xs, w_gate, w_up, w_down):
    n_tiles = tile_expert.shape[0]
    tile_rows = MOE_TILE * TOKEN_ROWS
    d, ff = w_gate.shape[1:]
    last = lambda na: na[0] - 1
    w_spec = lambda shape: pl.BlockSpec(
        (None,) + shape, lambda i, te, na: (te[jnp.minimum(i, last(na))], 0, 0))
    return pl.pallas_call(
        _moe_ffn_kernel,
        grid_spec=pltpu.PrefetchScalarGridSpec(
            num_scalar_prefetch=2,
            grid=(n_tiles,),
            in_specs=[pl.BlockSpec((tile_rows, LANES),
                                   lambda i, te, na: (jnp.minimum(i, last(na)), 0)),
                      w_spec((d, ff)), w_spec((d, ff)), w_spec((ff, d))],
            out_specs=pl.BlockSpec((tile_rows, LANES), lambda i, te, na: (i, 0)),
            scratch_shapes=[pltpu.VMEM((d, ff), BF16), pltpu.VMEM((d, ff), BF16),
                            pltpu.VMEM((ff, d), BF16)],
        ),
        out_shape=jax.ShapeDtypeStruct(xs.shape, F32),
        compiler_params=pltpu.CompilerParams(
            dimension_semantics=("arbitrary",), vmem_limit_bytes=VMEM_LIMIT),
        name="moe_ffn",
    )(tile_expert, n_active, xs, w_gate, w_up, w_down)


def _moe_combine_kernel(d1_ref, d2_ref, route_ref, hres_ref, gf_ref, ys_ref, y_ref, buf, sem, *, tm):
    def copies(t):
        return (pltpu.make_async_copy(_token_rows(ys_ref, d1_ref[0, t]), _token_rows(buf.at[0], t),
                                      sem.at[0]),
                pltpu.make_async_copy(_token_rows(ys_ref, d2_ref[0, t]), _token_rows(buf.at[1], t),
                                      sem.at[1]))

    _start_token_copies(copies, tm)
    for k in range(EXPERT_TOPK):
        pltpu.make_async_copy(ys_ref.at[pl.ds(0, tm * TOKEN_ROWS), :], buf.at[k], sem.at[k]).wait()

    route = route_ref[...]
    w1, w2 = route[:, 4:5], route[:, 5:6]
    for s in range(TOKEN_ROWS):
        cols = slice(s * LANES, (s + 1) * LANES)
        y_ref[:, cols] = (hres_ref[:, cols]
                          + w1 * buf[0, pl.ds(s, tm, stride=TOKEN_ROWS), :]
                          + w2 * buf[1, pl.ds(s, tm, stride=TOKEN_ROWS), :])
    y_ref[...] = _rms(y_ref[...], gf_ref[...])


def _moe_combine(dest1, dest2, route, hres, g_final, ys, tm):
    n, d = hres.shape
    d1 = dest1.reshape(n // tm, 1, tm)
    d2 = dest2.reshape(n // tm, 1, tm)
    smem_row = pl.BlockSpec((None, 1, tm), lambda i: (i, 0, 0), memory_space=pltpu.SMEM)
    rows = lambda width: pl.BlockSpec((tm, width), lambda i: (i, 0))
    return pl.pallas_call(
        functools.partial(_moe_combine_kernel, tm=tm),
        grid=(n // tm,),
        in_specs=[smem_row, smem_row, rows(LANES), rows(d), pl.BlockSpec((1, d), lambda i: (0, 0)),
                  pl.BlockSpec(memory_space=pl.ANY)],
        out_specs=rows(d),
        out_shape=jax.ShapeDtypeStruct((n, d), F32),
        scratch_shapes=[pltpu.VMEM((2, tm * TOKEN_ROWS, LANES), F32),
                        pltpu.SemaphoreType.DMA((2,))],
        compiler_params=pltpu.CompilerParams(
            dimension_semantics=("arbitrary",), vmem_limit_bytes=VMEM_LIMIT),
        name="moe_combine",
    )(d1, d2, route, hres, g_final, ys)


def _moe_plan(counts, n_tiles):
    counts = counts[0, :N_EXPERTS].astype(jnp.int32)
    padded = (counts + MOE_TILE - 1) // MOE_TILE * MOE_TILE
    ends = jnp.cumsum(padded)
    offsets = ends - padded
    n_active = (ends[-1:] // MOE_TILE).astype(jnp.int32)
    tile_start = jnp.arange(n_tiles, dtype=jnp.int32) * MOE_TILE
    tile_expert = jnp.minimum(jnp.sum(tile_start[:, None] >= ends[None, :], axis=1), N_EXPERTS - 1)
    fill_starts = jnp.where(padded > 0, ends - MOE_TILE, -1)
    return offsets, tile_expert.astype(jnp.int32), n_active, fill_starts.astype(jnp.int32)


def _moe_dests(route, offsets):
    def dest(e_col, r_col):
        onehot = route[:, e_col:e_col + 1].astype(jnp.int32) == jnp.arange(N_EXPERTS)[None, :]
        return jnp.sum(jnp.where(onehot, offsets[None, :], 0), axis=1) + route[:, r_col].astype(jnp.int32)
    return dest(0, 2), dest(1, 3)


def _sample_inproj_kernel(x_ref, g_ref, w_ref, cos_ref, slo_ref, shi_ref, cw_ref, p0_ref, p1_ref,
                          q_ref, k_ref, v_ref, cy_ref, u_ref):
    xn = _rms(x_ref[...], g_ref[...])
    reps = ATT_WIDTH // LANES
    cos = _tile_lanes(cos_ref[...], reps)
    slo = _tile_lanes(slo_ref[...], reps)
    shi = _tile_lanes(shi_ref[...], reps)

    def proj(i):
        return jnp.dot(xn, w_ref[:, i * ATT_WIDTH:(i + 1) * ATT_WIDTH], precision=HIGHEST,
                       preferred_element_type=F32)

    q_ref[...] = _rope_rows(proj(0), cos, slo, shi)
    k_ref[...] = _rope_rows(proj(1), cos, slo, shi)
    v_ref[...] = proj(2)
    b_gate = proj(3)
    u = proj(4) * proj(5)
    u_ref[...] = u
    cw = cw_ref[...]
    cy_ref[...] = b_gate * (cw[0:1, :] * p0_ref[...] + cw[1:2, :] * p1_ref[...] + cw[2:3, :] * u)


def _sample_inproj(x, g_mix, w_in, tables, conv_w, prev0, prev1):
    n = x.shape[0]
    out = jax.ShapeDtypeStruct((n, ATT_WIDTH), F32)
    return pl.pallas_call(
        _sample_inproj_kernel,
        out_shape=[out] * 5,
        compiler_params=pltpu.CompilerParams(vmem_limit_bytes=VMEM_LIMIT),
        name="sample_inproj",
    )(x, g_mix, w_in, *tables, conv_w, prev0, prev1)


PAGES_PER_STEP = 16


def _sample_select_kernel(pt_ref, *refs, page_size, n_blocks):
    q_ref = refs[0]
    page_refs = refs[1:1 + PAGES_PER_STEP]
    sel_ref, part = refs[1 + PAGES_PER_STEP], refs[2 + PAGES_PER_STEP]
    g = pl.program_id(1)
    pages_per_block = MOBA_BLOCK // page_size
    blocks_per_step = PAGES_PER_STEP // pages_per_block

    qb = q_ref[...]
    for i in range(blocks_per_step):
        tot = page_refs[i * pages_per_block][...]
        for p in range(1, pages_per_block):
            tot = tot + page_refs[i * pages_per_block + p][...]
        part[g * blocks_per_step + i] = jnp.sum(tot * qb, axis=1)

    @pl.when(g == pl.num_programs(1) - 1)
    def _():
        lane = lax.broadcasted_iota(jnp.int32, (N_HEADS, LANES), 1)
        acc = jnp.full((N_HEADS, LANES), NEG_INF, F32)
        for j in range(n_blocks):
            gate_j = jnp.sum(part[j], axis=1, keepdims=True) * (1.0 / MOBA_BLOCK)
            acc = jnp.where(lane == j, gate_j, acc)
        out = jnp.zeros((N_HEADS, LANES), jnp.int32)
        for j in range(n_blocks):
            gj = acc[:, j:j + 1]
            beats = (acc > gj) | ((acc == gj) & (lane < j))
            cnt = jnp.sum(beats.astype(jnp.int32), axis=1, keepdims=True)
            out = jnp.where(lane == cnt, j, out)
        sel_ref[...] = out


def _sample_select(page_table, q_bcast, cache_kt):
    n_dec, n_pages = page_table.shape
    _, n_heads, head_dim, page_size = cache_kt.shape
    n_blocks = n_pages * page_size // MOBA_BLOCK

    def page_spec(i):
        return pl.BlockSpec((None, n_heads, head_dim, page_size),
                            lambda b, g, pt: (pt[b, g * PAGES_PER_STEP + i], 0, 0, 0))

    per_seq = lambda shape: pl.BlockSpec((None,) + shape, lambda b, g, pt: (b,) + (0,) * len(shape))
    return pl.pallas_call(
        functools.partial(_sample_select_kernel, page_size=page_size, n_blocks=n_blocks),
        grid_spec=pltpu.PrefetchScalarGridSpec(
            num_scalar_prefetch=1,
            grid=(n_dec, n_pages // PAGES_PER_STEP),
            in_specs=[per_seq((n_heads, head_dim, page_size))]
                     + [page_spec(i) for i in range(PAGES_PER_STEP)],
            out_specs=per_seq((n_heads, LANES)),
            scratch_shapes=[pltpu.VMEM((n_blocks, n_heads, LANES), F32)],
        ),
        out_shape=jax.ShapeDtypeStruct((n_dec, n_heads, LANES), jnp.int32),
        compiler_params=pltpu.CompilerParams(
            dimension_semantics=("arbitrary", "arbitrary"), vmem_limit_bytes=VMEM_LIMIT),
        name="sample_select",
    )(page_table, q_bcast, *([cache_kt] * PAGES_PER_STEP))


def _sample_attn_kernel(sel_ref, pt_ref, q_ref, kn_ref, vn_ref, ck_ref, cv_ref, o_ref,
                        kbuf, vbuf, sem, *, pages_per_block, page_size):
    b = pl.program_id(0)
    nb = pl.num_programs(0)

    def copies(bb, slot):
        out = []
        for h in range(N_HEADS):
            for r in range(MOBA_TOPK):
                block = sel_ref[(bb * N_HEADS + h) * MOBA_TOPK + r]
                for p in range(pages_per_block):
                    page = pt_ref[bb, block * pages_per_block + p]
                    dst = pl.ds((r * pages_per_block + p) * page_size, page_size)
                    out.append(pltpu.make_async_copy(ck_ref.at[page, h], kbuf.at[slot, h, :, dst],
                                                     sem.at[slot, 0]))
                    out.append(pltpu.make_async_copy(cv_ref.at[page, h], vbuf.at[slot, h, :, dst],
                                                     sem.at[slot, 1]))
        return out

    @pl.when(b == 0)
    def _():
        for c in copies(0, 0):
            c.start()

    @pl.when(b + 1 < nb)
    def _():
        for c in copies(b + 1, (b + 1) % 2):
            c.start()

    slot = b % 2
    for c in copies(b, slot):
        c.wait()

    q = q_ref[...]
    kn = kn_ref[...]
    vn = vn_ref[...]
    outs = []
    for h in range(N_HEADS):
        hs = slice(h * HEAD_DIM, (h + 1) * HEAD_DIM)
        qh = q[:, hs] * ATT_SCALE
        q8 = jnp.broadcast_to(qh, (SUBLANES, HEAD_DIM))
        s = jnp.dot(q8, kbuf[slot, h], precision=HIGHEST,
                    preferred_element_type=F32)
        s_self = jnp.sum(qh * kn[:, hs], axis=1, keepdims=True)
        m = jnp.maximum(jnp.max(s, axis=1, keepdims=True), s_self)
        p = jnp.exp(s - m)
        p_self = jnp.exp(s_self - m)
        l = jnp.sum(p, axis=1, keepdims=True) + p_self
        o = lax.dot_general(p, vbuf[slot, h], NT_DIMS, precision=HIGHEST,
                            preferred_element_type=F32)
        o = (o + p_self * vn[:, hs]) / l
        outs.append(o[0:1, :])
    o_ref[...] = jnp.concatenate(outs, axis=1)


def _sample_attn(sel_flat, page_table, q3, k3, v3, cache_kt, cache_vt):
    n_dec = q3.shape[0]
    page_size = cache_kt.shape[3]
    pages_per_block = MOBA_BLOCK // page_size
    n_keys = MOBA_TOPK * MOBA_BLOCK
    row = pl.BlockSpec((None, 1, ATT_WIDTH), lambda b, sel, pt: (b, 0, 0))
    hbm = pl.BlockSpec(memory_space=pl.ANY)
    return pl.pallas_call(
        functools.partial(_sample_attn_kernel, pages_per_block=pages_per_block,
                          page_size=page_size),
        grid_spec=pltpu.PrefetchScalarGridSpec(
            num_scalar_prefetch=2,
            grid=(n_dec,),
            in_specs=[row, row, row, hbm, hbm],
            out_specs=row,
            scratch_shapes=[pltpu.VMEM((2, N_HEADS, HEAD_DIM, n_keys), F32),
                            pltpu.VMEM((2, N_HEADS, HEAD_DIM, n_keys), F32),
                            pltpu.SemaphoreType.DMA((2, 2))],
        ),
        out_shape=jax.ShapeDtypeStruct((n_dec, 1, ATT_WIDTH), F32),
        compiler_params=pltpu.CompilerParams(
            dimension_semantics=("arbitrary",), vmem_limit_bytes=VMEM_LIMIT),
        name="sample_attn",
    )(sel_flat, page_table, q3, k3, v3, cache_kt, cache_vt)


def kernel(x_prompt, x_sample, cache_k, cache_v, state_conv, page_table, g_mix, w_in, conv_w, w_out,
           g_ffn, w_router_group, b_router_group, w_router_expert, b_router_expert, w_gate, w_up,
           w_down, g_final):
    depth = g_mix.shape[0]
    assert depth == 1
    bsz, seq, d = x_prompt.shape
    n_dec, dec_len, _ = x_sample.shape
    assert dec_len == 1
    page_size = cache_k.shape[2]
    past_len = page_table.shape[1] * page_size
    assert past_len % MOBA_BLOCK == 0 and MOBA_BLOCK % page_size == 0
    assert past_len // MOBA_BLOCK >= MOBA_TOPK and seq % MOBA_BLOCK == 0

    g_mix2 = g_mix[0][None, :]
    g_ffn2 = g_ffn[0][None, :]
    g_final2 = g_final[None, :]
    w_in_f, w_out_f = w_in[0], w_out[0]
    w_out_bf = w_out_f.astype(BF16)
    pad = LANES - N_EXPERTS - N_EXPERT_GROUPS
    w_r = jnp.concatenate([w_router_expert[0], w_router_group[0], jnp.zeros((d, pad), F32)], axis=1)
    b_r = jnp.concatenate([b_router_expert[0], b_router_group[0], jnp.zeros((pad,), F32)])[None, :]
    cw = conv_w[0]

    cos_p, sin_p = _rope_angles(jnp.arange(seq, dtype=jnp.int32))
    conv0 = jnp.zeros((bsz, CONV_K - 1, CONV_CH), F32)
    qt_p, kt_p, vt_p, kb_p, vtb_p, kmeans_p, cy_p, conv_p = _prompt_inproj(
        x_prompt, g_mix2, w_in_f, _rope_row_tables(cos_p, sin_p), (cos_p.T, sin_p.T), cw, conv0)
    att_p = _prompt_attn(qt_p, kb_p, vtb_p, kmeans_p)
    n_p = bsz * seq
    hres_p, hn_p, route_p, counts_p = _outproj_router(
        x_prompt.reshape(n_p, d), att_p.reshape(n_p, ATT_WIDTH), cy_p.reshape(n_p, CONV_CH),
        w_out_bf, g_ffn2, w_r, b_r, jnp.zeros((1, LANES), F32), tm=512, precise=False)

    cos_s, sin_s = _rope_angles(past_len + jnp.arange(dec_len, dtype=jnp.int32))
    x_s = x_sample.reshape(n_dec, d)
    prev0, prev1 = state_conv[0, :, 0, :], state_conv[0, :, 1, :]
    q_s, k_s, v_s, cy_s, u_s = _sample_inproj(x_s, g_mix2, w_in_f, _rope_row_tables(cos_s, sin_s),
                                              cw, prev0, prev1)
    cache_kt = jnp.transpose(cache_k[0], (0, 2, 3, 1))
    cache_vt = jnp.transpose(cache_v[0], (0, 2, 3, 1))
    q_bcast = jnp.broadcast_to(q_s.reshape(n_dec, N_HEADS, HEAD_DIM, 1),
                               (n_dec, N_HEADS, HEAD_DIM, page_size))
    sel = _sample_select(page_table, q_bcast, cache_kt)
    sel_flat = sel[:, :, :MOBA_TOPK].reshape(-1)
    q3 = q_s.reshape(n_dec, 1, ATT_WIDTH)
    att_s = _sample_attn(sel_flat, page_table, q3, k_s.reshape(n_dec, 1, ATT_WIDTH),
                         v_s.reshape(n_dec, 1, ATT_WIDTH), cache_kt, cache_vt)
    hres_s, hn_s, route_s, counts = _outproj_router(
        x_s, att_s.reshape(n_dec, ATT_WIDTH), cy_s, w_out_f, g_ffn2, w_r, b_r, counts_p,
        tm=n_dec, precise=True)

    n_assign = EXPERT_TOPK * (n_p + n_dec)
    n_tiles = -(-(n_assign + N_EXPERTS * (MOE_TILE - 1)) // MOE_TILE)
    offsets, tile_expert, n_active, fill_starts = _moe_plan(counts, n_tiles)
    d1_p, d2_p = _moe_dests(route_p, offsets)
    d1_s, d2_s = _moe_dests(route_s, offsets)
    xs = _moe_dispatch(fill_starts, n_active, d1_p, d2_p, hn_p, None, n_tiles, tm=512)
    xs = _moe_dispatch(fill_starts, n_active, d1_s, d2_s, hn_s, xs, n_tiles, tm=n_dec)
    ys = _moe_ffn(tile_expert, n_active, xs, w_gate[0], w_up[0], w_down[0])
    y_p = _moe_combine(d1_p, d2_p, route_p, hres_p, g_final2, ys, tm=256)
    y_s = _moe_combine(d1_s, d2_s, route_s, hres_s, g_final2, ys, tm=n_dec)

    conv_s = jnp.stack([prev1, u_s], axis=1)
    to_bthd = lambda t: jnp.transpose(t.reshape(bsz, N_HEADS, HEAD_DIM, seq), (0, 3, 1, 2))[None]
    return (y_p.reshape(bsz, seq, d), y_s.reshape(n_dec, dec_len, d),
            to_bthd(kt_p), to_bthd(vt_p), conv_p[None],
            k_s.reshape(1, n_dec, dec_len, N_HEADS, HEAD_DIM),
            v_s.reshape(1, n_dec, dec_len, N_HEADS, HEAD_DIM),
            conv_s[None])
```

```python
import functools

import jax
import jax.numpy as jnp
from jax import lax
from jax.experimental import pallas as pl
from jax.experimental.pallas import tpu as pltpu

F32 = jnp.float32
BF16 = jnp.bfloat16
HIGHEST = lax.Precision.HIGHEST

D_MODEL = 1024
HEAD_DIM = 64
N_HEADS = 8
ATT_WIDTH = N_HEADS * HEAD_DIM
CONV_CH = D_MODEL - ATT_WIDTH
ROPE_DIM = HEAD_DIM // 4
ROPE_HALF = ROPE_DIM // 2
ROPE_THETA = 500000.0
MOBA_BLOCK = 256
MOBA_TOPK = 3
ATT_SCALE = HEAD_DIM ** -0.5
CONV_K = 3
N_EXPERT_GROUPS = 4
EXPERTS_PER_GROUP = 8
N_EXPERTS = N_EXPERT_GROUPS * EXPERTS_PER_GROUP
EXPERT_TOPK = 2
EXPERT_FF = D_MODEL // 4
RMS_EPS = 1e-6

LANES = 128
SUBLANES = 8
HEADS_PER_VREG = LANES // HEAD_DIM
TOKEN_ROWS = D_MODEL // LANES
MOE_TILE = 256
DMA_QUEUES = 2
ISSUE_UNROLL = 8
VMEM_LIMIT = 56 * 1024 * 1024

NEG_INF = float("-inf")


def _rms(x, g):
    ms = jnp.mean(x * x, axis=-1, keepdims=True)
    return x * lax.rsqrt(ms + RMS_EPS) * g


def _mm(a, w, precise):
    if precise:
        return jnp.dot(a, w, precision=HIGHEST, preferred_element_type=F32)
    return jnp.dot(a.astype(BF16), w, preferred_element_type=F32)


NN_DIMS = (((1,), (0,)), ((), ()))
NT_DIMS = (((1,), (1,)), ((), ()))


def _dot_f32(a, b, dims, full):
    if full:
        return lax.dot_general(a, b, dims, precision=HIGHEST, preferred_element_type=F32)
    a_hi, b_hi = a.astype(BF16), b.astype(BF16)
    a_lo = (a - a_hi.astype(F32)).astype(BF16)
    b_lo = (b - b_hi.astype(F32)).astype(BF16)
    dot = functools.partial(lax.dot_general, dimension_numbers=dims, preferred_element_type=F32)
    return dot(a_hi, b_hi) + (dot(a_hi, b_lo) + dot(a_lo, b_hi))


def _rope_rows(a, cos, sin_lo, sin_hi):
    n = a.shape[-1]
    return (a * cos + pltpu.roll(a, n - ROPE_HALF, 1) * sin_lo
            + pltpu.roll(a, ROPE_HALF, 1) * sin_hi)


def _tile_lanes(t, reps):
    return jnp.concatenate([t] * reps, axis=-1)


def _rope_angles(pos):
    inv = ROPE_THETA ** (-jnp.arange(0, ROPE_DIM, 2, dtype=F32) / ROPE_DIM)
    ang = pos.astype(F32)[:, None] * inv[None, :]
    return jnp.cos(ang), jnp.sin(ang)


def _rope_row_tables(cos, sin):
    rows = cos.shape[0]
    ones = jnp.ones((rows, HEAD_DIM - ROPE_DIM), F32)
    zeros = jnp.zeros((rows, HEAD_DIM - ROPE_HALF), F32)
    c = jnp.concatenate([cos, cos, ones], axis=1)
    s_lo = jnp.concatenate([-sin, zeros], axis=1)
    s_hi = jnp.concatenate([jnp.zeros((rows, ROPE_HALF), F32), sin,
                            jnp.zeros((rows, HEAD_DIM - ROPE_DIM), F32)], axis=1)
    rep = lambda t: jnp.concatenate([t] * HEADS_PER_VREG, axis=1)
    return rep(c), rep(s_lo), rep(s_hi)


def _rope_cols(ref, cos_t, sin_t):
    for h in range(N_HEADS):
        r = h * HEAD_DIM
        x1 = ref[r:r + ROPE_HALF, :]
        x2 = ref[r + ROPE_HALF:r + ROPE_DIM, :]
        ref[r:r + ROPE_HALF, :] = x1 * cos_t - x2 * sin_t
        ref[r + ROPE_HALF:r + ROPE_DIM, :] = x2 * cos_t + x1 * sin_t


def _prompt_inproj_kernel(x_ref, g_ref, wqt_ref, wkt_ref, wvt_ref, wc_ref,
                          cost_ref, sint_ref, cw_ref, prev_ref,
                          qt_ref, kt_ref, vt_ref, kb_ref, vtb_ref, km_ref, cy_ref, cn_ref,
                          ubuf, halo, km_acc, *, tt):
    t = pl.program_id(1)
    blk = MOBA_BLOCK
    xn = _rms(x_ref[...], g_ref[...]).astype(BF16)

    qt_ref[...] = lax.dot_general(wqt_ref[...], xn, NT_DIMS, preferred_element_type=F32)
    _rope_cols(qt_ref, cost_ref[...], sint_ref[...])
    kt_ref[...] = lax.dot_general(wkt_ref[...], xn, NT_DIMS, preferred_element_type=F32)
    _rope_cols(kt_ref, cost_ref[...], sint_ref[...])
    vt = lax.dot_general(wvt_ref[...], xn, NT_DIMS, preferred_element_type=F32)
    vt_ref[...] = vt
    for i in range(tt // blk):
        vtb_ref[i] = vt[:, i * blk:(i + 1) * blk].astype(BF16)

    k = kt_ref[...].T
    kb_ref[...] = k.astype(BF16)

    @pl.when(t == 0)
    def _():
        km_acc[...] = jnp.zeros_like(km_acc)

    rid = lax.broadcasted_iota(jnp.int32, km_acc.shape, 0)
    km = km_acc[...]
    for i in range(tt // blk):
        mean_i = jnp.mean(k[i * blk:(i + 1) * blk, :], axis=0, keepdims=True)
        km = jnp.where(rid == t * (tt // blk) + i, mean_i, km)
    km_acc[...] = km
    km_ref[...] = km

    def proj(i):
        return jnp.dot(xn, wc_ref[:, i * CONV_CH:(i + 1) * CONV_CH], preferred_element_type=F32)

    b_gate = proj(0)
    u = proj(1) * proj(2)

    @pl.when(t == 0)
    def _():
        ubuf[SUBLANES - (CONV_K - 1):SUBLANES, :] = prev_ref[...]

    @pl.when(t > 0)
    def _():
        ubuf[0:SUBLANES, :] = halo[...]

    ubuf[SUBLANES:SUBLANES + tt, :] = u
    cw = cw_ref[...]
    conv = (cw[0:1, :] * ubuf[SUBLANES - 2:SUBLANES - 2 + tt, :]
            + cw[1:2, :] * ubuf[SUBLANES - 1:SUBLANES - 1 + tt, :]
            + cw[2:3, :] * u)
    cy_ref[...] = (b_gate * conv).astype(cy_ref.dtype)
    halo[...] = ubuf[tt:tt + SUBLANES, :]
    cn_ref[...] = ubuf[tt + SUBLANES - (CONV_K - 1):tt + SUBLANES, :]


def _prompt_inproj(x, g_mix, w_in, col_tables, conv_w, conv_prev, tt=512):
    bsz, seq, d = x.shape
    nt = seq // tt
    nb = seq // MOBA_BLOCK
    w_bf = w_in.astype(BF16)
    wqt = w_bf[:, :ATT_WIDTH].T
    wkt = w_bf[:, ATT_WIDTH:2 * ATT_WIDTH].T
    wvt = w_bf[:, 2 * ATT_WIDTH:3 * ATT_WIDTH].T
    wc = w_bf[:, 3 * ATT_WIDTH:]
    row_spec = lambda width: pl.BlockSpec((None, tt, width), lambda b, t: (b, t, 0))
    col_spec = pl.BlockSpec((None, ATT_WIDTH, tt), lambda b, t: (b, 0, t))
    tabt_spec = pl.BlockSpec((ROPE_HALF, tt), lambda b, t: (0, t))
    full = lambda shape: pl.BlockSpec(shape, lambda b, t: (0,) * len(shape))
    per_b = lambda shape: pl.BlockSpec((None,) + shape, lambda b, t: (b,) + (0,) * len(shape))
    vtb_spec = pl.BlockSpec((None, tt // MOBA_BLOCK, ATT_WIDTH, MOBA_BLOCK),
                            lambda b, t: (b, t, 0, 0))
    col_shape = jax.ShapeDtypeStruct((bsz, ATT_WIDTH, seq), F32)
    return pl.pallas_call(
        functools.partial(_prompt_inproj_kernel, tt=tt),
        grid=(bsz, nt),
        in_specs=[row_spec(d), full((1, d)), full(wqt.shape), full(wkt.shape), full(wvt.shape),
                  full(wc.shape), tabt_spec, tabt_spec,
                  full((CONV_K, CONV_CH)), per_b((CONV_K - 1, CONV_CH))],
        out_specs=[col_spec, col_spec, col_spec, row_spec(ATT_WIDTH), vtb_spec,
                   per_b((nb, ATT_WIDTH)), row_spec(CONV_CH), per_b((CONV_K - 1, CONV_CH))],
        out_shape=[col_shape, col_shape, col_shape,
                   jax.ShapeDtypeStruct((bsz, seq, ATT_WIDTH), BF16),
                   jax.ShapeDtypeStruct((bsz, nb, ATT_WIDTH, MOBA_BLOCK), BF16),
                   jax.ShapeDtypeStruct((bsz, nb, ATT_WIDTH), F32),
                   jax.ShapeDtypeStruct((bsz, seq, CONV_CH), BF16),
                   jax.ShapeDtypeStruct((bsz, CONV_K - 1, CONV_CH), F32)],
        scratch_shapes=[pltpu.VMEM((tt + SUBLANES, CONV_CH), F32),
                        pltpu.VMEM((SUBLANES, CONV_CH), F32),
                        pltpu.VMEM((nb, ATT_WIDTH), F32)],
        compiler_params=pltpu.CompilerParams(
            dimension_semantics=("arbitrary", "arbitrary"), vmem_limit_bytes=VMEM_LIMIT),
        name="prompt_inproj",
    )(x, g_mix, wqt, wkt, wvt, wc, *col_tables, conv_w, conv_prev)


def _prompt_attn_kernel(qt_ref, kb_ref, vtb_ref, km_ref, o_ref, s_ref, ot_ref, *, nb):
    blk = MOBA_BLOCK
    means = km_ref[...]
    feat = lax.broadcasted_iota(jnp.int32, (LANES, 1), 0)
    bid = lax.broadcasted_iota(jnp.int32, (nb, blk), 0)
    key = lax.broadcasted_iota(jnp.int32, (blk, blk), 0)
    qry = lax.broadcasted_iota(jnp.int32, (blk, blk), 1)

    def block_bias(qth, qi):
        if qi <= MOBA_TOPK:
            return [None] * qi
        gates = jnp.dot(means, qth, precision=HIGHEST, preferred_element_type=F32)
        gates = jnp.where(bid < qi, gates, NEG_INF)
        rows = []
        for j in range(qi):
            gj = gates[j:j + 1, :]
            beats = (gates > gj) | ((gates == gj) & (bid < j))
            cnt = jnp.sum(beats.astype(F32), axis=0, keepdims=True)
            rows.append(jnp.where(cnt < MOBA_TOPK, 0.0, NEG_INF))
        return rows

    def scores_pass(qi, h, slot, state):
        qt = qt_ref[:, qi * blk:(qi + 1) * blk]
        qth = jnp.where((feat >= h * HEAD_DIM) & (feat < (h + 1) * HEAD_DIM), qt, 0.0)
        bias = block_bias(qth, qi)
        qs = (qth * ATT_SCALE).astype(BF16)
        m = None
        for j in range(qi + 1):
            s = jnp.dot(kb_ref[j * blk:(j + 1) * blk, :], qs,
                        preferred_element_type=F32)
            if j == qi:
                s = jnp.where(key <= qry, s, NEG_INF)
            elif bias[j] is not None:
                s = s + bias[j]
            s_ref[slot, j] = s
            m_blk = jnp.max(s, axis=0, keepdims=True)
            m = m_blk if m is None else jnp.maximum(m, m_blk)
            yield
        state["m"] = m

    def values_pass(qi, h, slot, state):
        m = state["m"]
        l = acc = None
        for j in range(qi + 1):
            p = jnp.exp(s_ref[slot, j] - m)
            l_blk = jnp.sum(p, axis=0, keepdims=True)
            pv = jnp.dot(vtb_ref[j, h * HEAD_DIM:(h + 1) * HEAD_DIM, :], p.astype(BF16),
                         preferred_element_type=F32)
            l, acc = (l_blk, pv) if l is None else (l + l_blk, acc + pv)
            yield
        ot_ref[h * HEAD_DIM:(h + 1) * HEAD_DIM, qi * blk:(qi + 1) * blk] = acc / l

    def run_interleaved(*gens):
        live = [g for g in gens if g is not None]
        while live:
            for g in list(live):
                if next(g, "done") == "done":
                    live.remove(g)

    pending = None
    for i, (qi, h) in enumerate((qi, h) for qi in range(nb) for h in range(HEADS_PER_VREG)):
        state = {}
        run_interleaved(scores_pass(qi, h, i % 2, state), pending)
        pending = values_pass(qi, h, i % 2, state)
    run_interleaved(pending)
    o_ref[...] = ot_ref[...].T.astype(o_ref.dtype)


def _prompt_attn(qt, kb, vtb, kmeans):
    bsz, seq, _ = kb.shape
    nb = seq // MOBA_BLOCK
    n_pairs = ATT_WIDTH // LANES
    return pl.pallas_call(
        functools.partial(_prompt_attn_kernel, nb=nb),
        grid=(bsz, n_pairs),
        in_specs=[pl.BlockSpec((None, LANES, seq), lambda b, hp: (b, hp, 0)),
                  pl.BlockSpec((None, seq, LANES), lambda b, hp: (b, 0, hp)),
                  pl.BlockSpec((None, nb, LANES, MOBA_BLOCK), lambda b, hp: (b, 0, hp, 0)),
                  pl.BlockSpec((None, nb, LANES), lambda b, hp: (b, 0, hp))],
        out_specs=pl.BlockSpec((None, seq, LANES), lambda b, hp: (b, 0, hp)),
        out_shape=jax.ShapeDtypeStruct((bsz, seq, ATT_WIDTH), BF16),
        scratch_shapes=[pltpu.VMEM((2, nb, MOBA_BLOCK, MOBA_BLOCK), F32),
                        pltpu.VMEM((LANES, seq), F32)],
        compiler_params=pltpu.CompilerParams(
            dimension_semantics=("arbitrary", "arbitrary"), vmem_limit_bytes=VMEM_LIMIT),
        name="prompt_attn",
    )(qt, kb, vtb, kmeans)


def _outproj_router_kernel(x_ref, att_ref, cy_ref, wo_ref, g_ref, wr_ref, br_ref, cnt_in_ref,
                           hres_ref, hn_ref, route_ref, cnt_ref, tri_ref, *, precise):
    step = pl.program_id(0)
    tm = x_ref.shape[0]
    hres = (x_ref[...] + _mm(att_ref[...], wo_ref[0:ATT_WIDTH, :], precise)
            + _mm(cy_ref[...], wo_ref[ATT_WIDTH:, :], precise))
    hres_ref[...] = hres
    hn = _rms(hres, g_ref[...])
    for s in range(TOKEN_ROWS):
        hn_ref[pl.ds(s, tm, stride=TOKEN_ROWS), :] = hn[:, s * LANES:(s + 1) * LANES]

    @pl.when(step == 0)
    def _():
        cnt_ref[...] = cnt_in_ref[...]
        r = lax.broadcasted_iota(jnp.int32, (tm, tm), 0)
        c = lax.broadcasted_iota(jnp.int32, (tm, tm), 1)
        tri_ref[...] = jnp.where(c < r, 1.0, 0.0).astype(BF16)

    lo = _dot_f32(hn, wr_ref[...], NN_DIMS, precise) + br_ref[...]
    lane = lax.broadcasted_iota(jnp.int32, lo.shape, 1)
    is_group = (lane >= N_EXPERTS) & (lane < N_EXPERTS + N_EXPERT_GROUPS)
    lg = jnp.where(is_group, lo, NEG_INF)
    mg = jnp.max(lg, axis=1, keepdims=True)
    g_lane = jnp.min(jnp.where(lg == mg, lane, LANES), axis=1, keepdims=True)
    pg = 1.0 / jnp.sum(jnp.exp(lg - mg), axis=1, keepdims=True)
    e_lo = (g_lane - N_EXPERTS) * EXPERTS_PER_GROUP
    in_g = (lane >= e_lo) & (lane < e_lo + EXPERTS_PER_GROUP)
    le = jnp.where(in_g, lo, NEG_INF)
    m1 = jnp.max(le, axis=1, keepdims=True)
    i1 = jnp.min(jnp.where(le == m1, lane, LANES), axis=1, keepdims=True)
    le2 = jnp.where(lane == i1, NEG_INF, le)
    m2 = jnp.max(le2, axis=1, keepdims=True)
    i2 = jnp.min(jnp.where(le2 == m2, lane, LANES), axis=1, keepdims=True)
    e2 = jnp.exp(m2 - m1)
    w1 = pg / (1.0 + e2)
    w2 = pg * e2 / (1.0 + e2)

    oh1, oh2 = lane == i1, lane == i2
    assigned = jnp.where(oh1 | oh2, 1.0, 0.0)
    before = cnt_ref[...] + jnp.dot(tri_ref[...], assigned.astype(BF16),
                                    preferred_element_type=F32)
    r1 = jnp.sum(jnp.where(oh1, before, 0.0), axis=1, keepdims=True)
    r2 = jnp.sum(jnp.where(oh2, before, 0.0), axis=1, keepdims=True)
    cnt_ref[...] += jnp.sum(assigned, axis=0, keepdims=True)
    cols = (i1.astype(F32), i2.astype(F32), r1, r2, w1, w2)
    route = jnp.zeros(lo.shape, F32)
    for c, v in enumerate(cols):
        route = jnp.where(lane == c, v, route)
    route_ref[...] = route


def _outproj_router(x, att, cy, w_out, g_ffn, w_r, b_r, counts_in, tm, precise):
    n, d = x.shape
    rows = lambda width: pl.BlockSpec((tm, width), lambda i: (i, 0))
    full = lambda shape: pl.BlockSpec(shape, lambda i: (0,) * len(shape))
    return pl.pallas_call(
        functools.partial(_outproj_router_kernel, precise=precise),
        grid=(n // tm,),
        in_specs=[rows(d), rows(ATT_WIDTH), rows(CONV_CH), full(w_out.shape), full((1, d)),
                  full(w_r.shape), full((1, LANES)), full((1, LANES))],
        out_specs=[rows(d), pl.BlockSpec((tm * TOKEN_ROWS, LANES), lambda i: (i, 0)), rows(LANES),
                   full((1, LANES))],
        out_shape=[jax.ShapeDtypeStruct((n, d), F32),
                   jax.ShapeDtypeStruct((n * TOKEN_ROWS, LANES), F32),
                   jax.ShapeDtypeStruct((n, LANES), F32),
                   jax.ShapeDtypeStruct((1, LANES), F32)],
        scratch_shapes=[pltpu.VMEM((tm, tm), BF16)],
        compiler_params=pltpu.CompilerParams(
            dimension_semantics=("arbitrary",), vmem_limit_bytes=VMEM_LIMIT),
        name="outproj_router_precise" if precise else "outproj_router",
    )(x, att, cy, w_out, g_ffn, w_r, b_r, counts_in)


def _token_rows(ref, t):
    return ref.at[pl.ds(pl.multiple_of(t * TOKEN_ROWS, TOKEN_ROWS), TOKEN_ROWS), :]


def _start_token_copies(copies, n):
    def start(g, carry):
        for u in range(ISSUE_UNROLL):
            for c in copies(g * ISSUE_UNROLL + u):
                c.start(priority=u % DMA_QUEUES)
        return carry

    lax.fori_loop(0, n // ISSUE_UNROLL, start, 0)


def _moe_dispatch_kernel(fill_ref, na_ref, d1_ref, d2_ref, hn_ref, *refs, tm, n_tiles, zero_fill):
    xs_ref, zbuf, sem = refs[-3:]
    step = pl.program_id(0)

    if zero_fill:
        @pl.when(step == 0)
        def _():
            zbuf[...] = jnp.zeros_like(zbuf)

            def fill(slot):
                return pltpu.make_async_copy(
                    zbuf, xs_ref.at[pl.ds(pl.multiple_of(slot * TOKEN_ROWS, TOKEN_ROWS),
                                          MOE_TILE * TOKEN_ROWS), :], sem.at[1])

            def start_fill(i, carry):
                fill(i * MOE_TILE).start()
                return carry

            def wait_fill(i, carry):
                fill(i * MOE_TILE).wait()
                return carry

            def group_fills(act):
                for e in range(N_EXPERTS):
                    @pl.when(fill_ref[e] >= 0)
                    def _():
                        act(fill(fill_ref[e]))

            group_fills(lambda c: c.start())
            lax.fori_loop(na_ref[0], n_tiles, start_fill, 0)
            group_fills(lambda c: c.wait())
            lax.fori_loop(na_ref[0], n_tiles, wait_fill, 0)

    def copies(t):
        src = _token_rows(hn_ref, t)
        return (pltpu.make_async_copy(src, _token_rows(xs_ref, d1_ref[0, t]), sem.at[0]),
                pltpu.make_async_copy(src, _token_rows(xs_ref, d2_ref[0, t]), sem.at[0]))

    _start_token_copies(copies, tm)
    for _ in range(EXPERT_TOPK):
        pltpu.make_async_copy(hn_ref, xs_ref.at[pl.ds(0, tm * TOKEN_ROWS), :], sem.at[0]).wait()


def _moe_dispatch(fill_starts, n_active, dest1, dest2, hn_tiles, xs_prev, n_tiles, tm):
    n = dest1.size
    d1 = dest1.reshape(n // tm, 1, tm)
    d2 = dest2.reshape(n // tm, 1, tm)
    zero_fill = xs_prev is None
    smem_row = pl.BlockSpec((None, 1, tm), lambda i, fs, na: (i, 0, 0), memory_space=pltpu.SMEM)
    hbm = pl.BlockSpec(memory_space=pl.ANY)
    in_specs = [smem_row, smem_row,
                pl.BlockSpec((tm * TOKEN_ROWS, LANES), lambda i, fs, na: (i, 0))]
    args = [fill_starts, n_active, d1, d2, hn_tiles]
    aliases = {}
    if not zero_fill:
        in_specs.append(hbm)
        args.append(xs_prev)
        aliases = {len(args) - 1: 0}
    return pl.pallas_call(
        functools.partial(_moe_dispatch_kernel, tm=tm, n_tiles=n_tiles, zero_fill=zero_fill),
        grid_spec=pltpu.PrefetchScalarGridSpec(
            num_scalar_prefetch=2,
            grid=(n // tm,),
            in_specs=in_specs,
            out_specs=hbm,
            scratch_shapes=[pltpu.VMEM((MOE_TILE * TOKEN_ROWS, LANES), F32),
                            pltpu.SemaphoreType.DMA((2,))],
        ),
        out_shape=jax.ShapeDtypeStruct((n_tiles * MOE_TILE * TOKEN_ROWS, LANES), F32),
        input_output_aliases=aliases,
        compiler_params=pltpu.CompilerParams(
            dimension_semantics=("arbitrary",), vmem_limit_bytes=VMEM_LIMIT),
        name="moe_dispatch" if zero_fill else "moe_dispatch_more",
    )(*args)


def _moe_ffn_kernel(te_ref, na_ref, xs_ref, wg_ref, wu_ref, wd_ref, ys_ref, wg_b, wu_b, wd_b):
    i = pl.program_id(0)
    tg = MOE_TILE
    active = i < na_ref[0]

    @pl.when(active)
    def _():
        @pl.when((i == 0) | (te_ref[i] != te_ref[jnp.maximum(i - 1, 0)]))
        def _():
            wg_b[...] = wg_ref[...].astype(BF16)
            wu_b[...] = wu_ref[...].astype(BF16)
            wd_b[...] = wd_ref[...].astype(BF16)

        x = jnp.concatenate(
            [xs_ref[pl.ds(s, tg, stride=TOKEN_ROWS), :].astype(BF16) for s in range(TOKEN_ROWS)],
            axis=1)
        hg = jnp.dot(x, wg_b[...], preferred_element_type=F32)
        hu = jnp.dot(x, wu_b[...], preferred_element_type=F32)
        h = hg * (1.0 / (1.0 + jnp.exp(-hg))) * hu
        y = jnp.dot(h.astype(BF16), wd_b[...], preferred_element_type=F32)
        for s in range(TOKEN_ROWS):
            ys_ref[pl.ds(s, tg, stride=TOKEN_ROWS), :] = y[:, s * LANES:(s + 1) * LANES]

    @pl.when(jnp.logical_not(active))
    def _():
        ys_ref[...] = jnp.zeros_like(ys_ref)


def _moe_ffn(tile_expert, n_active, xs, w_gate, w_up, w_down):
    n_tiles = tile_expert.shape[0]
    tile_rows = MOE_TILE * TOKEN_ROWS
    d, ff = w_gate.shape[1:]
    last = lambda na: na[0] - 1
    w_spec = lambda shape: pl.BlockSpec(
        (None,) + shape, lambda i, te, na: (te[jnp.minimum(i, last(na))], 0, 0))
    return pl.pallas_call(
        _moe_ffn_kernel,
        grid_spec=pltpu.PrefetchScalarGridSpec(
            num_scalar_prefetch=2,
            grid=(n_tiles,),
            in_specs=[pl.BlockSpec((tile_rows, LANES),
                                   lambda i, te, na: (jnp.minimum(i, last(na)), 0)),
                      w_spec((d, ff)), w_spec((d, ff)), w_spec((ff, d))],
            out_specs=pl.BlockSpec((tile_rows, LANES), lambda i, te, na: (i, 0)),
            scratch_shapes=[pltpu.VMEM((d, ff), BF16), pltpu.VMEM((d, ff), BF16),
                            pltpu.VMEM((ff, d), BF16)],
        ),
        out_shape=jax.ShapeDtypeStruct(xs.shape, F32),
        compiler_params=pltpu.CompilerParams(
            dimension_semantics=("arbitrary",), vmem_limit_bytes=VMEM_LIMIT),
        name="moe_ffn",
    )(tile_expert, n_active, xs, w_gate, w_up, w_down)


def _moe_combine_kernel(d1_ref, d2_ref, route_ref, hres_ref, gf_ref, ys_ref, y_ref, buf, sem, *, tm):
    def copies(t):
        return (pltpu.make_async_copy(_token_rows(ys_ref, d1_ref[0, t]), _token_rows(buf.at[0], t),
                                      sem.at[0]),
                pltpu.make_async_copy(_token_rows(ys_ref, d2_ref[0, t]), _token_rows(buf.at[1], t),
                                      sem.at[1]))

    _start_token_copies(copies, tm)
    for k in range(EXPERT_TOPK):
        pltpu.make_async_copy(ys_ref.at[pl.ds(0, tm * TOKEN_ROWS), :], buf.at[k], sem.at[k]).wait()

    route = route_ref[...]
    w1, w2 = route[:, 4:5], route[:, 5:6]
    for s in range(TOKEN_ROWS):
        cols = slice(s * LANES, (s + 1) * LANES)
        y_ref[:, cols] = (hres_ref[:, cols]
                          + w1 * buf[0, pl.ds(s, tm, stride=TOKEN_ROWS), :]
                          + w2 * buf[1, pl.ds(s, tm, stride=TOKEN_ROWS), :])
    y_ref[...] = _rms(y_ref[...], gf_ref[...])


def _moe_combine(dest1, dest2, route, hres, g_final, ys, tm):
    n, d = hres.shape
    d1 = dest1.reshape(n // tm, 1, tm)
    d2 = dest2.reshape(n // tm, 1, tm)
    smem_row = pl.BlockSpec((None, 1, tm), lambda i: (i, 0, 0), memory_space=pltpu.SMEM)
    rows = lambda width: pl.BlockSpec((tm, width), lambda i: (i, 0))
    return pl.pallas_call(
        functools.partial(_moe_combine_kernel, tm=tm),
        grid=(n // tm,),
        in_specs=[smem_row, smem_row, rows(LANES), rows(d), pl.BlockSpec((1, d), lambda i: (0, 0)),
                  pl.BlockSpec(memory_space=pl.ANY)],
        out_specs=rows(d),
        out_shape=jax.ShapeDtypeStruct((n, d), F32),
        scratch_shapes=[pltpu.VMEM((2, tm * TOKEN_ROWS, LANES), F32),
                        pltpu.SemaphoreType.DMA((2,))],
        compiler_params=pltpu.CompilerParams(
            dimension_semantics=("arbitrary",), vmem_limit_bytes=VMEM_LIMIT),
        name="moe_combine",
    )(d1, d2, route, hres, g_final, ys)


def _moe_plan(counts, n_tiles):
    counts = counts[0, :N_EXPERTS].astype(jnp.int32)
    padded = (counts + MOE_TILE - 1) // MOE_TILE * MOE_TILE
    ends = jnp.cumsum(padded)
    offsets = ends - padded
    n_active = (ends[-1:] // MOE_TILE).astype(jnp.int32)
    tile_start = jnp.arange(n_tiles, dtype=jnp.int32) * MOE_TILE
    tile_expert = jnp.minimum(jnp.sum(tile_start[:, None] >= ends[None, :], axis=1), N_EXPERTS - 1)
    fill_starts = jnp.where(padded > 0, ends - MOE_TILE, -1)
    return offsets, tile_expert.astype(jnp.int32), n_active, fill_starts.astype(jnp.int32)


def _moe_dests(route, offsets):
    def dest(e_col, r_col):
        onehot = route[:, e_col:e_col + 1].astype(jnp.int32) == jnp.arange(N_EXPERTS)[None, :]
        return jnp.sum(jnp.where(onehot, offsets[None, :], 0), axis=1) + route[:, r_col].astype(jnp.int32)
    return dest(0, 2), dest(1, 3)


def _sample_inproj_kernel(x_ref, g_ref, w_ref, cos_ref, slo_ref, shi_ref, cw_ref, p0_ref, p1_ref,
                          q_ref, k_ref, v_ref, cy_ref, u_ref):
    xn = _rms(x_ref[...], g_ref[...])
    reps = ATT_WIDTH // LANES
    cos = _tile_lanes(cos_ref[...], reps)
    slo = _tile_lanes(slo_ref[...], reps)
    shi = _tile_lanes(shi_ref[...], reps)

    def proj(i):
        return jnp.dot(xn, w_ref[:, i * ATT_WIDTH:(i + 1) * ATT_WIDTH], precision=HIGHEST,
                       preferred_element_type=F32)

    q_ref[...] = _rope_rows(proj(0), cos, slo, shi)
    k_ref[...] = _rope_rows(proj(1), cos, slo, shi)
    v_ref[...] = proj(2)
    b_gate = proj(3)
    u = proj(4) * proj(5)
    u_ref[...] = u
    cw = cw_ref[...]
    cy_ref[...] = b_gate * (cw[0:1, :] * p0_ref[...] + cw[1:2, :] * p1_ref[...] + cw[2:3, :] * u)


def _sample_inproj(x, g_mix, w_in, tables, conv_w, prev0, prev1):
    n = x.shape[0]
    out = jax.ShapeDtypeStruct((n, ATT_WIDTH), F32)
    return pl.pallas_call(
        _sample_inproj_kernel,
        out_shape=[out] * 5,
        compiler_params=pltpu.CompilerParams(vmem_limit_bytes=VMEM_LIMIT),
        name="sample_inproj",
    )(x, g_mix, w_in, *tables, conv_w, prev0, prev1)


PAGES_PER_STEP = 32


def _sample_select_kernel(pt_ref, *refs, page_size, n_blocks):
    q_ref = refs[0]
    page_refs = refs[1:1 + PAGES_PER_STEP]
    sel_ref, part = refs[1 + PAGES_PER_STEP], refs[2 + PAGES_PER_STEP]
    g = pl.program_id(1)
    pages_per_block = MOBA_BLOCK // page_size
    blocks_per_step = PAGES_PER_STEP // pages_per_block

    qb = q_ref[...]
    for i in range(blocks_per_step):
        tot = page_refs[i * pages_per_block][...]
        for p in range(1, pages_per_block):
            tot = tot + page_refs[i * pages_per_block + p][...]
        part[g * blocks_per_step + i] = jnp.sum(tot * qb, axis=1)

    @pl.when(g == pl.num_programs(1) - 1)
    def _():
        lane = lax.broadcasted_iota(jnp.int32, (N_HEADS, LANES), 1)
        acc = jnp.full((N_HEADS, LANES), NEG_INF, F32)
        for j in range(n_blocks):
            gate_j = jnp.sum(part[j], axis=1, keepdims=True) * (1.0 / MOBA_BLOCK)
            acc = jnp.where(lane == j, gate_j, acc)
        out = jnp.zeros((N_HEADS, LANES), jnp.int32)
        for j in range(n_blocks):
            gj = acc[:, j:j + 1]
            beats = (acc > gj) | ((acc == gj) & (lane < j))
            cnt = jnp.sum(beats.astype(jnp.int32), axis=1, keepdims=True)
            out = jnp.where(lane == cnt, j, out)
        sel_ref[...] = out


def _sample_select(page_table, q_bcast, cache_kt):
    n_dec, n_pages = page_table.shape
    _, n_heads, head_dim, page_size = cache_kt.shape
    n_blocks = n_pages * page_size // MOBA_BLOCK

    def page_spec(i):
        return pl.BlockSpec((None, n_heads, head_dim, page_size),
                            lambda b, g, pt: (pt[b, g * PAGES_PER_STEP + i], 0, 0, 0))

    per_seq = lambda shape: pl.BlockSpec((None,) + shape, lambda b, g, pt: (b,) + (0,) * len(shape))
    return pl.pallas_call(
        functools.partial(_sample_select_kernel, page_size=page_size, n_blocks=n_blocks),
        grid_spec=pltpu.PrefetchScalarGridSpec(
            num_scalar_prefetch=1,
            grid=(n_dec, n_pages // PAGES_PER_STEP),
            in_specs=[per_seq((n_heads, head_dim, page_size))]
                     + [page_spec(i) for i in range(PAGES_PER_STEP)],
            out_specs=per_seq((n_heads, LANES)),
            scratch_shapes=[pltpu.VMEM((n_blocks, n_heads, LANES), F32)],
        ),
        out_shape=jax.ShapeDtypeStruct((n_dec, n_heads, LANES), jnp.int32),
        compiler_params=pltpu.CompilerParams(
            dimension_semantics=("arbitrary", "arbitrary"), vmem_limit_bytes=VMEM_LIMIT),
        name="sample_select",
    )(page_table, q_bcast, *([cache_kt] * PAGES_PER_STEP))


def _sample_attn_kernel(sel_ref, pt_ref, q_ref, qb_ref, kn_ref, vn_ref, ck_ref, cv_ref, o_ref,
                        kbuf, vbuf, sem, *, pages_per_block, page_size):
    b = pl.program_id(0)
    nb = pl.num_programs(0)

    def copies(bb, slot):
        out = []
        for h in range(N_HEADS):
            for r in range(MOBA_TOPK):
                block = sel_ref[(bb * N_HEADS + h) * MOBA_TOPK + r]
                for p in range(pages_per_block):
                    page = pt_ref[bb, block * pages_per_block + p]
                    dst = pl.ds((r * pages_per_block + p) * page_size, page_size)
                    out.append(pltpu.make_async_copy(ck_ref.at[page, h], kbuf.at[slot, h, :, dst],
                                                     sem.at[slot, 0]))
                    out.append(pltpu.make_async_copy(cv_ref.at[page, h], vbuf.at[slot, h, :, dst],
                                                     sem.at[slot, 1]))
        return out

    @pl.when(b == 0)
    def _():
        for c in copies(0, 0):
            c.start()

    @pl.when(b + 1 < nb)
    def _():
        for c in copies(b + 1, (b + 1) % 2):
            c.start()

    slot = b % 2
    for c in copies(b, slot):
        c.wait()

    q = q_ref[...]
    qb = qb_ref[...]
    kn = kn_ref[...]
    vn = vn_ref[...]
    n_keys = kbuf.shape[-1]
    head = lax.broadcasted_iota(jnp.int32, (N_HEADS, ATT_WIDTH), 0)
    feat = lax.broadcasted_iota(jnp.int32, (N_HEADS, ATT_WIDTH), 1)
    own = (feat >= head * HEAD_DIM) & (feat < (head + 1) * HEAD_DIM)

    s = jnp.concatenate(
        [jnp.sum(kbuf[slot, :, :, c * LANES:(c + 1) * LANES] * qb, axis=1)
         for c in range(n_keys // LANES)], axis=1) * ATT_SCALE
    s_self = jnp.sum(jnp.where(own, q * kn, 0.0), axis=1, keepdims=True) * ATT_SCALE
    m = jnp.maximum(jnp.max(s, axis=1, keepdims=True), s_self)
    p = jnp.exp(s - m)
    p_self = jnp.exp(s_self - m)
    l = jnp.sum(p, axis=1, keepdims=True) + p_self
    pv = _dot_f32(p, vbuf[slot].reshape(N_HEADS * HEAD_DIM, n_keys), NT_DIMS, False)
    o = (pv + p_self * vn) / l
    o_ref[...] = jnp.sum(jnp.where(own, o, 0.0), axis=0, keepdims=True)


def _sample_attn(sel_flat, page_table, q3, q_bcast, k3, v3, cache_kt, cache_vt):
    n_dec = q3.shape[0]
    page_size = cache_kt.shape[3]
    pages_per_block = MOBA_BLOCK // page_size
    n_keys = MOBA_TOPK * MOBA_BLOCK
    row = pl.BlockSpec((None, 1, ATT_WIDTH), lambda b, sel, pt: (b, 0, 0))
    hbm = pl.BlockSpec(memory_space=pl.ANY)
    return pl.pallas_call(
        functools.partial(_sample_attn_kernel, pages_per_block=pages_per_block,
                          page_size=page_size),
        grid_spec=pltpu.PrefetchScalarGridSpec(
            num_scalar_prefetch=2,
            grid=(n_dec,),
            in_specs=[row, pl.BlockSpec((None,) + q_bcast.shape[1:],
                                        lambda b, sel, pt: (b, 0, 0, 0)), row, row, hbm, hbm],
            out_specs=row,
            scratch_shapes=[pltpu.VMEM((2, N_HEADS, HEAD_DIM, n_keys), F32),
                            pltpu.VMEM((2, N_HEADS, HEAD_DIM, n_keys), F32),
                            pltpu.SemaphoreType.DMA((2, 2))],
        ),
        out_shape=jax.ShapeDtypeStruct((n_dec, 1, ATT_WIDTH), F32),
        compiler_params=pltpu.CompilerParams(
            dimension_semantics=("arbitrary",), vmem_limit_bytes=VMEM_LIMIT),
        name="sample_attn",
    )(sel_flat, page_table, q3, q_bcast, k3, v3, cache_kt, cache_vt)


def kernel(x_prompt, x_sample, cache_k, cache_v, state_conv, page_table, g_mix, w_in, conv_w, w_out,
           g_ffn, w_router_group, b_router_group, w_router_expert, b_router_expert, w_gate, w_up,
           w_down, g_final):
    depth = g_mix.shape[0]
    assert depth == 1
    bsz, seq, d = x_prompt.shape
    n_dec, dec_len, _ = x_sample.shape
    assert dec_len == 1
    page_size = cache_k.shape[2]
    past_len = page_table.shape[1] * page_size
    assert past_len % MOBA_BLOCK == 0 and MOBA_BLOCK % page_size == 0
    assert past_len // MOBA_BLOCK >= MOBA_TOPK and seq % MOBA_BLOCK == 0

    g_mix2 = g_mix[0][None, :]
    g_ffn2 = g_ffn[0][None, :]
    g_final2 = g_final[None, :]
    w_in_f, w_out_f = w_in[0], w_out[0]
    w_out_bf = w_out_f.astype(BF16)
    pad = LANES - N_EXPERTS - N_EXPERT_GROUPS
    w_r = jnp.concatenate([w_router_expert[0], w_router_group[0], jnp.zeros((d, pad), F32)], axis=1)
    b_r = jnp.concatenate([b_router_expert[0], b_router_group[0], jnp.zeros((pad,), F32)])[None, :]
    cw = conv_w[0]

    cos_p, sin_p = _rope_angles(jnp.arange(seq, dtype=jnp.int32))
    conv0 = jnp.zeros((bsz, CONV_K - 1, CONV_CH), F32)
    qt_p, kt_p, vt_p, kb_p, vtb_p, kmeans_p, cy_p, conv_p = _prompt_inproj(
        x_prompt, g_mix2, w_in_f, (cos_p.T, sin_p.T), cw, conv0)
    att_p = _prompt_attn(qt_p, kb_p, vtb_p, kmeans_p)
    n_p = bsz * seq
    hres_p, hn_p, route_p, counts_p = _outproj_router(
        x_prompt.reshape(n_p, d), att_p.reshape(n_p, ATT_WIDTH), cy_p.reshape(n_p, CONV_CH),
        w_out_bf, g_ffn2, w_r, b_r, jnp.zeros((1, LANES), F32), tm=512, precise=False)

    cos_s, sin_s = _rope_angles(past_len + jnp.arange(dec_len, dtype=jnp.int32))
    x_s = x_sample.reshape(n_dec, d)
    prev0, prev1 = state_conv[0, :, 0, :], state_conv[0, :, 1, :]
    q_s, k_s, v_s, cy_s, u_s = _sample_inproj(x_s, g_mix2, w_in_f, _rope_row_tables(cos_s, sin_s),
                                              cw, prev0, prev1)
    cache_kt = jnp.transpose(cache_k[0], (0, 2, 3, 1))
    cache_vt = jnp.transpose(cache_v[0], (0, 2, 3, 1))
    q_bcast = jnp.broadcast_to(q_s.reshape(n_dec, N_HEADS, HEAD_DIM, 1),
                               (n_dec, N_HEADS, HEAD_DIM, page_size))
    sel = _sample_select(page_table, q_bcast, cache_kt)
    sel_flat = sel[:, :, :MOBA_TOPK].reshape(-1)
    q3 = q_s.reshape(n_dec, 1, ATT_WIDTH)
    att_s = _sample_attn(sel_flat, page_table, q3, q_bcast, k_s.reshape(n_dec, 1, ATT_WIDTH),
                         v_s.reshape(n_dec, 1, ATT_WIDTH), cache_kt, cache_vt)
    hres_s, hn_s, route_s, counts = _outproj_router(
        x_s, att_s.reshape(n_dec, ATT_WIDTH), cy_s, w_out_f, g_ffn2, w_r, b_r, counts_p,
        tm=n_dec, precise=True)

    n_assign = EXPERT_TOPK * (n_p + n_dec)
    n_tiles = -(-(n_assign + N_EXPERTS * (MOE_TILE - 1)) // MOE_TILE)
    offsets, tile_expert, n_active, fill_starts = _moe_plan(counts, n_tiles)
    d1_p, d2_p = _moe_dests(route_p, offsets)
    d1_s, d2_s = _moe_dests(route_s, offsets)
    xs = _moe_dispatch(fill_starts, n_active, d1_p, d2_p, hn_p, None, n_tiles, tm=512)
    xs = _moe_dispatch(fill_starts, n_active, d1_s, d2_s, hn_s, xs, n_tiles, tm=n_dec)
    ys = _moe_ffn(tile_expert, n_active, xs, w_gate[0], w_up[0], w_down[0])
    y_p = _moe_combine(d1_p, d2_p, route_p, hres_p, g_final2, ys, tm=256)
    y_s = _moe_combine(d1_s, d2_s, route_s, hres_s, g_final2, ys, tm=n_dec)

    conv_s = jnp.stack([prev1, u_s], axis=1)
    to_bthd = lambda t: jnp.transpose(t.reshape(bsz, N_HEADS, HEAD_DIM, seq), (0, 3, 1, 2))[None]
    return (y_p.reshape(bsz, seq, d), y_s.reshape(n_dec, dec_len, d),
            to_bthd(kt_p), to_bthd(vt_p), conv_p[None],
            k_s.reshape(1, n_dec, dec_len, N_HEADS, HEAD_DIM),
            v_s.reshape(1, n_dec, dec_len, N_HEADS, HEAD_DIM),
            conv_s[None])
```

```python
import functools

import jax
import jax.numpy as jnp
from jax import lax
from jax.experimental import pallas as pl
from jax.experimental.pallas import tpu as pltpu

F32 = jnp.float32
BF16 = jnp.bfloat16
HIGHEST = lax.Precision.HIGHEST

D_MODEL = 1024
HEAD_DIM = 64
N_HEADS = 8
ATT_WIDTH = N_HEADS * HEAD_DIM
CONV_CH = D_MODEL - ATT_WIDTH
ROPE_DIM = HEAD_DIM // 4
ROPE_HALF = ROPE_DIM // 2
ROPE_THETA = 500000.0
MOBA_BLOCK = 256
MOBA_TOPK = 3
ATT_SCALE = HEAD_DIM ** -0.5
CONV_K = 3
N_EXPERT_GROUPS = 4
EXPERTS_PER_GROUP = 8
N_EXPERTS = N_EXPERT_GROUPS * EXPERTS_PER_GROUP
EXPERT_TOPK = 2
EXPERT_FF = D_MODEL // 4
RMS_EPS = 1e-6

LANES = 128
SUBLANES = 8
HEADS_PER_VREG = LANES // HEAD_DIM
TOKEN_ROWS = D_MODEL // LANES
MXU_WIDTH = 256
MOE_TILE = 512
BF16_SUBLANES = 16
V_AUG_ROWS = HEAD_DIM + BF16_SUBLANES
LOG2_E = 1.4426950408889634
DMA_QUEUES = 2
ISSUE_UNROLL = 8
VMEM_LIMIT = 56 * 1024 * 1024

NEG_INF = float("-inf")


def _rms(x, g):
    ms = jnp.mean(x * x, axis=-1, keepdims=True)
    return x * lax.rsqrt(ms + RMS_EPS) * g


def _mm(a, w, precise):
    if precise:
        return jnp.dot(a, w, precision=HIGHEST, preferred_element_type=F32)
    return jnp.dot(a.astype(BF16), w, preferred_element_type=F32)


NN_DIMS = (((1,), (0,)), ((), ()))
NT_DIMS = (((1,), (1,)), ((), ()))


def _dot_f32(a, b, dims, full):
    if full:
        return lax.dot_general(a, b, dims, precision=HIGHEST, preferred_element_type=F32)
    a_hi, b_hi = a.astype(BF16), b.astype(BF16)
    a_lo = (a - a_hi.astype(F32)).astype(BF16)
    b_lo = (b - b_hi.astype(F32)).astype(BF16)
    dot = functools.partial(lax.dot_general, dimension_numbers=dims, preferred_element_type=F32)
    n = b.shape[1]
    if dims == NN_DIMS and 2 * n <= MXU_WIDTH:
        both = dot(a_hi, jnp.concatenate([b_hi, b_lo], axis=1))
        return both[:, :n] + (both[:, n:] + dot(a_lo, b_hi))
    return dot(a_hi, b_hi) + (dot(a_hi, b_lo) + dot(a_lo, b_hi))


def _rope_rows(a, cos, sin_lo, sin_hi):
    n = a.shape[-1]
    return (a * cos + pltpu.roll(a, n - ROPE_HALF, 1) * sin_lo
            + pltpu.roll(a, ROPE_HALF, 1) * sin_hi)


def _tile_lanes(t, reps):
    return jnp.concatenate([t] * reps, axis=-1)


def _rope_angles(pos):
    inv = ROPE_THETA ** (-jnp.arange(0, ROPE_DIM, 2, dtype=F32) / ROPE_DIM)
    ang = pos.astype(F32)[:, None] * inv[None, :]
    return jnp.cos(ang), jnp.sin(ang)


def _rope_row_tables(cos, sin):
    rows = cos.shape[0]
    ones = jnp.ones((rows, HEAD_DIM - ROPE_DIM), F32)
    zeros = jnp.zeros((rows, HEAD_DIM - ROPE_HALF), F32)
    c = jnp.concatenate([cos, cos, ones], axis=1)
    s_lo = jnp.concatenate([-sin, zeros], axis=1)
    s_hi = jnp.concatenate([jnp.zeros((rows, ROPE_HALF), F32), sin,
                            jnp.zeros((rows, HEAD_DIM - ROPE_DIM), F32)], axis=1)
    rep = lambda t: jnp.concatenate([t] * HEADS_PER_VREG, axis=1)
    return rep(c), rep(s_lo), rep(s_hi)


def _rope_cols(ref, cos_t, sin_t):
    for h in range(N_HEADS):
        r = h * HEAD_DIM
        x1 = ref[r:r + ROPE_HALF, :]
        x2 = ref[r + ROPE_HALF:r + ROPE_DIM, :]
        ref[r:r + ROPE_HALF, :] = x1 * cos_t - x2 * sin_t
        ref[r + ROPE_HALF:r + ROPE_DIM, :] = x2 * cos_t + x1 * sin_t


def _prompt_inproj_kernel(x_ref, g_ref, wqt_ref, wkt_ref, wvt_ref, wc_ref,
                          cost_ref, sint_ref, cw_ref, prev_ref,
                          qt_ref, kt_ref, vt_ref, kb_ref, vtb_ref, km_ref, cy_ref, cn_ref,
                          ubuf, halo, km_acc, *, tt):
    t = pl.program_id(1)
    blk = MOBA_BLOCK
    xn = _rms(x_ref[...], g_ref[...]).astype(BF16)

    qt_ref[...] = lax.dot_general(wqt_ref[...], xn, NT_DIMS, preferred_element_type=F32)
    _rope_cols(qt_ref, cost_ref[...], sint_ref[...])
    kt_ref[...] = lax.dot_general(wkt_ref[...], xn, NT_DIMS, preferred_element_type=F32)
    _rope_cols(kt_ref, cost_ref[...], sint_ref[...])
    vt = lax.dot_general(wvt_ref[...], xn, NT_DIMS, preferred_element_type=F32)
    vt_ref[...] = vt
    for i in range(tt // blk):
        vtb_ref[i] = vt[:, i * blk:(i + 1) * blk].astype(BF16)

    k = kt_ref[...].T
    kb_ref[...] = k.astype(BF16)

    @pl.when(t == 0)
    def _():
        km_acc[...] = jnp.zeros_like(km_acc)

    rid = lax.broadcasted_iota(jnp.int32, km_acc.shape, 0)
    km = km_acc[...]
    for i in range(tt // blk):
        mean_i = jnp.mean(k[i * blk:(i + 1) * blk, :], axis=0, keepdims=True)
        km = jnp.where(rid == t * (tt // blk) + i, mean_i, km)
    km_acc[...] = km
    km_ref[...] = km

    def proj(i):
        return jnp.dot(xn, wc_ref[:, i * CONV_CH:(i + 1) * CONV_CH], preferred_element_type=F32)

    b_gate = proj(0)
    u = proj(1) * proj(2)

    @pl.when(t == 0)
    def _():
        ubuf[SUBLANES - (CONV_K - 1):SUBLANES, :] = prev_ref[...]

    @pl.when(t > 0)
    def _():
        ubuf[0:SUBLANES, :] = halo[...]

    ubuf[SUBLANES:SUBLANES + tt, :] = u
    cw = cw_ref[...]
    conv = (cw[0:1, :] * ubuf[SUBLANES - 2:SUBLANES - 2 + tt, :]
            + cw[1:2, :] * ubuf[SUBLANES - 1:SUBLANES - 1 + tt, :]
            + cw[2:3, :] * u)
    cy_ref[...] = (b_gate * conv).astype(cy_ref.dtype)
    halo[...] = ubuf[tt:tt + SUBLANES, :]
    cn_ref[...] = ubuf[tt + SUBLANES - (CONV_K - 1):tt + SUBLANES, :]


def _prompt_inproj(x, g_mix, w_in, col_tables, conv_w, conv_prev, tt=512):
    bsz, seq, d = x.shape
    nt = seq // tt
    nb = seq // MOBA_BLOCK
    w_bf = w_in.astype(BF16)
    wqt = w_bf[:, :ATT_WIDTH].T
    wkt = w_bf[:, ATT_WIDTH:2 * ATT_WIDTH].T
    wvt = w_bf[:, 2 * ATT_WIDTH:3 * ATT_WIDTH].T
    wc = w_bf[:, 3 * ATT_WIDTH:]
    row_spec = lambda width: pl.BlockSpec((None, tt, width), lambda b, t: (b, t, 0))
    col_spec = pl.BlockSpec((None, ATT_WIDTH, tt), lambda b, t: (b, 0, t))
    tabt_spec = pl.BlockSpec((ROPE_HALF, tt), lambda b, t: (0, t))
    full = lambda shape: pl.BlockSpec(shape, lambda b, t: (0,) * len(shape))
    per_b = lambda shape: pl.BlockSpec((None,) + shape, lambda b, t: (b,) + (0,) * len(shape))
    vtb_spec = pl.BlockSpec((None, tt // MOBA_BLOCK, ATT_WIDTH, MOBA_BLOCK),
                            lambda b, t: (b, t, 0, 0))
    col_shape = jax.ShapeDtypeStruct((bsz, ATT_WIDTH, seq), F32)
    return pl.pallas_call(
        functools.partial(_prompt_inproj_kernel, tt=tt),
        grid=(bsz, nt),
        in_specs=[row_spec(d), full((1, d)), full(wqt.shape), full(wkt.shape), full(wvt.shape),
                  full(wc.shape), tabt_spec, tabt_spec,
                  full((CONV_K, CONV_CH)), per_b((CONV_K - 1, CONV_CH))],
        out_specs=[col_spec, col_spec, col_spec, row_spec(ATT_WIDTH), vtb_spec,
                   per_b((nb, ATT_WIDTH)), row_spec(CONV_CH), per_b((CONV_K - 1, CONV_CH))],
        out_shape=[col_shape, col_shape, col_shape,
                   jax.ShapeDtypeStruct((bsz, seq, ATT_WIDTH), BF16),
                   jax.ShapeDtypeStruct((bsz, nb, ATT_WIDTH, MOBA_BLOCK), BF16),
                   jax.ShapeDtypeStruct((bsz, nb, ATT_WIDTH), F32),
                   jax.ShapeDtypeStruct((bsz, seq, CONV_CH), BF16),
                   jax.ShapeDtypeStruct((bsz, CONV_K - 1, CONV_CH), F32)],
        scratch_shapes=[pltpu.VMEM((tt + SUBLANES, CONV_CH), F32),
                        pltpu.VMEM((SUBLANES, CONV_CH), F32),
                        pltpu.VMEM((nb, ATT_WIDTH), F32)],
        compiler_params=pltpu.CompilerParams(
            dimension_semantics=("arbitrary", "arbitrary"), vmem_limit_bytes=VMEM_LIMIT),
        name="prompt_inproj",
    )(x, g_mix, wqt, wkt, wvt, wc, *col_tables, conv_w, conv_prev)


def _prompt_attn_kernel(pt_ref, qt_ref, kb_ref, vtb_ref, km_ref, qdec_ref, *refs, nb, n_pages):
    page_refs = refs[:n_pages]
    o_ref, sel_ref, s_ref, ot_ref, va_ref, part_ref = refs[n_pages:]
    _select_past_blocks(qdec_ref, page_refs, part_ref, sel_ref)
    blk = MOBA_BLOCK
    means = km_ref[...]
    ones_row = lax.broadcasted_iota(jnp.int32, (V_AUG_ROWS - HEAD_DIM, blk), 0) == 0
    for j in range(nb):
        for h in range(HEADS_PER_VREG):
            va_ref[j, h, 0:HEAD_DIM, :] = vtb_ref[j, h * HEAD_DIM:(h + 1) * HEAD_DIM, :]
            va_ref[j, h, HEAD_DIM:V_AUG_ROWS, :] = jnp.where(ones_row, 1.0, 0.0).astype(BF16)
    feat = lax.broadcasted_iota(jnp.int32, (LANES, 1), 0)
    bid = lax.broadcasted_iota(jnp.int32, (nb, blk), 0)
    key = lax.broadcasted_iota(jnp.int32, (blk, blk), 0)
    qry = lax.broadcasted_iota(jnp.int32, (blk, blk), 1)

    def block_bias(qth, qi):
        if qi <= MOBA_TOPK:
            return [None] * qi
        gates = jnp.dot(means, qth, precision=HIGHEST, preferred_element_type=F32)
        gates = jnp.where(bid < qi, gates, NEG_INF)
        rows = []
        for j in range(qi):
            gj = gates[j:j + 1, :]
            beats = (gates > gj) | ((gates == gj) & (bid < j))
            cnt = jnp.sum(beats.astype(F32), axis=0, keepdims=True)
            rows.append(jnp.where(cnt < MOBA_TOPK, 0.0, NEG_INF))
        return rows

    def scores_pass(qi, h, slot, state):
        qt = qt_ref[:, qi * blk:(qi + 1) * blk]
        qth = jnp.where((feat >= h * HEAD_DIM) & (feat < (h + 1) * HEAD_DIM), qt, 0.0)
        bias = block_bias(qth, qi)
        qs = (qth * (ATT_SCALE * LOG2_E)).astype(BF16)
        m = None
        for j in range(qi + 1):
            s = jnp.dot(kb_ref[j * blk:(j + 1) * blk, :], qs,
                        preferred_element_type=F32)
            if j == qi:
                s = jnp.where(key <= qry, s, NEG_INF)
            elif bias[j] is not None:
                s = s + bias[j]
            s_ref[slot, j] = s
            m_blk = jnp.max(s, axis=0, keepdims=True)
            m = m_blk if m is None else jnp.maximum(m, m_blk)
            yield
        state["m"] = m

    def values_pass(qi, h, slot, state):
        m = state["m"]
        acc = None
        for j in range(qi + 1):
            p = jnp.exp2(s_ref[slot, j] - m).astype(BF16)
            pv = jnp.dot(va_ref[j, h], p, preferred_element_type=F32)
            acc = pv if acc is None else acc + pv
            yield
        ot_ref[h * HEAD_DIM:(h + 1) * HEAD_DIM, qi * blk:(qi + 1) * blk] = (
            acc[0:HEAD_DIM, :] / acc[HEAD_DIM:HEAD_DIM + 1, :])

    def run_interleaved(*gens):
        live = [g for g in gens if g is not None]
        while live:
            for g in list(live):
                if next(g, "done") == "done":
                    live.remove(g)

    pending = None
    for i, (qi, h) in enumerate((qi, h) for qi in range(nb) for h in range(HEADS_PER_VREG)):
        state = {}
        run_interleaved(scores_pass(qi, h, i % 2, state), pending)
        pending = values_pass(qi, h, i % 2, state)
    run_interleaved(pending)
    o_ref[...] = ot_ref[...].T.astype(o_ref.dtype)


def _prompt_attn(qt, kb, vtb, kmeans, page_table, q_dec, cache_kt):
    bsz, seq, _ = kb.shape
    nb = seq // MOBA_BLOCK
    n_pairs = ATT_WIDTH // LANES
    n_dec, n_pages = page_table.shape
    assert n_dec == bsz * n_pairs, "one decode sequence per prompt attention grid step"
    _, n_heads, head_dim, page_size = cache_kt.shape
    n_blocks = n_pages * page_size // MOBA_BLOCK
    dec = lambda b, hp: b * n_pairs + hp

    def page_spec(i):
        return pl.BlockSpec((None, n_heads, head_dim, page_size),
                            lambda b, hp, pt: (pt[dec(b, hp), i], 0, 0, 0))

    return pl.pallas_call(
        functools.partial(_prompt_attn_kernel, nb=nb, n_pages=n_pages),
        grid_spec=pltpu.PrefetchScalarGridSpec(
            num_scalar_prefetch=1,
            grid=(bsz, n_pairs),
            in_specs=[pl.BlockSpec((None, LANES, seq), lambda b, hp, pt: (b, hp, 0)),
                      pl.BlockSpec((None, seq, LANES), lambda b, hp, pt: (b, 0, hp)),
                      pl.BlockSpec((None, nb, LANES, MOBA_BLOCK), lambda b, hp, pt: (b, 0, hp, 0)),
                      pl.BlockSpec((None, nb, LANES), lambda b, hp, pt: (b, 0, hp)),
                      pl.BlockSpec((None, n_heads, head_dim, page_size),
                                   lambda b, hp, pt: (dec(b, hp), 0, 0, 0))]
                     + [page_spec(i) for i in range(n_pages)],
            out_specs=[pl.BlockSpec((None, seq, LANES), lambda b, hp, pt: (b, 0, hp)),
                       pl.BlockSpec((None, n_heads, LANES), lambda b, hp, pt: (dec(b, hp), 0, 0))],
            scratch_shapes=[pltpu.VMEM((2, nb, MOBA_BLOCK, MOBA_BLOCK), F32),
                            pltpu.VMEM((LANES, seq), F32),
                            pltpu.VMEM((nb, HEADS_PER_VREG, V_AUG_ROWS, MOBA_BLOCK), BF16),
                            pltpu.VMEM((n_blocks, n_heads, LANES), F32)],
        ),
        out_shape=[jax.ShapeDtypeStruct((bsz, seq, ATT_WIDTH), BF16),
                   jax.ShapeDtypeStruct((n_dec, n_heads, LANES), jnp.int32)],
        compiler_params=pltpu.CompilerParams(
            dimension_semantics=("arbitrary", "arbitrary"), vmem_limit_bytes=VMEM_LIMIT),
        name="prompt_attn",
    )(page_table, qt, kb, vtb, kmeans, q_dec, *([cache_kt] * n_pages))


def _outproj_router_kernel(x_ref, att_ref, cy_ref, wo_ref, g_ref, wr_ref, br_ref, cnt_in_ref,
                           hres_ref, hn_ref, route_ref, idx_ref, cnt_ref, tri_ref, *, precise):
    step = pl.program_id(0)
    tm = x_ref.shape[0]
    hres = (x_ref[...] + _mm(att_ref[...], wo_ref[0:ATT_WIDTH, :], precise)
            + _mm(cy_ref[...], wo_ref[ATT_WIDTH:, :], precise))
    hres_ref[...] = hres
    hn = _rms(hres, g_ref[...])
    for s in range(TOKEN_ROWS):
        hn_ref[pl.ds(s, tm, stride=TOKEN_ROWS), :] = hn[:, s * LANES:(s + 1) * LANES]

    @pl.when(step == 0)
    def _():
        cnt_ref[...] = cnt_in_ref[...]
        r = lax.broadcasted_iota(jnp.int32, (tm, tm), 0)
        c = lax.broadcasted_iota(jnp.int32, (tm, tm), 1)
        tri_ref[...] = jnp.where(c < r, 1.0, 0.0).astype(BF16)

    lo = _dot_f32(hn, wr_ref[...], NN_DIMS, precise) + br_ref[...]
    lane = lax.broadcasted_iota(jnp.int32, lo.shape, 1)
    is_group = (lane >= N_EXPERTS) & (lane < N_EXPERTS + N_EXPERT_GROUPS)
    lg = jnp.where(is_group, lo, NEG_INF)
    mg = jnp.max(lg, axis=1, keepdims=True)
    g_lane = jnp.min(jnp.where(lg == mg, lane, LANES), axis=1, keepdims=True)
    pg = 1.0 / jnp.sum(jnp.exp(lg - mg), axis=1, keepdims=True)
    e_lo = (g_lane - N_EXPERTS) * EXPERTS_PER_GROUP
    in_g = (lane >= e_lo) & (lane < e_lo + EXPERTS_PER_GROUP)
    le = jnp.where(in_g, lo, NEG_INF)
    m1 = jnp.max(le, axis=1, keepdims=True)
    i1 = jnp.min(jnp.where(le == m1, lane, LANES), axis=1, keepdims=True)
    le2 = jnp.where(lane == i1, NEG_INF, le)
    m2 = jnp.max(le2, axis=1, keepdims=True)
    i2 = jnp.min(jnp.where(le2 == m2, lane, LANES), axis=1, keepdims=True)
    e2 = jnp.exp(m2 - m1)
    w1 = pg / (1.0 + e2)
    w2 = pg * e2 / (1.0 + e2)

    oh1, oh2 = lane == i1, lane == i2
    assigned = jnp.where(oh1 | oh2, 1.0, 0.0)
    before = cnt_ref[...] + jnp.dot(tri_ref[...], assigned.astype(BF16),
                                    preferred_element_type=F32)
    r1 = jnp.sum(jnp.where(oh1, before, 0.0), axis=1, keepdims=True)
    r2 = jnp.sum(jnp.where(oh2, before, 0.0), axis=1, keepdims=True)
    cnt_ref[...] += jnp.sum(assigned, axis=0, keepdims=True)
    cols = (i1.astype(F32), i2.astype(F32), r1, r2, w1, w2)
    route = jnp.zeros(lo.shape, F32)
    for c, v in enumerate(cols):
        route = jnp.where(lane == c, v, route)
    route_ref[...] = route
    pick = (lax.broadcasted_iota(jnp.int32, (SUBLANES, LANES), 0)
            == lax.broadcasted_iota(jnp.int32, (SUBLANES, LANES), 1))
    idx_ref[...] = lax.dot_general(jnp.where(pick, 1.0, 0.0), route, NT_DIMS, precision=HIGHEST,
                                   preferred_element_type=F32).astype(jnp.int32)


def _outproj_router(x, att, cy, w_out, g_ffn, w_r, b_r, counts_in, tm, precise):
    n, d = x.shape
    rows = lambda width: pl.BlockSpec((tm, width), lambda i: (i, 0))
    full = lambda shape: pl.BlockSpec(shape, lambda i: (0,) * len(shape))
    return pl.pallas_call(
        functools.partial(_outproj_router_kernel, precise=precise),
        grid=(n // tm,),
        in_specs=[rows(d), rows(ATT_WIDTH), rows(CONV_CH), full(w_out.shape), full((1, d)),
                  full(w_r.shape), full((1, LANES)), full((1, LANES))],
        out_specs=[rows(d), pl.BlockSpec((tm * TOKEN_ROWS, LANES), lambda i: (i, 0)), rows(LANES),
                   pl.BlockSpec((None, SUBLANES, tm), lambda i: (i, 0, 0)), full((1, LANES))],
        out_shape=[jax.ShapeDtypeStruct((n, d), F32),
                   jax.ShapeDtypeStruct((n * TOKEN_ROWS, LANES), F32),
                   jax.ShapeDtypeStruct((n, LANES), F32),
                   jax.ShapeDtypeStruct((n // tm, SUBLANES, tm), jnp.int32),
                   jax.ShapeDtypeStruct((1, LANES), F32)],
        scratch_shapes=[pltpu.VMEM((tm, tm), BF16)],
        compiler_params=pltpu.CompilerParams(
            dimension_semantics=("arbitrary",), vmem_limit_bytes=VMEM_LIMIT),
        name="outproj_router_precise" if precise else "outproj_router",
    )(x, att, cy, w_out, g_ffn, w_r, b_r, counts_in)


def _token_rows(ref, t):
    return ref.at[pl.ds(pl.multiple_of(t * TOKEN_ROWS, TOKEN_ROWS), TOKEN_ROWS), :]


def _start_token_copies(copies, n):
    def start(g, carry):
        for u in range(ISSUE_UNROLL):
            for c in copies(g * ISSUE_UNROLL + u):
                c.start(priority=u % DMA_QUEUES)
        return carry

    lax.fori_loop(0, n // ISSUE_UNROLL, start, 0)


def _moe_slots(idx, offsets):
    experts, inner = idx[:, :EXPERT_TOPK, :], idx[:, EXPERT_TOPK:2 * EXPERT_TOPK, :]
    onehot = experts[..., None] == jnp.arange(N_EXPERTS, dtype=jnp.int32)
    return inner + jnp.sum(jnp.where(onehot, offsets, 0), axis=-1)


def _moe_dispatch_kernel(fill_ref, na_ref, slot_ref, hn_ref, *refs, tm, n_tiles, zero_fill):
    xs_ref, zbuf, sem = refs[-3:]
    step = pl.program_id(0)

    if zero_fill:
        @pl.when(step == 0)
        def _():
            zbuf[...] = jnp.zeros_like(zbuf)

            def fill(slot):
                return pltpu.make_async_copy(
                    zbuf, xs_ref.at[pl.ds(pl.multiple_of(slot * TOKEN_ROWS, TOKEN_ROWS),
                                          MOE_TILE * TOKEN_ROWS), :], sem.at[1])

            def start_fill(i, carry):
                fill(i * MOE_TILE).start()
                return carry

            def wait_fill(i, carry):
                fill(i * MOE_TILE).wait()
                return carry

            def group_fills(act):
                for e in range(N_EXPERTS):
                    @pl.when(fill_ref[e] >= 0)
                    def _():
                        act(fill(fill_ref[e]))

            group_fills(lambda c: c.start())
            lax.fori_loop(na_ref[0], n_tiles, start_fill, 0)
            group_fills(lambda c: c.wait())
            lax.fori_loop(na_ref[0], n_tiles, wait_fill, 0)

    def copies(t):
        src = _token_rows(hn_ref, t)
        return tuple(pltpu.make_async_copy(src, _token_rows(xs_ref, slot_ref[k, t]), sem.at[0])
                     for k in range(EXPERT_TOPK))

    _start_token_copies(copies, tm)
    for _ in range(EXPERT_TOPK):
        pltpu.make_async_copy(hn_ref, xs_ref.at[pl.ds(0, tm * TOKEN_ROWS), :], sem.at[0]).wait()


def _moe_dispatch(fill_starts, n_active, slots, hn_tiles, xs_prev, n_tiles):
    n_steps, _, tm = slots.shape
    n = n_steps * tm
    zero_fill = xs_prev is None
    hbm = pl.BlockSpec(memory_space=pl.ANY)
    in_specs = [pl.BlockSpec((None, EXPERT_TOPK, tm), lambda i, *_: (i, 0, 0),
                             memory_space=pltpu.SMEM),
                pl.BlockSpec((tm * TOKEN_ROWS, LANES), lambda i, *_: (i, 0))]
    args = [fill_starts, n_active, slots, hn_tiles]
    aliases = {}
    if not zero_fill:
        in_specs.append(hbm)
        args.append(xs_prev)
        aliases = {len(args) - 1: 0}
    return pl.pallas_call(
        functools.partial(_moe_dispatch_kernel, tm=tm, n_tiles=n_tiles, zero_fill=zero_fill),
        grid_spec=pltpu.PrefetchScalarGridSpec(
            num_scalar_prefetch=2,
            grid=(n // tm,),
            in_specs=in_specs,
            out_specs=hbm,
            scratch_shapes=[pltpu.VMEM((MOE_TILE * TOKEN_ROWS, LANES), F32),
                            pltpu.SemaphoreType.DMA((2,))],
        ),
        out_shape=jax.ShapeDtypeStruct((n_tiles * MOE_TILE * TOKEN_ROWS, LANES), F32),
        input_output_aliases=aliases,
        compiler_params=pltpu.CompilerParams(
            dimension_semantics=("arbitrary",), vmem_limit_bytes=VMEM_LIMIT),
        name="moe_dispatch" if zero_fill else "moe_dispatch_more",
    )(*args)


def _moe_ffn_kernel(te_ref, na_ref, xs_ref, wg_ref, wu_ref, wd_ref, ys_ref, wg_b, wu_b, wd_b):
    i = pl.program_id(0)
    tg = MOE_TILE
    active = i < na_ref[0]

    @pl.when(active)
    def _():
        @pl.when((i == 0) | (te_ref[i] != te_ref[jnp.maximum(i - 1, 0)]))
        def _():
            wg_b[...] = wg_ref[...].astype(BF16)
            wu_b[...] = wu_ref[...].astype(BF16)
            wd_b[...] = wd_ref[...].astype(BF16)

        x = jnp.concatenate(
            [xs_ref[pl.ds(s, tg, stride=TOKEN_ROWS), :].astype(BF16) for s in range(TOKEN_ROWS)],
            axis=1)
        hg = jnp.dot(x, wg_b[...], preferred_element_type=F32)
        hu = jnp.dot(x, wu_b[...], preferred_element_type=F32)
        h = hg * (1.0 / (1.0 + jnp.exp(-hg))) * hu
        y = jnp.dot(h.astype(BF16), wd_b[...], preferred_element_type=F32)
        for s in range(TOKEN_ROWS):
            ys_ref[pl.ds(s, tg, stride=TOKEN_ROWS), :] = y[:, s * LANES:(s + 1) * LANES]

    @pl.when(jnp.logical_not(active))
    def _():
        ys_ref[...] = jnp.zeros_like(ys_ref)


def _moe_ffn(tile_expert, n_active, xs, w_gate, w_up, w_down):
    n_tiles = tile_expert.shape[0]
    tile_rows = MOE_TILE * TOKEN_ROWS
    d, ff = w_gate.shape[1:]
    last = lambda na: na[0] - 1
    w_spec = lambda shape: pl.BlockSpec(
        (None,) + shape, lambda i, te, na: (te[jnp.minimum(i, last(na))], 0, 0))
    return pl.pallas_call(
        _moe_ffn_kernel,
        grid_spec=pltpu.PrefetchScalarGridSpec(
            num_scalar_prefetch=2,
            grid=(n_tiles,),
            in_specs=[pl.BlockSpec((tile_rows, LANES),
                                   lambda i, te, na: (jnp.minimum(i, last(na)), 0)),
                      w_spec((d, ff)), w_spec((d, ff)), w_spec((ff, d))],
            out_specs=pl.BlockSpec((tile_rows, LANES), lambda i, te, na: (i, 0)),
            scratch_shapes=[pltpu.VMEM((d, ff), BF16), pltpu.VMEM((d, ff), BF16),
                            pltpu.VMEM((ff, d), BF16)],
        ),
        out_shape=jax.ShapeDtypeStruct(xs.shape, F32),
        compiler_params=pltpu.CompilerParams(
            dimension_semantics=("arbitrary",), vmem_limit_bytes=VMEM_LIMIT),
        name="moe_ffn",
    )(tile_expert, n_active, xs, w_gate, w_up, w_down)


def _moe_combine_kernel(slot_ref, route_ref, hres_ref, gf_ref, ys_ref, y_ref, buf, sem, *, tm):
    def copies(t):
        return tuple(pltpu.make_async_copy(_token_rows(ys_ref, slot_ref[k, t]),
                                           _token_rows(buf.at[k], t), sem.at[k])
                     for k in range(EXPERT_TOPK))

    _start_token_copies(copies, tm)
    for k in range(EXPERT_TOPK):
        pltpu.make_async_copy(ys_ref.at[pl.ds(0, tm * TOKEN_ROWS), :], buf.at[k], sem.at[k]).wait()

    route = route_ref[...]
    w1, w2 = route[:, 4:5], route[:, 5:6]
    for s in range(TOKEN_ROWS):
        cols = slice(s * LANES, (s + 1) * LANES)
        y_ref[:, cols] = (hres_ref[:, cols]
                          + w1 * buf[0, pl.ds(s, tm, stride=TOKEN_ROWS), :]
                          + w2 * buf[1, pl.ds(s, tm, stride=TOKEN_ROWS), :])
    y_ref[...] = _rms(y_ref[...], gf_ref[...])


def _moe_combine(slots, route, hres, g_final, ys):
    n, d = hres.shape
    tm = slots.shape[2]
    rows = lambda width: pl.BlockSpec((tm, width), lambda i: (i, 0))
    return pl.pallas_call(
        functools.partial(_moe_combine_kernel, tm=tm),
        grid=(n // tm,),
        in_specs=[pl.BlockSpec((None, EXPERT_TOPK, tm), lambda i: (i, 0, 0),
                               memory_space=pltpu.SMEM),
                  rows(LANES), rows(d), pl.BlockSpec((1, d), lambda i: (0, 0)),
                  pl.BlockSpec(memory_space=pl.ANY)],
        out_specs=rows(d),
        out_shape=jax.ShapeDtypeStruct((n, d), F32),
        scratch_shapes=[pltpu.VMEM((EXPERT_TOPK, tm * TOKEN_ROWS, LANES), F32),
                        pltpu.SemaphoreType.DMA((EXPERT_TOPK,))],
        compiler_params=pltpu.CompilerParams(
            dimension_semantics=("arbitrary",), vmem_limit_bytes=VMEM_LIMIT),
        name="moe_combine",
    )(slots, route, hres, g_final, ys)


def _moe_plan(counts, n_tiles):
    counts = counts[0, :N_EXPERTS].astype(jnp.int32)
    padded = (counts + MOE_TILE - 1) // MOE_TILE * MOE_TILE
    ends = jnp.cumsum(padded)
    offsets = ends - padded
    n_active = (ends[-1:] // MOE_TILE).astype(jnp.int32)
    tile_start = jnp.arange(n_tiles, dtype=jnp.int32) * MOE_TILE
    tile_expert = jnp.minimum(jnp.sum(tile_start[:, None] >= ends[None, :], axis=1), N_EXPERTS - 1)
    fill_starts = jnp.where(padded > 0, ends - MOE_TILE, -1)
    return offsets, tile_expert.astype(jnp.int32), n_active, fill_starts.astype(jnp.int32)


def _sample_inproj_kernel(x_ref, g_ref, w_ref, cos_ref, slo_ref, shi_ref, cw_ref, p0_ref, p1_ref,
                          q_ref, k_ref, v_ref, cy_ref, u_ref):
    xn = _rms(x_ref[...], g_ref[...])
    reps = ATT_WIDTH // LANES
    cos = _tile_lanes(cos_ref[...], reps)
    slo = _tile_lanes(slo_ref[...], reps)
    shi = _tile_lanes(shi_ref[...], reps)

    def proj(i):
        return jnp.dot(xn, w_ref[:, i * ATT_WIDTH:(i + 1) * ATT_WIDTH], precision=HIGHEST,
                       preferred_element_type=F32)

    q_ref[...] = _rope_rows(proj(0), cos, slo, shi)
    k_ref[...] = _rope_rows(proj(1), cos, slo, shi)
    v_ref[...] = proj(2)
    b_gate = proj(3)
    u = proj(4) * proj(5)
    u_ref[...] = u
    cw = cw_ref[...]
    cy_ref[...] = b_gate * (cw[0:1, :] * p0_ref[...] + cw[1:2, :] * p1_ref[...] + cw[2:3, :] * u)


def _sample_inproj(x, g_mix, w_in, tables, conv_w, prev0, prev1):
    n = x.shape[0]
    out = jax.ShapeDtypeStruct((n, ATT_WIDTH), F32)
    return pl.pallas_call(
        _sample_inproj_kernel,
        out_shape=[out] * 5,
        compiler_params=pltpu.CompilerParams(vmem_limit_bytes=VMEM_LIMIT),
        name="sample_inproj",
    )(x, g_mix, w_in, *tables, conv_w, prev0, prev1)


def _select_past_blocks(q_ref, page_refs, part, sel_ref):
    n_blocks = part.shape[0]
    pages_per_block = len(page_refs) // n_blocks
    block_rows = pages_per_block * page_refs[0].shape[-1]
    qb = q_ref[...]
    for i in range(n_blocks):
        tot = page_refs[i * pages_per_block][...]
        for p in range(1, pages_per_block):
            tot = tot + page_refs[i * pages_per_block + p][...]
        part[i] = jnp.sum(tot * qb, axis=1)

    lane = lax.broadcasted_iota(jnp.int32, (N_HEADS, LANES), 1)
    acc = jnp.full((N_HEADS, LANES), NEG_INF, F32)
    for j in range(n_blocks):
        gate_j = jnp.sum(part[j], axis=1, keepdims=True) * (1.0 / block_rows)
        acc = jnp.where(lane == j, gate_j, acc)
    out = jnp.zeros((N_HEADS, LANES), jnp.int32)
    for r in range(MOBA_TOPK):
        best = jnp.max(acc, axis=1, keepdims=True)
        idx = jnp.min(jnp.where(acc == best, lane, LANES), axis=1, keepdims=True)
        out = jnp.where(lane == r, idx, out)
        acc = jnp.where(lane == idx, NEG_INF, acc)
    sel_ref[...] = out


def _sample_attn_kernel(sel_ref, pt_ref, q_ref, qb_ref, kn_ref, vn_ref, ck_ref, cv_ref, o_ref,
                        kbuf, vbuf, sem, *, pages_per_block, page_size):
    b = pl.program_id(0)
    nb = pl.num_programs(0)

    def copies(bb, slot):
        out = []
        for h in range(N_HEADS):
            for r in range(MOBA_TOPK):
                block = sel_ref[(bb * N_HEADS + h) * MOBA_TOPK + r]
                for p in range(pages_per_block):
                    page = pt_ref[bb, block * pages_per_block + p]
                    dst = pl.ds((r * pages_per_block + p) * page_size, page_size)
                    out.append(pltpu.make_async_copy(ck_ref.at[page, h], kbuf.at[slot, h, :, dst],
                                                     sem.at[slot, 0]))
                    out.append(pltpu.make_async_copy(cv_ref.at[page, h], vbuf.at[slot, h, :, dst],
                                                     sem.at[slot, 1]))
        return out

    @pl.when(b == 0)
    def _():
        for c in copies(0, 0):
            c.start()

    @pl.when(b + 1 < nb)
    def _():
        for c in copies(b + 1, (b + 1) % 2):
            c.start()

    slot = b % 2
    for c in copies(b, slot):
        c.wait()

    q = q_ref[...]
    qb = qb_ref[...]
    kn = kn_ref[...]
    vn = vn_ref[...]
    n_keys = kbuf.shape[-1]
    head = lax.broadcasted_iota(jnp.int32, (N_HEADS, ATT_WIDTH), 0)
    feat = lax.broadcasted_iota(jnp.int32, (N_HEADS, ATT_WIDTH), 1)
    own = (feat >= head * HEAD_DIM) & (feat < (head + 1) * HEAD_DIM)

    s = jnp.concatenate(
        [jnp.sum(kbuf[slot, :, :, c * LANES:(c + 1) * LANES] * qb, axis=1)
         for c in range(n_keys // LANES)], axis=1) * ATT_SCALE
    s_self = jnp.sum(jnp.where(own, q * kn, 0.0), axis=1, keepdims=True) * ATT_SCALE
    m = jnp.maximum(jnp.max(s, axis=1, keepdims=True), s_self)
    p = jnp.exp(s - m)
    p_self = jnp.exp(s_self - m)
    l = jnp.sum(p, axis=1, keepdims=True) + p_self
    pv = _dot_f32(p, vbuf[slot].reshape(N_HEADS * HEAD_DIM, n_keys), NT_DIMS, False)
    o = (pv + p_self * vn) / l
    o_ref[...] = jnp.sum(jnp.where(own, o, 0.0), axis=0, keepdims=True)


def _sample_attn(sel_flat, page_table, q3, q_bcast, k3, v3, cache_kt, cache_vt):
    n_dec = q3.shape[0]
    page_size = cache_kt.shape[3]
    pages_per_block = MOBA_BLOCK // page_size
    n_keys = MOBA_TOPK * MOBA_BLOCK
    row = pl.BlockSpec((None, 1, ATT_WIDTH), lambda b, sel, pt: (b, 0, 0))
    hbm = pl.BlockSpec(memory_space=pl.ANY)
    return pl.pallas_call(
        functools.partial(_sample_attn_kernel, pages_per_block=pages_per_block,
                          page_size=page_size),
        grid_spec=pltpu.PrefetchScalarGridSpec(
            num_scalar_prefetch=2,
            grid=(n_dec,),
            in_specs=[row, pl.BlockSpec((None,) + q_bcast.shape[1:],
                                        lambda b, sel, pt: (b, 0, 0, 0)), row, row, hbm, hbm],
            out_specs=row,
            scratch_shapes=[pltpu.VMEM((2, N_HEADS, HEAD_DIM, n_keys), F32),
                            pltpu.VMEM((2, N_HEADS, HEAD_DIM, n_keys), F32),
                            pltpu.SemaphoreType.DMA((2, 2))],
        ),
        out_shape=jax.ShapeDtypeStruct((n_dec, 1, ATT_WIDTH), F32),
        compiler_params=pltpu.CompilerParams(
            dimension_semantics=("arbitrary",), vmem_limit_bytes=VMEM_LIMIT),
        name="sample_attn",
    )(sel_flat, page_table, q3, q_bcast, k3, v3, cache_kt, cache_vt)


def kernel(x_prompt, x_sample, cache_k, cache_v, state_conv, page_table, g_mix, w_in, conv_w, w_out,
           g_ffn, w_router_group, b_router_group, w_router_expert, b_router_expert, w_gate, w_up,
           w_down, g_final):
    depth = g_mix.shape[0]
    assert depth == 1
    bsz, seq, d = x_prompt.shape
    n_dec, dec_len, _ = x_sample.shape
    assert dec_len == 1
    page_size = cache_k.shape[2]
    past_len = page_table.shape[1] * page_size
    assert past_len % MOBA_BLOCK == 0 and MOBA_BLOCK % page_size == 0
    assert past_len // MOBA_BLOCK >= MOBA_TOPK and seq % MOBA_BLOCK == 0

    g_mix2 = g_mix[0][None, :]
    g_ffn2 = g_ffn[0][None, :]
    g_final2 = g_final[None, :]
    w_in_f, w_out_f = w_in[0], w_out[0]
    w_out_bf = w_out_f.astype(BF16)
    pad = LANES - N_EXPERTS - N_EXPERT_GROUPS
    w_r = jnp.concatenate([w_router_expert[0], w_router_group[0], jnp.zeros((d, pad), F32)], axis=1)
    b_r = jnp.concatenate([b_router_expert[0], b_router_group[0], jnp.zeros((pad,), F32)])[None, :]
    cw = conv_w[0]

    cos_s, sin_s = _rope_angles(past_len + jnp.arange(dec_len, dtype=jnp.int32))
    x_s = x_sample.reshape(n_dec, d)
    prev0, prev1 = state_conv[0, :, 0, :], state_conv[0, :, 1, :]
    q_s, k_s, v_s, cy_s, u_s = _sample_inproj(x_s, g_mix2, w_in_f, _rope_row_tables(cos_s, sin_s),
                                              cw, prev0, prev1)
    cache_kt = jnp.transpose(cache_k[0], (0, 2, 3, 1))
    cache_vt = jnp.transpose(cache_v[0], (0, 2, 3, 1))
    q_bcast = jnp.broadcast_to(q_s.reshape(n_dec, N_HEADS, HEAD_DIM, 1),
                               (n_dec, N_HEADS, HEAD_DIM, page_size))

    cos_p, sin_p = _rope_angles(jnp.arange(seq, dtype=jnp.int32))
    conv0 = jnp.zeros((bsz, CONV_K - 1, CONV_CH), F32)
    qt_p, kt_p, vt_p, kb_p, vtb_p, kmeans_p, cy_p, conv_p = _prompt_inproj(
        x_prompt, g_mix2, w_in_f, (cos_p.T, sin_p.T), cw, conv0)
    att_p, sel = _prompt_attn(qt_p, kb_p, vtb_p, kmeans_p, page_table, q_bcast, cache_kt)
    n_p = bsz * seq
    hres_p, hn_p, route_p, idx_p, counts_p = _outproj_router(
        x_prompt.reshape(n_p, d), att_p.reshape(n_p, ATT_WIDTH), cy_p.reshape(n_p, CONV_CH),
        w_out_bf, g_ffn2, w_r, b_r, jnp.zeros((1, LANES), F32), tm=512, precise=False)

    sel_flat = sel[:, :, :MOBA_TOPK].reshape(-1)
    q3 = q_s.reshape(n_dec, 1, ATT_WIDTH)
    att_s = _sample_attn(sel_flat, page_table, q3, q_bcast, k_s.reshape(n_dec, 1, ATT_WIDTH),
                         v_s.reshape(n_dec, 1, ATT_WIDTH), cache_kt, cache_vt)
    hres_s, hn_s, route_s, idx_s, counts = _outproj_router(
        x_s, att_s.reshape(n_dec, ATT_WIDTH), cy_s, w_out_f, g_ffn2, w_r, b_r, counts_p,
        tm=n_dec, precise=True)

    n_assign = EXPERT_TOPK * (n_p + n_dec)
    n_tiles = -(-(n_assign + N_EXPERTS * (MOE_TILE - 1)) // MOE_TILE)
    offsets, tile_expert, n_active, fill_starts = _moe_plan(counts, n_tiles)
    slots_p, slots_s = _moe_slots(idx_p, offsets), _moe_slots(idx_s, offsets)
    xs = _moe_dispatch(fill_starts, n_active, slots_p, hn_p, None, n_tiles)
    xs = _moe_dispatch(fill_starts, n_active, slots_s, hn_s, xs, n_tiles)
    ys = _moe_ffn(tile_expert, n_active, xs, w_gate[0], w_up[0], w_down[0])
    y_p = _moe_combine(slots_p, route_p, hres_p, g_final2, ys)
    y_s = _moe_combine(slots_s, route_s, hres_s, g_final2, ys)

    conv_s = jnp.stack([prev1, u_s], axis=1)
    to_bthd = lambda t: jnp.transpose(t.reshape(bsz, N_HEADS, HEAD_DIM, seq), (0, 3, 1, 2))[None]
    return (y_p.reshape(bsz, seq, d), y_s.reshape(n_dec, dec_len, d),
            to_bthd(kt_p), to_bthd(vt_p), conv_p[None],
            k_s.reshape(1, n_dec, dec_len, N_HEADS, HEAD_DIM),
            v_s.reshape(1, n_dec, dec_len, N_HEADS, HEAD_DIM),
            conv_s[None])
```

```python
import functools

import jax
import jax.numpy as jnp
from jax import lax
from jax.experimental import pallas as pl
from jax.experimental.pallas import tpu as pltpu

F32 = jnp.float32
BF16 = jnp.bfloat16
HIGHEST = lax.Precision.HIGHEST

D_MODEL = 1024
HEAD_DIM = 64
N_HEADS = 8
ATT_WIDTH = N_HEADS * HEAD_DIM
CONV_CH = D_MODEL - ATT_WIDTH
ROPE_DIM = HEAD_DIM // 4
ROPE_HALF = ROPE_DIM // 2
ROPE_THETA = 500000.0
MOBA_BLOCK = 256
MOBA_TOPK = 3
ATT_SCALE = HEAD_DIM ** -0.5
CONV_K = 3
N_EXPERT_GROUPS = 4
EXPERTS_PER_GROUP = 8
N_EXPERTS = N_EXPERT_GROUPS * EXPERTS_PER_GROUP
EXPERT_TOPK = 2
EXPERT_FF = D_MODEL // 4
RMS_EPS = 1e-6

LANES = 128
SUBLANES = 8
HEADS_PER_VREG = LANES // HEAD_DIM
TOKEN_ROWS = D_MODEL // LANES
MXU_WIDTH = 256
MOE_TILE = 512
BF16_SUBLANES = 16
V_AUG_ROWS = HEAD_DIM + BF16_SUBLANES
LOG2_E = 1.4426950408889634
DMA_QUEUES = 2
ISSUE_UNROLL = 8
VMEM_LIMIT = 56 * 1024 * 1024

NEG_INF = float("-inf")


def _rms(x, g):
    ms = jnp.mean(x * x, axis=-1, keepdims=True)
    return x * lax.rsqrt(ms + RMS_EPS) * g


def _mm(a, w, precise):
    if precise:
        return jnp.dot(a, w, precision=HIGHEST, preferred_element_type=F32)
    return jnp.dot(a.astype(BF16), w, preferred_element_type=F32)


NN_DIMS = (((1,), (0,)), ((), ()))
NT_DIMS = (((1,), (1,)), ((), ()))


def _dot_f32(a, b, dims, full):
    if full:
        return lax.dot_general(a, b, dims, precision=HIGHEST, preferred_element_type=F32)
    a_hi, b_hi = a.astype(BF16), b.astype(BF16)
    a_lo = (a - a_hi.astype(F32)).astype(BF16)
    b_lo = (b - b_hi.astype(F32)).astype(BF16)
    dot = functools.partial(lax.dot_general, dimension_numbers=dims, preferred_element_type=F32)
    n = b.shape[1]
    if dims == NN_DIMS and 2 * n <= MXU_WIDTH:
        both = dot(a_hi, jnp.concatenate([b_hi, b_lo], axis=1))
        return both[:, :n] + (both[:, n:] + dot(a_lo, b_hi))
    return dot(a_hi, b_hi) + (dot(a_hi, b_lo) + dot(a_lo, b_hi))


def _rope_rows(a, cos, sin_lo, sin_hi):
    n = a.shape[-1]
    return (a * cos + pltpu.roll(a, n - ROPE_HALF, 1) * sin_lo
            + pltpu.roll(a, ROPE_HALF, 1) * sin_hi)


def _tile_lanes(t, reps):
    return jnp.concatenate([t] * reps, axis=-1)


def _rope_angles(pos):
    inv = ROPE_THETA ** (-jnp.arange(0, ROPE_DIM, 2, dtype=F32) / ROPE_DIM)
    ang = pos.astype(F32)[:, None] * inv[None, :]
    return jnp.cos(ang), jnp.sin(ang)


def _rope_row_tables(cos, sin):
    rows = cos.shape[0]
    ones = jnp.ones((rows, HEAD_DIM - ROPE_DIM), F32)
    zeros = jnp.zeros((rows, HEAD_DIM - ROPE_HALF), F32)
    c = jnp.concatenate([cos, cos, ones], axis=1)
    s_lo = jnp.concatenate([-sin, zeros], axis=1)
    s_hi = jnp.concatenate([jnp.zeros((rows, ROPE_HALF), F32), sin,
                            jnp.zeros((rows, HEAD_DIM - ROPE_DIM), F32)], axis=1)
    rep = lambda t: jnp.concatenate([t] * HEADS_PER_VREG, axis=1)
    return rep(c), rep(s_lo), rep(s_hi)


def _rope_cols(ref, cos_t, sin_t):
    for h in range(N_HEADS):
        r = h * HEAD_DIM
        x1 = ref[r:r + ROPE_HALF, :]
        x2 = ref[r + ROPE_HALF:r + ROPE_DIM, :]
        ref[r:r + ROPE_HALF, :] = x1 * cos_t - x2 * sin_t
        ref[r + ROPE_HALF:r + ROPE_DIM, :] = x2 * cos_t + x1 * sin_t


def _prompt_inproj_kernel(x_ref, g_ref, wqt_ref, wkt_ref, wvt_ref, wc_ref,
                          cost_ref, sint_ref, cw_ref, prev_ref,
                          qt_ref, kt_ref, vt_ref, kb_ref, vtb_ref, km_ref, cy_ref, cn_ref,
                          ubuf, halo, km_acc, *, tt):
    t = pl.program_id(1)
    blk = MOBA_BLOCK
    xn = _rms(x_ref[...], g_ref[...]).astype(BF16)

    qt_ref[...] = lax.dot_general(wqt_ref[...], xn, NT_DIMS, preferred_element_type=F32)
    _rope_cols(qt_ref, cost_ref[...], sint_ref[...])
    kt_ref[...] = lax.dot_general(wkt_ref[...], xn, NT_DIMS, preferred_element_type=F32)
    _rope_cols(kt_ref, cost_ref[...], sint_ref[...])
    vt = lax.dot_general(wvt_ref[...], xn, NT_DIMS, preferred_element_type=F32)
    vt_ref[...] = vt
    for i in range(tt // blk):
        vtb_ref[i] = vt[:, i * blk:(i + 1) * blk].astype(BF16)

    k = kt_ref[...].T
    kb_ref[...] = k.astype(BF16)

    @pl.when(t == 0)
    def _():
        km_acc[...] = jnp.zeros_like(km_acc)

    rid = lax.broadcasted_iota(jnp.int32, km_acc.shape, 0)
    km = km_acc[...]
    for i in range(tt // blk):
        mean_i = jnp.mean(k[i * blk:(i + 1) * blk, :], axis=0, keepdims=True)
        km = jnp.where(rid == t * (tt // blk) + i, mean_i, km)
    km_acc[...] = km
    km_ref[...] = km

    def proj(i):
        return jnp.dot(xn, wc_ref[:, i * CONV_CH:(i + 1) * CONV_CH], preferred_element_type=F32)

    b_gate = proj(0)
    u = proj(1) * proj(2)

    @pl.when(t == 0)
    def _():
        ubuf[SUBLANES - (CONV_K - 1):SUBLANES, :] = prev_ref[...]

    @pl.when(t > 0)
    def _():
        ubuf[0:SUBLANES, :] = halo[...]

    ubuf[SUBLANES:SUBLANES + tt, :] = u
    cw = cw_ref[...]
    conv = (cw[0:1, :] * ubuf[SUBLANES - 2:SUBLANES - 2 + tt, :]
            + cw[1:2, :] * ubuf[SUBLANES - 1:SUBLANES - 1 + tt, :]
            + cw[2:3, :] * u)
    cy_ref[...] = (b_gate * conv).astype(cy_ref.dtype)
    halo[...] = ubuf[tt:tt + SUBLANES, :]
    cn_ref[...] = ubuf[tt + SUBLANES - (CONV_K - 1):tt + SUBLANES, :]


def _prompt_inproj(x, g_mix, w_in, col_tables, conv_w, conv_prev, tt=512):
    bsz, seq, d = x.shape
    nt = seq // tt
    nb = seq // MOBA_BLOCK
    w_bf = w_in.astype(BF16)
    wqt = w_bf[:, :ATT_WIDTH].T
    wkt = w_bf[:, ATT_WIDTH:2 * ATT_WIDTH].T
    wvt = w_bf[:, 2 * ATT_WIDTH:3 * ATT_WIDTH].T
    wc = w_bf[:, 3 * ATT_WIDTH:]
    row_spec = lambda width: pl.BlockSpec((None, tt, width), lambda b, t: (b, t, 0))
    col_spec = pl.BlockSpec((None, ATT_WIDTH, tt), lambda b, t: (b, 0, t))
    tabt_spec = pl.BlockSpec((ROPE_HALF, tt), lambda b, t: (0, t))
    full = lambda shape: pl.BlockSpec(shape, lambda b, t: (0,) * len(shape))
    per_b = lambda shape: pl.BlockSpec((None,) + shape, lambda b, t: (b,) + (0,) * len(shape))
    vtb_spec = pl.BlockSpec((None, tt // MOBA_BLOCK, ATT_WIDTH, MOBA_BLOCK),
                            lambda b, t: (b, t, 0, 0))
    col_shape = jax.ShapeDtypeStruct((bsz, ATT_WIDTH, seq), F32)
    return pl.pallas_call(
        functools.partial(_prompt_inproj_kernel, tt=tt),
        grid=(bsz, nt),
        in_specs=[row_spec(d), full((1, d)), full(wqt.shape), full(wkt.shape), full(wvt.shape),
                  full(wc.shape), tabt_spec, tabt_spec,
                  full((CONV_K, CONV_CH)), per_b((CONV_K - 1, CONV_CH))],
        out_specs=[col_spec, col_spec, col_spec, row_spec(ATT_WIDTH), vtb_spec,
                   per_b((nb, ATT_WIDTH)), row_spec(CONV_CH), per_b((CONV_K - 1, CONV_CH))],
        out_shape=[col_shape, col_shape, col_shape,
                   jax.ShapeDtypeStruct((bsz, seq, ATT_WIDTH), BF16),
                   jax.ShapeDtypeStruct((bsz, nb, ATT_WIDTH, MOBA_BLOCK), BF16),
                   jax.ShapeDtypeStruct((bsz, nb, ATT_WIDTH), F32),
                   jax.ShapeDtypeStruct((bsz, seq, CONV_CH), BF16),
                   jax.ShapeDtypeStruct((bsz, CONV_K - 1, CONV_CH), F32)],
        scratch_shapes=[pltpu.VMEM((tt + SUBLANES, CONV_CH), F32),
                        pltpu.VMEM((SUBLANES, CONV_CH), F32),
                        pltpu.VMEM((nb, ATT_WIDTH), F32)],
        compiler_params=pltpu.CompilerParams(
            dimension_semantics=("arbitrary", "arbitrary"), vmem_limit_bytes=VMEM_LIMIT),
        name="prompt_inproj",
    )(x, g_mix, wqt, wkt, wvt, wc, *col_tables, conv_w, conv_prev)


def _prompt_attn_kernel(pt_ref, qt_ref, kb_ref, vtb_ref, km_ref, qdec_ref, *refs, nb, n_pages):
    page_refs = refs[:n_pages]
    o_ref, sel_ref, s_ref, ot_ref, va_ref, part_ref = refs[n_pages:]
    _select_past_blocks(qdec_ref, page_refs, part_ref, sel_ref)
    blk = MOBA_BLOCK
    means = km_ref[...]
    ones_row = lax.broadcasted_iota(jnp.int32, (V_AUG_ROWS - HEAD_DIM, blk), 0) == 0
    for j in range(nb):
        for h in range(HEADS_PER_VREG):
            va_ref[j, h, 0:HEAD_DIM, :] = vtb_ref[j, h * HEAD_DIM:(h + 1) * HEAD_DIM, :]
            va_ref[j, h, HEAD_DIM:V_AUG_ROWS, :] = jnp.where(ones_row, 1.0, 0.0).astype(BF16)
    feat = lax.broadcasted_iota(jnp.int32, (LANES, 1), 0)
    bid = lax.broadcasted_iota(jnp.int32, (nb, blk), 0)
    key = lax.broadcasted_iota(jnp.int32, (blk, blk), 0)
    qry = lax.broadcasted_iota(jnp.int32, (blk, blk), 1)

    def block_bias(qth, qi):
        if qi <= MOBA_TOPK:
            return [None] * qi
        gates = jnp.dot(means, qth, precision=HIGHEST, preferred_element_type=F32)
        gates = jnp.where(bid < qi, gates, NEG_INF)
        rows = []
        for j in range(qi):
            gj = gates[j:j + 1, :]
            beats = (gates > gj) | ((gates == gj) & (bid < j))
            cnt = jnp.sum(beats.astype(F32), axis=0, keepdims=True)
            rows.append(jnp.where(cnt < MOBA_TOPK, 0.0, NEG_INF))
        return rows

    def scores_pass(qi, h, slot, state):
        qt = qt_ref[:, qi * blk:(qi + 1) * blk]
        qth = jnp.where((feat >= h * HEAD_DIM) & (feat < (h + 1) * HEAD_DIM), qt, 0.0)
        bias = block_bias(qth, qi)
        qs = (qth * (ATT_SCALE * LOG2_E)).astype(BF16)
        m = None
        for j in range(qi + 1):
            s = jnp.dot(kb_ref[j * blk:(j + 1) * blk, :], qs,
                        preferred_element_type=F32)
            if j == qi:
                s = jnp.where(key <= qry, s, NEG_INF)
            elif bias[j] is not None:
                s = s + bias[j]
            s_ref[slot, j] = s
            m_blk = jnp.max(s, axis=0, keepdims=True)
            m = m_blk if m is None else jnp.maximum(m, m_blk)
            yield
        state["m"] = m

    def values_pass(qi, h, slot, state):
        m = state["m"]
        acc = None
        for j in range(qi + 1):
            p = jnp.exp2(s_ref[slot, j] - m).astype(BF16)
            pv = jnp.dot(va_ref[j, h], p, preferred_element_type=F32)
            acc = pv if acc is None else acc + pv
            yield
        ot_ref[h * HEAD_DIM:(h + 1) * HEAD_DIM, qi * blk:(qi + 1) * blk] = (
            acc[0:HEAD_DIM, :] / acc[HEAD_DIM:HEAD_DIM + 1, :])

    def run_interleaved(*gens):
        live = [g for g in gens if g is not None]
        while live:
            for g in list(live):
                if next(g, "done") == "done":
                    live.remove(g)

    pending = None
    for i, (qi, h) in enumerate((qi, h) for qi in range(nb) for h in range(HEADS_PER_VREG)):
        state = {}
        run_interleaved(scores_pass(qi, h, i % 2, state), pending)
        pending = values_pass(qi, h, i % 2, state)
    run_interleaved(pending)
    o_ref[...] = ot_ref[...].T.astype(o_ref.dtype)


def _prompt_attn(qt, kb, vtb, kmeans, page_table, q_dec, cache_kt):
    bsz, seq, _ = kb.shape
    nb = seq // MOBA_BLOCK
    n_pairs = ATT_WIDTH // LANES
    n_dec, n_pages = page_table.shape
    assert n_dec == bsz * n_pairs, "one decode sequence per prompt attention grid step"
    _, n_heads, head_dim, page_size = cache_kt.shape
    n_blocks = n_pages * page_size // MOBA_BLOCK
    dec = lambda b, hp: b * n_pairs + hp

    def page_spec(i):
        return pl.BlockSpec((None, n_heads, head_dim, page_size),
                            lambda b, hp, pt: (pt[dec(b, hp), i], 0, 0, 0))

    return pl.pallas_call(
        functools.partial(_prompt_attn_kernel, nb=nb, n_pages=n_pages),
        grid_spec=pltpu.PrefetchScalarGridSpec(
            num_scalar_prefetch=1,
            grid=(bsz, n_pairs),
            in_specs=[pl.BlockSpec((None, LANES, seq), lambda b, hp, pt: (b, hp, 0)),
                      pl.BlockSpec((None, seq, LANES), lambda b, hp, pt: (b, 0, hp)),
                      pl.BlockSpec((None, nb, LANES, MOBA_BLOCK), lambda b, hp, pt: (b, 0, hp, 0)),
                      pl.BlockSpec((None, nb, LANES), lambda b, hp, pt: (b, 0, hp)),
                      pl.BlockSpec((None, n_heads, head_dim, page_size),
                                   lambda b, hp, pt: (dec(b, hp), 0, 0, 0))]
                     + [page_spec(i) for i in range(n_pages)],
            out_specs=[pl.BlockSpec((None, seq, LANES), lambda b, hp, pt: (b, 0, hp)),
                       pl.BlockSpec((None, n_heads, LANES), lambda b, hp, pt: (dec(b, hp), 0, 0))],
            scratch_shapes=[pltpu.VMEM((2, nb, MOBA_BLOCK, MOBA_BLOCK), F32),
                            pltpu.VMEM((LANES, seq), F32),
                            pltpu.VMEM((nb, HEADS_PER_VREG, V_AUG_ROWS, MOBA_BLOCK), BF16),
                            pltpu.VMEM((n_blocks, n_heads, LANES), F32)],
        ),
        out_shape=[jax.ShapeDtypeStruct((bsz, seq, ATT_WIDTH), BF16),
                   jax.ShapeDtypeStruct((n_dec, n_heads, LANES), jnp.int32)],
        compiler_params=pltpu.CompilerParams(
            dimension_semantics=("arbitrary", "arbitrary"), vmem_limit_bytes=VMEM_LIMIT),
        name="prompt_attn",
    )(page_table, qt, kb, vtb, kmeans, q_dec, *([cache_kt] * n_pages))


def _outproj_router_kernel(x_ref, att_ref, cy_ref, wo_ref, g_ref, wr_ref, br_ref, cnt_in_ref,
                           hres_ref, hn_ref, route_ref, idx_ref, cnt_ref, tri_ref, *, precise):
    step = pl.program_id(0)
    tm = x_ref.shape[0]
    hres = (x_ref[...] + _mm(att_ref[...], wo_ref[0:ATT_WIDTH, :], precise)
            + _mm(cy_ref[...], wo_ref[ATT_WIDTH:, :], precise))
    hres_ref[...] = hres
    hn = _rms(hres, g_ref[...])
    for s in range(TOKEN_ROWS):
        hn_ref[pl.ds(s, tm, stride=TOKEN_ROWS), :] = hn[:, s * LANES:(s + 1) * LANES]

    @pl.when(step == 0)
    def _():
        cnt_ref[...] = cnt_in_ref[...]
        r = lax.broadcasted_iota(jnp.int32, (tm, tm), 0)
        c = lax.broadcasted_iota(jnp.int32, (tm, tm), 1)
        tri_ref[...] = jnp.where(r < c, 1.0, 0.0).astype(BF16)

    lo = _dot_f32(wr_ref[...], hn, NT_DIMS, precise) + br_ref[...]
    sub = lax.broadcasted_iota(jnp.int32, (SUBLANES, tm), 0)
    first_max = lambda v, m: jnp.min(jnp.where(v == m, sub, SUBLANES), axis=0, keepdims=True)
    lg = jnp.where(sub < N_EXPERT_GROUPS, lo[0:SUBLANES], NEG_INF)
    mg = jnp.max(lg, axis=0, keepdims=True)
    g = first_max(lg, mg)
    pg = 1.0 / jnp.sum(jnp.exp(lg - mg), axis=0, keepdims=True)
    le = lo[SUBLANES:2 * SUBLANES]
    for gi in range(1, N_EXPERT_GROUPS):
        le = jnp.where(g == gi, lo[(gi + 1) * SUBLANES:(gi + 2) * SUBLANES], le)
    m1 = jnp.max(le, axis=0, keepdims=True)
    i1 = first_max(le, m1)
    le2 = jnp.where(sub == i1, NEG_INF, le)
    m2 = jnp.max(le2, axis=0, keepdims=True)
    i2 = first_max(le2, m2)
    e2 = jnp.exp(m2 - m1)
    w1 = pg / (1.0 + e2)
    w2 = pg * e2 / (1.0 + e2)
    ex1, ex2 = g * EXPERTS_PER_GROUP + i1, g * EXPERTS_PER_GROUP + i2

    eid = lax.broadcasted_iota(jnp.int32, (N_EXPERTS, tm), 0)
    oh1, oh2 = eid == ex1, eid == ex2
    assigned = jnp.where(oh1 | oh2, 1.0, 0.0)
    before = cnt_ref[...] + jnp.dot(assigned.astype(BF16), tri_ref[...],
                                    preferred_element_type=F32)
    r1 = jnp.sum(jnp.where(oh1, before, 0.0), axis=0, keepdims=True)
    r2 = jnp.sum(jnp.where(oh2, before, 0.0), axis=0, keepdims=True)
    cnt_ref[...] += jnp.sum(assigned, axis=1, keepdims=True)
    zrow = jnp.zeros((1, tm), jnp.int32)
    idx_ref[...] = jnp.concatenate(
        [ex1, ex2, r1.astype(jnp.int32), r2.astype(jnp.int32)] + [zrow] * (SUBLANES - 4), axis=0)
    zf = jnp.zeros((1, tm), F32)
    wrows = jnp.concatenate([zf] * 4 + [w1, w2] + [zf] * 2 + [jnp.zeros((LANES - SUBLANES, tm), F32)],
                            axis=0)
    route_ref[...] = wrows.T


def _outproj_router(x, att, cy, w_out, g_ffn, w_r, b_r, counts_in, tm, precise):
    n, d = x.shape
    rows = lambda width: pl.BlockSpec((tm, width), lambda i: (i, 0))
    full = lambda shape: pl.BlockSpec(shape, lambda i: (0,) * len(shape))
    return pl.pallas_call(
        functools.partial(_outproj_router_kernel, precise=precise),
        grid=(n // tm,),
        in_specs=[rows(d), rows(ATT_WIDTH), rows(CONV_CH), full(w_out.shape), full((1, d)),
                  full(w_r.shape), full((LANES, 1)), full((N_EXPERTS, 1))],
        out_specs=[rows(d), pl.BlockSpec((tm * TOKEN_ROWS, LANES), lambda i: (i, 0)), rows(LANES),
                   pl.BlockSpec((None, SUBLANES, tm), lambda i: (i, 0, 0)), full((N_EXPERTS, 1))],
        out_shape=[jax.ShapeDtypeStruct((n, d), F32),
                   jax.ShapeDtypeStruct((n * TOKEN_ROWS, LANES), F32),
                   jax.ShapeDtypeStruct((n, LANES), F32),
                   jax.ShapeDtypeStruct((n // tm, SUBLANES, tm), jnp.int32),
                   jax.ShapeDtypeStruct((N_EXPERTS, 1), F32)],
        scratch_shapes=[pltpu.VMEM((tm, tm), BF16)],
        compiler_params=pltpu.CompilerParams(
            dimension_semantics=("arbitrary",), vmem_limit_bytes=VMEM_LIMIT),
        name="outproj_router_precise" if precise else "outproj_router",
    )(x, att, cy, w_out, g_ffn, w_r, b_r, counts_in)


def _token_rows(ref, t):
    return ref.at[pl.ds(pl.multiple_of(t * TOKEN_ROWS, TOKEN_ROWS), TOKEN_ROWS), :]


def _start_token_copies(copies, n):
    def start(g, carry):
        for u in range(ISSUE_UNROLL):
            for c in copies(g * ISSUE_UNROLL + u):
                c.start(priority=u % DMA_QUEUES)
        return carry

    lax.fori_loop(0, n // ISSUE_UNROLL, start, 0)


def _moe_slots(idx, offsets):
    experts, inner = idx[:, :EXPERT_TOPK, :], idx[:, EXPERT_TOPK:2 * EXPERT_TOPK, :]
    onehot = experts[..., None] == jnp.arange(N_EXPERTS, dtype=jnp.int32)
    return inner + jnp.sum(jnp.where(onehot, offsets, 0), axis=-1)


def _moe_dispatch_kernel(fill_ref, na_ref, slot_ref, hn_ref, *refs, tm, n_tiles, zero_fill):
    xs_ref, zbuf, sem = refs[-3:]
    step = pl.program_id(0)

    if zero_fill:
        @pl.when(step == 0)
        def _():
            zbuf[...] = jnp.zeros_like(zbuf)

            def fill(slot):
                return pltpu.make_async_copy(
                    zbuf, xs_ref.at[pl.ds(pl.multiple_of(slot * TOKEN_ROWS, TOKEN_ROWS),
                                          MOE_TILE * TOKEN_ROWS), :], sem.at[1])

            def start_fill(i, carry):
                fill(i * MOE_TILE).start()
                return carry

            def wait_fill(i, carry):
                fill(i * MOE_TILE).wait()
                return carry

            def group_fills(act):
                for e in range(N_EXPERTS):
                    @pl.when(fill_ref[e] >= 0)
                    def _():
                        act(fill(fill_ref[e]))

            group_fills(lambda c: c.start())
            lax.fori_loop(na_ref[0], n_tiles, start_fill, 0)
            group_fills(lambda c: c.wait())
            lax.fori_loop(na_ref[0], n_tiles, wait_fill, 0)

    def copies(t):
        src = _token_rows(hn_ref, t)
        return tuple(pltpu.make_async_copy(src, _token_rows(xs_ref, slot_ref[k, t]), sem.at[0])
                     for k in range(EXPERT_TOPK))

    _start_token_copies(copies, tm)
    for _ in range(EXPERT_TOPK):
        pltpu.make_async_copy(hn_ref, xs_ref.at[pl.ds(0, tm * TOKEN_ROWS), :], sem.at[0]).wait()


def _moe_dispatch(fill_starts, n_active, slots, hn_tiles, xs_prev, n_tiles):
    n_steps, _, tm = slots.shape
    n = n_steps * tm
    zero_fill = xs_prev is None
    hbm = pl.BlockSpec(memory_space=pl.ANY)
    in_specs = [pl.BlockSpec((None, EXPERT_TOPK, tm), lambda i, *_: (i, 0, 0),
                             memory_space=pltpu.SMEM),
                pl.BlockSpec((tm * TOKEN_ROWS, LANES), lambda i, *_: (i, 0))]
    args = [fill_starts, n_active, slots, hn_tiles]
    aliases = {}
    if not zero_fill:
        in_specs.append(hbm)
        args.append(xs_prev)
        aliases = {len(args) - 1: 0}
    return pl.pallas_call(
        functools.partial(_moe_dispatch_kernel, tm=tm, n_tiles=n_tiles, zero_fill=zero_fill),
        grid_spec=pltpu.PrefetchScalarGridSpec(
            num_scalar_prefetch=2,
            grid=(n // tm,),
            in_specs=in_specs,
            out_specs=hbm,
            scratch_shapes=[pltpu.VMEM((MOE_TILE * TOKEN_ROWS, LANES), F32),
                            pltpu.SemaphoreType.DMA((2,))],
        ),
        out_shape=jax.ShapeDtypeStruct((n_tiles * MOE_TILE * TOKEN_ROWS, LANES), F32),
        input_output_aliases=aliases,
        compiler_params=pltpu.CompilerParams(
            dimension_semantics=("arbitrary",), vmem_limit_bytes=VMEM_LIMIT),
        name="moe_dispatch" if zero_fill else "moe_dispatch_more",
    )(*args)


def _moe_ffn_kernel(te_ref, na_ref, xs_ref, wg_ref, wu_ref, wd_ref, ys_ref, wg_b, wu_b, wd_b):
    i = pl.program_id(0)
    tg = MOE_TILE
    active = i < na_ref[0]

    @pl.when(active)
    def _():
        @pl.when((i == 0) | (te_ref[i] != te_ref[jnp.maximum(i - 1, 0)]))
        def _():
            wg_b[...] = wg_ref[...].astype(BF16)
            wu_b[...] = wu_ref[...].astype(BF16)
            wd_b[...] = wd_ref[...].astype(BF16)

        x = jnp.concatenate(
            [xs_ref[pl.ds(s, tg, stride=TOKEN_ROWS), :].astype(BF16) for s in range(TOKEN_ROWS)],
            axis=1)
        hg = jnp.dot(x, wg_b[...], preferred_element_type=F32)
        hu = jnp.dot(x, wu_b[...], preferred_element_type=F32)
        h = hg * (1.0 / (1.0 + jnp.exp(-hg))) * hu
        y = jnp.dot(h.astype(BF16), wd_b[...], preferred_element_type=F32)
        for s in range(TOKEN_ROWS):
            ys_ref[pl.ds(s, tg, stride=TOKEN_ROWS), :] = y[:, s * LANES:(s + 1) * LANES]

    @pl.when(jnp.logical_not(active))
    def _():
        ys_ref[...] = jnp.zeros_like(ys_ref)


def _moe_ffn(tile_expert, n_active, xs, w_gate, w_up, w_down):
    n_tiles = tile_expert.shape[0]
    tile_rows = MOE_TILE * TOKEN_ROWS
    d, ff = w_gate.shape[1:]
    last = lambda na: na[0] - 1
    w_spec = lambda shape: pl.BlockSpec(
        (None,) + shape, lambda i, te, na: (te[jnp.minimum(i, last(na))], 0, 0))
    return pl.pallas_call(
        _moe_ffn_kernel,
        grid_spec=pltpu.PrefetchScalarGridSpec(
            num_scalar_prefetch=2,
            grid=(n_tiles,),
            in_specs=[pl.BlockSpec((tile_rows, LANES),
                                   lambda i, te, na: (jnp.minimum(i, last(na)), 0)),
                      w_spec((d, ff)), w_spec((d, ff)), w_spec((ff, d))],
            out_specs=pl.BlockSpec((tile_rows, LANES), lambda i, te, na: (i, 0)),
            scratch_shapes=[pltpu.VMEM((d, ff), BF16), pltpu.VMEM((d, ff), BF16),
                            pltpu.VMEM((ff, d), BF16)],
        ),
        out_shape=jax.ShapeDtypeStruct(xs.shape, F32),
        compiler_params=pltpu.CompilerParams(
            dimension_semantics=("arbitrary",), vmem_limit_bytes=VMEM_LIMIT),
        name="moe_ffn",
    )(tile_expert, n_active, xs, w_gate, w_up, w_down)


def _moe_combine_kernel(slot_ref, route_ref, hres_ref, gf_ref, ys_ref, y_ref, buf, sem, *, tm):
    def copies(t):
        return tuple(pltpu.make_async_copy(_token_rows(ys_ref, slot_ref[k, t]),
                                           _token_rows(buf.at[k], t), sem.at[k])
                     for k in range(EXPERT_TOPK))

    _start_token_copies(copies, tm)
    for k in range(EXPERT_TOPK):
        pltpu.make_async_copy(ys_ref.at[pl.ds(0, tm * TOKEN_ROWS), :], buf.at[k], sem.at[k]).wait()

    route = route_ref[...]
    w1, w2 = route[:, 4:5], route[:, 5:6]
    for s in range(TOKEN_ROWS):
        cols = slice(s * LANES, (s + 1) * LANES)
        y_ref[:, cols] = (hres_ref[:, cols]
                          + w1 * buf[0, pl.ds(s, tm, stride=TOKEN_ROWS), :]
                          + w2 * buf[1, pl.ds(s, tm, stride=TOKEN_ROWS), :])
    y_ref[...] = _rms(y_ref[...], gf_ref[...])


def _moe_combine(slots, route, hres, g_final, ys):
    n, d = hres.shape
    tm = slots.shape[2]
    rows = lambda width: pl.BlockSpec((tm, width), lambda i: (i, 0))
    return pl.pallas_call(
        functools.partial(_moe_combine_kernel, tm=tm),
        grid=(n // tm,),
        in_specs=[pl.BlockSpec((None, EXPERT_TOPK, tm), lambda i: (i, 0, 0),
                               memory_space=pltpu.SMEM),
                  rows(LANES), rows(d), pl.BlockSpec((1, d), lambda i: (0, 0)),
                  pl.BlockSpec(memory_space=pl.ANY)],
        out_specs=rows(d),
        out_shape=jax.ShapeDtypeStruct((n, d), F32),
        scratch_shapes=[pltpu.VMEM((EXPERT_TOPK, tm * TOKEN_ROWS, LANES), F32),
                        pltpu.SemaphoreType.DMA((EXPERT_TOPK,))],
        compiler_params=pltpu.CompilerParams(
            dimension_semantics=("arbitrary",), vmem_limit_bytes=VMEM_LIMIT),
        name="moe_combine",
    )(slots, route, hres, g_final, ys)


def _moe_plan(counts, n_tiles):
    counts = counts[:, 0].astype(jnp.int32)
    padded = (counts + MOE_TILE - 1) // MOE_TILE * MOE_TILE
    ends = jnp.cumsum(padded)
    offsets = ends - padded
    n_active = (ends[-1:] // MOE_TILE).astype(jnp.int32)
    tile_start = jnp.arange(n_tiles, dtype=jnp.int32) * MOE_TILE
    tile_expert = jnp.minimum(jnp.sum(tile_start[:, None] >= ends[None, :], axis=1), N_EXPERTS - 1)
    fill_starts = jnp.where(padded > 0, ends - MOE_TILE, -1)
    return offsets, tile_expert.astype(jnp.int32), n_active, fill_starts.astype(jnp.int32)


def _sample_inproj_kernel(x_ref, g_ref, w_ref, cos_ref, slo_ref, shi_ref, cw_ref, p0_ref, p1_ref,
                          q_ref, k_ref, v_ref, cy_ref, u_ref):
    xn = _rms(x_ref[...], g_ref[...])
    reps = ATT_WIDTH // LANES
    cos = _tile_lanes(cos_ref[...], reps)
    slo = _tile_lanes(slo_ref[...], reps)
    shi = _tile_lanes(shi_ref[...], reps)

    def proj(i):
        return jnp.dot(xn, w_ref[:, i * ATT_WIDTH:(i + 1) * ATT_WIDTH], precision=HIGHEST,
                       preferred_element_type=F32)

    q_ref[...] = _rope_rows(proj(0), cos, slo, shi)
    k_ref[...] = _rope_rows(proj(1), cos, slo, shi)
    v_ref[...] = proj(2)
    b_gate = proj(3)
    u = proj(4) * proj(5)
    u_ref[...] = u
    cw = cw_ref[...]
    cy_ref[...] = b_gate * (cw[0:1, :] * p0_ref[...] + cw[1:2, :] * p1_ref[...] + cw[2:3, :] * u)


def _sample_inproj(x, g_mix, w_in, tables, conv_w, prev0, prev1):
    n = x.shape[0]
    out = jax.ShapeDtypeStruct((n, ATT_WIDTH), F32)
    return pl.pallas_call(
        _sample_inproj_kernel,
        out_shape=[out] * 5,
        compiler_params=pltpu.CompilerParams(vmem_limit_bytes=VMEM_LIMIT),
        name="sample_inproj",
    )(x, g_mix, w_in, *tables, conv_w, prev0, prev1)


def _select_past_blocks(q_ref, page_refs, part, sel_ref):
    n_blocks = part.shape[0]
    pages_per_block = len(page_refs) // n_blocks
    block_rows = pages_per_block * page_refs[0].shape[-1]
    qb = q_ref[...]
    for i in range(n_blocks):
        tot = page_refs[i * pages_per_block][...]
        for p in range(1, pages_per_block):
            tot = tot + page_refs[i * pages_per_block + p][...]
        part[i] = jnp.sum(tot * qb, axis=1)

    lane = lax.broadcasted_iota(jnp.int32, (N_HEADS, LANES), 1)
    acc = jnp.full((N_HEADS, LANES), NEG_INF, F32)
    for j in range(n_blocks):
        gate_j = jnp.sum(part[j], axis=1, keepdims=True) * (1.0 / block_rows)
        acc = jnp.where(lane == j, gate_j, acc)
    out = jnp.zeros((N_HEADS, LANES), jnp.int32)
    for r in range(MOBA_TOPK):
        best = jnp.max(acc, axis=1, keepdims=True)
        idx = jnp.min(jnp.where(acc == best, lane, LANES), axis=1, keepdims=True)
        out = jnp.where(lane == r, idx, out)
        acc = jnp.where(lane == idx, NEG_INF, acc)
    sel_ref[...] = out


def _sample_attn_kernel(sel_ref, pt_ref, q_ref, qb_ref, kn_ref, vn_ref, ck_ref, cv_ref, o_ref,
                        kbuf, vbuf, sem, *, pages_per_block, page_size):
    b = pl.program_id(0)
    nb = pl.num_programs(0)

    def copies(bb, slot):
        out = []
        for h in range(N_HEADS):
            for r in range(MOBA_TOPK):
                block = sel_ref[(bb * N_HEADS + h) * MOBA_TOPK + r]
                for p in range(pages_per_block):
                    page = pt_ref[bb, block * pages_per_block + p]
                    dst = pl.ds((r * pages_per_block + p) * page_size, page_size)
                    out.append(pltpu.make_async_copy(ck_ref.at[page, h], kbuf.at[slot, h, :, dst],
                                                     sem.at[slot, 0]))
                    out.append(pltpu.make_async_copy(cv_ref.at[page, h], vbuf.at[slot, h, :, dst],
                                                     sem.at[slot, 1]))
        return out

    @pl.when(b == 0)
    def _():
        for c in copies(0, 0):
            c.start()

    @pl.when(b + 1 < nb)
    def _():
        for c in copies(b + 1, (b + 1) % 2):
            c.start()

    slot = b % 2
    for c in copies(b, slot):
        c.wait()

    q = q_ref[...]
    qb = qb_ref[...]
    kn = kn_ref[...]
    vn = vn_ref[...]
    n_keys = kbuf.shape[-1]
    head = lax.broadcasted_iota(jnp.int32, (N_HEADS, ATT_WIDTH), 0)
    feat = lax.broadcasted_iota(jnp.int32, (N_HEADS, ATT_WIDTH), 1)
    own = (feat >= head * HEAD_DIM) & (feat < (head + 1) * HEAD_DIM)

    s = jnp.concatenate(
        [jnp.sum(kbuf[slot, :, :, c * LANES:(c + 1) * LANES] * qb, axis=1)
         for c in range(n_keys // LANES)], axis=1) * ATT_SCALE
    s_self = jnp.sum(jnp.where(own, q * kn, 0.0), axis=1, keepdims=True) * ATT_SCALE
    m = jnp.maximum(jnp.max(s, axis=1, keepdims=True), s_self)
    p = jnp.exp(s - m)
    p_self = jnp.exp(s_self - m)
    l = jnp.sum(p, axis=1, keepdims=True) + p_self
    pv = _dot_f32(p, vbuf[slot].reshape(N_HEADS * HEAD_DIM, n_keys), NT_DIMS, False)
    o = (pv + p_self * vn) / l
    o_ref[...] = jnp.sum(jnp.where(own, o, 0.0), axis=0, keepdims=True)


def _sample_attn(sel_flat, page_table, q3, q_bcast, k3, v3, cache_kt, cache_vt):
    n_dec = q3.shape[0]
    page_size = cache_kt.shape[3]
    pages_per_block = MOBA_BLOCK // page_size
    n_keys = MOBA_TOPK * MOBA_BLOCK
    row = pl.BlockSpec((None, 1, ATT_WIDTH), lambda b, sel, pt: (b, 0, 0))
    hbm = pl.BlockSpec(memory_space=pl.ANY)
    return pl.pallas_call(
        functools.partial(_sample_attn_kernel, pages_per_block=pages_per_block,
                          page_size=page_size),
        grid_spec=pltpu.PrefetchScalarGridSpec(
            num_scalar_prefetch=2,
            grid=(n_dec,),
            in_specs=[row, pl.BlockSpec((None,) + q_bcast.shape[1:],
                                        lambda b, sel, pt: (b, 0, 0, 0)), row, row, hbm, hbm],
            out_specs=row,
            scratch_shapes=[pltpu.VMEM((2, N_HEADS, HEAD_DIM, n_keys), F32),
                            pltpu.VMEM((2, N_HEADS, HEAD_DIM, n_keys), F32),
                            pltpu.SemaphoreType.DMA((2, 2))],
        ),
        out_shape=jax.ShapeDtypeStruct((n_dec, 1, ATT_WIDTH), F32),
        compiler_params=pltpu.CompilerParams(
            dimension_semantics=("arbitrary",), vmem_limit_bytes=VMEM_LIMIT),
        name="sample_attn",
    )(sel_flat, page_table, q3, q_bcast, k3, v3, cache_kt, cache_vt)


def kernel(x_prompt, x_sample, cache_k, cache_v, state_conv, page_table, g_mix, w_in, conv_w, w_out,
           g_ffn, w_router_group, b_router_group, w_router_expert, b_router_expert, w_gate, w_up,
           w_down, g_final):
    depth = g_mix.shape[0]
    assert depth == 1
    bsz, seq, d = x_prompt.shape
    n_dec, dec_len, _ = x_sample.shape
    assert dec_len == 1
    page_size = cache_k.shape[2]
    past_len = page_table.shape[1] * page_size
    assert past_len % MOBA_BLOCK == 0 and MOBA_BLOCK % page_size == 0
    assert past_len // MOBA_BLOCK >= MOBA_TOPK and seq % MOBA_BLOCK == 0

    g_mix2 = g_mix[0][None, :]
    g_ffn2 = g_ffn[0][None, :]
    g_final2 = g_final[None, :]
    w_in_f, w_out_f = w_in[0], w_out[0]
    w_out_bf = w_out_f.astype(BF16)
    gap = SUBLANES - N_EXPERT_GROUPS
    tail = LANES - SUBLANES - N_EXPERTS
    w_r = jnp.concatenate([w_router_group[0].T, jnp.zeros((gap, d), F32),
                           w_router_expert[0].T, jnp.zeros((tail, d), F32)], axis=0)
    b_r = jnp.concatenate([b_router_group[0], jnp.zeros((gap,), F32),
                           b_router_expert[0], jnp.zeros((tail,), F32)])[:, None]
    cw = conv_w[0]

    cos_s, sin_s = _rope_angles(past_len + jnp.arange(dec_len, dtype=jnp.int32))
    x_s = x_sample.reshape(n_dec, d)
    prev0, prev1 = state_conv[0, :, 0, :], state_conv[0, :, 1, :]
    q_s, k_s, v_s, cy_s, u_s = _sample_inproj(x_s, g_mix2, w_in_f, _rope_row_tables(cos_s, sin_s),
                                              cw, prev0, prev1)
    cache_kt = jnp.transpose(cache_k[0], (0, 2, 3, 1))
    cache_vt = jnp.transpose(cache_v[0], (0, 2, 3, 1))
    q_bcast = jnp.broadcast_to(q_s.reshape(n_dec, N_HEADS, HEAD_DIM, 1),
                               (n_dec, N_HEADS, HEAD_DIM, page_size))

    cos_p, sin_p = _rope_angles(jnp.arange(seq, dtype=jnp.int32))
    conv0 = jnp.zeros((bsz, CONV_K - 1, CONV_CH), F32)
    qt_p, kt_p, vt_p, kb_p, vtb_p, kmeans_p, cy_p, conv_p = _prompt_inproj(
        x_prompt, g_mix2, w_in_f, (cos_p.T, sin_p.T), cw, conv0)
    att_p, sel = _prompt_attn(qt_p, kb_p, vtb_p, kmeans_p, page_table, q_bcast, cache_kt)
    n_p = bsz * seq
    hres_p, hn_p, route_p, idx_p, counts_p = _outproj_router(
        x_prompt.reshape(n_p, d), att_p.reshape(n_p, ATT_WIDTH), cy_p.reshape(n_p, CONV_CH),
        w_out_bf, g_ffn2, w_r, b_r, jnp.zeros((N_EXPERTS, 1), F32), tm=512, precise=False)

    sel_flat = sel[:, :, :MOBA_TOPK].reshape(-1)
    q3 = q_s.reshape(n_dec, 1, ATT_WIDTH)
    att_s = _sample_attn(sel_flat, page_table, q3, q_bcast, k_s.reshape(n_dec, 1, ATT_WIDTH),
                         v_s.reshape(n_dec, 1, ATT_WIDTH), cache_kt, cache_vt)
    hres_s, hn_s, route_s, idx_s, counts = _outproj_router(
        x_s, att_s.reshape(n_dec, ATT_WIDTH), cy_s, w_out_f, g_ffn2, w_r, b_r, counts_p,
        tm=n_dec, precise=True)

    n_assign = EXPERT_TOPK * (n_p + n_dec)
    n_tiles = -(-(n_assign + N_EXPERTS * (MOE_TILE - 1)) // MOE_TILE)
    offsets, tile_expert, n_active, fill_starts = _moe_plan(counts, n_tiles)
    slots_p, slots_s = _moe_slots(idx_p, offsets), _moe_slots(idx_s, offsets)
    xs = _moe_dispatch(fill_starts, n_active, slots_p, hn_p, None, n_tiles)
    xs = _moe_dispatch(fill_starts, n_active, slots_s, hn_s, xs, n_tiles)
    ys = _moe_ffn(tile_expert, n_active, xs, w_gate[0], w_up[0], w_down[0])
    y_p = _moe_combine(slots_p, route_p, hres_p, g_final2, ys)
    y_s = _moe_combine(slots_s, route_s, hres_s, g_final2, ys)

    conv_s = jnp.stack([prev1, u_s], axis=1)
    to_bthd = lambda t: jnp.transpose(t.reshape(bsz, N_HEADS, HEAD_DIM, seq), (0, 3, 1, 2))[None]
    return (y_p.reshape(bsz, seq, d), y_s.reshape(n_dec, dec_len, d),
            to_bthd(kt_p), to_bthd(vt_p), conv_p[None],
            k_s.reshape(1, n_dec, dec_len, N_HEADS, HEAD_DIM),
            v_s.reshape(1, n_dec, dec_len, N_HEADS, HEAD_DIM),
            conv_s[None])
```

```python
import functools

import jax
import jax.numpy as jnp
from jax import lax
from jax.experimental import pallas as pl
from jax.experimental.pallas import tpu as pltpu

F32 = jnp.float32
BF16 = jnp.bfloat16
HIGHEST = lax.Precision.HIGHEST

D_MODEL = 1024
HEAD_DIM = 64
N_HEADS = 8
ATT_WIDTH = N_HEADS * HEAD_DIM
CONV_CH = D_MODEL - ATT_WIDTH
ROPE_DIM = HEAD_DIM // 4
ROPE_HALF = ROPE_DIM // 2
ROPE_THETA = 500000.0
MOBA_BLOCK = 256
MOBA_TOPK = 3
ATT_SCALE = HEAD_DIM ** -0.5
CONV_K = 3
N_EXPERT_GROUPS = 4
EXPERTS_PER_GROUP = 8
N_EXPERTS = N_EXPERT_GROUPS * EXPERTS_PER_GROUP
EXPERT_TOPK = 2
EXPERT_FF = D_MODEL // 4
RMS_EPS = 1e-6

LANES = 128
SUBLANES = 8
HEADS_PER_VREG = LANES // HEAD_DIM
TOKEN_ROWS = D_MODEL // LANES
MXU_WIDTH = 256
MOE_TILE = 512
BF16_SUBLANES = 16
V_AUG_ROWS = HEAD_DIM + BF16_SUBLANES
LOG2_E = 1.4426950408889634
DMA_QUEUES = 2
ISSUE_UNROLL = 8
COPY_STEP_TOKENS = 1024
VMEM_LIMIT = 56 * 1024 * 1024

NEG_INF = float("-inf")


def _rms(x, g):
    ms = jnp.mean(x * x, axis=-1, keepdims=True)
    return x * lax.rsqrt(ms + RMS_EPS) * g


def _mm(a, w, precise):
    if precise:
        return jnp.dot(a, w, precision=HIGHEST, preferred_element_type=F32)
    return jnp.dot(a.astype(BF16), w, preferred_element_type=F32)


NN_DIMS = (((1,), (0,)), ((), ()))
NT_DIMS = (((1,), (1,)), ((), ()))


def _dot_f32(a, b, dims, full):
    if full:
        return lax.dot_general(a, b, dims, precision=HIGHEST, preferred_element_type=F32)
    a_hi, b_hi = a.astype(BF16), b.astype(BF16)
    a_lo = (a - a_hi.astype(F32)).astype(BF16)
    b_lo = (b - b_hi.astype(F32)).astype(BF16)
    dot = functools.partial(lax.dot_general, dimension_numbers=dims, preferred_element_type=F32)
    n = b.shape[1]
    if dims == NN_DIMS and 2 * n <= MXU_WIDTH:
        both = dot(a_hi, jnp.concatenate([b_hi, b_lo], axis=1))
        return both[:, :n] + (both[:, n:] + dot(a_lo, b_hi))
    return dot(a_hi, b_hi) + (dot(a_hi, b_lo) + dot(a_lo, b_hi))


def _rope_rows(a, cos, sin_lo, sin_hi):
    n = a.shape[-1]
    return (a * cos + pltpu.roll(a, n - ROPE_HALF, 1) * sin_lo
            + pltpu.roll(a, ROPE_HALF, 1) * sin_hi)


def _tile_lanes(t, reps):
    return jnp.concatenate([t] * reps, axis=-1)


def _rope_angles(pos):
    inv = ROPE_THETA ** (-jnp.arange(0, ROPE_DIM, 2, dtype=F32) / ROPE_DIM)
    ang = pos.astype(F32)[:, None] * inv[None, :]
    return jnp.cos(ang), jnp.sin(ang)


def _rope_row_tables(cos, sin):
    rows = cos.shape[0]
    ones = jnp.ones((rows, HEAD_DIM - ROPE_DIM), F32)
    zeros = jnp.zeros((rows, HEAD_DIM - ROPE_HALF), F32)
    c = jnp.concatenate([cos, cos, ones], axis=1)
    s_lo = jnp.concatenate([-sin, zeros], axis=1)
    s_hi = jnp.concatenate([jnp.zeros((rows, ROPE_HALF), F32), sin,
                            jnp.zeros((rows, HEAD_DIM - ROPE_DIM), F32)], axis=1)
    rep = lambda t: jnp.concatenate([t] * HEADS_PER_VREG, axis=1)
    return rep(c), rep(s_lo), rep(s_hi)


def _rope_cols(ref, cos_t, sin_t):
    for h in range(N_HEADS):
        r = h * HEAD_DIM
        x1 = ref[r:r + ROPE_HALF, :]
        x2 = ref[r + ROPE_HALF:r + ROPE_DIM, :]
        ref[r:r + ROPE_HALF, :] = x1 * cos_t - x2 * sin_t
        ref[r + ROPE_HALF:r + ROPE_DIM, :] = x2 * cos_t + x1 * sin_t


def _prompt_inproj_kernel(x_ref, g_ref, wqt_ref, wkt_ref, wvt_ref, wc_ref,
                          cost_ref, sint_ref, cw_ref, prev_ref,
                          qt_ref, kt_ref, vt_ref, kb_ref, vtb_ref, km_ref, cy_ref, cn_ref,
                          ubuf, halo, km_acc, *, tt):
    t = pl.program_id(1)
    blk = MOBA_BLOCK
    xn = _rms(x_ref[...], g_ref[...]).astype(BF16)

    qt_ref[...] = lax.dot_general(wqt_ref[...], xn, NT_DIMS, preferred_element_type=F32)
    _rope_cols(qt_ref, cost_ref[...], sint_ref[...])
    kt_ref[...] = lax.dot_general(wkt_ref[...], xn, NT_DIMS, preferred_element_type=F32)
    _rope_cols(kt_ref, cost_ref[...], sint_ref[...])
    vt = lax.dot_general(wvt_ref[...], xn, NT_DIMS, preferred_element_type=F32)
    vt_ref[...] = vt
    for i in range(tt // blk):
        vtb_ref[i] = vt[:, i * blk:(i + 1) * blk].astype(BF16)

    k = kt_ref[...].T
    kb_ref[...] = k.astype(BF16)

    @pl.when(t == 0)
    def _():
        km_acc[...] = jnp.zeros_like(km_acc)

    rid = lax.broadcasted_iota(jnp.int32, km_acc.shape, 0)
    km = km_acc[...]
    for i in range(tt // blk):
        mean_i = jnp.mean(k[i * blk:(i + 1) * blk, :], axis=0, keepdims=True)
        km = jnp.where(rid == t * (tt // blk) + i, mean_i, km)
    km_acc[...] = km
    km_ref[...] = km

    def proj(i):
        return jnp.dot(xn, wc_ref[:, i * CONV_CH:(i + 1) * CONV_CH], preferred_element_type=F32)

    b_gate = proj(0)
    u = proj(1) * proj(2)

    @pl.when(t == 0)
    def _():
        ubuf[SUBLANES - (CONV_K - 1):SUBLANES, :] = prev_ref[...]

    @pl.when(t > 0)
    def _():
        ubuf[0:SUBLANES, :] = halo[...]

    ubuf[SUBLANES:SUBLANES + tt, :] = u
    cw = cw_ref[...]
    conv = (cw[0:1, :] * ubuf[SUBLANES - 2:SUBLANES - 2 + tt, :]
            + cw[1:2, :] * ubuf[SUBLANES - 1:SUBLANES - 1 + tt, :]
            + cw[2:3, :] * u)
    cy_ref[...] = (b_gate * conv).astype(cy_ref.dtype)
    halo[...] = ubuf[tt:tt + SUBLANES, :]
    cn_ref[...] = ubuf[tt + SUBLANES - (CONV_K - 1):tt + SUBLANES, :]


def _prompt_inproj(x, g_mix, w_in, col_tables, conv_w, conv_prev, tt=512):
    bsz, seq, d = x.shape
    nt = seq // tt
    nb = seq // MOBA_BLOCK
    w_bf = w_in.astype(BF16)
    wqt = w_bf[:, :ATT_WIDTH].T
    wkt = w_bf[:, ATT_WIDTH:2 * ATT_WIDTH].T
    wvt = w_bf[:, 2 * ATT_WIDTH:3 * ATT_WIDTH].T
    wc = w_bf[:, 3 * ATT_WIDTH:]
    row_spec = lambda width: pl.BlockSpec((None, tt, width), lambda b, t: (b, t, 0))
    col_spec = pl.BlockSpec((None, ATT_WIDTH, tt), lambda b, t: (b, 0, t))
    tabt_spec = pl.BlockSpec((ROPE_HALF, tt), lambda b, t: (0, t))
    full = lambda shape: pl.BlockSpec(shape, lambda b, t: (0,) * len(shape))
    per_b = lambda shape: pl.BlockSpec((None,) + shape, lambda b, t: (b,) + (0,) * len(shape))
    vtb_spec = pl.BlockSpec((None, tt // MOBA_BLOCK, ATT_WIDTH, MOBA_BLOCK),
                            lambda b, t: (b, t, 0, 0))
    col_shape = jax.ShapeDtypeStruct((bsz, ATT_WIDTH, seq), F32)
    return pl.pallas_call(
        functools.partial(_prompt_inproj_kernel, tt=tt),
        grid=(bsz, nt),
        in_specs=[row_spec(d), full((1, d)), full(wqt.shape), full(wkt.shape), full(wvt.shape),
                  full(wc.shape), tabt_spec, tabt_spec,
                  full((CONV_K, CONV_CH)), per_b((CONV_K - 1, CONV_CH))],
        out_specs=[col_spec, col_spec, col_spec, row_spec(ATT_WIDTH), vtb_spec,
                   per_b((nb, ATT_WIDTH)), row_spec(CONV_CH), per_b((CONV_K - 1, CONV_CH))],
        out_shape=[col_shape, col_shape, col_shape,
                   jax.ShapeDtypeStruct((bsz, seq, ATT_WIDTH), BF16),
                   jax.ShapeDtypeStruct((bsz, nb, ATT_WIDTH, MOBA_BLOCK), BF16),
                   jax.ShapeDtypeStruct((bsz, nb, ATT_WIDTH), F32),
                   jax.ShapeDtypeStruct((bsz, seq, CONV_CH), BF16),
                   jax.ShapeDtypeStruct((bsz, CONV_K - 1, CONV_CH), F32)],
        scratch_shapes=[pltpu.VMEM((tt + SUBLANES, CONV_CH), F32),
                        pltpu.VMEM((SUBLANES, CONV_CH), F32),
                        pltpu.VMEM((nb, ATT_WIDTH), F32)],
        compiler_params=pltpu.CompilerParams(
            dimension_semantics=("arbitrary", "arbitrary"), vmem_limit_bytes=VMEM_LIMIT),
        name="prompt_inproj",
    )(x, g_mix, wqt, wkt, wvt, wc, *col_tables, conv_w, conv_prev)


def _prompt_attn_kernel(pt_ref, qt_ref, kb_ref, vtb_ref, km_ref, qdec_ref, *refs, nb, n_pages):
    page_refs = refs[:n_pages]
    o_ref, sel_ref, s_ref, ot_ref, va_ref, part_ref = refs[n_pages:]
    _select_past_blocks(qdec_ref, page_refs, part_ref, sel_ref)
    blk = MOBA_BLOCK
    means = km_ref[...]
    ones_row = lax.broadcasted_iota(jnp.int32, (V_AUG_ROWS - HEAD_DIM, blk), 0) == 0
    for j in range(nb):
        for h in range(HEADS_PER_VREG):
            va_ref[j, h, 0:HEAD_DIM, :] = vtb_ref[j, h * HEAD_DIM:(h + 1) * HEAD_DIM, :]
            va_ref[j, h, HEAD_DIM:V_AUG_ROWS, :] = jnp.where(ones_row, 1.0, 0.0).astype(BF16)
    feat = lax.broadcasted_iota(jnp.int32, (LANES, 1), 0)
    bid = lax.broadcasted_iota(jnp.int32, (nb, blk), 0)
    key = lax.broadcasted_iota(jnp.int32, (blk, blk), 0)
    qry = lax.broadcasted_iota(jnp.int32, (blk, blk), 1)

    def block_bias(qth, qi):
        if qi <= MOBA_TOPK:
            return [None] * qi
        gates = jnp.dot(means, qth, precision=HIGHEST, preferred_element_type=F32)
        gates = jnp.where(bid < qi, gates, NEG_INF)
        rows = []
        for j in range(qi):
            gj = gates[j:j + 1, :]
            beats = (gates > gj) | ((gates == gj) & (bid < j))
            cnt = jnp.sum(beats.astype(F32), axis=0, keepdims=True)
            rows.append(jnp.where(cnt < MOBA_TOPK, 0.0, NEG_INF))
        return rows

    def scores_pass(qi, h, slot, state):
        qt = qt_ref[:, qi * blk:(qi + 1) * blk]
        qth = jnp.where((feat >= h * HEAD_DIM) & (feat < (h + 1) * HEAD_DIM), qt, 0.0)
        bias = block_bias(qth, qi)
        qs = (qth * (ATT_SCALE * LOG2_E)).astype(BF16)
        m = None
        for j in range(qi + 1):
            s = jnp.dot(kb_ref[j * blk:(j + 1) * blk, :], qs,
                        preferred_element_type=F32)
            if j == qi:
                s = jnp.where(key <= qry, s, NEG_INF)
            elif bias[j] is not None:
                s = s + bias[j]
            s_ref[slot, j] = s
            m_blk = jnp.max(s, axis=0, keepdims=True)
            m = m_blk if m is None else jnp.maximum(m, m_blk)
            yield
        state["m"] = m

    def values_pass(qi, h, slot, state):
        m = state["m"]
        acc = None
        for j in range(qi + 1):
            p = jnp.exp2(s_ref[slot, j] - m).astype(BF16)
            pv = jnp.dot(va_ref[j, h], p, preferred_element_type=F32)
            acc = pv if acc is None else acc + pv
            yield
        ot_ref[h * HEAD_DIM:(h + 1) * HEAD_DIM, qi * blk:(qi + 1) * blk] = (
            acc[0:HEAD_DIM, :] / acc[HEAD_DIM:HEAD_DIM + 1, :])

    def run_interleaved(*gens):
        live = [g for g in gens if g is not None]
        while live:
            for g in list(live):
                if next(g, "done") == "done":
                    live.remove(g)

    pending = None
    for i, (qi, h) in enumerate((qi, h) for qi in range(nb) for h in range(HEADS_PER_VREG)):
        state = {}
        run_interleaved(scores_pass(qi, h, i % 2, state), pending)
        pending = values_pass(qi, h, i % 2, state)
    run_interleaved(pending)
    o_ref[...] = ot_ref[...].T.astype(o_ref.dtype)


def _prompt_attn(qt, kb, vtb, kmeans, page_table, q_dec, cache_kt):
    bsz, seq, _ = kb.shape
    nb = seq // MOBA_BLOCK
    n_pairs = ATT_WIDTH // LANES
    n_dec, n_pages = page_table.shape
    assert n_dec == bsz * n_pairs, "one decode sequence per prompt attention grid step"
    _, n_heads, head_dim, page_size = cache_kt.shape
    n_blocks = n_pages * page_size // MOBA_BLOCK
    dec = lambda b, hp: b * n_pairs + hp

    def page_spec(i):
        return pl.BlockSpec((None, n_heads, head_dim, page_size),
                            lambda b, hp, pt: (pt[dec(b, hp), i], 0, 0, 0))

    return pl.pallas_call(
        functools.partial(_prompt_attn_kernel, nb=nb, n_pages=n_pages),
        grid_spec=pltpu.PrefetchScalarGridSpec(
            num_scalar_prefetch=1,
            grid=(bsz, n_pairs),
            in_specs=[pl.BlockSpec((None, LANES, seq), lambda b, hp, pt: (b, hp, 0)),
                      pl.BlockSpec((None, seq, LANES), lambda b, hp, pt: (b, 0, hp)),
                      pl.BlockSpec((None, nb, LANES, MOBA_BLOCK), lambda b, hp, pt: (b, 0, hp, 0)),
                      pl.BlockSpec((None, nb, LANES), lambda b, hp, pt: (b, 0, hp)),
                      pl.BlockSpec((None, n_heads, head_dim, page_size),
                                   lambda b, hp, pt: (dec(b, hp), 0, 0, 0))]
                     + [page_spec(i) for i in range(n_pages)],
            out_specs=[pl.BlockSpec((None, seq, LANES), lambda b, hp, pt: (b, 0, hp)),
                       pl.BlockSpec((None, n_heads, LANES), lambda b, hp, pt: (dec(b, hp), 0, 0))],
            scratch_shapes=[pltpu.VMEM((2, nb, MOBA_BLOCK, MOBA_BLOCK), F32),
                            pltpu.VMEM((LANES, seq), F32),
                            pltpu.VMEM((nb, HEADS_PER_VREG, V_AUG_ROWS, MOBA_BLOCK), BF16),
                            pltpu.VMEM((n_blocks, n_heads, LANES), F32)],
        ),
        out_shape=[jax.ShapeDtypeStruct((bsz, seq, ATT_WIDTH), BF16),
                   jax.ShapeDtypeStruct((n_dec, n_heads, LANES), jnp.int32)],
        compiler_params=pltpu.CompilerParams(
            dimension_semantics=("arbitrary", "arbitrary"), vmem_limit_bytes=VMEM_LIMIT),
        name="prompt_attn",
    )(page_table, qt, kb, vtb, kmeans, q_dec, *([cache_kt] * n_pages))


def _outproj_router_kernel(x_ref, att_ref, cy_ref, wo_ref, g_ref, wr_ref, br_ref, cnt_in_ref,
                           hres_ref, hn_ref, route_ref, idx_ref, cnt_ref, tri_ref, *, precise):
    step = pl.program_id(0)
    tm = x_ref.shape[0]
    hres = (x_ref[...] + _mm(att_ref[...], wo_ref[0:ATT_WIDTH, :], precise)
            + _mm(cy_ref[...], wo_ref[ATT_WIDTH:, :], precise))
    hres_ref[...] = hres
    hn = _rms(hres, g_ref[...])
    for s in range(TOKEN_ROWS):
        hn_ref[pl.ds(s, tm, stride=TOKEN_ROWS), :] = hn[:, s * LANES:(s + 1) * LANES]

    @pl.when(step == 0)
    def _():
        cnt_ref[...] = cnt_in_ref[...]
        r = lax.broadcasted_iota(jnp.int32, (tm, tm), 0)
        c = lax.broadcasted_iota(jnp.int32, (tm, tm), 1)
        tri_ref[...] = jnp.where(r < c, 1.0, 0.0).astype(BF16)

    lo = _dot_f32(wr_ref[...], hn, NT_DIMS, precise) + br_ref[...]
    sub = lax.broadcasted_iota(jnp.int32, (SUBLANES, tm), 0)
    first_max = lambda v, m: jnp.min(jnp.where(v == m, sub, SUBLANES), axis=0, keepdims=True)
    lg = jnp.where(sub < N_EXPERT_GROUPS, lo[0:SUBLANES], NEG_INF)
    mg = jnp.max(lg, axis=0, keepdims=True)
    g = first_max(lg, mg)
    pg = 1.0 / jnp.sum(jnp.exp(lg - mg), axis=0, keepdims=True)
    le = lo[SUBLANES:2 * SUBLANES]
    for gi in range(1, N_EXPERT_GROUPS):
        le = jnp.where(g == gi, lo[(gi + 1) * SUBLANES:(gi + 2) * SUBLANES], le)
    m1 = jnp.max(le, axis=0, keepdims=True)
    i1 = first_max(le, m1)
    le2 = jnp.where(sub == i1, NEG_INF, le)
    m2 = jnp.max(le2, axis=0, keepdims=True)
    i2 = first_max(le2, m2)
    e2 = jnp.exp(m2 - m1)
    w1 = pg / (1.0 + e2)
    w2 = pg * e2 / (1.0 + e2)
    ex1, ex2 = g * EXPERTS_PER_GROUP + i1, g * EXPERTS_PER_GROUP + i2

    eid = lax.broadcasted_iota(jnp.int32, (N_EXPERTS, tm), 0)
    oh1, oh2 = eid == ex1, eid == ex2
    assigned = jnp.where(oh1 | oh2, 1.0, 0.0)
    before = cnt_ref[...] + jnp.dot(assigned.astype(BF16), tri_ref[...],
                                    preferred_element_type=F32)
    r1 = jnp.sum(jnp.where(oh1, before, 0.0), axis=0, keepdims=True)
    r2 = jnp.sum(jnp.where(oh2, before, 0.0), axis=0, keepdims=True)
    cnt_ref[...] += jnp.sum(assigned, axis=1, keepdims=True)
    zrow = jnp.zeros((1, tm), jnp.int32)
    idx_ref[...] = jnp.concatenate(
        [ex1, ex2, r1.astype(jnp.int32), r2.astype(jnp.int32)] + [zrow] * (SUBLANES - 4), axis=0)
    zf = jnp.zeros((1, tm), F32)
    wrows = jnp.concatenate([zf] * 4 + [w1, w2] + [zf] * 2 + [jnp.zeros((LANES - SUBLANES, tm), F32)],
                            axis=0)
    route_ref[...] = wrows.T


def _outproj_router(x, att, cy, w_out, g_ffn, w_r, b_r, counts_in, tm, precise):
    n, d = x.shape
    rows = lambda width: pl.BlockSpec((tm, width), lambda i: (i, 0))
    full = lambda shape: pl.BlockSpec(shape, lambda i: (0,) * len(shape))
    return pl.pallas_call(
        functools.partial(_outproj_router_kernel, precise=precise),
        grid=(n // tm,),
        in_specs=[rows(d), rows(ATT_WIDTH), rows(CONV_CH), full(w_out.shape), full((1, d)),
                  full(w_r.shape), full((LANES, 1)), full((N_EXPERTS, 1))],
        out_specs=[rows(d), pl.BlockSpec((tm * TOKEN_ROWS, LANES), lambda i: (i, 0)), rows(LANES),
                   pl.BlockSpec((None, SUBLANES, tm), lambda i: (i, 0, 0)), full((N_EXPERTS, 1))],
        out_shape=[jax.ShapeDtypeStruct((n, d), F32),
                   jax.ShapeDtypeStruct((n * TOKEN_ROWS, LANES), F32),
                   jax.ShapeDtypeStruct((n, LANES), F32),
                   jax.ShapeDtypeStruct((n // tm, SUBLANES, tm), jnp.int32),
                   jax.ShapeDtypeStruct((N_EXPERTS, 1), F32)],
        scratch_shapes=[pltpu.VMEM((tm, tm), BF16)],
        compiler_params=pltpu.CompilerParams(
            dimension_semantics=("arbitrary",), vmem_limit_bytes=VMEM_LIMIT),
        name="outproj_router_precise" if precise else "outproj_router",
    )(x, att, cy, w_out, g_ffn, w_r, b_r, counts_in)


def _token_rows(ref, t):
    return ref.at[pl.ds(pl.multiple_of(t * TOKEN_ROWS, TOKEN_ROWS), TOKEN_ROWS), :]


def _start_token_copies(copies, n):
    def start(g, carry):
        for u in range(ISSUE_UNROLL):
            for c in copies(g * ISSUE_UNROLL + u):
                c.start(priority=u % DMA_QUEUES)
        return carry

    lax.fori_loop(0, n // ISSUE_UNROLL, start, 0)


def _moe_slots(idx, offsets):
    experts, inner = idx[:, :EXPERT_TOPK, :], idx[:, EXPERT_TOPK:2 * EXPERT_TOPK, :]
    onehot = experts[..., None] == jnp.arange(N_EXPERTS, dtype=jnp.int32)
    return inner + jnp.sum(jnp.where(onehot, offsets, 0), axis=-1)


def _regroup_steps(slots, tm):
    steps, k, tm0 = slots.shape
    if tm0 >= tm or (steps * tm0) % tm:
        return slots
    per = tm // tm0
    return slots.reshape(steps // per, per, k, tm0).transpose(0, 2, 1, 3).reshape(steps // per, k, tm)


def _moe_dispatch_kernel(fill_ref, na_ref, slot_ref, hn_ref, *refs, tm, n_tiles, zero_fill):
    xs_ref, zbuf, sem = refs[-3:]
    step = pl.program_id(0)

    if zero_fill:
        @pl.when(step == 0)
        def _():
            zbuf[...] = jnp.zeros_like(zbuf)

            def fill(slot):
                return pltpu.make_async_copy(
                    zbuf, xs_ref.at[pl.ds(pl.multiple_of(slot * TOKEN_ROWS, TOKEN_ROWS),
                                          MOE_TILE * TOKEN_ROWS), :], sem.at[1])

            def start_fill(i, carry):
                fill(i * MOE_TILE).start()
                return carry

            def wait_fill(i, carry):
                fill(i * MOE_TILE).wait()
                return carry

            def group_fills(act):
                for e in range(N_EXPERTS):
                    @pl.when(fill_ref[e] >= 0)
                    def _():
                        act(fill(fill_ref[e]))

            group_fills(lambda c: c.start())
            lax.fori_loop(na_ref[0], n_tiles, start_fill, 0)
            group_fills(lambda c: c.wait())
            lax.fori_loop(na_ref[0], n_tiles, wait_fill, 0)

    def copies(t):
        src = _token_rows(hn_ref, t)
        return tuple(pltpu.make_async_copy(src, _token_rows(xs_ref, slot_ref[k, t]), sem.at[0])
                     for k in range(EXPERT_TOPK))

    _start_token_copies(copies, tm)
    for _ in range(EXPERT_TOPK):
        pltpu.make_async_copy(hn_ref, xs_ref.at[pl.ds(0, tm * TOKEN_ROWS), :], sem.at[0]).wait()


def _moe_dispatch(fill_starts, n_active, slots, hn_tiles, xs_prev, n_tiles):
    n_steps, _, tm = slots.shape
    n = n_steps * tm
    zero_fill = xs_prev is None
    hbm = pl.BlockSpec(memory_space=pl.ANY)
    in_specs = [pl.BlockSpec((None, EXPERT_TOPK, tm), lambda i, *_: (i, 0, 0),
                             memory_space=pltpu.SMEM),
                pl.BlockSpec((tm * TOKEN_ROWS, LANES), lambda i, *_: (i, 0))]
    args = [fill_starts, n_active, slots, hn_tiles]
    aliases = {}
    if not zero_fill:
        in_specs.append(hbm)
        args.append(xs_prev)
        aliases = {len(args) - 1: 0}
    return pl.pallas_call(
        functools.partial(_moe_dispatch_kernel, tm=tm, n_tiles=n_tiles, zero_fill=zero_fill),
        grid_spec=pltpu.PrefetchScalarGridSpec(
            num_scalar_prefetch=2,
            grid=(n // tm,),
            in_specs=in_specs,
            out_specs=hbm,
            scratch_shapes=[pltpu.VMEM((MOE_TILE * TOKEN_ROWS, LANES), F32),
                            pltpu.SemaphoreType.DMA((2,))],
        ),
        out_shape=jax.ShapeDtypeStruct((n_tiles * MOE_TILE * TOKEN_ROWS, LANES), F32),
        input_output_aliases=aliases,
        compiler_params=pltpu.CompilerParams(
            dimension_semantics=("arbitrary",), vmem_limit_bytes=VMEM_LIMIT),
        name="moe_dispatch" if zero_fill else "moe_dispatch_more",
    )(*args)


def _moe_ffn_kernel(te_ref, na_ref, xs_ref, wg_ref, wu_ref, wd_ref, ys_ref, wg_b, wu_b, wd_b):
    i = pl.program_id(0)
    tg = MOE_TILE
    active = i < na_ref[0]

    @pl.when(active)
    def _():
        @pl.when((i == 0) | (te_ref[i] != te_ref[jnp.maximum(i - 1, 0)]))
        def _():
            wg_b[...] = wg_ref[...].astype(BF16)
            wu_b[...] = wu_ref[...].astype(BF16)
            wd_b[...] = wd_ref[...].astype(BF16)

        x = jnp.concatenate(
            [xs_ref[pl.ds(s, tg, stride=TOKEN_ROWS), :].astype(BF16) for s in range(TOKEN_ROWS)],
            axis=1)
        hg = jnp.dot(x, wg_b[...], preferred_element_type=F32)
        hu = jnp.dot(x, wu_b[...], preferred_element_type=F32)
        h = hg * (1.0 / (1.0 + jnp.exp(-hg))) * hu
        y = jnp.dot(h.astype(BF16), wd_b[...], preferred_element_type=F32)
        for s in range(TOKEN_ROWS):
            ys_ref[pl.ds(s, tg, stride=TOKEN_ROWS), :] = y[:, s * LANES:(s + 1) * LANES]

    @pl.when(jnp.logical_not(active))
    def _():
        ys_ref[...] = jnp.zeros_like(ys_ref)


def _moe_ffn(tile_expert, n_active, xs, w_gate, w_up, w_down):
    n_tiles = tile_expert.shape[0]
    tile_rows = MOE_TILE * TOKEN_ROWS
    d, ff = w_gate.shape[1:]
    last = lambda na: na[0] - 1
    w_spec = lambda shape: pl.BlockSpec(
        (None,) + shape, lambda i, te, na: (te[jnp.minimum(i, last(na))], 0, 0))
    return pl.pallas_call(
        _moe_ffn_kernel,
        grid_spec=pltpu.PrefetchScalarGridSpec(
            num_scalar_prefetch=2,
            grid=(n_tiles,),
            in_specs=[pl.BlockSpec((tile_rows, LANES),
                                   lambda i, te, na: (jnp.minimum(i, last(na)), 0)),
                      w_spec((d, ff)), w_spec((d, ff)), w_spec((ff, d))],
            out_specs=pl.BlockSpec((tile_rows, LANES), lambda i, te, na: (i, 0)),
            scratch_shapes=[pltpu.VMEM((d, ff), BF16), pltpu.VMEM((d, ff), BF16),
                            pltpu.VMEM((ff, d), BF16)],
        ),
        out_shape=jax.ShapeDtypeStruct(xs.shape, F32),
        compiler_params=pltpu.CompilerParams(
            dimension_semantics=("arbitrary",), vmem_limit_bytes=VMEM_LIMIT),
        name="moe_ffn",
    )(tile_expert, n_active, xs, w_gate, w_up, w_down)


def _moe_combine_kernel(slot_ref, route_ref, hres_ref, gf_ref, ys_ref, y_ref, buf, sem, *, tm):
    def copies(t):
        return tuple(pltpu.make_async_copy(_token_rows(ys_ref, slot_ref[k, t]),
                                           _token_rows(buf.at[k], t), sem.at[k])
                     for k in range(EXPERT_TOPK))

    _start_token_copies(copies, tm)
    for k in range(EXPERT_TOPK):
        pltpu.make_async_copy(ys_ref.at[pl.ds(0, tm * TOKEN_ROWS), :], buf.at[k], sem.at[k]).wait()

    route = route_ref[...]
    w1, w2 = route[:, 4:5], route[:, 5:6]
    for s in range(TOKEN_ROWS):
        cols = slice(s * LANES, (s + 1) * LANES)
        y_ref[:, cols] = (hres_ref[:, cols]
                          + w1 * buf[0, pl.ds(s, tm, stride=TOKEN_ROWS), :]
                          + w2 * buf[1, pl.ds(s, tm, stride=TOKEN_ROWS), :])
    y_ref[...] = _rms(y_ref[...], gf_ref[...])


def _moe_combine(slots, route, hres, g_final, ys):
    n, d = hres.shape
    tm = slots.shape[2]
    rows = lambda width: pl.BlockSpec((tm, width), lambda i: (i, 0))
    return pl.pallas_call(
        functools.partial(_moe_combine_kernel, tm=tm),
        grid=(n // tm,),
        in_specs=[pl.BlockSpec((None, EXPERT_TOPK, tm), lambda i: (i, 0, 0),
                               memory_space=pltpu.SMEM),
                  rows(LANES), rows(d), pl.BlockSpec((1, d), lambda i: (0, 0)),
                  pl.BlockSpec(memory_space=pl.ANY)],
        out_specs=rows(d),
        out_shape=jax.ShapeDtypeStruct((n, d), F32),
        scratch_shapes=[pltpu.VMEM((EXPERT_TOPK, tm * TOKEN_ROWS, LANES), F32),
                        pltpu.SemaphoreType.DMA((EXPERT_TOPK,))],
        compiler_params=pltpu.CompilerParams(
            dimension_semantics=("arbitrary",), vmem_limit_bytes=VMEM_LIMIT),
        name="moe_combine",
    )(slots, route, hres, g_final, ys)


def _moe_plan(counts, n_tiles):
    counts = counts[:, 0].astype(jnp.int32)
    padded = (counts + MOE_TILE - 1) // MOE_TILE * MOE_TILE
    ends = jnp.cumsum(padded)
    offsets = ends - padded
    n_active = (ends[-1:] // MOE_TILE).astype(jnp.int32)
    tile_start = jnp.arange(n_tiles, dtype=jnp.int32) * MOE_TILE
    tile_expert = jnp.minimum(jnp.sum(tile_start[:, None] >= ends[None, :], axis=1), N_EXPERTS - 1)
    fill_starts = jnp.where(padded > 0, ends - MOE_TILE, -1)
    return offsets, tile_expert.astype(jnp.int32), n_active, fill_starts.astype(jnp.int32)


def _sample_inproj_kernel(x_ref, g_ref, w_ref, cos_ref, slo_ref, shi_ref, cw_ref, p0_ref, p1_ref,
                          q_ref, k_ref, v_ref, cy_ref, u_ref):
    xn = _rms(x_ref[...], g_ref[...])
    reps = ATT_WIDTH // LANES
    cos = _tile_lanes(cos_ref[...], reps)
    slo = _tile_lanes(slo_ref[...], reps)
    shi = _tile_lanes(shi_ref[...], reps)

    def proj(i):
        return jnp.dot(xn, w_ref[:, i * ATT_WIDTH:(i + 1) * ATT_WIDTH], precision=HIGHEST,
                       preferred_element_type=F32)

    q_ref[...] = _rope_rows(proj(0), cos, slo, shi)
    k_ref[...] = _rope_rows(proj(1), cos, slo, shi)
    v_ref[...] = proj(2)
    b_gate = proj(3)
    u = proj(4) * proj(5)
    u_ref[...] = u
    cw = cw_ref[...]
    cy_ref[...] = b_gate * (cw[0:1, :] * p0_ref[...] + cw[1:2, :] * p1_ref[...] + cw[2:3, :] * u)


def _sample_inproj(x, g_mix, w_in, tables, conv_w, prev0, prev1):
    n = x.shape[0]
    out = jax.ShapeDtypeStruct((n, ATT_WIDTH), F32)
    return pl.pallas_call(
        _sample_inproj_kernel,
        out_shape=[out] * 5,
        compiler_params=pltpu.CompilerParams(vmem_limit_bytes=VMEM_LIMIT),
        name="sample_inproj",
    )(x, g_mix, w_in, *tables, conv_w, prev0, prev1)


def _select_past_blocks(q_ref, page_refs, part, sel_ref):
    n_blocks = part.shape[0]
    pages_per_block = len(page_refs) // n_blocks
    block_rows = pages_per_block * page_refs[0].shape[-1]
    for i in range(n_blocks):
        acc = None
        for c in range(HEAD_DIM // SUBLANES):
            rows = slice(c * SUBLANES, (c + 1) * SUBLANES)
            k_rows = page_refs[i * pages_per_block][:, rows, :]
            for p in range(1, pages_per_block):
                k_rows = k_rows + page_refs[i * pages_per_block + p][:, rows, :]
            prod = k_rows * q_ref[:, rows, :]
            acc = prod if acc is None else acc + prod
        part[i] = jnp.sum(acc, axis=1)

    lane = lax.broadcasted_iota(jnp.int32, (N_HEADS, LANES), 1)
    acc = jnp.full((N_HEADS, LANES), NEG_INF, F32)
    for j in range(n_blocks):
        gate_j = jnp.sum(part[j], axis=1, keepdims=True) * (1.0 / block_rows)
        acc = jnp.where(lane == j, gate_j, acc)
    out = jnp.zeros((N_HEADS, LANES), jnp.int32)
    for r in range(MOBA_TOPK):
        best = jnp.max(acc, axis=1, keepdims=True)
        idx = jnp.min(jnp.where(acc == best, lane, LANES), axis=1, keepdims=True)
        out = jnp.where(lane == r, idx, out)
        acc = jnp.where(lane == idx, NEG_INF, acc)
    sel_ref[...] = out


def _sample_attn_kernel(sel_ref, pt_ref, q_ref, qb_ref, kn_ref, vn_ref, ck_ref, cv_ref, o_ref,
                        kbuf, vbuf, sem, *, pages_per_block, page_size):
    b = pl.program_id(0)
    nb = pl.num_programs(0)

    def copies(bb, slot):
        out = []
        for h in range(N_HEADS):
            for r in range(MOBA_TOPK):
                block = sel_ref[(bb * N_HEADS + h) * MOBA_TOPK + r]
                for p in range(pages_per_block):
                    page = pt_ref[bb, block * pages_per_block + p]
                    dst = pl.ds((r * pages_per_block + p) * page_size, page_size)
                    out.append(pltpu.make_async_copy(ck_ref.at[page, h], kbuf.at[slot, h, :, dst],
                                                     sem.at[slot, 0]))
                    out.append(pltpu.make_async_copy(cv_ref.at[page, h], vbuf.at[slot, h, :, dst],
                                                     sem.at[slot, 1]))
        return out

    @pl.when(b == 0)
    def _():
        for c in copies(0, 0):
            c.start()

    @pl.when(b + 1 < nb)
    def _():
        for c in copies(b + 1, (b + 1) % 2):
            c.start()

    slot = b % 2
    for c in copies(b, slot):
        c.wait()

    q = q_ref[...]
    qb = qb_ref[...]
    kn = kn_ref[...]
    vn = vn_ref[...]
    n_keys = kbuf.shape[-1]
    head = lax.broadcasted_iota(jnp.int32, (N_HEADS, ATT_WIDTH), 0)
    feat = lax.broadcasted_iota(jnp.int32, (N_HEADS, ATT_WIDTH), 1)
    own = (feat >= head * HEAD_DIM) & (feat < (head + 1) * HEAD_DIM)

    s = jnp.concatenate(
        [jnp.sum(kbuf[slot, :, :, c * LANES:(c + 1) * LANES] * qb, axis=1)
         for c in range(n_keys // LANES)], axis=1) * ATT_SCALE
    s_self = jnp.sum(jnp.where(own, q * kn, 0.0), axis=1, keepdims=True) * ATT_SCALE
    m = jnp.maximum(jnp.max(s, axis=1, keepdims=True), s_self)
    p = jnp.exp(s - m)
    p_self = jnp.exp(s_self - m)
    l = jnp.sum(p, axis=1, keepdims=True) + p_self
    pv = _dot_f32(p, vbuf[slot].reshape(N_HEADS * HEAD_DIM, n_keys), NT_DIMS, False)
    o = (pv + p_self * vn) / l
    o_ref[...] = jnp.sum(jnp.where(own, o, 0.0), axis=0, keepdims=True)


def _sample_attn(sel_flat, page_table, q3, q_bcast, k3, v3, cache_kt, cache_vt):
    n_dec = q3.shape[0]
    page_size = cache_kt.shape[3]
    pages_per_block = MOBA_BLOCK // page_size
    n_keys = MOBA_TOPK * MOBA_BLOCK
    row = pl.BlockSpec((None, 1, ATT_WIDTH), lambda b, sel, pt: (b, 0, 0))
    hbm = pl.BlockSpec(memory_space=pl.ANY)
    return pl.pallas_call(
        functools.partial(_sample_attn_kernel, pages_per_block=pages_per_block,
                          page_size=page_size),
        grid_spec=pltpu.PrefetchScalarGridSpec(
            num_scalar_prefetch=2,
            grid=(n_dec,),
            in_specs=[row, pl.BlockSpec((None,) + q_bcast.shape[1:],
                                        lambda b, sel, pt: (b, 0, 0, 0)), row, row, hbm, hbm],
            out_specs=row,
            scratch_shapes=[pltpu.VMEM((2, N_HEADS, HEAD_DIM, n_keys), F32),
                            pltpu.VMEM((2, N_HEADS, HEAD_DIM, n_keys), F32),
                            pltpu.SemaphoreType.DMA((2, 2))],
        ),
        out_shape=jax.ShapeDtypeStruct((n_dec, 1, ATT_WIDTH), F32),
        compiler_params=pltpu.CompilerParams(
            dimension_semantics=("arbitrary",), vmem_limit_bytes=VMEM_LIMIT),
        name="sample_attn",
    )(sel_flat, page_table, q3, q_bcast, k3, v3, cache_kt, cache_vt)


def kernel(x_prompt, x_sample, cache_k, cache_v, state_conv, page_table, g_mix, w_in, conv_w, w_out,
           g_ffn, w_router_group, b_router_group, w_router_expert, b_router_expert, w_gate, w_up,
           w_down, g_final):
    depth = g_mix.shape[0]
    assert depth == 1
    bsz, seq, d = x_prompt.shape
    n_dec, dec_len, _ = x_sample.shape
    assert dec_len == 1
    page_size = cache_k.shape[2]
    past_len = page_table.shape[1] * page_size
    assert past_len % MOBA_BLOCK == 0 and MOBA_BLOCK % page_size == 0
    assert past_len // MOBA_BLOCK >= MOBA_TOPK and seq % MOBA_BLOCK == 0

    g_mix2 = g_mix[0][None, :]
    g_ffn2 = g_ffn[0][None, :]
    g_final2 = g_final[None, :]
    w_in_f, w_out_f = w_in[0], w_out[0]
    w_out_bf = w_out_f.astype(BF16)
    gap = SUBLANES - N_EXPERT_GROUPS
    tail = LANES - SUBLANES - N_EXPERTS
    w_r = jnp.concatenate([w_router_group[0].T, jnp.zeros((gap, d), F32),
                           w_router_expert[0].T, jnp.zeros((tail, d), F32)], axis=0)
    b_r = jnp.concatenate([b_router_group[0], jnp.zeros((gap,), F32),
                           b_router_expert[0], jnp.zeros((tail,), F32)])[:, None]
    cw = conv_w[0]

    cos_s, sin_s = _rope_angles(past_len + jnp.arange(dec_len, dtype=jnp.int32))
    x_s = x_sample.reshape(n_dec, d)
    prev0, prev1 = state_conv[0, :, 0, :], state_conv[0, :, 1, :]
    q_s, k_s, v_s, cy_s, u_s = _sample_inproj(x_s, g_mix2, w_in_f, _rope_row_tables(cos_s, sin_s),
                                              cw, prev0, prev1)
    cache_kt = jnp.transpose(cache_k[0], (0, 2, 3, 1))
    cache_vt = jnp.transpose(cache_v[0], (0, 2, 3, 1))
    q_bcast = jnp.broadcast_to(q_s.reshape(n_dec, N_HEADS, HEAD_DIM, 1),
                               (n_dec, N_HEADS, HEAD_DIM, page_size))

    cos_p, sin_p = _rope_angles(jnp.arange(seq, dtype=jnp.int32))
    conv0 = jnp.zeros((bsz, CONV_K - 1, CONV_CH), F32)
    qt_p, kt_p, vt_p, kb_p, vtb_p, kmeans_p, cy_p, conv_p = _prompt_inproj(
        x_prompt, g_mix2, w_in_f, (cos_p.T, sin_p.T), cw, conv0)
    att_p, sel = _prompt_attn(qt_p, kb_p, vtb_p, kmeans_p, page_table, q_bcast, cache_kt)
    n_p = bsz * seq
    hres_p, hn_p, route_p, idx_p, counts_p = _outproj_router(
        x_prompt.reshape(n_p, d), att_p.reshape(n_p, ATT_WIDTH), cy_p.reshape(n_p, CONV_CH),
        w_out_bf, g_ffn2, w_r, b_r, jnp.zeros((N_EXPERTS, 1), F32), tm=512, precise=False)

    sel_flat = sel[:, :, :MOBA_TOPK].reshape(-1)
    q3 = q_s.reshape(n_dec, 1, ATT_WIDTH)
    att_s = _sample_attn(sel_flat, page_table, q3, q_bcast, k_s.reshape(n_dec, 1, ATT_WIDTH),
                         v_s.reshape(n_dec, 1, ATT_WIDTH), cache_kt, cache_vt)
    hres_s, hn_s, route_s, idx_s, counts = _outproj_router(
        x_s, att_s.reshape(n_dec, ATT_WIDTH), cy_s, w_out_f, g_ffn2, w_r, b_r, counts_p,
        tm=n_dec, precise=True)

    n_assign = EXPERT_TOPK * (n_p + n_dec)
    n_tiles = -(-(n_assign + N_EXPERTS * (MOE_TILE - 1)) // MOE_TILE)
    offsets, tile_expert, n_active, fill_starts = _moe_plan(counts, n_tiles)
    slots_p, slots_s = _moe_slots(idx_p, offsets), _moe_slots(idx_s, offsets)
    slots_p = _regroup_steps(slots_p, COPY_STEP_TOKENS)
    xs = _moe_dispatch(fill_starts, n_active, slots_p, hn_p, None, n_tiles)
    xs = _moe_dispatch(fill_starts, n_active, slots_s, hn_s, xs, n_tiles)
    ys = _moe_ffn(tile_expert, n_active, xs, w_gate[0], w_up[0], w_down[0])
    y_p = _moe_combine(slots_p, route_p, hres_p, g_final2, ys)
    y_s = _moe_combine(slots_s, route_s, hres_s, g_final2, ys)

    conv_s = jnp.stack([prev1, u_s], axis=1)
    to_bthd = lambda t: jnp.transpose(t.reshape(bsz, N_HEADS, HEAD_DIM, seq), (0, 3, 1, 2))[None]
    return (y_p.reshape(bsz, seq, d), y_s.reshape(n_dec, dec_len, d),
            to_bthd(kt_p), to_bthd(vt_p), conv_p[None],
            k_s.reshape(1, n_dec, dec_len, N_HEADS, HEAD_DIM),
            v_s.reshape(1, n_dec, dec_len, N_HEADS, HEAD_DIM),
            conv_s[None])
```

```python
import functools

import jax
import jax.numpy as jnp
from jax import lax
from jax.experimental import pallas as pl
from jax.experimental.pallas import tpu as pltpu

F32 = jnp.float32
BF16 = jnp.bfloat16
HIGHEST = lax.Precision.HIGHEST

D_MODEL = 1024
HEAD_DIM = 64
N_HEADS = 8
ATT_WIDTH = N_HEADS * HEAD_DIM
CONV_CH = D_MODEL - ATT_WIDTH
ROPE_DIM = HEAD_DIM // 4
ROPE_HALF = ROPE_DIM // 2
ROPE_THETA = 500000.0
MOBA_BLOCK = 256
MOBA_TOPK = 3
ATT_SCALE = HEAD_DIM ** -0.5
CONV_K = 3
N_EXPERT_GROUPS = 4
EXPERTS_PER_GROUP = 8
N_EXPERTS = N_EXPERT_GROUPS * EXPERTS_PER_GROUP
EXPERT_TOPK = 2
EXPERT_FF = D_MODEL // 4
RMS_EPS = 1e-6

LANES = 128
SUBLANES = 8
HEADS_PER_VREG = LANES // HEAD_DIM
TOKEN_ROWS = D_MODEL // LANES
MXU_WIDTH = 256
TOKEN_TILE = 512
MOE_TILE = 512
BF16_SUBLANES = 16
V_AUG_ROWS = HEAD_DIM + BF16_SUBLANES
LOG2_E = 1.4426950408889634
SCORE_SLOTS = 2
DMA_QUEUES = 2
ISSUE_UNROLL = 16
COPY_STEP_TOKENS = 1024
VMEM_LIMIT = 56 * 1024 * 1024

NEG_INF = float("-inf")


def _rms(x, g):
    ms = jnp.mean(x * x, axis=-1, keepdims=True)
    return x * lax.rsqrt(ms + RMS_EPS) * g


def _mm(a, w, precise):
    if precise:
        return jnp.dot(a, w, precision=HIGHEST, preferred_element_type=F32)
    return jnp.dot(a.astype(BF16), w, preferred_element_type=F32)


NN_DIMS = (((1,), (0,)), ((), ()))
NT_DIMS = (((1,), (1,)), ((), ()))


def _dot_f32(a, b, dims, full):
    if full:
        return lax.dot_general(a, b, dims, precision=HIGHEST, preferred_element_type=F32)
    a_hi, b_hi = a.astype(BF16), b.astype(BF16)
    a_lo = (a - a_hi.astype(F32)).astype(BF16)
    b_lo = (b - b_hi.astype(F32)).astype(BF16)
    dot = functools.partial(lax.dot_general, dimension_numbers=dims, preferred_element_type=F32)
    n = b.shape[1]
    if dims == NN_DIMS and 2 * n <= MXU_WIDTH:
        both = dot(a_hi, jnp.concatenate([b_hi, b_lo], axis=1))
        return both[:, :n] + (both[:, n:] + dot(a_lo, b_hi))
    return dot(a_hi, b_hi) + (dot(a_hi, b_lo) + dot(a_lo, b_hi))


def _rope_rows(a, cos, sin_lo, sin_hi):
    n = a.shape[-1]
    return (a * cos + pltpu.roll(a, n - ROPE_HALF, 1) * sin_lo
            + pltpu.roll(a, ROPE_HALF, 1) * sin_hi)


def _tile_lanes(t, reps):
    return jnp.concatenate([t] * reps, axis=-1)


def _rope_angles(pos):
    inv = ROPE_THETA ** (-jnp.arange(0, ROPE_DIM, 2, dtype=F32) / ROPE_DIM)
    ang = pos.astype(F32)[:, None] * inv[None, :]
    return jnp.cos(ang), jnp.sin(ang)


def _rope_row_tables(cos, sin):
    rows = cos.shape[0]
    ones = jnp.ones((rows, HEAD_DIM - ROPE_DIM), F32)
    zeros = jnp.zeros((rows, HEAD_DIM - ROPE_HALF), F32)
    c = jnp.concatenate([cos, cos, ones], axis=1)
    s_lo = jnp.concatenate([-sin, zeros], axis=1)
    s_hi = jnp.concatenate([jnp.zeros((rows, ROPE_HALF), F32), sin,
                            jnp.zeros((rows, HEAD_DIM - ROPE_DIM), F32)], axis=1)
    rep = lambda t: jnp.concatenate([t] * HEADS_PER_VREG, axis=1)
    return rep(c), rep(s_lo), rep(s_hi)


def _rope_cols(ref, cos_t, sin_t):
    for h in range(N_HEADS):
        r = h * HEAD_DIM
        x1 = ref[r:r + ROPE_HALF, :]
        x2 = ref[r + ROPE_HALF:r + ROPE_DIM, :]
        ref[r:r + ROPE_HALF, :] = x1 * cos_t - x2 * sin_t
        ref[r + ROPE_HALF:r + ROPE_DIM, :] = x2 * cos_t + x1 * sin_t


def _prompt_inproj_kernel(x_ref, g_ref, wqt_ref, wkt_ref, wvt_ref, wc_ref,
                          cost_ref, sint_ref, cw_ref, prev_ref,
                          qt_ref, kt_ref, vt_ref, kb_ref, vtb_ref, km_ref, cy_ref, cn_ref,
                          ubuf, halo, km_acc, *, tt):
    t = pl.program_id(1)
    blk = MOBA_BLOCK
    xn = _rms(x_ref[...], g_ref[...]).astype(BF16)

    qt_ref[...] = lax.dot_general(wqt_ref[...], xn, NT_DIMS, preferred_element_type=F32)
    _rope_cols(qt_ref, cost_ref[...], sint_ref[...])
    kt_ref[...] = lax.dot_general(wkt_ref[...], xn, NT_DIMS, preferred_element_type=F32)
    _rope_cols(kt_ref, cost_ref[...], sint_ref[...])
    vt = lax.dot_general(wvt_ref[...], xn, NT_DIMS, preferred_element_type=F32)
    vt_ref[...] = vt
    for i in range(tt // blk):
        vtb_ref[i] = vt[:, i * blk:(i + 1) * blk].astype(BF16)

    k = kt_ref[...].T
    kb_ref[...] = k.astype(BF16)

    @pl.when(t == 0)
    def _():
        km_acc[...] = jnp.zeros_like(km_acc)

    rid = lax.broadcasted_iota(jnp.int32, km_acc.shape, 0)
    km = km_acc[...]
    for i in range(tt // blk):
        mean_i = jnp.mean(k[i * blk:(i + 1) * blk, :], axis=0, keepdims=True)
        km = jnp.where(rid == t * (tt // blk) + i, mean_i, km)
    km_acc[...] = km
    km_ref[...] = km

    def proj(i):
        return jnp.dot(xn, wc_ref[:, i * CONV_CH:(i + 1) * CONV_CH], preferred_element_type=F32)

    b_gate = proj(0)
    u = proj(1) * proj(2)

    @pl.when(t == 0)
    def _():
        ubuf[SUBLANES - (CONV_K - 1):SUBLANES, :] = prev_ref[...]

    @pl.when(t > 0)
    def _():
        ubuf[0:SUBLANES, :] = halo[...]

    ubuf[SUBLANES:SUBLANES + tt, :] = u
    cw = cw_ref[...]
    conv = (cw[0:1, :] * ubuf[SUBLANES - 2:SUBLANES - 2 + tt, :]
            + cw[1:2, :] * ubuf[SUBLANES - 1:SUBLANES - 1 + tt, :]
            + cw[2:3, :] * u)
    cy_ref[...] = (b_gate * conv).astype(cy_ref.dtype)
    halo[...] = ubuf[tt:tt + SUBLANES, :]
    cn_ref[...] = ubuf[tt + SUBLANES - (CONV_K - 1):tt + SUBLANES, :]


def _prompt_inproj(x, g_mix, w_in, col_tables, conv_w, conv_prev, tt=TOKEN_TILE):
    bsz, seq, d = x.shape
    nt = seq // tt
    nb = seq // MOBA_BLOCK
    w_bf = w_in.astype(BF16)
    wqt = w_bf[:, :ATT_WIDTH].T
    wkt = w_bf[:, ATT_WIDTH:2 * ATT_WIDTH].T
    wvt = w_bf[:, 2 * ATT_WIDTH:3 * ATT_WIDTH].T
    wc = w_bf[:, 3 * ATT_WIDTH:]
    row_spec = lambda width: pl.BlockSpec((None, tt, width), lambda b, t: (b, t, 0))
    col_spec = pl.BlockSpec((None, ATT_WIDTH, tt), lambda b, t: (b, 0, t))
    tabt_spec = pl.BlockSpec((ROPE_HALF, tt), lambda b, t: (0, t))
    full = lambda shape: pl.BlockSpec(shape, lambda b, t: (0,) * len(shape))
    per_b = lambda shape: pl.BlockSpec((None,) + shape, lambda b, t: (b,) + (0,) * len(shape))
    vtb_spec = pl.BlockSpec((None, tt // MOBA_BLOCK, ATT_WIDTH, MOBA_BLOCK),
                            lambda b, t: (b, t, 0, 0))
    col_shape = jax.ShapeDtypeStruct((bsz, ATT_WIDTH, seq), F32)
    return pl.pallas_call(
        functools.partial(_prompt_inproj_kernel, tt=tt),
        grid=(bsz, nt),
        in_specs=[row_spec(d), full((1, d)), full(wqt.shape), full(wkt.shape), full(wvt.shape),
                  full(wc.shape), tabt_spec, tabt_spec,
                  full((CONV_K, CONV_CH)), per_b((CONV_K - 1, CONV_CH))],
        out_specs=[col_spec, col_spec, col_spec, row_spec(ATT_WIDTH), vtb_spec,
                   per_b((nb, ATT_WIDTH)), row_spec(CONV_CH), per_b((CONV_K - 1, CONV_CH))],
        out_shape=[col_shape, col_shape, col_shape,
                   jax.ShapeDtypeStruct((bsz, seq, ATT_WIDTH), BF16),
                   jax.ShapeDtypeStruct((bsz, nb, ATT_WIDTH, MOBA_BLOCK), BF16),
                   jax.ShapeDtypeStruct((bsz, nb, ATT_WIDTH), F32),
                   jax.ShapeDtypeStruct((bsz, seq, CONV_CH), BF16),
                   jax.ShapeDtypeStruct((bsz, CONV_K - 1, CONV_CH), F32)],
        scratch_shapes=[pltpu.VMEM((tt + SUBLANES, CONV_CH), F32),
                        pltpu.VMEM((SUBLANES, CONV_CH), F32),
                        pltpu.VMEM((nb, ATT_WIDTH), F32)],
        compiler_params=pltpu.CompilerParams(
            dimension_semantics=("arbitrary", "arbitrary"), vmem_limit_bytes=VMEM_LIMIT),
        name="prompt_inproj",
    )(x, g_mix, wqt, wkt, wvt, wc, *col_tables, conv_w, conv_prev)


def _prompt_attn_kernel(pt_ref, qt_ref, kb_ref, vtb_ref, km_ref, qdec_ref, *refs, nb, n_pages):
    page_refs = refs[:n_pages]
    o_ref, sel_ref, s_ref, ot_ref, va_ref, part_ref = refs[n_pages:]
    _select_past_blocks(qdec_ref, page_refs, part_ref, sel_ref)
    blk = MOBA_BLOCK
    means = km_ref[...]
    ones_row = lax.broadcasted_iota(jnp.int32, (V_AUG_ROWS - HEAD_DIM, blk), 0) == 0
    for j in range(nb):
        for h in range(HEADS_PER_VREG):
            va_ref[j, h, 0:HEAD_DIM, :] = vtb_ref[j, h * HEAD_DIM:(h + 1) * HEAD_DIM, :]
            va_ref[j, h, HEAD_DIM:V_AUG_ROWS, :] = jnp.where(ones_row, 1.0, 0.0).astype(BF16)
    feat = lax.broadcasted_iota(jnp.int32, (LANES, 1), 0)
    bid = lax.broadcasted_iota(jnp.int32, (nb, blk), 0)
    key = lax.broadcasted_iota(jnp.int32, (blk, blk), 0)
    qry = lax.broadcasted_iota(jnp.int32, (blk, blk), 1)

    def block_bias(qth, qi):
        if qi <= MOBA_TOPK:
            return [None] * qi
        gates = jnp.dot(means, qth, precision=HIGHEST, preferred_element_type=F32)
        gates = jnp.where(bid < qi, gates, NEG_INF)
        rows = []
        for j in range(qi):
            gj = gates[j:j + 1, :]
            beats = (gates > gj) | ((gates == gj) & (bid < j))
            cnt = jnp.sum(beats.astype(F32), axis=0, keepdims=True)
            rows.append(jnp.where(cnt < MOBA_TOPK, 0.0, NEG_INF))
        return rows

    def scores_pass(qi, h, slot, state):
        qt = qt_ref[:, qi * blk:(qi + 1) * blk]
        qth = jnp.where((feat >= h * HEAD_DIM) & (feat < (h + 1) * HEAD_DIM), qt, 0.0)
        bias = block_bias(qth, qi)
        qs = (qth * (ATT_SCALE * LOG2_E)).astype(BF16)
        m = None
        for j in range(qi + 1):
            s = jnp.dot(kb_ref[j * blk:(j + 1) * blk, :], qs,
                        preferred_element_type=F32)
            if j == qi:
                s = jnp.where(key <= qry, s, NEG_INF)
            elif bias[j] is not None:
                s = s + bias[j]
            s_ref[slot, j] = s
            m_blk = jnp.max(s, axis=0, keepdims=True)
            m = m_blk if m is None else jnp.maximum(m, m_blk)
            yield
        state["m"] = m

    def values_pass(qi, h, slot, state):
        m = state["m"]
        acc = None
        for j in range(qi + 1):
            p = jnp.exp2(s_ref[slot, j] - m).astype(BF16)
            pv = jnp.dot(va_ref[j, h], p, preferred_element_type=F32)
            acc = pv if acc is None else acc + pv
            yield
        ot_ref[h * HEAD_DIM:(h + 1) * HEAD_DIM, qi * blk:(qi + 1) * blk] = (
            acc[0:HEAD_DIM, :] / acc[HEAD_DIM:HEAD_DIM + 1, :])

    def run_interleaved(*gens):
        live = [g for g in gens if g is not None]
        while live:
            for g in list(live):
                if next(g, "done") == "done":
                    live.remove(g)

    pending = None
    for i, (qi, h) in enumerate((qi, h) for qi in range(nb) for h in range(HEADS_PER_VREG)):
        state = {}
        run_interleaved(scores_pass(qi, h, i % SCORE_SLOTS, state), pending)
        pending = values_pass(qi, h, i % SCORE_SLOTS, state)
    run_interleaved(pending)
    o_ref[...] = ot_ref[...].T.astype(o_ref.dtype)


def _prompt_attn(qt, kb, vtb, kmeans, page_table, q_dec, cache_kt):
    bsz, seq, _ = kb.shape
    nb = seq // MOBA_BLOCK
    n_pairs = ATT_WIDTH // LANES
    n_dec, n_pages = page_table.shape
    assert n_dec == bsz * n_pairs, "one decode sequence per prompt attention grid step"
    _, n_heads, head_dim, page_size = cache_kt.shape
    n_blocks = n_pages * page_size // MOBA_BLOCK
    dec = lambda b, hp: b * n_pairs + hp

    def page_spec(i):
        return pl.BlockSpec((None, n_heads, head_dim, page_size),
                            lambda b, hp, pt: (pt[dec(b, hp), i], 0, 0, 0))

    return pl.pallas_call(
        functools.partial(_prompt_attn_kernel, nb=nb, n_pages=n_pages),
        grid_spec=pltpu.PrefetchScalarGridSpec(
            num_scalar_prefetch=1,
            grid=(bsz, n_pairs),
            in_specs=[pl.BlockSpec((None, LANES, seq), lambda b, hp, pt: (b, hp, 0)),
                      pl.BlockSpec((None, seq, LANES), lambda b, hp, pt: (b, 0, hp)),
                      pl.BlockSpec((None, nb, LANES, MOBA_BLOCK), lambda b, hp, pt: (b, 0, hp, 0)),
                      pl.BlockSpec((None, nb, LANES), lambda b, hp, pt: (b, 0, hp)),
                      pl.BlockSpec((None, n_heads, head_dim, page_size),
                                   lambda b, hp, pt: (dec(b, hp), 0, 0, 0))]
                     + [page_spec(i) for i in range(n_pages)],
            out_specs=[pl.BlockSpec((None, seq, LANES), lambda b, hp, pt: (b, 0, hp)),
                       pl.BlockSpec((None, n_heads, LANES), lambda b, hp, pt: (dec(b, hp), 0, 0))],
            scratch_shapes=[pltpu.VMEM((SCORE_SLOTS, nb, MOBA_BLOCK, MOBA_BLOCK), F32),
                            pltpu.VMEM((LANES, seq), F32),
                            pltpu.VMEM((nb, HEADS_PER_VREG, V_AUG_ROWS, MOBA_BLOCK), BF16),
                            pltpu.VMEM((n_blocks, n_heads, LANES), F32)],
        ),
        out_shape=[jax.ShapeDtypeStruct((bsz, seq, ATT_WIDTH), BF16),
                   jax.ShapeDtypeStruct((n_dec, n_heads, LANES), jnp.int32)],
        compiler_params=pltpu.CompilerParams(
            dimension_semantics=("arbitrary", "arbitrary"), vmem_limit_bytes=VMEM_LIMIT),
        name="prompt_attn",
    )(page_table, qt, kb, vtb, kmeans, q_dec, *([cache_kt] * n_pages))


def _outproj_router_kernel(x_ref, att_ref, cy_ref, wo_ref, g_ref, wr_ref, br_ref, cnt_in_ref,
                           hres_ref, hn_ref, route_ref, idx_ref, cnt_ref, tri_ref, *, precise):
    step = pl.program_id(0)
    tm = x_ref.shape[0]
    hres = (x_ref[...] + _mm(att_ref[...], wo_ref[0:ATT_WIDTH, :], precise)
            + _mm(cy_ref[...], wo_ref[ATT_WIDTH:, :], precise))
    hres_ref[...] = hres
    hn = _rms(hres, g_ref[...])
    for s in range(TOKEN_ROWS):
        hn_ref[pl.ds(s, tm, stride=TOKEN_ROWS), :] = hn[:, s * LANES:(s + 1) * LANES]

    @pl.when(step == 0)
    def _():
        cnt_ref[...] = cnt_in_ref[...]
        r = lax.broadcasted_iota(jnp.int32, (tm, tm), 0)
        c = lax.broadcasted_iota(jnp.int32, (tm, tm), 1)
        tri_ref[...] = jnp.where(r < c, 1.0, 0.0).astype(BF16)

    lo = _dot_f32(wr_ref[...], hn, NT_DIMS, precise) + br_ref[...]
    sub = lax.broadcasted_iota(jnp.int32, (SUBLANES, tm), 0)
    first_max = lambda v, m: jnp.min(jnp.where(v == m, sub, SUBLANES), axis=0, keepdims=True)
    lg = jnp.where(sub < N_EXPERT_GROUPS, lo[0:SUBLANES], NEG_INF)
    mg = jnp.max(lg, axis=0, keepdims=True)
    g = first_max(lg, mg)
    pg = 1.0 / jnp.sum(jnp.exp(lg - mg), axis=0, keepdims=True)
    le = lo[SUBLANES:2 * SUBLANES]
    for gi in range(1, N_EXPERT_GROUPS):
        le = jnp.where(g == gi, lo[(gi + 1) * SUBLANES:(gi + 2) * SUBLANES], le)
    m1 = jnp.max(le, axis=0, keepdims=True)
    i1 = first_max(le, m1)
    le2 = jnp.where(sub == i1, NEG_INF, le)
    m2 = jnp.max(le2, axis=0, keepdims=True)
    i2 = first_max(le2, m2)
    e2 = jnp.exp(m2 - m1)
    w1 = pg / (1.0 + e2)
    w2 = pg * e2 / (1.0 + e2)
    ex1, ex2 = g * EXPERTS_PER_GROUP + i1, g * EXPERTS_PER_GROUP + i2

    eid = lax.broadcasted_iota(jnp.int32, (N_EXPERTS, tm), 0)
    oh1, oh2 = eid == ex1, eid == ex2
    assigned = jnp.where(oh1 | oh2, 1.0, 0.0)
    before = cnt_ref[...] + jnp.dot(assigned.astype(BF16), tri_ref[...],
                                    preferred_element_type=F32)
    r1 = jnp.sum(jnp.where(oh1, before, 0.0), axis=0, keepdims=True)
    r2 = jnp.sum(jnp.where(oh2, before, 0.0), axis=0, keepdims=True)
    cnt_ref[...] += jnp.sum(assigned, axis=1, keepdims=True)
    zrow = jnp.zeros((1, tm), jnp.int32)
    idx_ref[...] = jnp.concatenate(
        [ex1, ex2, r1.astype(jnp.int32), r2.astype(jnp.int32)] + [zrow] * (SUBLANES - 4), axis=0)
    zf = jnp.zeros((1, tm), F32)
    wrows = jnp.concatenate([zf] * 4 + [w1, w2] + [zf] * 2 + [jnp.zeros((LANES - SUBLANES, tm), F32)],
                            axis=0)
    route_ref[...] = wrows.T


def _outproj_router(x, att, cy, w_out, g_ffn, w_r, b_r, counts_in, tm, precise):
    n, d = x.shape
    rows = lambda width: pl.BlockSpec((tm, width), lambda i: (i, 0))
    full = lambda shape: pl.BlockSpec(shape, lambda i: (0,) * len(shape))
    return pl.pallas_call(
        functools.partial(_outproj_router_kernel, precise=precise),
        grid=(n // tm,),
        in_specs=[rows(d), rows(ATT_WIDTH), rows(CONV_CH), full(w_out.shape), full((1, d)),
                  full(w_r.shape), full((LANES, 1)), full((N_EXPERTS, 1))],
        out_specs=[rows(d), pl.BlockSpec((tm * TOKEN_ROWS, LANES), lambda i: (i, 0)), rows(LANES),
                   pl.BlockSpec((None, SUBLANES, tm), lambda i: (i, 0, 0)), full((N_EXPERTS, 1))],
        out_shape=[jax.ShapeDtypeStruct((n, d), F32),
                   jax.ShapeDtypeStruct((n * TOKEN_ROWS, LANES), F32),
                   jax.ShapeDtypeStruct((n, LANES), F32),
                   jax.ShapeDtypeStruct((n // tm, SUBLANES, tm), jnp.int32),
                   jax.ShapeDtypeStruct((N_EXPERTS, 1), F32)],
        scratch_shapes=[pltpu.VMEM((tm, tm), BF16)],
        compiler_params=pltpu.CompilerParams(
            dimension_semantics=("arbitrary",), vmem_limit_bytes=VMEM_LIMIT),
        name="outproj_router_precise" if precise else "outproj_router",
    )(x, att, cy, w_out, g_ffn, w_r, b_r, counts_in)


def _token_rows(ref, t):
    return ref.at[pl.ds(pl.multiple_of(t * TOKEN_ROWS, TOKEN_ROWS), TOKEN_ROWS), :]


def _start_token_copies(copies, n):
    def start(g, carry):
        for u in range(ISSUE_UNROLL):
            for c in copies(g * ISSUE_UNROLL + u):
                c.start(priority=u % DMA_QUEUES)
        return carry

    lax.fori_loop(0, n // ISSUE_UNROLL, start, 0)


def _moe_slots(idx, offsets):
    experts, inner = idx[:, :EXPERT_TOPK, :], idx[:, EXPERT_TOPK:2 * EXPERT_TOPK, :]
    onehot = experts[..., None] == jnp.arange(N_EXPERTS, dtype=jnp.int32)
    return inner + jnp.sum(jnp.where(onehot, offsets, 0), axis=-1)


def _regroup_steps(slots, tm):
    steps, k, tm0 = slots.shape
    if tm0 >= tm or (steps * tm0) % tm:
        return slots
    per = tm // tm0
    return slots.reshape(steps // per, per, k, tm0).transpose(0, 2, 1, 3).reshape(steps // per, k, tm)


def _moe_dispatch_kernel(fill_ref, na_ref, slot_ref, hn_ref, *refs, tm, n_tiles, zero_fill):
    xs_ref, zbuf, sem = refs[-3:]
    step = pl.program_id(0)

    if zero_fill:
        @pl.when(step == 0)
        def _():
            zbuf[...] = jnp.zeros_like(zbuf)

            def fill(slot):
                return pltpu.make_async_copy(
                    zbuf, xs_ref.at[pl.ds(pl.multiple_of(slot * TOKEN_ROWS, TOKEN_ROWS),
                                          MOE_TILE * TOKEN_ROWS), :], sem.at[1])

            def start_fill(i, carry):
                fill(i * MOE_TILE).start()
                return carry

            def wait_fill(i, carry):
                fill(i * MOE_TILE).wait()
                return carry

            def group_fills(act):
                for e in range(N_EXPERTS):
                    @pl.when(fill_ref[e] >= 0)
                    def _():
                        act(fill(fill_ref[e]))

            group_fills(lambda c: c.start())
            lax.fori_loop(na_ref[0], n_tiles, start_fill, 0)
            group_fills(lambda c: c.wait())
            lax.fori_loop(na_ref[0], n_tiles, wait_fill, 0)

    def copies(t):
        src = _token_rows(hn_ref, t)
        return tuple(pltpu.make_async_copy(src, _token_rows(xs_ref, slot_ref[k, t]), sem.at[0])
                     for k in range(EXPERT_TOPK))

    _start_token_copies(copies, tm)
    for _ in range(EXPERT_TOPK):
        pltpu.make_async_copy(hn_ref, xs_ref.at[pl.ds(0, tm * TOKEN_ROWS), :], sem.at[0]).wait()


def _moe_dispatch(fill_starts, n_active, slots, hn_tiles, xs_prev, n_tiles):
    n_steps, _, tm = slots.shape
    n = n_steps * tm
    zero_fill = xs_prev is None
    hbm = pl.BlockSpec(memory_space=pl.ANY)
    in_specs = [pl.BlockSpec((None, EXPERT_TOPK, tm), lambda i, *_: (i, 0, 0),
                             memory_space=pltpu.SMEM),
                pl.BlockSpec((tm * TOKEN_ROWS, LANES), lambda i, *_: (i, 0))]
    args = [fill_starts, n_active, slots, hn_tiles]
    aliases = {}
    if not zero_fill:
        in_specs.append(hbm)
        args.append(xs_prev)
        aliases = {len(args) - 1: 0}
    return pl.pallas_call(
        functools.partial(_moe_dispatch_kernel, tm=tm, n_tiles=n_tiles, zero_fill=zero_fill),
        grid_spec=pltpu.PrefetchScalarGridSpec(
            num_scalar_prefetch=2,
            grid=(n // tm,),
            in_specs=in_specs,
            out_specs=hbm,
            scratch_shapes=[pltpu.VMEM((MOE_TILE * TOKEN_ROWS, LANES), F32),
                            pltpu.SemaphoreType.DMA((2,))],
        ),
        out_shape=jax.ShapeDtypeStruct((n_tiles * MOE_TILE * TOKEN_ROWS, LANES), F32),
        input_output_aliases=aliases,
        compiler_params=pltpu.CompilerParams(
            dimension_semantics=("arbitrary",), vmem_limit_bytes=VMEM_LIMIT),
        name="moe_dispatch" if zero_fill else "moe_dispatch_more",
    )(*args)


def _moe_ffn_kernel(te_ref, na_ref, xs_ref, wg_ref, wu_ref, wd_ref, ys_ref, wg_b, wu_b, wd_b):
    i = pl.program_id(0)
    tg = MOE_TILE
    active = i < na_ref[0]

    @pl.when(active)
    def _():
        @pl.when((i == 0) | (te_ref[i] != te_ref[jnp.maximum(i - 1, 0)]))
        def _():
            wg_b[...] = wg_ref[...].astype(BF16)
            wu_b[...] = wu_ref[...].astype(BF16)
            wd_b[...] = wd_ref[...].astype(BF16)

        x = jnp.concatenate(
            [xs_ref[pl.ds(s, tg, stride=TOKEN_ROWS), :].astype(BF16) for s in range(TOKEN_ROWS)],
            axis=1)
        hg = jnp.dot(x, wg_b[...], preferred_element_type=F32)
        hu = jnp.dot(x, wu_b[...], preferred_element_type=F32)
        h = hg * (1.0 / (1.0 + jnp.exp(-hg))) * hu
        y = jnp.dot(h.astype(BF16), wd_b[...], preferred_element_type=F32)
        for s in range(TOKEN_ROWS):
            ys_ref[pl.ds(s, tg, stride=TOKEN_ROWS), :] = y[:, s * LANES:(s + 1) * LANES]


def _moe_ffn(tile_expert, n_active, xs, w_gate, w_up, w_down):
    n_tiles = tile_expert.shape[0]
    tile_rows = MOE_TILE * TOKEN_ROWS
    d, ff = w_gate.shape[1:]
    last = lambda na: na[0] - 1
    w_spec = lambda shape: pl.BlockSpec(
        (None,) + shape, lambda i, te, na: (te[jnp.minimum(i, last(na))], 0, 0))
    tile_spec = pl.BlockSpec((tile_rows, LANES), lambda i, te, na: (jnp.minimum(i, last(na)), 0))
    return pl.pallas_call(
        _moe_ffn_kernel,
        grid_spec=pltpu.PrefetchScalarGridSpec(
            num_scalar_prefetch=2,
            grid=(n_tiles,),
            in_specs=[tile_spec, w_spec((d, ff)), w_spec((d, ff)), w_spec((ff, d))],
            out_specs=tile_spec,
            scratch_shapes=[pltpu.VMEM((d, ff), BF16), pltpu.VMEM((d, ff), BF16),
                            pltpu.VMEM((ff, d), BF16)],
        ),
        out_shape=jax.ShapeDtypeStruct(xs.shape, F32),
        input_output_aliases={2: 0},
        compiler_params=pltpu.CompilerParams(
            dimension_semantics=("arbitrary",), vmem_limit_bytes=VMEM_LIMIT),
        name="moe_ffn",
    )(tile_expert, n_active, xs, w_gate, w_up, w_down)


def _moe_combine_kernel(slot_ref, route_ref, hres_ref, gf_ref, ys_ref, y_ref, buf, sem, *, tm):
    def copies(t):
        return tuple(pltpu.make_async_copy(_token_rows(ys_ref, slot_ref[k, t]),
                                           _token_rows(buf.at[k], t), sem.at[k])
                     for k in range(EXPERT_TOPK))

    _start_token_copies(copies, tm)
    for k in range(EXPERT_TOPK):
        pltpu.make_async_copy(ys_ref.at[pl.ds(0, tm * TOKEN_ROWS), :], buf.at[k], sem.at[k]).wait()

    route = route_ref[...]
    w1, w2 = route[:, 4:5], route[:, 5:6]
    for s in range(TOKEN_ROWS):
        cols = slice(s * LANES, (s + 1) * LANES)
        y_ref[:, cols] = (hres_ref[:, cols]
                          + w1 * buf[0, pl.ds(s, tm, stride=TOKEN_ROWS), :]
                          + w2 * buf[1, pl.ds(s, tm, stride=TOKEN_ROWS), :])
    y_ref[...] = _rms(y_ref[...], gf_ref[...])


def _moe_combine(slots, route, hres, g_final, ys):
    n, d = hres.shape
    tm = slots.shape[2]
    rows = lambda width: pl.BlockSpec((tm, width), lambda i: (i, 0))
    return pl.pallas_call(
        functools.partial(_moe_combine_kernel, tm=tm),
        grid=(n // tm,),
        in_specs=[pl.BlockSpec((None, EXPERT_TOPK, tm), lambda i: (i, 0, 0),
                               memory_space=pltpu.SMEM),
                  rows(LANES), rows(d), pl.BlockSpec((1, d), lambda i: (0, 0)),
                  pl.BlockSpec(memory_space=pl.ANY)],
        out_specs=rows(d),
        out_shape=jax.ShapeDtypeStruct((n, d), F32),
        scratch_shapes=[pltpu.VMEM((EXPERT_TOPK, tm * TOKEN_ROWS, LANES), F32),
                        pltpu.SemaphoreType.DMA((EXPERT_TOPK,))],
        compiler_params=pltpu.CompilerParams(
            dimension_semantics=("arbitrary",), vmem_limit_bytes=VMEM_LIMIT),
        name="moe_combine",
    )(slots, route, hres, g_final, ys)


def _moe_plan(counts, n_tiles):
    counts = counts[:, 0].astype(jnp.int32)
    padded = (counts + MOE_TILE - 1) // MOE_TILE * MOE_TILE
    ends = jnp.cumsum(padded)
    offsets = ends - padded
    n_active = (ends[-1:] // MOE_TILE).astype(jnp.int32)
    tile_start = jnp.arange(n_tiles, dtype=jnp.int32) * MOE_TILE
    tile_expert = jnp.minimum(jnp.sum(tile_start[:, None] >= ends[None, :], axis=1), N_EXPERTS - 1)
    fill_starts = jnp.where(padded > 0, ends - MOE_TILE, -1)
    return offsets, tile_expert.astype(jnp.int32), n_active, fill_starts.astype(jnp.int32)


def _sample_inproj_kernel(x_ref, g_ref, w_ref, cos_ref, slo_ref, shi_ref, cw_ref, p0_ref, p1_ref,
                          q_ref, k_ref, v_ref, cy_ref, u_ref):
    xn = _rms(x_ref[...], g_ref[...])
    reps = ATT_WIDTH // LANES
    cos = _tile_lanes(cos_ref[...], reps)
    slo = _tile_lanes(slo_ref[...], reps)
    shi = _tile_lanes(shi_ref[...], reps)

    def proj(i):
        return _dot_f32(xn, w_ref[:, i * ATT_WIDTH:(i + 1) * ATT_WIDTH], NN_DIMS, False)

    q_ref[...] = _rope_rows(proj(0), cos, slo, shi)
    k_ref[...] = _rope_rows(proj(1), cos, slo, shi)
    v_ref[...] = proj(2)
    b_gate = proj(3)
    u = proj(4) * proj(5)
    u_ref[...] = u
    cw = cw_ref[...]
    cy_ref[...] = b_gate * (cw[0:1, :] * p0_ref[...] + cw[1:2, :] * p1_ref[...] + cw[2:3, :] * u)


def _sample_inproj(x, g_mix, w_in, tables, conv_w, prev0, prev1):
    n = x.shape[0]
    out = jax.ShapeDtypeStruct((n, ATT_WIDTH), F32)
    return pl.pallas_call(
        _sample_inproj_kernel,
        out_shape=[out] * 5,
        compiler_params=pltpu.CompilerParams(vmem_limit_bytes=VMEM_LIMIT),
        name="sample_inproj",
    )(x, g_mix, w_in, *tables, conv_w, prev0, prev1)


def _select_past_blocks(q_ref, page_refs, part, sel_ref):
    n_blocks = part.shape[0]
    pages_per_block = len(page_refs) // n_blocks
    block_rows = pages_per_block * page_refs[0].shape[-1]
    for i in range(n_blocks):
        acc = None
        for c in range(HEAD_DIM // SUBLANES):
            rows = slice(c * SUBLANES, (c + 1) * SUBLANES)
            k_rows = page_refs[i * pages_per_block][:, rows, :]
            for p in range(1, pages_per_block):
                k_rows = k_rows + page_refs[i * pages_per_block + p][:, rows, :]
            prod = k_rows * q_ref[:, rows, :]
            acc = prod if acc is None else acc + prod
        part[i] = jnp.sum(acc, axis=1)

    lane = lax.broadcasted_iota(jnp.int32, (N_HEADS, LANES), 1)
    acc = jnp.full((N_HEADS, LANES), NEG_INF, F32)
    for j in range(n_blocks):
        gate_j = jnp.sum(part[j], axis=1, keepdims=True) * (1.0 / block_rows)
        acc = jnp.where(lane == j, gate_j, acc)
    out = jnp.zeros((N_HEADS, LANES), jnp.int32)
    for r in range(MOBA_TOPK):
        best = jnp.max(acc, axis=1, keepdims=True)
        idx = jnp.min(jnp.where(acc == best, lane, LANES), axis=1, keepdims=True)
        out = jnp.where(lane == r, idx, out)
        acc = jnp.where(lane == idx, NEG_INF, acc)
    sel_ref[...] = out


def _sample_attn_kernel(sel_ref, pt_ref, q_ref, qb_ref, kn_ref, vn_ref, ck_ref, cv_ref, o_ref,
                        kbuf, vbuf, sem, *, pages_per_block, page_size):
    b = pl.program_id(0)
    nb = pl.num_programs(0)

    def copies(bb, slot):
        out = []
        for h in range(N_HEADS):
            for r in range(MOBA_TOPK):
                block = sel_ref[(bb * N_HEADS + h) * MOBA_TOPK + r]
                for p in range(pages_per_block):
                    page = pt_ref[bb, block * pages_per_block + p]
                    dst = pl.ds((r * pages_per_block + p) * page_size, page_size)
                    out.append(pltpu.make_async_copy(ck_ref.at[page, h], kbuf.at[slot, h, :, dst],
                                                     sem.at[slot, 0]))
                    out.append(pltpu.make_async_copy(cv_ref.at[page, h], vbuf.at[slot, h, :, dst],
                                                     sem.at[slot, 1]))
        return out

    @pl.when(b == 0)
    def _():
        for c in copies(0, 0):
            c.start()

    @pl.when(b + 1 < nb)
    def _():
        for c in copies(b + 1, (b + 1) % 2):
            c.start()

    slot = b % 2
    for c in copies(b, slot):
        c.wait()

    q = q_ref[...]
    qb = qb_ref[...]
    kn = kn_ref[...]
    vn = vn_ref[...]
    n_keys = kbuf.shape[-1]
    head = lax.broadcasted_iota(jnp.int32, (N_HEADS, ATT_WIDTH), 0)
    feat = lax.broadcasted_iota(jnp.int32, (N_HEADS, ATT_WIDTH), 1)
    own = (feat >= head * HEAD_DIM) & (feat < (head + 1) * HEAD_DIM)

    s = jnp.concatenate(
        [jnp.sum(kbuf[slot, :, :, c * LANES:(c + 1) * LANES] * qb, axis=1)
         for c in range(n_keys // LANES)], axis=1) * ATT_SCALE
    s_self = jnp.sum(jnp.where(own, q * kn, 0.0), axis=1, keepdims=True) * ATT_SCALE
    m = jnp.maximum(jnp.max(s, axis=1, keepdims=True), s_self)
    p = jnp.exp(s - m)
    p_self = jnp.exp(s_self - m)
    l = jnp.sum(p, axis=1, keepdims=True) + p_self
    pv = _dot_f32(p, vbuf[slot].reshape(N_HEADS * HEAD_DIM, n_keys), NT_DIMS, False)
    o = (pv + p_self * vn) / l
    o_ref[...] = jnp.sum(jnp.where(own, o, 0.0), axis=0, keepdims=True)


def _sample_attn(sel_flat, page_table, q3, q_bcast, k3, v3, cache_kt, cache_vt):
    n_dec = q3.shape[0]
    page_size = cache_kt.shape[3]
    pages_per_block = MOBA_BLOCK // page_size
    n_keys = MOBA_TOPK * MOBA_BLOCK
    row = pl.BlockSpec((None, 1, ATT_WIDTH), lambda b, sel, pt: (b, 0, 0))
    hbm = pl.BlockSpec(memory_space=pl.ANY)
    return pl.pallas_call(
        functools.partial(_sample_attn_kernel, pages_per_block=pages_per_block,
                          page_size=page_size),
        grid_spec=pltpu.PrefetchScalarGridSpec(
            num_scalar_prefetch=2,
            grid=(n_dec,),
            in_specs=[row, pl.BlockSpec((None,) + q_bcast.shape[1:],
                                        lambda b, sel, pt: (b, 0, 0, 0)), row, row, hbm, hbm],
            out_specs=row,
            scratch_shapes=[pltpu.VMEM((2, N_HEADS, HEAD_DIM, n_keys), F32),
                            pltpu.VMEM((2, N_HEADS, HEAD_DIM, n_keys), F32),
                            pltpu.SemaphoreType.DMA((2, 2))],
        ),
        out_shape=jax.ShapeDtypeStruct((n_dec, 1, ATT_WIDTH), F32),
        compiler_params=pltpu.CompilerParams(
            dimension_semantics=("arbitrary",), vmem_limit_bytes=VMEM_LIMIT),
        name="sample_attn",
    )(sel_flat, page_table, q3, q_bcast, k3, v3, cache_kt, cache_vt)


def kernel(x_prompt, x_sample, cache_k, cache_v, state_conv, page_table, g_mix, w_in, conv_w, w_out,
           g_ffn, w_router_group, b_router_group, w_router_expert, b_router_expert, w_gate, w_up,
           w_down, g_final):
    depth = g_mix.shape[0]
    assert depth == 1
    bsz, seq, d = x_prompt.shape
    n_dec, dec_len, _ = x_sample.shape
    assert dec_len == 1
    page_size = cache_k.shape[2]
    past_len = page_table.shape[1] * page_size
    assert past_len % MOBA_BLOCK == 0 and MOBA_BLOCK % page_size == 0
    assert past_len // MOBA_BLOCK >= MOBA_TOPK and seq % MOBA_BLOCK == 0

    g_mix2 = g_mix[0][None, :]
    g_ffn2 = g_ffn[0][None, :]
    g_final2 = g_final[None, :]
    w_in_f, w_out_f = w_in[0], w_out[0]
    w_out_bf = w_out_f.astype(BF16)
    gap = SUBLANES - N_EXPERT_GROUPS
    tail = LANES - SUBLANES - N_EXPERTS
    w_r = jnp.concatenate([w_router_group[0].T, jnp.zeros((gap, d), F32),
                           w_router_expert[0].T, jnp.zeros((tail, d), F32)], axis=0)
    b_r = jnp.concatenate([b_router_group[0], jnp.zeros((gap,), F32),
                           b_router_expert[0], jnp.zeros((tail,), F32)])[:, None]
    cw = conv_w[0]

    cos_s, sin_s = _rope_angles(past_len + jnp.arange(dec_len, dtype=jnp.int32))
    x_s = x_sample.reshape(n_dec, d)
    prev0, prev1 = state_conv[0, :, 0, :], state_conv[0, :, 1, :]
    q_s, k_s, v_s, cy_s, u_s = _sample_inproj(x_s, g_mix2, w_in_f, _rope_row_tables(cos_s, sin_s),
                                              cw, prev0, prev1)
    cache_kt = jnp.transpose(cache_k[0], (0, 2, 3, 1))
    cache_vt = jnp.transpose(cache_v[0], (0, 2, 3, 1))
    q_bcast = jnp.broadcast_to(q_s.reshape(n_dec, N_HEADS, HEAD_DIM, 1),
                               (n_dec, N_HEADS, HEAD_DIM, page_size))

    cos_p, sin_p = _rope_angles(jnp.arange(seq, dtype=jnp.int32))
    conv0 = jnp.zeros((bsz, CONV_K - 1, CONV_CH), F32)
    qt_p, kt_p, vt_p, kb_p, vtb_p, kmeans_p, cy_p, conv_p = _prompt_inproj(
        x_prompt, g_mix2, w_in_f, (cos_p.T, sin_p.T), cw, conv0)
    att_p, sel = _prompt_attn(qt_p, kb_p, vtb_p, kmeans_p, page_table, q_bcast, cache_kt)
    n_p = bsz * seq
    hres_p, hn_p, route_p, idx_p, counts_p = _outproj_router(
        x_prompt.reshape(n_p, d), att_p.reshape(n_p, ATT_WIDTH), cy_p.reshape(n_p, CONV_CH),
        w_out_bf, g_ffn2, w_r, b_r, jnp.zeros((N_EXPERTS, 1), F32), tm=TOKEN_TILE, precise=False)

    sel_flat = sel[:, :, :MOBA_TOPK].reshape(-1)
    q3 = q_s.reshape(n_dec, 1, ATT_WIDTH)
    att_s = _sample_attn(sel_flat, page_table, q3, q_bcast, k_s.reshape(n_dec, 1, ATT_WIDTH),
                         v_s.reshape(n_dec, 1, ATT_WIDTH), cache_kt, cache_vt)
    hres_s, hn_s, route_s, idx_s, counts = _outproj_router(
        x_s, att_s.reshape(n_dec, ATT_WIDTH), cy_s, w_out_f, g_ffn2, w_r, b_r, counts_p,
        tm=n_dec, precise=True)

    n_assign = EXPERT_TOPK * (n_p + n_dec)
    n_tiles = -(-(n_assign + N_EXPERTS * (MOE_TILE - 1)) // MOE_TILE)
    offsets, tile_expert, n_active, fill_starts = _moe_plan(counts, n_tiles)
    slots_p, slots_s = _moe_slots(idx_p, offsets), _moe_slots(idx_s, offsets)
    slots_p = _regroup_steps(slots_p, COPY_STEP_TOKENS)
    xs = _moe_dispatch(fill_starts, n_active, slots_p, hn_p, None, n_tiles)
    xs = _moe_dispatch(fill_starts, n_active, slots_s, hn_s, xs, n_tiles)
    ys = _moe_ffn(tile_expert, n_active, xs, w_gate[0], w_up[0], w_down[0])
    y_p = _moe_combine(slots_p, route_p, hres_p, g_final2, ys)
    y_s = _moe_combine(slots_s, route_s, hres_s, g_final2, ys)

    conv_s = jnp.stack([prev1, u_s], axis=1)
    to_bthd = lambda t: jnp.transpose(t.reshape(bsz, N_HEADS, HEAD_DIM, seq), (0, 3, 1, 2))[None]
    return (y_p.reshape(bsz, seq, d), y_s.reshape(n_dec, dec_len, d),
            to_bthd(kt_p), to_bthd(vt_p), conv_p[None],
            k_s.reshape(1, n_dec, dec_len, N_HEADS, HEAD_DIM),
            v_s.reshape(1, n_dec, dec_len, N_HEADS, HEAD_DIM),
            conv_s[None])
```

```python
import functools

import jax
import jax.numpy as jnp
from jax import lax
from jax.experimental import pallas as pl
from jax.experimental.pallas import tpu as pltpu

F32 = jnp.float32
BF16 = jnp.bfloat16
HIGHEST = lax.Precision.HIGHEST

D_MODEL = 1024
HEAD_DIM = 64
N_HEADS = 8
ATT_WIDTH = N_HEADS * HEAD_DIM
CONV_CH = D_MODEL - ATT_WIDTH
ROPE_DIM = HEAD_DIM // 4
ROPE_HALF = ROPE_DIM // 2
ROPE_THETA = 500000.0
MOBA_BLOCK = 256
MOBA_TOPK = 3
ATT_SCALE = HEAD_DIM ** -0.5
CONV_K = 3
N_EXPERT_GROUPS = 4
EXPERTS_PER_GROUP = 8
N_EXPERTS = N_EXPERT_GROUPS * EXPERTS_PER_GROUP
EXPERT_TOPK = 2
EXPERT_FF = D_MODEL // 4
RMS_EPS = 1e-6

LANES = 128
SUBLANES = 8
HEADS_PER_VREG = LANES // HEAD_DIM
TOKEN_ROWS = D_MODEL // LANES
MXU_WIDTH = 256
TOKEN_TILE = 512
MOE_TILE = 512
BF16_SUBLANES = 16
V_AUG_ROWS = HEAD_DIM + BF16_SUBLANES
LOG2_E = 1.4426950408889634
SCORE_SLOTS = 2
DMA_QUEUES = 2
ISSUE_UNROLL = 16
COPY_STEP_TOKENS = 1024
COMBINE_GROUP = 32
VMEM_LIMIT = 56 * 1024 * 1024

NEG_INF = float("-inf")


def _rms(x, g):
    ms = jnp.mean(x * x, axis=-1, keepdims=True)
    return x * lax.rsqrt(ms + RMS_EPS) * g


def _mm(a, w, precise):
    if precise:
        return jnp.dot(a, w, precision=HIGHEST, preferred_element_type=F32)
    return jnp.dot(a.astype(BF16), w, preferred_element_type=F32)


NN_DIMS = (((1,), (0,)), ((), ()))
NT_DIMS = (((1,), (1,)), ((), ()))


def _dot_f32(a, b, dims, full):
    if full:
        return lax.dot_general(a, b, dims, precision=HIGHEST, preferred_element_type=F32)
    a_hi, b_hi = a.astype(BF16), b.astype(BF16)
    a_lo = (a - a_hi.astype(F32)).astype(BF16)
    b_lo = (b - b_hi.astype(F32)).astype(BF16)
    dot = functools.partial(lax.dot_general, dimension_numbers=dims, preferred_element_type=F32)
    n = b.shape[1]
    if dims == NN_DIMS and 2 * n <= MXU_WIDTH:
        both = dot(a_hi, jnp.concatenate([b_hi, b_lo], axis=1))
        return both[:, :n] + (both[:, n:] + dot(a_lo, b_hi))
    return dot(a_hi, b_hi) + (dot(a_hi, b_lo) + dot(a_lo, b_hi))


def _rope_rows(a, cos, sin_lo, sin_hi):
    n = a.shape[-1]
    return (a * cos + pltpu.roll(a, n - ROPE_HALF, 1) * sin_lo
            + pltpu.roll(a, ROPE_HALF, 1) * sin_hi)


def _tile_lanes(t, reps):
    return jnp.concatenate([t] * reps, axis=-1)


def _rope_angles(pos):
    inv = ROPE_THETA ** (-jnp.arange(0, ROPE_DIM, 2, dtype=F32) / ROPE_DIM)
    ang = pos.astype(F32)[:, None] * inv[None, :]
    return jnp.cos(ang), jnp.sin(ang)


def _rope_row_tables(cos, sin):
    rows = cos.shape[0]
    ones = jnp.ones((rows, HEAD_DIM - ROPE_DIM), F32)
    zeros = jnp.zeros((rows, HEAD_DIM - ROPE_HALF), F32)
    c = jnp.concatenate([cos, cos, ones], axis=1)
    s_lo = jnp.concatenate([-sin, zeros], axis=1)
    s_hi = jnp.concatenate([jnp.zeros((rows, ROPE_HALF), F32), sin,
                            jnp.zeros((rows, HEAD_DIM - ROPE_DIM), F32)], axis=1)
    rep = lambda t: jnp.concatenate([t] * HEADS_PER_VREG, axis=1)
    return rep(c), rep(s_lo), rep(s_hi)


def _rope_cols(ref, cos_t, sin_t):
    for h in range(N_HEADS):
        r = h * HEAD_DIM
        x1 = ref[r:r + ROPE_HALF, :]
        x2 = ref[r + ROPE_HALF:r + ROPE_DIM, :]
        ref[r:r + ROPE_HALF, :] = x1 * cos_t - x2 * sin_t
        ref[r + ROPE_HALF:r + ROPE_DIM, :] = x2 * cos_t + x1 * sin_t


def _prompt_inproj_kernel(x_ref, g_ref, wqt_ref, wkt_ref, wvt_ref, wc_ref,
                          cost_ref, sint_ref, cw_ref, prev_ref,
                          qt_ref, kt_ref, vt_ref, kb_ref, vtb_ref, km_ref, cy_ref, cn_ref,
                          ubuf, halo, km_acc, *, tt):
    t = pl.program_id(1)
    blk = MOBA_BLOCK
    xn = _rms(x_ref[...], g_ref[...]).astype(BF16)

    qt_ref[...] = lax.dot_general(wqt_ref[...], xn, NT_DIMS, preferred_element_type=F32)
    _rope_cols(qt_ref, cost_ref[...], sint_ref[...])
    kt_ref[...] = lax.dot_general(wkt_ref[...], xn, NT_DIMS, preferred_element_type=F32)
    _rope_cols(kt_ref, cost_ref[...], sint_ref[...])
    vt = lax.dot_general(wvt_ref[...], xn, NT_DIMS, preferred_element_type=F32)
    vt_ref[...] = vt
    for i in range(tt // blk):
        vtb_ref[i] = vt[:, i * blk:(i + 1) * blk].astype(BF16)

    k = kt_ref[...].T
    kb_ref[...] = k.astype(BF16)

    @pl.when(t == 0)
    def _():
        km_acc[...] = jnp.zeros_like(km_acc)

    rid = lax.broadcasted_iota(jnp.int32, km_acc.shape, 0)
    km = km_acc[...]
    for i in range(tt // blk):
        mean_i = jnp.mean(k[i * blk:(i + 1) * blk, :], axis=0, keepdims=True)
        km = jnp.where(rid == t * (tt // blk) + i, mean_i, km)
    km_acc[...] = km
    km_ref[...] = km

    def proj(i):
        return jnp.dot(xn, wc_ref[:, i * CONV_CH:(i + 1) * CONV_CH], preferred_element_type=F32)

    b_gate = proj(0)
    u = proj(1) * proj(2)

    @pl.when(t == 0)
    def _():
        ubuf[SUBLANES - (CONV_K - 1):SUBLANES, :] = prev_ref[...]

    @pl.when(t > 0)
    def _():
        ubuf[0:SUBLANES, :] = halo[...]

    ubuf[SUBLANES:SUBLANES + tt, :] = u
    cw = cw_ref[...]
    conv = (cw[0:1, :] * ubuf[SUBLANES - 2:SUBLANES - 2 + tt, :]
            + cw[1:2, :] * ubuf[SUBLANES - 1:SUBLANES - 1 + tt, :]
            + cw[2:3, :] * u)
    cy_ref[...] = (b_gate * conv).astype(cy_ref.dtype)
    halo[...] = ubuf[tt:tt + SUBLANES, :]
    cn_ref[...] = ubuf[tt + SUBLANES - (CONV_K - 1):tt + SUBLANES, :]


def _prompt_inproj(x, g_mix, w_in, col_tables, conv_w, conv_prev, tt=TOKEN_TILE):
    bsz, seq, d = x.shape
    nt = seq // tt
    nb = seq // MOBA_BLOCK
    w_bf = w_in.astype(BF16)
    wqt = w_bf[:, :ATT_WIDTH].T
    wkt = w_bf[:, ATT_WIDTH:2 * ATT_WIDTH].T
    wvt = w_bf[:, 2 * ATT_WIDTH:3 * ATT_WIDTH].T
    wc = w_bf[:, 3 * ATT_WIDTH:]
    row_spec = lambda width: pl.BlockSpec((None, tt, width), lambda b, t: (b, t, 0))
    col_spec = pl.BlockSpec((None, ATT_WIDTH, tt), lambda b, t: (b, 0, t))
    tabt_spec = pl.BlockSpec((ROPE_HALF, tt), lambda b, t: (0, t))
    full = lambda shape: pl.BlockSpec(shape, lambda b, t: (0,) * len(shape))
    per_b = lambda shape: pl.BlockSpec((None,) + shape, lambda b, t: (b,) + (0,) * len(shape))
    vtb_spec = pl.BlockSpec((None, tt // MOBA_BLOCK, ATT_WIDTH, MOBA_BLOCK),
                            lambda b, t: (b, t, 0, 0))
    col_shape = jax.ShapeDtypeStruct((bsz, ATT_WIDTH, seq), F32)
    return pl.pallas_call(
        functools.partial(_prompt_inproj_kernel, tt=tt),
        grid=(bsz, nt),
        in_specs=[row_spec(d), full((1, d)), full(wqt.shape), full(wkt.shape), full(wvt.shape),
                  full(wc.shape), tabt_spec, tabt_spec,
                  full((CONV_K, CONV_CH)), per_b((CONV_K - 1, CONV_CH))],
        out_specs=[col_spec, col_spec, col_spec, row_spec(ATT_WIDTH), vtb_spec,
                   per_b((nb, ATT_WIDTH)), row_spec(CONV_CH), per_b((CONV_K - 1, CONV_CH))],
        out_shape=[col_shape, col_shape, col_shape,
                   jax.ShapeDtypeStruct((bsz, seq, ATT_WIDTH), BF16),
                   jax.ShapeDtypeStruct((bsz, nb, ATT_WIDTH, MOBA_BLOCK), BF16),
                   jax.ShapeDtypeStruct((bsz, nb, ATT_WIDTH), F32),
                   jax.ShapeDtypeStruct((bsz, seq, CONV_CH), BF16),
                   jax.ShapeDtypeStruct((bsz, CONV_K - 1, CONV_CH), F32)],
        scratch_shapes=[pltpu.VMEM((tt + SUBLANES, CONV_CH), F32),
                        pltpu.VMEM((SUBLANES, CONV_CH), F32),
                        pltpu.VMEM((nb, ATT_WIDTH), F32)],
        compiler_params=pltpu.CompilerParams(
            dimension_semantics=("arbitrary", "arbitrary"), vmem_limit_bytes=VMEM_LIMIT),
        name="prompt_inproj",
    )(x, g_mix, wqt, wkt, wvt, wc, *col_tables, conv_w, conv_prev)


def _prompt_attn_kernel(pt_ref, qt_ref, kb_ref, vtb_ref, km_ref, qdec_ref, *refs, nb, n_pages):
    page_refs = refs[:n_pages]
    o_ref, sel_ref, s_ref, ot_ref, va_ref, part_ref = refs[n_pages:]
    _select_past_blocks(qdec_ref, page_refs, part_ref, sel_ref)
    blk = MOBA_BLOCK
    means = km_ref[...]
    ones_row = lax.broadcasted_iota(jnp.int32, (V_AUG_ROWS - HEAD_DIM, blk), 0) == 0
    for j in range(nb):
        for h in range(HEADS_PER_VREG):
            va_ref[j, h, 0:HEAD_DIM, :] = vtb_ref[j, h * HEAD_DIM:(h + 1) * HEAD_DIM, :]
            va_ref[j, h, HEAD_DIM:V_AUG_ROWS, :] = jnp.where(ones_row, 1.0, 0.0).astype(BF16)
    feat = lax.broadcasted_iota(jnp.int32, (LANES, 1), 0)
    bid = lax.broadcasted_iota(jnp.int32, (nb, blk), 0)
    key = lax.broadcasted_iota(jnp.int32, (blk, blk), 0)
    qry = lax.broadcasted_iota(jnp.int32, (blk, blk), 1)

    def block_bias(qth, qi):
        if qi <= MOBA_TOPK:
            return [None] * qi
        gates = jnp.dot(means, qth, precision=HIGHEST, preferred_element_type=F32)
        gates = jnp.where(bid < qi, gates, NEG_INF)
        rows = []
        for j in range(qi):
            gj = gates[j:j + 1, :]
            beats = (gates > gj) | ((gates == gj) & (bid < j))
            cnt = jnp.sum(beats.astype(F32), axis=0, keepdims=True)
            rows.append(jnp.where(cnt < MOBA_TOPK, 0.0, NEG_INF))
        return rows

    def scores_pass(qi, h, slot, state):
        qt = qt_ref[:, qi * blk:(qi + 1) * blk]
        qth = jnp.where((feat >= h * HEAD_DIM) & (feat < (h + 1) * HEAD_DIM), qt, 0.0)
        bias = block_bias(qth, qi)
        qs = (qth * (ATT_SCALE * LOG2_E)).astype(BF16)
        m = None
        for j in range(qi + 1):
            s = jnp.dot(kb_ref[j * blk:(j + 1) * blk, :], qs,
                        preferred_element_type=F32)
            if j == qi:
                s = jnp.where(key <= qry, s, NEG_INF)
            elif bias[j] is not None:
                s = s + bias[j]
            s_ref[slot, j] = s
            m_blk = jnp.max(s, axis=0, keepdims=True)
            m = m_blk if m is None else jnp.maximum(m, m_blk)
            yield
        state["m"] = m

    def values_pass(qi, h, slot, state):
        m = state["m"]
        acc = None
        for j in range(qi + 1):
            p = jnp.exp2(s_ref[slot, j] - m).astype(BF16)
            pv = jnp.dot(va_ref[j, h], p, preferred_element_type=F32)
            acc = pv if acc is None else acc + pv
            yield
        ot_ref[h * HEAD_DIM:(h + 1) * HEAD_DIM, qi * blk:(qi + 1) * blk] = (
            acc[0:HEAD_DIM, :] / acc[HEAD_DIM:HEAD_DIM + 1, :])

    def run_interleaved(*gens):
        live = [g for g in gens if g is not None]
        while live:
            for g in list(live):
                if next(g, "done") == "done":
                    live.remove(g)

    pending = None
    for i, (qi, h) in enumerate((qi, h) for qi in range(nb) for h in range(HEADS_PER_VREG)):
        state = {}
        run_interleaved(scores_pass(qi, h, i % SCORE_SLOTS, state), pending)
        pending = values_pass(qi, h, i % SCORE_SLOTS, state)
    run_interleaved(pending)
    o_ref[...] = ot_ref[...].T.astype(o_ref.dtype)


def _prompt_attn(qt, kb, vtb, kmeans, page_table, q_dec, cache_kt):
    bsz, seq, _ = kb.shape
    nb = seq // MOBA_BLOCK
    n_pairs = ATT_WIDTH // LANES
    n_dec, n_pages = page_table.shape
    assert n_dec == bsz * n_pairs, "one decode sequence per prompt attention grid step"
    _, n_heads, head_dim, page_size = cache_kt.shape
    n_blocks = n_pages * page_size // MOBA_BLOCK
    dec = lambda b, hp: b * n_pairs + hp

    def page_spec(i):
        return pl.BlockSpec((None, n_heads, head_dim, page_size),
                            lambda b, hp, pt: (pt[dec(b, hp), i], 0, 0, 0))

    return pl.pallas_call(
        functools.partial(_prompt_attn_kernel, nb=nb, n_pages=n_pages),
        grid_spec=pltpu.PrefetchScalarGridSpec(
            num_scalar_prefetch=1,
            grid=(bsz, n_pairs),
            in_specs=[pl.BlockSpec((None, LANES, seq), lambda b, hp, pt: (b, hp, 0)),
                      pl.BlockSpec((None, seq, LANES), lambda b, hp, pt: (b, 0, hp)),
                      pl.BlockSpec((None, nb, LANES, MOBA_BLOCK), lambda b, hp, pt: (b, 0, hp, 0)),
                      pl.BlockSpec((None, nb, LANES), lambda b, hp, pt: (b, 0, hp)),
                      pl.BlockSpec((None, n_heads, head_dim, page_size),
                                   lambda b, hp, pt: (dec(b, hp), 0, 0, 0))]
                     + [page_spec(i) for i in range(n_pages)],
            out_specs=[pl.BlockSpec((None, seq, LANES), lambda b, hp, pt: (b, 0, hp)),
                       pl.BlockSpec((None, n_heads, LANES), lambda b, hp, pt: (dec(b, hp), 0, 0))],
            scratch_shapes=[pltpu.VMEM((SCORE_SLOTS, nb, MOBA_BLOCK, MOBA_BLOCK), F32),
                            pltpu.VMEM((LANES, seq), F32),
                            pltpu.VMEM((nb, HEADS_PER_VREG, V_AUG_ROWS, MOBA_BLOCK), BF16),
                            pltpu.VMEM((n_blocks, n_heads, LANES), F32)],
        ),
        out_shape=[jax.ShapeDtypeStruct((bsz, seq, ATT_WIDTH), BF16),
                   jax.ShapeDtypeStruct((n_dec, n_heads, LANES), jnp.int32)],
        compiler_params=pltpu.CompilerParams(
            dimension_semantics=("arbitrary", "arbitrary"), vmem_limit_bytes=VMEM_LIMIT),
        name="prompt_attn",
    )(page_table, qt, kb, vtb, kmeans, q_dec, *([cache_kt] * n_pages))


def _outproj_router_kernel(x_ref, att_ref, cy_ref, wo_ref, g_ref, wr_ref, br_ref, cnt_in_ref,
                           hres_ref, hn_ref, route_ref, idx_ref, cnt_ref, tri_ref, *, precise):
    step = pl.program_id(0)
    tm = x_ref.shape[0]
    hres = (x_ref[...] + _mm(att_ref[...], wo_ref[0:ATT_WIDTH, :], precise)
            + _mm(cy_ref[...], wo_ref[ATT_WIDTH:, :], precise))
    hres_ref[...] = hres
    hn = _rms(hres, g_ref[...])
    for s in range(TOKEN_ROWS):
        hn_ref[pl.ds(s, tm, stride=TOKEN_ROWS), :] = hn[:, s * LANES:(s + 1) * LANES]

    @pl.when(step == 0)
    def _():
        cnt_ref[...] = cnt_in_ref[...]
        r = lax.broadcasted_iota(jnp.int32, (tm, tm), 0)
        c = lax.broadcasted_iota(jnp.int32, (tm, tm), 1)
        tri_ref[...] = jnp.where(r < c, 1.0, 0.0).astype(BF16)

    lo = _dot_f32(wr_ref[...], hn, NT_DIMS, precise) + br_ref[...]
    sub = lax.broadcasted_iota(jnp.int32, (SUBLANES, tm), 0)
    first_max = lambda v, m: jnp.min(jnp.where(v == m, sub, SUBLANES), axis=0, keepdims=True)
    lg = jnp.where(sub < N_EXPERT_GROUPS, lo[0:SUBLANES], NEG_INF)
    mg = jnp.max(lg, axis=0, keepdims=True)
    g = first_max(lg, mg)
    pg = 1.0 / jnp.sum(jnp.exp(lg - mg), axis=0, keepdims=True)
    le = lo[SUBLANES:2 * SUBLANES]
    for gi in range(1, N_EXPERT_GROUPS):
        le = jnp.where(g == gi, lo[(gi + 1) * SUBLANES:(gi + 2) * SUBLANES], le)
    m1 = jnp.max(le, axis=0, keepdims=True)
    i1 = first_max(le, m1)
    le2 = jnp.where(sub == i1, NEG_INF, le)
    m2 = jnp.max(le2, axis=0, keepdims=True)
    i2 = first_max(le2, m2)
    e2 = jnp.exp(m2 - m1)
    w1 = pg / (1.0 + e2)
    w2 = pg * e2 / (1.0 + e2)
    ex1, ex2 = g * EXPERTS_PER_GROUP + i1, g * EXPERTS_PER_GROUP + i2

    eid = lax.broadcasted_iota(jnp.int32, (N_EXPERTS, tm), 0)
    oh1, oh2 = eid == ex1, eid == ex2
    assigned = jnp.where(oh1 | oh2, 1.0, 0.0)
    before = cnt_ref[...] + jnp.dot(assigned.astype(BF16), tri_ref[...],
                                    preferred_element_type=F32)
    r1 = jnp.sum(jnp.where(oh1, before, 0.0), axis=0, keepdims=True)
    r2 = jnp.sum(jnp.where(oh2, before, 0.0), axis=0, keepdims=True)
    cnt_ref[...] += jnp.sum(assigned, axis=1, keepdims=True)
    zrow = jnp.zeros((1, tm), jnp.int32)
    idx_ref[...] = jnp.concatenate(
        [ex1, ex2, r1.astype(jnp.int32), r2.astype(jnp.int32)] + [zrow] * (SUBLANES - 4), axis=0)
    zf = jnp.zeros((1, tm), F32)
    wrows = jnp.concatenate([zf] * 4 + [w1, w2] + [zf] * 2 + [jnp.zeros((LANES - SUBLANES, tm), F32)],
                            axis=0)
    route_ref[...] = wrows.T


def _outproj_router(x, att, cy, w_out, g_ffn, w_r, b_r, counts_in, tm, precise):
    n, d = x.shape
    rows = lambda width: pl.BlockSpec((tm, width), lambda i: (i, 0))
    full = lambda shape: pl.BlockSpec(shape, lambda i: (0,) * len(shape))
    return pl.pallas_call(
        functools.partial(_outproj_router_kernel, precise=precise),
        grid=(n // tm,),
        in_specs=[rows(d), rows(ATT_WIDTH), rows(CONV_CH), full(w_out.shape), full((1, d)),
                  full(w_r.shape), full((LANES, 1)), full((N_EXPERTS, 1))],
        out_specs=[rows(d), pl.BlockSpec((tm * TOKEN_ROWS, LANES), lambda i: (i, 0)), rows(LANES),
                   pl.BlockSpec((None, SUBLANES, tm), lambda i: (i, 0, 0)), full((N_EXPERTS, 1))],
        out_shape=[jax.ShapeDtypeStruct((n, d), F32),
                   jax.ShapeDtypeStruct((n * TOKEN_ROWS, LANES), F32),
                   jax.ShapeDtypeStruct((n, LANES), F32),
                   jax.ShapeDtypeStruct((n // tm, SUBLANES, tm), jnp.int32),
                   jax.ShapeDtypeStruct((N_EXPERTS, 1), F32)],
        scratch_shapes=[pltpu.VMEM((tm, tm), BF16)],
        compiler_params=pltpu.CompilerParams(
            dimension_semantics=("arbitrary",), vmem_limit_bytes=VMEM_LIMIT),
        name="outproj_router_precise" if precise else "outproj_router",
    )(x, att, cy, w_out, g_ffn, w_r, b_r, counts_in)


def _token_rows(ref, t):
    return ref.at[pl.ds(pl.multiple_of(t * TOKEN_ROWS, TOKEN_ROWS), TOKEN_ROWS), :]


def _start_token_copies(copies, n):
    def start(g, carry):
        for u in range(ISSUE_UNROLL):
            for c in copies(g * ISSUE_UNROLL + u):
                c.start(priority=u % DMA_QUEUES)
        return carry

    lax.fori_loop(0, n // ISSUE_UNROLL, start, 0)


def _moe_slots(idx, offsets):
    experts, inner = idx[:, :EXPERT_TOPK, :], idx[:, EXPERT_TOPK:2 * EXPERT_TOPK, :]
    onehot = experts[..., None] == jnp.arange(N_EXPERTS, dtype=jnp.int32)
    return inner + jnp.sum(jnp.where(onehot, offsets, 0), axis=-1)


def _regroup_steps(slots, tm):
    steps, k, tm0 = slots.shape
    if tm0 >= tm or (steps * tm0) % tm:
        return slots
    per = tm // tm0
    return slots.reshape(steps // per, per, k, tm0).transpose(0, 2, 1, 3).reshape(steps // per, k, tm)


def _moe_dispatch_kernel(fill_ref, na_ref, slot_ref, hn_ref, *refs, tm, n_tiles, zero_fill):
    xs_ref, zbuf, sem = refs[-3:]
    step = pl.program_id(0)

    if zero_fill:
        @pl.when(step == 0)
        def _():
            zbuf[...] = jnp.zeros_like(zbuf)

            def fill(slot):
                return pltpu.make_async_copy(
                    zbuf, xs_ref.at[pl.ds(pl.multiple_of(slot * TOKEN_ROWS, TOKEN_ROWS),
                                          MOE_TILE * TOKEN_ROWS), :], sem.at[1])

            def start_fill(i, carry):
                fill(i * MOE_TILE).start()
                return carry

            def wait_fill(i, carry):
                fill(i * MOE_TILE).wait()
                return carry

            def group_fills(act):
                for e in range(N_EXPERTS):
                    @pl.when(fill_ref[e] >= 0)
                    def _():
                        act(fill(fill_ref[e]))

            group_fills(lambda c: c.start())
            lax.fori_loop(na_ref[0], n_tiles, start_fill, 0)
            group_fills(lambda c: c.wait())
            lax.fori_loop(na_ref[0], n_tiles, wait_fill, 0)

    def copies(t):
        src = _token_rows(hn_ref, t)
        return tuple(pltpu.make_async_copy(src, _token_rows(xs_ref, slot_ref[k, t]), sem.at[0])
                     for k in range(EXPERT_TOPK))

    _start_token_copies(copies, tm)
    for _ in range(EXPERT_TOPK):
        pltpu.make_async_copy(hn_ref, xs_ref.at[pl.ds(0, tm * TOKEN_ROWS), :], sem.at[0]).wait()


def _moe_dispatch(fill_starts, n_active, slots, hn_tiles, xs_prev, n_tiles):
    n_steps, _, tm = slots.shape
    n = n_steps * tm
    zero_fill = xs_prev is None
    hbm = pl.BlockSpec(memory_space=pl.ANY)
    in_specs = [pl.BlockSpec((None, EXPERT_TOPK, tm), lambda i, *_: (i, 0, 0),
                             memory_space=pltpu.SMEM),
                pl.BlockSpec((tm * TOKEN_ROWS, LANES), lambda i, *_: (i, 0))]
    args = [fill_starts, n_active, slots, hn_tiles]
    aliases = {}
    if not zero_fill:
        in_specs.append(hbm)
        args.append(xs_prev)
        aliases = {len(args) - 1: 0}
    return pl.pallas_call(
        functools.partial(_moe_dispatch_kernel, tm=tm, n_tiles=n_tiles, zero_fill=zero_fill),
        grid_spec=pltpu.PrefetchScalarGridSpec(
            num_scalar_prefetch=2,
            grid=(n // tm,),
            in_specs=in_specs,
            out_specs=hbm,
            scratch_shapes=[pltpu.VMEM((MOE_TILE * TOKEN_ROWS, LANES), F32),
                            pltpu.SemaphoreType.DMA((2,))],
        ),
        out_shape=jax.ShapeDtypeStruct((n_tiles * MOE_TILE * TOKEN_ROWS, LANES), F32),
        input_output_aliases=aliases,
        compiler_params=pltpu.CompilerParams(
            dimension_semantics=("arbitrary",), vmem_limit_bytes=VMEM_LIMIT),
        name="moe_dispatch" if zero_fill else "moe_dispatch_more",
    )(*args)


def _moe_ffn_kernel(te_ref, na_ref, xs_ref, wg_ref, wu_ref, wd_ref, ys_ref, wg_b, wu_b, wd_b):
    i = pl.program_id(0)
    tg = MOE_TILE
    active = i < na_ref[0]

    @pl.when(active)
    def _():
        @pl.when((i == 0) | (te_ref[i] != te_ref[jnp.maximum(i - 1, 0)]))
        def _():
            wg_b[...] = wg_ref[...].astype(BF16)
            wu_b[...] = wu_ref[...].astype(BF16)
            wd_b[...] = wd_ref[...].astype(BF16)

        x = jnp.concatenate(
            [xs_ref[pl.ds(s, tg, stride=TOKEN_ROWS), :].astype(BF16) for s in range(TOKEN_ROWS)],
            axis=1)
        hg = jnp.dot(x, wg_b[...], preferred_element_type=F32)
        hu = jnp.dot(x, wu_b[...], preferred_element_type=F32)
        h = hg * (1.0 / (1.0 + jnp.exp(-hg))) * hu
        y = jnp.dot(h.astype(BF16), wd_b[...], preferred_element_type=F32)
        for s in range(TOKEN_ROWS):
            ys_ref[pl.ds(s, tg, stride=TOKEN_ROWS), :] = y[:, s * LANES:(s + 1) * LANES]


def _moe_ffn(tile_expert, n_active, xs, w_gate, w_up, w_down):
    n_tiles = tile_expert.shape[0]
    tile_rows = MOE_TILE * TOKEN_ROWS
    d, ff = w_gate.shape[1:]
    last = lambda na: na[0] - 1
    w_spec = lambda shape: pl.BlockSpec(
        (None,) + shape, lambda i, te, na: (te[jnp.minimum(i, last(na))], 0, 0))
    tile_spec = pl.BlockSpec((tile_rows, LANES), lambda i, te, na: (jnp.minimum(i, last(na)), 0))
    return pl.pallas_call(
        _moe_ffn_kernel,
        grid_spec=pltpu.PrefetchScalarGridSpec(
            num_scalar_prefetch=2,
            grid=(n_tiles,),
            in_specs=[tile_spec, w_spec((d, ff)), w_spec((d, ff)), w_spec((ff, d))],
            out_specs=tile_spec,
            scratch_shapes=[pltpu.VMEM((d, ff), BF16), pltpu.VMEM((d, ff), BF16),
                            pltpu.VMEM((ff, d), BF16)],
        ),
        out_shape=jax.ShapeDtypeStruct(xs.shape, F32),
        input_output_aliases={2: 0},
        compiler_params=pltpu.CompilerParams(
            dimension_semantics=("arbitrary",), vmem_limit_bytes=VMEM_LIMIT),
        name="moe_ffn",
    )(tile_expert, n_active, xs, w_gate, w_up, w_down)


def _moe_combine_kernel(slot_ref, next_slot_ref, route_ref, hres_ref, gf_ref, ys_ref, y_ref,
                        buf, sem, *, tm):
    i = pl.program_id(0)
    cur = i % 2
    nxt = 1 - cur

    def copies(table, t, half):
        return tuple(pltpu.make_async_copy(_token_rows(ys_ref, table[k, t]),
                                           _token_rows(buf.at[half, k], t), sem.at[half, k])
                     for k in range(EXPERT_TOPK))

    def wait_rows(half):
        for k in range(EXPERT_TOPK):
            pltpu.make_async_copy(ys_ref.at[pl.ds(0, tm * TOKEN_ROWS), :], buf.at[half, k],
                                  sem.at[half, k]).wait()

    @pl.when(i == 0)
    def _():
        _start_token_copies(lambda t: copies(slot_ref, t, 0), tm)

    wait_rows(cur)
    group = min(COMBINE_GROUP, tm)
    for g in range(tm // group):
        rows = slice(g * group, (g + 1) * group)
        w1, w2 = route_ref[rows, 4:5], route_ref[rows, 5:6]
        parts = []
        for s in range(TOKEN_ROWS):
            tile_rows = pl.ds(g * group * TOKEN_ROWS + s, group, stride=TOKEN_ROWS)
            parts.append(hres_ref[rows, s * LANES:(s + 1) * LANES]
                         + w1 * buf[cur, 0, tile_rows, :] + w2 * buf[cur, 1, tile_rows, :])
        y_ref[rows, :] = _rms(jnp.concatenate(parts, axis=1), gf_ref[...])
        for u in range(group):
            for c in copies(next_slot_ref, g * group + u, nxt):
                c.start(priority=u % DMA_QUEUES)

    @pl.when(i == pl.num_programs(0) - 1)
    def _():
        wait_rows(nxt)


def _moe_combine(slots, route, hres, g_final, ys):
    n, d = hres.shape
    steps, _, tm = slots.shape
    rows = lambda width: pl.BlockSpec((tm, width), lambda i: (i, 0))
    slot_spec = lambda step_of: pl.BlockSpec((None, EXPERT_TOPK, tm), lambda i: (step_of(i), 0, 0),
                                             memory_space=pltpu.SMEM)
    return pl.pallas_call(
        functools.partial(_moe_combine_kernel, tm=tm),
        grid=(steps,),
        in_specs=[slot_spec(lambda i: i), slot_spec(lambda i: jnp.minimum(i + 1, steps - 1)),
                  rows(LANES), rows(d), pl.BlockSpec((1, d), lambda i: (0, 0)),
                  pl.BlockSpec(memory_space=pl.ANY)],
        out_specs=rows(d),
        out_shape=jax.ShapeDtypeStruct((n, d), F32),
        scratch_shapes=[pltpu.VMEM((2, EXPERT_TOPK, tm * TOKEN_ROWS, LANES), F32),
                        pltpu.SemaphoreType.DMA((2, EXPERT_TOPK))],
        compiler_params=pltpu.CompilerParams(
            dimension_semantics=("arbitrary",), vmem_limit_bytes=VMEM_LIMIT),
        name="moe_combine",
    )(slots, slots, route, hres, g_final, ys)


def _moe_plan(counts, n_tiles):
    counts = counts[:, 0].astype(jnp.int32)
    padded = (counts + MOE_TILE - 1) // MOE_TILE * MOE_TILE
    ends = jnp.cumsum(padded)
    offsets = ends - padded
    n_active = (ends[-1:] // MOE_TILE).astype(jnp.int32)
    tile_start = jnp.arange(n_tiles, dtype=jnp.int32) * MOE_TILE
    tile_expert = jnp.minimum(jnp.sum(tile_start[:, None] >= ends[None, :], axis=1), N_EXPERTS - 1)
    fill_starts = jnp.where(padded > 0, ends - MOE_TILE, -1)
    return offsets, tile_expert.astype(jnp.int32), n_active, fill_starts.astype(jnp.int32)


def _sample_inproj_kernel(x_ref, g_ref, w_ref, cos_ref, slo_ref, shi_ref, cw_ref, p0_ref, p1_ref,
                          q_ref, k_ref, v_ref, cy_ref, u_ref):
    xn = _rms(x_ref[...], g_ref[...])
    reps = ATT_WIDTH // LANES
    cos = _tile_lanes(cos_ref[...], reps)
    slo = _tile_lanes(slo_ref[...], reps)
    shi = _tile_lanes(shi_ref[...], reps)

    def proj(i):
        return _dot_f32(xn, w_ref[:, i * ATT_WIDTH:(i + 1) * ATT_WIDTH], NN_DIMS, False)

    q_ref[...] = _rope_rows(proj(0), cos, slo, shi)
    k_ref[...] = _rope_rows(proj(1), cos, slo, shi)
    v_ref[...] = proj(2)
    b_gate = proj(3)
    u = proj(4) * proj(5)
    u_ref[...] = u
    cw = cw_ref[...]
    cy_ref[...] = b_gate * (cw[0:1, :] * p0_ref[...] + cw[1:2, :] * p1_ref[...] + cw[2:3, :] * u)


def _sample_inproj(x, g_mix, w_in, tables, conv_w, prev0, prev1):
    n = x.shape[0]
    out = jax.ShapeDtypeStruct((n, ATT_WIDTH), F32)
    return pl.pallas_call(
        _sample_inproj_kernel,
        out_shape=[out] * 5,
        compiler_params=pltpu.CompilerParams(vmem_limit_bytes=VMEM_LIMIT),
        name="sample_inproj",
    )(x, g_mix, w_in, *tables, conv_w, prev0, prev1)


def _select_past_blocks(q_ref, page_refs, part, sel_ref):
    n_blocks = part.shape[0]
    pages_per_block = len(page_refs) // n_blocks
    block_rows = pages_per_block * page_refs[0].shape[-1]
    for i in range(n_blocks):
        acc = None
        for c in range(HEAD_DIM // SUBLANES):
            rows = slice(c * SUBLANES, (c + 1) * SUBLANES)
            k_rows = page_refs[i * pages_per_block][:, rows, :]
            for p in range(1, pages_per_block):
                k_rows = k_rows + page_refs[i * pages_per_block + p][:, rows, :]
            prod = k_rows * q_ref[:, rows, :]
            acc = prod if acc is None else acc + prod
        part[i] = jnp.sum(acc, axis=1)

    lane = lax.broadcasted_iota(jnp.int32, (N_HEADS, LANES), 1)
    acc = jnp.full((N_HEADS, LANES), NEG_INF, F32)
    for j in range(n_blocks):
        gate_j = jnp.sum(part[j], axis=1, keepdims=True) * (1.0 / block_rows)
        acc = jnp.where(lane == j, gate_j, acc)
    out = jnp.zeros((N_HEADS, LANES), jnp.int32)
    for r in range(MOBA_TOPK):
        best = jnp.max(acc, axis=1, keepdims=True)
        idx = jnp.min(jnp.where(acc == best, lane, LANES), axis=1, keepdims=True)
        out = jnp.where(lane == r, idx, out)
        acc = jnp.where(lane == idx, NEG_INF, acc)
    sel_ref[...] = out


def _sample_attn_kernel(sel_ref, pt_ref, q_ref, qb_ref, kn_ref, vn_ref, ck_ref, cv_ref, o_ref,
                        kbuf, vbuf, sem, *, pages_per_block, page_size):
    b = pl.program_id(0)
    nb = pl.num_programs(0)

    def copies(bb, slot):
        out = []
        for h in range(N_HEADS):
            for r in range(MOBA_TOPK):
                block = sel_ref[(bb * N_HEADS + h) * MOBA_TOPK + r]
                for p in range(pages_per_block):
                    page = pt_ref[bb, block * pages_per_block + p]
                    dst = pl.ds((r * pages_per_block + p) * page_size, page_size)
                    out.append(pltpu.make_async_copy(ck_ref.at[page, h], kbuf.at[slot, h, :, dst],
                                                     sem.at[slot, 0]))
                    out.append(pltpu.make_async_copy(cv_ref.at[page, h], vbuf.at[slot, h, :, dst],
                                                     sem.at[slot, 1]))
        return out

    @pl.when(b == 0)
    def _():
        for c in copies(0, 0):
            c.start()

    @pl.when(b + 1 < nb)
    def _():
        for c in copies(b + 1, (b + 1) % 2):
            c.start()

    slot = b % 2
    for c in copies(b, slot):
        c.wait()

    q = q_ref[...]
    qb = qb_ref[...]
    kn = kn_ref[...]
    vn = vn_ref[...]
    n_keys = kbuf.shape[-1]
    head = lax.broadcasted_iota(jnp.int32, (N_HEADS, ATT_WIDTH), 0)
    feat = lax.broadcasted_iota(jnp.int32, (N_HEADS, ATT_WIDTH), 1)
    own = (feat >= head * HEAD_DIM) & (feat < (head + 1) * HEAD_DIM)

    s = jnp.concatenate(
        [jnp.sum(kbuf[slot, :, :, c * LANES:(c + 1) * LANES] * qb, axis=1)
         for c in range(n_keys // LANES)], axis=1) * ATT_SCALE
    s_self = jnp.sum(jnp.where(own, q * kn, 0.0), axis=1, keepdims=True) * ATT_SCALE
    m = jnp.maximum(jnp.max(s, axis=1, keepdims=True), s_self)
    p = jnp.exp(s - m)
    p_self = jnp.exp(s_self - m)
    l = jnp.sum(p, axis=1, keepdims=True) + p_self
    pv = _dot_f32(p, vbuf[slot].reshape(N_HEADS * HEAD_DIM, n_keys), NT_DIMS, False)
    o = (pv + p_self * vn) / l
    o_ref[...] = jnp.sum(jnp.where(own, o, 0.0), axis=0, keepdims=True)


def _sample_attn(sel_flat, page_table, q3, q_bcast, k3, v3, cache_kt, cache_vt):
    n_dec = q3.shape[0]
    page_size = cache_kt.shape[3]
    pages_per_block = MOBA_BLOCK // page_size
    n_keys = MOBA_TOPK * MOBA_BLOCK
    row = pl.BlockSpec((None, 1, ATT_WIDTH), lambda b, sel, pt: (b, 0, 0))
    hbm = pl.BlockSpec(memory_space=pl.ANY)
    return pl.pallas_call(
        functools.partial(_sample_attn_kernel, pages_per_block=pages_per_block,
                          page_size=page_size),
        grid_spec=pltpu.PrefetchScalarGridSpec(
            num_scalar_prefetch=2,
            grid=(n_dec,),
            in_specs=[row, pl.BlockSpec((None,) + q_bcast.shape[1:],
                                        lambda b, sel, pt: (b, 0, 0, 0)), row, row, hbm, hbm],
            out_specs=row,
            scratch_shapes=[pltpu.VMEM((2, N_HEADS, HEAD_DIM, n_keys), F32),
                            pltpu.VMEM((2, N_HEADS, HEAD_DIM, n_keys), F32),
                            pltpu.SemaphoreType.DMA((2, 2))],
        ),
        out_shape=jax.ShapeDtypeStruct((n_dec, 1, ATT_WIDTH), F32),
        compiler_params=pltpu.CompilerParams(
            dimension_semantics=("arbitrary",), vmem_limit_bytes=VMEM_LIMIT),
        name="sample_attn",
    )(sel_flat, page_table, q3, q_bcast, k3, v3, cache_kt, cache_vt)


def kernel(x_prompt, x_sample, cache_k, cache_v, state_conv, page_table, g_mix, w_in, conv_w, w_out,
           g_ffn, w_router_group, b_router_group, w_router_expert, b_router_expert, w_gate, w_up,
           w_down, g_final):
    depth = g_mix.shape[0]
    assert depth == 1
    bsz, seq, d = x_prompt.shape
    n_dec, dec_len, _ = x_sample.shape
    assert dec_len == 1
    page_size = cache_k.shape[2]
    past_len = page_table.shape[1] * page_size
    assert past_len % MOBA_BLOCK == 0 and MOBA_BLOCK % page_size == 0
    assert past_len // MOBA_BLOCK >= MOBA_TOPK and seq % MOBA_BLOCK == 0

    g_mix2 = g_mix[0][None, :]
    g_ffn2 = g_ffn[0][None, :]
    g_final2 = g_final[None, :]
    w_in_f, w_out_f = w_in[0], w_out[0]
    w_out_bf = w_out_f.astype(BF16)
    gap = SUBLANES - N_EXPERT_GROUPS
    tail = LANES - SUBLANES - N_EXPERTS
    w_r = jnp.concatenate([w_router_group[0].T, jnp.zeros((gap, d), F32),
                           w_router_expert[0].T, jnp.zeros((tail, d), F32)], axis=0)
    b_r = jnp.concatenate([b_router_group[0], jnp.zeros((gap,), F32),
                           b_router_expert[0], jnp.zeros((tail,), F32)])[:, None]
    cw = conv_w[0]

    cos_s, sin_s = _rope_angles(past_len + jnp.arange(dec_len, dtype=jnp.int32))
    x_s = x_sample.reshape(n_dec, d)
    prev0, prev1 = state_conv[0, :, 0, :], state_conv[0, :, 1, :]
    q_s, k_s, v_s, cy_s, u_s = _sample_inproj(x_s, g_mix2, w_in_f, _rope_row_tables(cos_s, sin_s),
                                              cw, prev0, prev1)
    cache_kt = jnp.transpose(cache_k[0], (0, 2, 3, 1))
    cache_vt = jnp.transpose(cache_v[0], (0, 2, 3, 1))
    q_bcast = jnp.broadcast_to(q_s.reshape(n_dec, N_HEADS, HEAD_DIM, 1),
                               (n_dec, N_HEADS, HEAD_DIM, page_size))

    cos_p, sin_p = _rope_angles(jnp.arange(seq, dtype=jnp.int32))
    conv0 = jnp.zeros((bsz, CONV_K - 1, CONV_CH), F32)
    qt_p, kt_p, vt_p, kb_p, vtb_p, kmeans_p, cy_p, conv_p = _prompt_inproj(
        x_prompt, g_mix2, w_in_f, (cos_p.T, sin_p.T), cw, conv0)
    att_p, sel = _prompt_attn(qt_p, kb_p, vtb_p, kmeans_p, page_table, q_bcast, cache_kt)
    n_p = bsz * seq
    hres_p, hn_p, route_p, idx_p, counts_p = _outproj_router(
        x_prompt.reshape(n_p, d), att_p.reshape(n_p, ATT_WIDTH), cy_p.reshape(n_p, CONV_CH),
        w_out_bf, g_ffn2, w_r, b_r, jnp.zeros((N_EXPERTS, 1), F32), tm=TOKEN_TILE, precise=False)

    sel_flat = sel[:, :, :MOBA_TOPK].reshape(-1)
    q3 = q_s.reshape(n_dec, 1, ATT_WIDTH)
    att_s = _sample_attn(sel_flat, page_table, q3, q_bcast, k_s.reshape(n_dec, 1, ATT_WIDTH),
                         v_s.reshape(n_dec, 1, ATT_WIDTH), cache_kt, cache_vt)
    hres_s, hn_s, route_s, idx_s, counts = _outproj_router(
        x_s, att_s.reshape(n_dec, ATT_WIDTH), cy_s, w_out_f, g_ffn2, w_r, b_r, counts_p,
        tm=n_dec, precise=True)

    n_assign = EXPERT_TOPK * (n_p + n_dec)
    n_tiles = -(-(n_assign + N_EXPERTS * (MOE_TILE - 1)) // MOE_TILE)
    offsets, tile_expert, n_active, fill_starts = _moe_plan(counts, n_tiles)
    slots_p, slots_s = _moe_slots(idx_p, offsets), _moe_slots(idx_s, offsets)
    slots_p = _regroup_steps(slots_p, COPY_STEP_TOKENS)
    xs = _moe_dispatch(fill_starts, n_active, slots_p, hn_p, None, n_tiles)
    xs = _moe_dispatch(fill_starts, n_active, slots_s, hn_s, xs, n_tiles)
    ys = _moe_ffn(tile_expert, n_active, xs, w_gate[0], w_up[0], w_down[0])
    y_p = _moe_combine(slots_p, route_p, hres_p, g_final2, ys)
    y_s = _moe_combine(slots_s, route_s, hres_s, g_final2, ys)

    conv_s = jnp.stack([prev1, u_s], axis=1)
    to_bthd = lambda t: jnp.transpose(t.reshape(bsz, N_HEADS, HEAD_DIM, seq), (0, 3, 1, 2))[None]
    return (y_p.reshape(bsz, seq, d), y_s.reshape(n_dec, dec_len, d),
            to_bthd(kt_p), to_bthd(vt_p), conv_p[None],
            k_s.reshape(1, n_dec, dec_len, N_HEADS, HEAD_DIM),
            v_s.reshape(1, n_dec, dec_len, N_HEADS, HEAD_DIM),
            conv_s[None])
```

```python
import functools

import jax
import jax.numpy as jnp
from jax import lax
from jax.experimental import pallas as pl
from jax.experimental.pallas import tpu as pltpu

F32 = jnp.float32
BF16 = jnp.bfloat16
HIGHEST = lax.Precision.HIGHEST

D_MODEL = 1024
HEAD_DIM = 64
N_HEADS = 8
ATT_WIDTH = N_HEADS * HEAD_DIM
CONV_CH = D_MODEL - ATT_WIDTH
ROPE_DIM = HEAD_DIM // 4
ROPE_HALF = ROPE_DIM // 2
ROPE_THETA = 500000.0
MOBA_BLOCK = 256
MOBA_TOPK = 3
ATT_SCALE = HEAD_DIM ** -0.5
CONV_K = 3
N_EXPERT_GROUPS = 4
EXPERTS_PER_GROUP = 8
N_EXPERTS = N_EXPERT_GROUPS * EXPERTS_PER_GROUP
EXPERT_TOPK = 2
EXPERT_FF = D_MODEL // 4
RMS_EPS = 1e-6

LANES = 128
SUBLANES = 8
HEADS_PER_VREG = LANES // HEAD_DIM
TOKEN_ROWS = D_MODEL // LANES
MXU_WIDTH = 256
TOKEN_TILE = 512
MOE_TILE = 512
BF16_SUBLANES = 16
V_AUG_ROWS = HEAD_DIM + BF16_SUBLANES
LOG2_E = 1.4426950408889634
SCORE_SLOTS = 2
DMA_QUEUES = 2
ISSUE_UNROLL = 16
COPY_STEP_TOKENS = 1024
COMBINE_GROUP = 32
VMEM_LIMIT = 56 * 1024 * 1024

NEG_INF = float("-inf")


def _rms(x, g):
    ms = jnp.mean(x * x, axis=-1, keepdims=True)
    return x * lax.rsqrt(ms + RMS_EPS) * g


def _mm(a, w, precise):
    if precise:
        return jnp.dot(a, w, precision=HIGHEST, preferred_element_type=F32)
    return jnp.dot(a.astype(BF16), w, preferred_element_type=F32)


NN_DIMS = (((1,), (0,)), ((), ()))
NT_DIMS = (((1,), (1,)), ((), ()))


def _dot_f32(a, b, dims, full):
    if full:
        return lax.dot_general(a, b, dims, precision=HIGHEST, preferred_element_type=F32)
    a_hi, b_hi = a.astype(BF16), b.astype(BF16)
    a_lo = (a - a_hi.astype(F32)).astype(BF16)
    b_lo = (b - b_hi.astype(F32)).astype(BF16)
    dot = functools.partial(lax.dot_general, dimension_numbers=dims, preferred_element_type=F32)
    n = b.shape[1]
    if dims == NN_DIMS and 2 * n <= MXU_WIDTH:
        both = dot(a_hi, jnp.concatenate([b_hi, b_lo], axis=1))
        return both[:, :n] + (both[:, n:] + dot(a_lo, b_hi))
    return dot(a_hi, b_hi) + (dot(a_hi, b_lo) + dot(a_lo, b_hi))


def _rope_rows(a, cos, sin_lo, sin_hi):
    n = a.shape[-1]
    return (a * cos + pltpu.roll(a, n - ROPE_HALF, 1) * sin_lo
            + pltpu.roll(a, ROPE_HALF, 1) * sin_hi)


def _tile_lanes(t, reps):
    return jnp.concatenate([t] * reps, axis=-1)


def _rope_angles(pos):
    inv = ROPE_THETA ** (-jnp.arange(0, ROPE_DIM, 2, dtype=F32) / ROPE_DIM)
    ang = pos.astype(F32)[:, None] * inv[None, :]
    return jnp.cos(ang), jnp.sin(ang)


def _rope_row_tables(cos, sin):
    rows = cos.shape[0]
    ones = jnp.ones((rows, HEAD_DIM - ROPE_DIM), F32)
    zeros = jnp.zeros((rows, HEAD_DIM - ROPE_HALF), F32)
    c = jnp.concatenate([cos, cos, ones], axis=1)
    s_lo = jnp.concatenate([-sin, zeros], axis=1)
    s_hi = jnp.concatenate([jnp.zeros((rows, ROPE_HALF), F32), sin,
                            jnp.zeros((rows, HEAD_DIM - ROPE_DIM), F32)], axis=1)
    rep = lambda t: jnp.concatenate([t] * HEADS_PER_VREG, axis=1)
    return rep(c), rep(s_lo), rep(s_hi)


def _rope_cols(ref, cos_t, sin_t):
    for h in range(N_HEADS):
        r = h * HEAD_DIM
        x1 = ref[r:r + ROPE_HALF, :]
        x2 = ref[r + ROPE_HALF:r + ROPE_DIM, :]
        ref[r:r + ROPE_HALF, :] = x1 * cos_t - x2 * sin_t
        ref[r + ROPE_HALF:r + ROPE_DIM, :] = x2 * cos_t + x1 * sin_t


def _prompt_inproj_kernel(x_ref, g_ref, w_ref, cost_ref, sint_ref, cw_ref, prev_ref,
                          qt_ref, kt_ref, vt_ref, kb_ref, vtb_ref, km_ref, cy_ref, cn_ref,
                          ubuf, halo, km_acc, wt_ref, wc_ref, *, tt):
    t = pl.program_id(1)
    blk = MOBA_BLOCK

    @pl.when((pl.program_id(0) == 0) & (t == 0))
    def _():
        for i in range(wt_ref.shape[0]):
            wt_ref[i] = w_ref[:, i * ATT_WIDTH:(i + 1) * ATT_WIDTH].T.astype(BF16)
        wc_ref[...] = w_ref[:, wt_ref.shape[0] * ATT_WIDTH:].astype(BF16)

    xn = _rms(x_ref[...], g_ref[...]).astype(BF16)

    qt_ref[...] = lax.dot_general(wt_ref[0], xn, NT_DIMS, preferred_element_type=F32)
    _rope_cols(qt_ref, cost_ref[...], sint_ref[...])
    kt_ref[...] = lax.dot_general(wt_ref[1], xn, NT_DIMS, preferred_element_type=F32)
    _rope_cols(kt_ref, cost_ref[...], sint_ref[...])
    vt = lax.dot_general(wt_ref[2], xn, NT_DIMS, preferred_element_type=F32)
    vt_ref[...] = vt
    for i in range(tt // blk):
        vtb_ref[i] = vt[:, i * blk:(i + 1) * blk].astype(BF16)

    k = kt_ref[...].T
    kb_ref[...] = k.astype(BF16)

    @pl.when(t == 0)
    def _():
        km_acc[...] = jnp.zeros_like(km_acc)

    rid = lax.broadcasted_iota(jnp.int32, km_acc.shape, 0)
    km = km_acc[...]
    for i in range(tt // blk):
        mean_i = jnp.mean(k[i * blk:(i + 1) * blk, :], axis=0, keepdims=True)
        km = jnp.where(rid == t * (tt // blk) + i, mean_i, km)
    km_acc[...] = km
    km_ref[...] = km

    def proj(i):
        return jnp.dot(xn, wc_ref[:, i * CONV_CH:(i + 1) * CONV_CH], preferred_element_type=F32)

    b_gate = proj(0)
    u = proj(1) * proj(2)

    @pl.when(t == 0)
    def _():
        ubuf[SUBLANES - (CONV_K - 1):SUBLANES, :] = prev_ref[...]

    @pl.when(t > 0)
    def _():
        ubuf[0:SUBLANES, :] = halo[...]

    ubuf[SUBLANES:SUBLANES + tt, :] = u
    cw = cw_ref[...]
    conv = (cw[0:1, :] * ubuf[SUBLANES - 2:SUBLANES - 2 + tt, :]
            + cw[1:2, :] * ubuf[SUBLANES - 1:SUBLANES - 1 + tt, :]
            + cw[2:3, :] * u)
    cy_ref[...] = (b_gate * conv).astype(cy_ref.dtype)
    halo[...] = ubuf[tt:tt + SUBLANES, :]
    cn_ref[...] = ubuf[tt + SUBLANES - (CONV_K - 1):tt + SUBLANES, :]


def _prompt_inproj(x, g_mix, w_in, col_tables, conv_w, conv_prev, tt=TOKEN_TILE):
    bsz, seq, d = x.shape
    nt = seq // tt
    nb = seq // MOBA_BLOCK
    n_att = 3
    conv_cols = w_in.shape[1] - n_att * ATT_WIDTH
    row_spec = lambda width: pl.BlockSpec((None, tt, width), lambda b, t: (b, t, 0))
    col_spec = pl.BlockSpec((None, ATT_WIDTH, tt), lambda b, t: (b, 0, t))
    tabt_spec = pl.BlockSpec((ROPE_HALF, tt), lambda b, t: (0, t))
    full = lambda shape: pl.BlockSpec(shape, lambda b, t: (0,) * len(shape))
    per_b = lambda shape: pl.BlockSpec((None,) + shape, lambda b, t: (b,) + (0,) * len(shape))
    vtb_spec = pl.BlockSpec((None, tt // MOBA_BLOCK, ATT_WIDTH, MOBA_BLOCK),
                            lambda b, t: (b, t, 0, 0))
    col_shape = jax.ShapeDtypeStruct((bsz, ATT_WIDTH, seq), F32)
    return pl.pallas_call(
        functools.partial(_prompt_inproj_kernel, tt=tt),
        grid=(bsz, nt),
        in_specs=[row_spec(d), full((1, d)), full(w_in.shape), tabt_spec, tabt_spec,
                  full((CONV_K, CONV_CH)), per_b((CONV_K - 1, CONV_CH))],
        out_specs=[col_spec, col_spec, col_spec, row_spec(ATT_WIDTH), vtb_spec,
                   per_b((nb, ATT_WIDTH)), row_spec(CONV_CH), per_b((CONV_K - 1, CONV_CH))],
        out_shape=[col_shape, col_shape, col_shape,
                   jax.ShapeDtypeStruct((bsz, seq, ATT_WIDTH), BF16),
                   jax.ShapeDtypeStruct((bsz, nb, ATT_WIDTH, MOBA_BLOCK), BF16),
                   jax.ShapeDtypeStruct((bsz, nb, ATT_WIDTH), F32),
                   jax.ShapeDtypeStruct((bsz, seq, CONV_CH), BF16),
                   jax.ShapeDtypeStruct((bsz, CONV_K - 1, CONV_CH), F32)],
        scratch_shapes=[pltpu.VMEM((tt + SUBLANES, CONV_CH), F32),
                        pltpu.VMEM((SUBLANES, CONV_CH), F32),
                        pltpu.VMEM((nb, ATT_WIDTH), F32),
                        pltpu.VMEM((n_att, ATT_WIDTH, d), BF16),
                        pltpu.VMEM((d, conv_cols), BF16)],
        compiler_params=pltpu.CompilerParams(
            dimension_semantics=("arbitrary", "arbitrary"), vmem_limit_bytes=VMEM_LIMIT),
        name="prompt_inproj",
    )(x, g_mix, w_in, *col_tables, conv_w, conv_prev)


def _prompt_attn_kernel(pt_ref, qt_ref, kb_ref, vtb_ref, km_ref, qdec_ref, *refs, nb, n_pages):
    page_refs = refs[:n_pages]
    o_ref, sel_ref, s_ref, ot_ref, va_ref, part_ref = refs[n_pages:]
    _select_past_blocks(qdec_ref, page_refs, part_ref, sel_ref)
    blk = MOBA_BLOCK
    means = km_ref[...]
    ones_row = lax.broadcasted_iota(jnp.int32, (V_AUG_ROWS - HEAD_DIM, blk), 0) == 0
    for j in range(nb):
        for h in range(HEADS_PER_VREG):
            va_ref[j, h, 0:HEAD_DIM, :] = vtb_ref[j, h * HEAD_DIM:(h + 1) * HEAD_DIM, :]
            va_ref[j, h, HEAD_DIM:V_AUG_ROWS, :] = jnp.where(ones_row, 1.0, 0.0).astype(BF16)
    feat = lax.broadcasted_iota(jnp.int32, (LANES, 1), 0)
    bid = lax.broadcasted_iota(jnp.int32, (nb, blk), 0)
    key = lax.broadcasted_iota(jnp.int32, (blk, blk), 0)
    qry = lax.broadcasted_iota(jnp.int32, (blk, blk), 1)

    def block_bias(qth, qi):
        if qi <= MOBA_TOPK:
            return [None] * qi
        gates = jnp.dot(means, qth, precision=HIGHEST, preferred_element_type=F32)
        gates = jnp.where(bid < qi, gates, NEG_INF)
        rows = []
        for j in range(qi):
            gj = gates[j:j + 1, :]
            beats = (gates > gj) | ((gates == gj) & (bid < j))
            cnt = jnp.sum(beats.astype(F32), axis=0, keepdims=True)
            rows.append(jnp.where(cnt < MOBA_TOPK, 0.0, NEG_INF))
        return rows

    def scores_pass(qi, h, slot, state):
        qt = qt_ref[:, qi * blk:(qi + 1) * blk]
        qth = jnp.where((feat >= h * HEAD_DIM) & (feat < (h + 1) * HEAD_DIM), qt, 0.0)
        bias = block_bias(qth, qi)
        qs = (qth * (ATT_SCALE * LOG2_E)).astype(BF16)
        m = None
        for j in range(qi + 1):
            s = jnp.dot(kb_ref[j * blk:(j + 1) * blk, :], qs,
                        preferred_element_type=F32)
            if j == qi:
                s = jnp.where(key <= qry, s, NEG_INF)
            elif bias[j] is not None:
                s = s + bias[j]
            s_ref[slot, j] = s
            m_blk = jnp.max(s, axis=0, keepdims=True)
            m = m_blk if m is None else jnp.maximum(m, m_blk)
            yield
        state["m"] = m

    def values_pass(qi, h, slot, state):
        m = state["m"]
        acc = None
        for j in range(qi + 1):
            p = jnp.exp2(s_ref[slot, j] - m).astype(BF16)
            pv = jnp.dot(va_ref[j, h], p, preferred_element_type=F32)
            acc = pv if acc is None else acc + pv
            yield
        ot_ref[h * HEAD_DIM:(h + 1) * HEAD_DIM, qi * blk:(qi + 1) * blk] = (
            acc[0:HEAD_DIM, :] / acc[HEAD_DIM:HEAD_DIM + 1, :])

    def run_interleaved(*gens):
        live = [g for g in gens if g is not None]
        while live:
            for g in list(live):
                if next(g, "done") == "done":
                    live.remove(g)

    pending = None
    for i, (qi, h) in enumerate((qi, h) for qi in range(nb) for h in range(HEADS_PER_VREG)):
        state = {}
        run_interleaved(scores_pass(qi, h, i % SCORE_SLOTS, state), pending)
        pending = values_pass(qi, h, i % SCORE_SLOTS, state)
    run_interleaved(pending)
    o_ref[...] = ot_ref[...].T.astype(o_ref.dtype)


def _prompt_attn(qt, kb, vtb, kmeans, page_table, q_dec, cache_kt):
    bsz, seq, _ = kb.shape
    nb = seq // MOBA_BLOCK
    n_pairs = ATT_WIDTH // LANES
    n_dec, n_pages = page_table.shape
    assert n_dec == bsz * n_pairs, "one decode sequence per prompt attention grid step"
    _, n_heads, head_dim, page_size = cache_kt.shape
    n_blocks = n_pages * page_size // MOBA_BLOCK
    dec = lambda b, hp: b * n_pairs + hp

    def page_spec(i):
        return pl.BlockSpec((None, n_heads, head_dim, page_size),
                            lambda b, hp, pt: (pt[dec(b, hp), i], 0, 0, 0))

    return pl.pallas_call(
        functools.partial(_prompt_attn_kernel, nb=nb, n_pages=n_pages),
        grid_spec=pltpu.PrefetchScalarGridSpec(
            num_scalar_prefetch=1,
            grid=(bsz, n_pairs),
            in_specs=[pl.BlockSpec((None, LANES, seq), lambda b, hp, pt: (b, hp, 0)),
                      pl.BlockSpec((None, seq, LANES), lambda b, hp, pt: (b, 0, hp)),
                      pl.BlockSpec((None, nb, LANES, MOBA_BLOCK), lambda b, hp, pt: (b, 0, hp, 0)),
                      pl.BlockSpec((None, nb, LANES), lambda b, hp, pt: (b, 0, hp)),
                      pl.BlockSpec((None, n_heads, head_dim, page_size),
                                   lambda b, hp, pt: (dec(b, hp), 0, 0, 0))]
                     + [page_spec(i) for i in range(n_pages)],
            out_specs=[pl.BlockSpec((None, seq, LANES), lambda b, hp, pt: (b, 0, hp)),
                       pl.BlockSpec((None, n_heads, LANES), lambda b, hp, pt: (dec(b, hp), 0, 0))],
            scratch_shapes=[pltpu.VMEM((SCORE_SLOTS, nb, MOBA_BLOCK, MOBA_BLOCK), F32),
                            pltpu.VMEM((LANES, seq), F32),
                            pltpu.VMEM((nb, HEADS_PER_VREG, V_AUG_ROWS, MOBA_BLOCK), BF16),
                            pltpu.VMEM((n_blocks, n_heads, LANES), F32)],
        ),
        out_shape=[jax.ShapeDtypeStruct((bsz, seq, ATT_WIDTH), BF16),
                   jax.ShapeDtypeStruct((n_dec, n_heads, LANES), jnp.int32)],
        compiler_params=pltpu.CompilerParams(
            dimension_semantics=("arbitrary", "arbitrary"), vmem_limit_bytes=VMEM_LIMIT),
        name="prompt_attn",
    )(page_table, qt, kb, vtb, kmeans, q_dec, *([cache_kt] * n_pages))


def _outproj_router_kernel(x_ref, att_ref, cy_ref, wo_ref, g_ref, wr_ref, br_ref, cnt_in_ref,
                           hres_ref, hn_ref, route_ref, idx_ref, cnt_ref, tri_ref, *, precise):
    step = pl.program_id(0)
    tm = x_ref.shape[0]
    hres = (x_ref[...] + _mm(att_ref[...], wo_ref[0:ATT_WIDTH, :], precise)
            + _mm(cy_ref[...], wo_ref[ATT_WIDTH:, :], precise))
    hres_ref[...] = hres
    hn = _rms(hres, g_ref[...])
    for s in range(TOKEN_ROWS):
        hn_ref[pl.ds(s, tm, stride=TOKEN_ROWS), :] = hn[:, s * LANES:(s + 1) * LANES]

    @pl.when(step == 0)
    def _():
        cnt_ref[...] = cnt_in_ref[...]
        r = lax.broadcasted_iota(jnp.int32, (tm, tm), 0)
        c = lax.broadcasted_iota(jnp.int32, (tm, tm), 1)
        tri_ref[...] = jnp.where(r < c, 1.0, 0.0).astype(BF16)

    lo = _dot_f32(wr_ref[...], hn, NT_DIMS, precise) + br_ref[...]
    sub = lax.broadcasted_iota(jnp.int32, (SUBLANES, tm), 0)
    first_max = lambda v, m: jnp.min(jnp.where(v == m, sub, SUBLANES), axis=0, keepdims=True)
    lg = jnp.where(sub < N_EXPERT_GROUPS, lo[0:SUBLANES], NEG_INF)
    mg = jnp.max(lg, axis=0, keepdims=True)
    g = first_max(lg, mg)
    pg = 1.0 / jnp.sum(jnp.exp(lg - mg), axis=0, keepdims=True)
    le = lo[SUBLANES:2 * SUBLANES]
    for gi in range(1, N_EXPERT_GROUPS):
        le = jnp.where(g == gi, lo[(gi + 1) * SUBLANES:(gi + 2) * SUBLANES], le)
    m1 = jnp.max(le, axis=0, keepdims=True)
    i1 = first_max(le, m1)
    le2 = jnp.where(sub == i1, NEG_INF, le)
    m2 = jnp.max(le2, axis=0, keepdims=True)
    i2 = first_max(le2, m2)
    e2 = jnp.exp(m2 - m1)
    w1 = pg / (1.0 + e2)
    w2 = pg * e2 / (1.0 + e2)
    ex1, ex2 = g * EXPERTS_PER_GROUP + i1, g * EXPERTS_PER_GROUP + i2

    eid = lax.broadcasted_iota(jnp.int32, (N_EXPERTS, tm), 0)
    oh1, oh2 = eid == ex1, eid == ex2
    assigned = jnp.where(oh1 | oh2, 1.0, 0.0)
    before = cnt_ref[...] + jnp.dot(assigned.astype(BF16), tri_ref[...],
                                    preferred_element_type=F32)
    r1 = jnp.sum(jnp.where(oh1, before, 0.0), axis=0, keepdims=True)
    r2 = jnp.sum(jnp.where(oh2, before, 0.0), axis=0, keepdims=True)
    cnt_ref[...] += jnp.sum(assigned, axis=1, keepdims=True)
    zrow = jnp.zeros((1, tm), jnp.int32)
    idx_ref[...] = jnp.concatenate(
        [ex1, ex2, r1.astype(jnp.int32), r2.astype(jnp.int32)] + [zrow] * (SUBLANES - 4), axis=0)
    zf = jnp.zeros((1, tm), F32)
    wrows = jnp.concatenate([zf] * 4 + [w1, w2] + [zf] * 2 + [jnp.zeros((LANES - SUBLANES, tm), F32)],
                            axis=0)
    route_ref[...] = wrows.T


def _outproj_router(x, att, cy, w_out, g_ffn, w_r, b_r, counts_in, tm, precise):
    n, d = x.shape
    rows = lambda width: pl.BlockSpec((tm, width), lambda i: (i, 0))
    full = lambda shape: pl.BlockSpec(shape, lambda i: (0,) * len(shape))
    return pl.pallas_call(
        functools.partial(_outproj_router_kernel, precise=precise),
        grid=(n // tm,),
        in_specs=[rows(d), rows(ATT_WIDTH), rows(CONV_CH), full(w_out.shape), full((1, d)),
                  full(w_r.shape), full((LANES, 1)), full((N_EXPERTS, 1))],
        out_specs=[rows(d), pl.BlockSpec((tm * TOKEN_ROWS, LANES), lambda i: (i, 0)), rows(LANES),
                   pl.BlockSpec((None, SUBLANES, tm), lambda i: (i, 0, 0)), full((N_EXPERTS, 1))],
        out_shape=[jax.ShapeDtypeStruct((n, d), F32),
                   jax.ShapeDtypeStruct((n * TOKEN_ROWS, LANES), F32),
                   jax.ShapeDtypeStruct((n, LANES), F32),
                   jax.ShapeDtypeStruct((n // tm, SUBLANES, tm), jnp.int32),
                   jax.ShapeDtypeStruct((N_EXPERTS, 1), F32)],
        scratch_shapes=[pltpu.VMEM((tm, tm), BF16)],
        compiler_params=pltpu.CompilerParams(
            dimension_semantics=("arbitrary",), vmem_limit_bytes=VMEM_LIMIT),
        name="outproj_router_precise" if precise else "outproj_router",
    )(x, att, cy, w_out, g_ffn, w_r, b_r, counts_in)


def _token_rows(ref, t):
    return ref.at[pl.ds(pl.multiple_of(t * TOKEN_ROWS, TOKEN_ROWS), TOKEN_ROWS), :]


def _start_token_copies(copies, n):
    def start(g, carry):
        for u in range(ISSUE_UNROLL):
            for c in copies(g * ISSUE_UNROLL + u):
                c.start(priority=u % DMA_QUEUES)
        return carry

    lax.fori_loop(0, n // ISSUE_UNROLL, start, 0)


def _moe_slots(idx, offsets):
    experts, inner = idx[:, :EXPERT_TOPK, :], idx[:, EXPERT_TOPK:2 * EXPERT_TOPK, :]
    onehot = experts[..., None] == jnp.arange(N_EXPERTS, dtype=jnp.int32)
    return inner + jnp.sum(jnp.where(onehot, offsets, 0), axis=-1)


def _regroup_steps(slots, tm):
    steps, k, tm0 = slots.shape
    if tm0 >= tm or (steps * tm0) % tm:
        return slots
    per = tm // tm0
    return slots.reshape(steps // per, per, k, tm0).transpose(0, 2, 1, 3).reshape(steps // per, k, tm)


def _moe_dispatch_kernel(fill_ref, na_ref, slot_ref, hn_ref, *refs, tm, n_tiles, zero_fill):
    xs_ref, zbuf, sem = refs[-3:]
    step = pl.program_id(0)

    if zero_fill:
        @pl.when(step == 0)
        def _():
            zbuf[...] = jnp.zeros_like(zbuf)

            def fill(slot):
                return pltpu.make_async_copy(
                    zbuf, xs_ref.at[pl.ds(pl.multiple_of(slot * TOKEN_ROWS, TOKEN_ROWS),
                                          MOE_TILE * TOKEN_ROWS), :], sem.at[1])

            def start_fill(i, carry):
                fill(i * MOE_TILE).start()
                return carry

            def wait_fill(i, carry):
                fill(i * MOE_TILE).wait()
                return carry

            def group_fills(act):
                for e in range(N_EXPERTS):
                    @pl.when(fill_ref[e] >= 0)
                    def _():
                        act(fill(fill_ref[e]))

            group_fills(lambda c: c.start())
            lax.fori_loop(na_ref[0], n_tiles, start_fill, 0)
            group_fills(lambda c: c.wait())
            lax.fori_loop(na_ref[0], n_tiles, wait_fill, 0)

    def copies(t):
        src = _token_rows(hn_ref, t)
        return tuple(pltpu.make_async_copy(src, _token_rows(xs_ref, slot_ref[k, t]), sem.at[0])
                     for k in range(EXPERT_TOPK))

    _start_token_copies(copies, tm)
    for _ in range(EXPERT_TOPK):
        pltpu.make_async_copy(hn_ref, xs_ref.at[pl.ds(0, tm * TOKEN_ROWS), :], sem.at[0]).wait()


def _moe_dispatch(fill_starts, n_active, slots, hn_tiles, xs_prev, n_tiles):
    n_steps, _, tm = slots.shape
    n = n_steps * tm
    zero_fill = xs_prev is None
    hbm = pl.BlockSpec(memory_space=pl.ANY)
    in_specs = [pl.BlockSpec((None, EXPERT_TOPK, tm), lambda i, *_: (i, 0, 0),
                             memory_space=pltpu.SMEM),
                pl.BlockSpec((tm * TOKEN_ROWS, LANES), lambda i, *_: (i, 0))]
    args = [fill_starts, n_active, slots, hn_tiles]
    aliases = {}
    if not zero_fill:
        in_specs.append(hbm)
        args.append(xs_prev)
        aliases = {len(args) - 1: 0}
    return pl.pallas_call(
        functools.partial(_moe_dispatch_kernel, tm=tm, n_tiles=n_tiles, zero_fill=zero_fill),
        grid_spec=pltpu.PrefetchScalarGridSpec(
            num_scalar_prefetch=2,
            grid=(n // tm,),
            in_specs=in_specs,
            out_specs=hbm,
            scratch_shapes=[pltpu.VMEM((MOE_TILE * TOKEN_ROWS, LANES), F32),
                            pltpu.SemaphoreType.DMA((2,))],
        ),
        out_shape=jax.ShapeDtypeStruct((n_tiles * MOE_TILE * TOKEN_ROWS, LANES), F32),
        input_output_aliases=aliases,
        compiler_params=pltpu.CompilerParams(
            dimension_semantics=("arbitrary",), vmem_limit_bytes=VMEM_LIMIT),
        name="moe_dispatch" if zero_fill else "moe_dispatch_more",
    )(*args)


def _moe_ffn_kernel(te_ref, na_ref, xs_ref, wg_ref, wu_ref, wd_ref, ys_ref, wg_b, wu_b, wd_b):
    i = pl.program_id(0)
    tg = MOE_TILE
    active = i < na_ref[0]

    @pl.when(active)
    def _():
        @pl.when((i == 0) | (te_ref[i] != te_ref[jnp.maximum(i - 1, 0)]))
        def _():
            wg_b[...] = wg_ref[...].astype(BF16)
            wu_b[...] = wu_ref[...].astype(BF16)
            wd_b[...] = wd_ref[...].astype(BF16)

        x = jnp.concatenate(
            [xs_ref[pl.ds(s, tg, stride=TOKEN_ROWS), :].astype(BF16) for s in range(TOKEN_ROWS)],
            axis=1)
        hg = jnp.dot(x, wg_b[...], preferred_element_type=F32)
        hu = jnp.dot(x, wu_b[...], preferred_element_type=F32)
        h = hg * (1.0 / (1.0 + jnp.exp(-hg))) * hu
        y = jnp.dot(h.astype(BF16), wd_b[...], preferred_element_type=F32)
        for s in range(TOKEN_ROWS):
            ys_ref[pl.ds(s, tg, stride=TOKEN_ROWS), :] = y[:, s * LANES:(s + 1) * LANES]


def _moe_ffn(tile_expert, n_active, xs, w_gate, w_up, w_down):
    n_tiles = tile_expert.shape[0]
    tile_rows = MOE_TILE * TOKEN_ROWS
    d, ff = w_gate.shape[1:]
    last = lambda na: na[0] - 1
    w_spec = lambda shape: pl.BlockSpec(
        (None,) + shape, lambda i, te, na: (te[jnp.minimum(i, last(na))], 0, 0))
    tile_spec = pl.BlockSpec((tile_rows, LANES), lambda i, te, na: (jnp.minimum(i, last(na)), 0))
    return pl.pallas_call(
        _moe_ffn_kernel,
        grid_spec=pltpu.PrefetchScalarGridSpec(
            num_scalar_prefetch=2,
            grid=(n_tiles,),
            in_specs=[tile_spec, w_spec((d, ff)), w_spec((d, ff)), w_spec((ff, d))],
            out_specs=tile_spec,
            scratch_shapes=[pltpu.VMEM((d, ff), BF16), pltpu.VMEM((d, ff), BF16),
                            pltpu.VMEM((ff, d), BF16)],
        ),
        out_shape=jax.ShapeDtypeStruct(xs.shape, F32),
        input_output_aliases={2: 0},
        compiler_params=pltpu.CompilerParams(
            dimension_semantics=("arbitrary",), vmem_limit_bytes=VMEM_LIMIT),
        name="moe_ffn",
    )(tile_expert, n_active, xs, w_gate, w_up, w_down)


def _moe_combine_kernel(slot_ref, next_slot_ref, route_ref, hres_ref, gf_ref, ys_ref, y_ref,
                        buf, sem, *, tm):
    i = pl.program_id(0)
    cur = i % 2
    nxt = 1 - cur

    def copies(table, t, half):
        return tuple(pltpu.make_async_copy(_token_rows(ys_ref, table[k, t]),
                                           _token_rows(buf.at[half, k], t), sem.at[half, k])
                     for k in range(EXPERT_TOPK))

    def wait_rows(half):
        for k in range(EXPERT_TOPK):
            pltpu.make_async_copy(ys_ref.at[pl.ds(0, tm * TOKEN_ROWS), :], buf.at[half, k],
                                  sem.at[half, k]).wait()

    @pl.when(i == 0)
    def _():
        _start_token_copies(lambda t: copies(slot_ref, t, 0), tm)

    wait_rows(cur)
    group = min(COMBINE_GROUP, tm)
    for g in range(tm // group):
        rows = slice(g * group, (g + 1) * group)
        w1, w2 = route_ref[rows, 4:5], route_ref[rows, 5:6]
        parts = []
        for s in range(TOKEN_ROWS):
            tile_rows = pl.ds(g * group * TOKEN_ROWS + s, group, stride=TOKEN_ROWS)
            parts.append(hres_ref[rows, s * LANES:(s + 1) * LANES]
                         + w1 * buf[cur, 0, tile_rows, :] + w2 * buf[cur, 1, tile_rows, :])
        y_ref[rows, :] = _rms(jnp.concatenate(parts, axis=1), gf_ref[...])
        for u in range(group):
            for c in copies(next_slot_ref, g * group + u, nxt):
                c.start(priority=u % DMA_QUEUES)

    @pl.when(i == pl.num_programs(0) - 1)
    def _():
        wait_rows(nxt)


def _moe_combine(slots, route, hres, g_final, ys):
    n, d = hres.shape
    steps, _, tm = slots.shape
    rows = lambda width: pl.BlockSpec((tm, width), lambda i: (i, 0))
    slot_spec = lambda step_of: pl.BlockSpec((None, EXPERT_TOPK, tm), lambda i: (step_of(i), 0, 0),
                                             memory_space=pltpu.SMEM)
    return pl.pallas_call(
        functools.partial(_moe_combine_kernel, tm=tm),
        grid=(steps,),
        in_specs=[slot_spec(lambda i: i), slot_spec(lambda i: jnp.minimum(i + 1, steps - 1)),
                  rows(LANES), rows(d), pl.BlockSpec((1, d), lambda i: (0, 0)),
                  pl.BlockSpec(memory_space=pl.ANY)],
        out_specs=rows(d),
        out_shape=jax.ShapeDtypeStruct((n, d), F32),
        scratch_shapes=[pltpu.VMEM((2, EXPERT_TOPK, tm * TOKEN_ROWS, LANES), F32),
                        pltpu.SemaphoreType.DMA((2, EXPERT_TOPK))],
        compiler_params=pltpu.CompilerParams(
            dimension_semantics=("arbitrary",), vmem_limit_bytes=VMEM_LIMIT),
        name="moe_combine",
    )(slots, slots, route, hres, g_final, ys)


def _moe_plan(counts, n_tiles):
    counts = counts[:, 0].astype(jnp.int32)
    padded = (counts + MOE_TILE - 1) // MOE_TILE * MOE_TILE
    ends = jnp.cumsum(padded)
    offsets = ends - padded
    n_active = (ends[-1:] // MOE_TILE).astype(jnp.int32)
    tile_start = jnp.arange(n_tiles, dtype=jnp.int32) * MOE_TILE
    tile_expert = jnp.minimum(jnp.sum(tile_start[:, None] >= ends[None, :], axis=1), N_EXPERTS - 1)
    fill_starts = jnp.where(padded > 0, ends - MOE_TILE, -1)
    return offsets, tile_expert.astype(jnp.int32), n_active, fill_starts.astype(jnp.int32)


def _sample_inproj_kernel(x_ref, g_ref, w_ref, cos_ref, slo_ref, shi_ref, cw_ref, p0_ref, p1_ref,
                          q_ref, k_ref, v_ref, cy_ref, u_ref):
    xn = _rms(x_ref[...], g_ref[...])
    reps = ATT_WIDTH // LANES
    cos = _tile_lanes(cos_ref[...], reps)
    slo = _tile_lanes(slo_ref[...], reps)
    shi = _tile_lanes(shi_ref[...], reps)

    def proj(i):
        return _dot_f32(xn, w_ref[:, i * ATT_WIDTH:(i + 1) * ATT_WIDTH], NN_DIMS, False)

    q_ref[...] = _rope_rows(proj(0), cos, slo, shi)
    k_ref[...] = _rope_rows(proj(1), cos, slo, shi)
    v_ref[...] = proj(2)
    b_gate = proj(3)
    u = proj(4) * proj(5)
    u_ref[...] = u
    cw = cw_ref[...]
    cy_ref[...] = b_gate * (cw[0:1, :] * p0_ref[...] + cw[1:2, :] * p1_ref[...] + cw[2:3, :] * u)


def _sample_inproj(x, g_mix, w_in, tables, conv_w, prev0, prev1):
    n = x.shape[0]
    out = jax.ShapeDtypeStruct((n, ATT_WIDTH), F32)
    return pl.pallas_call(
        _sample_inproj_kernel,
        out_shape=[out] * 5,
        compiler_params=pltpu.CompilerParams(vmem_limit_bytes=VMEM_LIMIT),
        name="sample_inproj",
    )(x, g_mix, w_in, *tables, conv_w, prev0, prev1)


def _select_past_blocks(q_ref, page_refs, part, sel_ref):
    n_blocks = part.shape[0]
    pages_per_block = len(page_refs) // n_blocks
    block_rows = pages_per_block * page_refs[0].shape[-1]
    for i in range(n_blocks):
        acc = None
        for c in range(HEAD_DIM // SUBLANES):
            rows = slice(c * SUBLANES, (c + 1) * SUBLANES)
            k_rows = page_refs[i * pages_per_block][:, rows, :]
            for p in range(1, pages_per_block):
                k_rows = k_rows + page_refs[i * pages_per_block + p][:, rows, :]
            prod = k_rows * q_ref[:, rows, :]
            acc = prod if acc is None else acc + prod
        part[i] = jnp.sum(acc, axis=1)

    lane = lax.broadcasted_iota(jnp.int32, (N_HEADS, LANES), 1)
    acc = jnp.full((N_HEADS, LANES), NEG_INF, F32)
    for j in range(n_blocks):
        gate_j = jnp.sum(part[j], axis=1, keepdims=True) * (1.0 / block_rows)
        acc = jnp.where(lane == j, gate_j, acc)
    out = jnp.zeros((N_HEADS, LANES), jnp.int32)
    for r in range(MOBA_TOPK):
        best = jnp.max(acc, axis=1, keepdims=True)
        idx = jnp.min(jnp.where(acc == best, lane, LANES), axis=1, keepdims=True)
        out = jnp.where(lane == r, idx, out)
        acc = jnp.where(lane == idx, NEG_INF, acc)
    sel_ref[...] = out


def _sample_attn_kernel(sel_ref, pt_ref, q_ref, qb_ref, kn_ref, vn_ref, ck_ref, cv_ref, o_ref,
                        kbuf, vbuf, sem, *, pages_per_block, page_size):
    b = pl.program_id(0)
    nb = pl.num_programs(0)

    def copies(bb, slot):
        out = []
        for h in range(N_HEADS):
            for r in range(MOBA_TOPK):
                block = sel_ref[(bb * N_HEADS + h) * MOBA_TOPK + r]
                for p in range(pages_per_block):
                    page = pt_ref[bb, block * pages_per_block + p]
                    dst = pl.ds((r * pages_per_block + p) * page_size, page_size)
                    out.append(pltpu.make_async_copy(ck_ref.at[page, h], kbuf.at[slot, h, :, dst],
                                                     sem.at[slot, 0]))
                    out.append(pltpu.make_async_copy(cv_ref.at[page, h], vbuf.at[slot, h, :, dst],
                                                     sem.at[slot, 1]))
        return out

    @pl.when(b == 0)
    def _():
        for c in copies(0, 0):
            c.start()

    @pl.when(b + 1 < nb)
    def _():
        for c in copies(b + 1, (b + 1) % 2):
            c.start()

    slot = b % 2
    for c in copies(b, slot):
        c.wait()

    q = q_ref[...]
    qb = qb_ref[...]
    kn = kn_ref[...]
    vn = vn_ref[...]
    n_keys = kbuf.shape[-1]
    head = lax.broadcasted_iota(jnp.int32, (N_HEADS, ATT_WIDTH), 0)
    feat = lax.broadcasted_iota(jnp.int32, (N_HEADS, ATT_WIDTH), 1)
    own = (feat >= head * HEAD_DIM) & (feat < (head + 1) * HEAD_DIM)

    s = jnp.concatenate(
        [jnp.sum(kbuf[slot, :, :, c * LANES:(c + 1) * LANES] * qb, axis=1)
         for c in range(n_keys // LANES)], axis=1) * ATT_SCALE
    s_self = jnp.sum(jnp.where(own, q * kn, 0.0), axis=1, keepdims=True) * ATT_SCALE
    m = jnp.maximum(jnp.max(s, axis=1, keepdims=True), s_self)
    p = jnp.exp(s - m)
    p_self = jnp.exp(s_self - m)
    l = jnp.sum(p, axis=1, keepdims=True) + p_self
    pv = _dot_f32(p, vbuf[slot].reshape(N_HEADS * HEAD_DIM, n_keys), NT_DIMS, False)
    o = (pv + p_self * vn) / l
    o_ref[...] = jnp.sum(jnp.where(own, o, 0.0), axis=0, keepdims=True)


def _sample_attn(sel_flat, page_table, q3, q_bcast, k3, v3, cache_kt, cache_vt):
    n_dec = q3.shape[0]
    page_size = cache_kt.shape[3]
    pages_per_block = MOBA_BLOCK // page_size
    n_keys = MOBA_TOPK * MOBA_BLOCK
    row = pl.BlockSpec((None, 1, ATT_WIDTH), lambda b, sel, pt: (b, 0, 0))
    hbm = pl.BlockSpec(memory_space=pl.ANY)
    return pl.pallas_call(
        functools.partial(_sample_attn_kernel, pages_per_block=pages_per_block,
                          page_size=page_size),
        grid_spec=pltpu.PrefetchScalarGridSpec(
            num_scalar_prefetch=2,
            grid=(n_dec,),
            in_specs=[row, pl.BlockSpec((None,) + q_bcast.shape[1:],
                                        lambda b, sel, pt: (b, 0, 0, 0)), row, row, hbm, hbm],
            out_specs=row,
            scratch_shapes=[pltpu.VMEM((2, N_HEADS, HEAD_DIM, n_keys), F32),
                            pltpu.VMEM((2, N_HEADS, HEAD_DIM, n_keys), F32),
                            pltpu.SemaphoreType.DMA((2, 2))],
        ),
        out_shape=jax.ShapeDtypeStruct((n_dec, 1, ATT_WIDTH), F32),
        compiler_params=pltpu.CompilerParams(
            dimension_semantics=("arbitrary",), vmem_limit_bytes=VMEM_LIMIT),
        name="sample_attn",
    )(sel_flat, page_table, q3, q_bcast, k3, v3, cache_kt, cache_vt)


def kernel(x_prompt, x_sample, cache_k, cache_v, state_conv, page_table, g_mix, w_in, conv_w, w_out,
           g_ffn, w_router_group, b_router_group, w_router_expert, b_router_expert, w_gate, w_up,
           w_down, g_final):
    depth = g_mix.shape[0]
    assert depth == 1
    bsz, seq, d = x_prompt.shape
    n_dec, dec_len, _ = x_sample.shape
    assert dec_len == 1
    page_size = cache_k.shape[2]
    past_len = page_table.shape[1] * page_size
    assert past_len % MOBA_BLOCK == 0 and MOBA_BLOCK % page_size == 0
    assert past_len // MOBA_BLOCK >= MOBA_TOPK and seq % MOBA_BLOCK == 0

    g_mix2 = g_mix[0][None, :]
    g_ffn2 = g_ffn[0][None, :]
    g_final2 = g_final[None, :]
    w_in_f, w_out_f = w_in[0], w_out[0]
    w_out_bf = w_out_f.astype(BF16)
    gap = SUBLANES - N_EXPERT_GROUPS
    tail = LANES - SUBLANES - N_EXPERTS
    w_r = jnp.concatenate([w_router_group[0].T, jnp.zeros((gap, d), F32),
                           w_router_expert[0].T, jnp.zeros((tail, d), F32)], axis=0)
    b_r = jnp.concatenate([b_router_group[0], jnp.zeros((gap,), F32),
                           b_router_expert[0], jnp.zeros((tail,), F32)])[:, None]
    cw = conv_w[0]

    cos_s, sin_s = _rope_angles(past_len + jnp.arange(dec_len, dtype=jnp.int32))
    x_s = x_sample.reshape(n_dec, d)
    prev0, prev1 = state_conv[0, :, 0, :], state_conv[0, :, 1, :]
    q_s, k_s, v_s, cy_s, u_s = _sample_inproj(x_s, g_mix2, w_in_f, _rope_row_tables(cos_s, sin_s),
                                              cw, prev0, prev1)
    cache_kt = jnp.transpose(cache_k[0], (0, 2, 3, 1))
    cache_vt = jnp.transpose(cache_v[0], (0, 2, 3, 1))
    q_bcast = jnp.broadcast_to(q_s.reshape(n_dec, N_HEADS, HEAD_DIM, 1),
                               (n_dec, N_HEADS, HEAD_DIM, page_size))

    cos_p, sin_p = _rope_angles(jnp.arange(seq, dtype=jnp.int32))
    conv0 = jnp.zeros((bsz, CONV_K - 1, CONV_CH), F32)
    qt_p, kt_p, vt_p, kb_p, vtb_p, kmeans_p, cy_p, conv_p = _prompt_inproj(
        x_prompt, g_mix2, w_in_f, (cos_p.T, sin_p.T), cw, conv0)
    att_p, sel = _prompt_attn(qt_p, kb_p, vtb_p, kmeans_p, page_table, q_bcast, cache_kt)
    n_p = bsz * seq
    hres_p, hn_p, route_p, idx_p, counts_p = _outproj_router(
        x_prompt.reshape(n_p, d), att_p.reshape(n_p, ATT_WIDTH), cy_p.reshape(n_p, CONV_CH),
        w_out_bf, g_ffn2, w_r, b_r, jnp.zeros((N_EXPERTS, 1), F32), tm=TOKEN_TILE, precise=False)

    sel_flat = sel[:, :, :MOBA_TOPK].reshape(-1)
    q3 = q_s.reshape(n_dec, 1, ATT_WIDTH)
    att_s = _sample_attn(sel_flat, page_table, q3, q_bcast, k_s.reshape(n_dec, 1, ATT_WIDTH),
                         v_s.reshape(n_dec, 1, ATT_WIDTH), cache_kt, cache_vt)
    hres_s, hn_s, route_s, idx_s, counts = _outproj_router(
        x_s, att_s.reshape(n_dec, ATT_WIDTH), cy_s, w_out_f, g_ffn2, w_r, b_r, counts_p,
        tm=n_dec, precise=True)

    n_assign = EXPERT_TOPK * (n_p + n_dec)
    n_tiles = -(-(n_assign + N_EXPERTS * (MOE_TILE - 1)) // MOE_TILE)
    offsets, tile_expert, n_active, fill_starts = _moe_plan(counts, n_tiles)
    slots_p, slots_s = _moe_slots(idx_p, offsets), _moe_slots(idx_s, offsets)
    slots_p = _regroup_steps(slots_p, COPY_STEP_TOKENS)
    xs = _moe_dispatch(fill_starts, n_active, slots_p, hn_p, None, n_tiles)
    xs = _moe_dispatch(fill_starts, n_active, slots_s, hn_s, xs, n_tiles)
    ys = _moe_ffn(tile_expert, n_active, xs, w_gate[0], w_up[0], w_down[0])
    y_p = _moe_combine(slots_p, route_p, hres_p, g_final2, ys)
    y_s = _moe_combine(slots_s, route_s, hres_s, g_final2, ys)

    conv_s = jnp.stack([prev1, u_s], axis=1)
    to_bthd = lambda t: jnp.transpose(t.reshape(bsz, N_HEADS, HEAD_DIM, seq), (0, 3, 1, 2))[None]
    return (y_p.reshape(bsz, seq, d), y_s.reshape(n_dec, dec_len, d),
            to_bthd(kt_p), to_bthd(vt_p), conv_p[None],
            k_s.reshape(1, n_dec, dec_len, N_HEADS, HEAD_DIM),
            v_s.reshape(1, n_dec, dec_len, N_HEADS, HEAD_DIM),
            conv_s[None])
```

```python
import functools

import jax
import jax.numpy as jnp
from jax import lax
from jax.experimental import pallas as pl
from jax.experimental.pallas import tpu as pltpu

F32 = jnp.float32
BF16 = jnp.bfloat16
HIGHEST = lax.Precision.HIGHEST

D_MODEL = 1024
HEAD_DIM = 64
N_HEADS = 8
ATT_WIDTH = N_HEADS * HEAD_DIM
CONV_CH = D_MODEL - ATT_WIDTH
ROPE_DIM = HEAD_DIM // 4
ROPE_HALF = ROPE_DIM // 2
ROPE_THETA = 500000.0
MOBA_BLOCK = 256
MOBA_TOPK = 3
ATT_SCALE = HEAD_DIM ** -0.5
CONV_K = 3
N_EXPERT_GROUPS = 4
EXPERTS_PER_GROUP = 8
N_EXPERTS = N_EXPERT_GROUPS * EXPERTS_PER_GROUP
EXPERT_TOPK = 2
EXPERT_FF = D_MODEL // 4
RMS_EPS = 1e-6

LANES = 128
SUBLANES = 8
HEADS_PER_VREG = LANES // HEAD_DIM
TOKEN_ROWS = D_MODEL // LANES
MXU_WIDTH = 256
TOKEN_TILE = 512
MOE_TILE = 512
BF16_SUBLANES = 16
V_AUG_ROWS = HEAD_DIM + BF16_SUBLANES
LOG2_E = 1.4426950408889634
SCORE_SLOTS = 2
DMA_QUEUES = 2
ISSUE_UNROLL = 16
COPY_STEP_TOKENS = 1024
COMBINE_GROUP = 32
VMEM_LIMIT = 56 * 1024 * 1024

NEG_INF = float("-inf")


def _rms(x, g):
    ms = jnp.mean(x * x, axis=-1, keepdims=True)
    return x * lax.rsqrt(ms + RMS_EPS) * g


def _mm(a, w, precise):
    if precise:
        return jnp.dot(a, w, precision=HIGHEST, preferred_element_type=F32)
    return jnp.dot(a.astype(BF16), w, preferred_element_type=F32)


NN_DIMS = (((1,), (0,)), ((), ()))
NT_DIMS = (((1,), (1,)), ((), ()))


def _dot_f32(a, b, dims, full):
    if full:
        return lax.dot_general(a, b, dims, precision=HIGHEST, preferred_element_type=F32)
    a_hi, b_hi = a.astype(BF16), b.astype(BF16)
    a_lo = (a - a_hi.astype(F32)).astype(BF16)
    b_lo = (b - b_hi.astype(F32)).astype(BF16)
    dot = functools.partial(lax.dot_general, dimension_numbers=dims, preferred_element_type=F32)
    n = b.shape[1]
    if dims == NN_DIMS and 2 * n <= MXU_WIDTH:
        both = dot(a_hi, jnp.concatenate([b_hi, b_lo], axis=1))
        return both[:, :n] + (both[:, n:] + dot(a_lo, b_hi))
    return dot(a_hi, b_hi) + (dot(a_hi, b_lo) + dot(a_lo, b_hi))


def _rope_rows(a, cos, sin_lo, sin_hi):
    n = a.shape[-1]
    return (a * cos + pltpu.roll(a, n - ROPE_HALF, 1) * sin_lo
            + pltpu.roll(a, ROPE_HALF, 1) * sin_hi)


def _tile_lanes(t, reps):
    return jnp.concatenate([t] * reps, axis=-1)


def _rope_angles(pos):
    inv = ROPE_THETA ** (-jnp.arange(0, ROPE_DIM, 2, dtype=F32) / ROPE_DIM)
    ang = pos.astype(F32)[:, None] * inv[None, :]
    return jnp.cos(ang), jnp.sin(ang)


def _rope_row_tables(cos, sin):
    rows = cos.shape[0]
    ones = jnp.ones((rows, HEAD_DIM - ROPE_DIM), F32)
    zeros = jnp.zeros((rows, HEAD_DIM - ROPE_HALF), F32)
    c = jnp.concatenate([cos, cos, ones], axis=1)
    s_lo = jnp.concatenate([-sin, zeros], axis=1)
    s_hi = jnp.concatenate([jnp.zeros((rows, ROPE_HALF), F32), sin,
                            jnp.zeros((rows, HEAD_DIM - ROPE_DIM), F32)], axis=1)
    rep = lambda t: jnp.concatenate([t] * HEADS_PER_VREG, axis=1)
    return rep(c), rep(s_lo), rep(s_hi)


def _rope_cols(ref, cos_t, sin_t):
    for h in range(N_HEADS):
        r = h * HEAD_DIM
        x1 = ref[r:r + ROPE_HALF, :]
        x2 = ref[r + ROPE_HALF:r + ROPE_DIM, :]
        ref[r:r + ROPE_HALF, :] = x1 * cos_t - x2 * sin_t
        ref[r + ROPE_HALF:r + ROPE_DIM, :] = x2 * cos_t + x1 * sin_t


def _prompt_inproj_kernel(x_ref, g_ref, w_ref, cost_ref, sint_ref, cw_ref, prev_ref,
                          eg_ref, eu_ref, ed_ref,
                          qt_ref, kt_ref, vt_ref, kb_ref, vtb_ref, km_ref, cy_ref, cn_ref,
                          egb_ref, eub_ref, edb_ref,
                          ubuf, halo, km_acc, wt_ref, wc_ref, *, tt):
    t = pl.program_id(1)
    blk = MOBA_BLOCK
    egb_ref[...] = eg_ref[...].astype(BF16)
    eub_ref[...] = eu_ref[...].astype(BF16)
    edb_ref[...] = ed_ref[...].astype(BF16)

    @pl.when((pl.program_id(0) == 0) & (t == 0))
    def _():
        for i in range(wt_ref.shape[0]):
            wt_ref[i] = w_ref[:, i * ATT_WIDTH:(i + 1) * ATT_WIDTH].T.astype(BF16)
        wc_ref[...] = w_ref[:, wt_ref.shape[0] * ATT_WIDTH:].astype(BF16)

    xn = _rms(x_ref[...], g_ref[...]).astype(BF16)

    qt_ref[...] = lax.dot_general(wt_ref[0], xn, NT_DIMS, preferred_element_type=F32)
    _rope_cols(qt_ref, cost_ref[...], sint_ref[...])
    kt_ref[...] = lax.dot_general(wt_ref[1], xn, NT_DIMS, preferred_element_type=F32)
    _rope_cols(kt_ref, cost_ref[...], sint_ref[...])
    vt = lax.dot_general(wt_ref[2], xn, NT_DIMS, preferred_element_type=F32)
    vt_ref[...] = vt
    for i in range(tt // blk):
        vtb_ref[i] = vt[:, i * blk:(i + 1) * blk].astype(BF16)

    k = kt_ref[...].T
    kb_ref[...] = k.astype(BF16)

    @pl.when(t == 0)
    def _():
        km_acc[...] = jnp.zeros_like(km_acc)

    rid = lax.broadcasted_iota(jnp.int32, km_acc.shape, 0)
    km = km_acc[...]
    for i in range(tt // blk):
        mean_i = jnp.mean(k[i * blk:(i + 1) * blk, :], axis=0, keepdims=True)
        km = jnp.where(rid == t * (tt // blk) + i, mean_i, km)
    km_acc[...] = km
    km_ref[...] = km

    def proj(i):
        return jnp.dot(xn, wc_ref[:, i * CONV_CH:(i + 1) * CONV_CH], preferred_element_type=F32)

    b_gate = proj(0)
    u = proj(1) * proj(2)

    @pl.when(t == 0)
    def _():
        ubuf[SUBLANES - (CONV_K - 1):SUBLANES, :] = prev_ref[...]

    @pl.when(t > 0)
    def _():
        ubuf[0:SUBLANES, :] = halo[...]

    ubuf[SUBLANES:SUBLANES + tt, :] = u
    cw = cw_ref[...]
    conv = (cw[0:1, :] * ubuf[SUBLANES - 2:SUBLANES - 2 + tt, :]
            + cw[1:2, :] * ubuf[SUBLANES - 1:SUBLANES - 1 + tt, :]
            + cw[2:3, :] * u)
    cy_ref[...] = (b_gate * conv).astype(cy_ref.dtype)
    halo[...] = ubuf[tt:tt + SUBLANES, :]
    cn_ref[...] = ubuf[tt + SUBLANES - (CONV_K - 1):tt + SUBLANES, :]


def _prompt_inproj(x, g_mix, w_in, col_tables, conv_w, conv_prev, expert_w, tt=TOKEN_TILE):
    bsz, seq, d = x.shape
    nt = seq // tt
    nb = seq // MOBA_BLOCK
    assert all(w.shape[0] == bsz * nt for w in expert_w), "one expert per projection grid step"
    expert_spec = lambda w: pl.BlockSpec((None,) + w.shape[1:], lambda b, t: (b * nt + t, 0, 0))
    n_att = 3
    conv_cols = w_in.shape[1] - n_att * ATT_WIDTH
    row_spec = lambda width: pl.BlockSpec((None, tt, width), lambda b, t: (b, t, 0))
    col_spec = pl.BlockSpec((None, ATT_WIDTH, tt), lambda b, t: (b, 0, t))
    tabt_spec = pl.BlockSpec((ROPE_HALF, tt), lambda b, t: (0, t))
    full = lambda shape: pl.BlockSpec(shape, lambda b, t: (0,) * len(shape))
    per_b = lambda shape: pl.BlockSpec((None,) + shape, lambda b, t: (b,) + (0,) * len(shape))
    vtb_spec = pl.BlockSpec((None, tt // MOBA_BLOCK, ATT_WIDTH, MOBA_BLOCK),
                            lambda b, t: (b, t, 0, 0))
    col_shape = jax.ShapeDtypeStruct((bsz, ATT_WIDTH, seq), F32)
    return pl.pallas_call(
        functools.partial(_prompt_inproj_kernel, tt=tt),
        grid=(bsz, nt),
        in_specs=[row_spec(d), full((1, d)),
                  pl.BlockSpec(w_in.shape, lambda b, t: (0, 0), pipeline_mode=pl.Buffered(1)),
                  tabt_spec, tabt_spec, full((CONV_K, CONV_CH)), per_b((CONV_K - 1, CONV_CH))]
                 + [expert_spec(w) for w in expert_w],
        out_specs=[col_spec, col_spec, col_spec, row_spec(ATT_WIDTH), vtb_spec,
                   per_b((nb, ATT_WIDTH)), row_spec(CONV_CH), per_b((CONV_K - 1, CONV_CH))]
                  + [expert_spec(w) for w in expert_w],
        out_shape=[col_shape, col_shape, col_shape,
                   jax.ShapeDtypeStruct((bsz, seq, ATT_WIDTH), BF16),
                   jax.ShapeDtypeStruct((bsz, nb, ATT_WIDTH, MOBA_BLOCK), BF16),
                   jax.ShapeDtypeStruct((bsz, nb, ATT_WIDTH), F32),
                   jax.ShapeDtypeStruct((bsz, seq, CONV_CH), BF16),
                   jax.ShapeDtypeStruct((bsz, CONV_K - 1, CONV_CH), F32)]
                  + [jax.ShapeDtypeStruct(w.shape, BF16) for w in expert_w],
        scratch_shapes=[pltpu.VMEM((tt + SUBLANES, CONV_CH), F32),
                        pltpu.VMEM((SUBLANES, CONV_CH), F32),
                        pltpu.VMEM((nb, ATT_WIDTH), F32),
                        pltpu.VMEM((n_att, ATT_WIDTH, d), BF16),
                        pltpu.VMEM((d, conv_cols), BF16)],
        compiler_params=pltpu.CompilerParams(
            dimension_semantics=("arbitrary", "arbitrary"), vmem_limit_bytes=VMEM_LIMIT),
        name="prompt_inproj",
    )(x, g_mix, w_in, *col_tables, conv_w, conv_prev, *expert_w)


def _prompt_attn_kernel(pt_ref, qt_ref, kb_ref, vtb_ref, km_ref, qdec_ref, *refs, nb, n_pages):
    page_refs = refs[:n_pages]
    o_ref, sel_ref, s_ref, ot_ref, va_ref, part_ref = refs[n_pages:]
    _select_past_blocks(qdec_ref, page_refs, part_ref, sel_ref)
    blk = MOBA_BLOCK
    means = km_ref[...]
    ones_row = lax.broadcasted_iota(jnp.int32, (V_AUG_ROWS - HEAD_DIM, blk), 0) == 0
    for j in range(nb):
        for h in range(HEADS_PER_VREG):
            va_ref[j, h, 0:HEAD_DIM, :] = vtb_ref[j, h * HEAD_DIM:(h + 1) * HEAD_DIM, :]
            va_ref[j, h, HEAD_DIM:V_AUG_ROWS, :] = jnp.where(ones_row, 1.0, 0.0).astype(BF16)
    feat = lax.broadcasted_iota(jnp.int32, (LANES, 1), 0)
    bid = lax.broadcasted_iota(jnp.int32, (nb, blk), 0)
    key = lax.broadcasted_iota(jnp.int32, (blk, blk), 0)
    qry = lax.broadcasted_iota(jnp.int32, (blk, blk), 1)

    def block_bias(qth, qi):
        if qi <= MOBA_TOPK:
            return [None] * qi
        gates = jnp.dot(means, qth, precision=HIGHEST, preferred_element_type=F32)
        gates = jnp.where(bid < qi, gates, NEG_INF)
        rows = []
        for j in range(qi):
            gj = gates[j:j + 1, :]
            beats = (gates > gj) | ((gates == gj) & (bid < j))
            cnt = jnp.sum(beats.astype(F32), axis=0, keepdims=True)
            rows.append(jnp.where(cnt < MOBA_TOPK, 0.0, NEG_INF))
        return rows

    def scores_pass(qi, h, slot, state):
        qt = qt_ref[:, qi * blk:(qi + 1) * blk]
        qth = jnp.where((feat >= h * HEAD_DIM) & (feat < (h + 1) * HEAD_DIM), qt, 0.0)
        bias = block_bias(qth, qi)
        qs = (qth * (ATT_SCALE * LOG2_E)).astype(BF16)
        m = None
        for j in range(qi + 1):
            s = jnp.dot(kb_ref[j * blk:(j + 1) * blk, :], qs,
                        preferred_element_type=F32)
            if j == qi:
                s = jnp.where(key <= qry, s, NEG_INF)
            elif bias[j] is not None:
                s = s + bias[j]
            s_ref[slot, j] = s
            m_blk = jnp.max(s, axis=0, keepdims=True)
            m = m_blk if m is None else jnp.maximum(m, m_blk)
            yield
        state["m"] = m

    def values_pass(qi, h, slot, state):
        m = state["m"]
        acc = None
        for j in range(qi + 1):
            p = jnp.exp2(s_ref[slot, j] - m).astype(BF16)
            pv = jnp.dot(va_ref[j, h], p, preferred_element_type=F32)
            acc = pv if acc is None else acc + pv
            yield
        ot_ref[h * HEAD_DIM:(h + 1) * HEAD_DIM, qi * blk:(qi + 1) * blk] = (
            acc[0:HEAD_DIM, :] / acc[HEAD_DIM:HEAD_DIM + 1, :])

    def run_interleaved(*gens):
        live = [g for g in gens if g is not None]
        while live:
            for g in list(live):
                if next(g, "done") == "done":
                    live.remove(g)

    pending = None
    for i, (qi, h) in enumerate((qi, h) for qi in range(nb) for h in range(HEADS_PER_VREG)):
        state = {}
        run_interleaved(scores_pass(qi, h, i % SCORE_SLOTS, state), pending)
        pending = values_pass(qi, h, i % SCORE_SLOTS, state)
    run_interleaved(pending)
    o_ref[...] = ot_ref[...].T.astype(o_ref.dtype)


def _prompt_attn(qt, kb, vtb, kmeans, page_table, q_dec, cache_kt):
    bsz, seq, _ = kb.shape
    nb = seq // MOBA_BLOCK
    n_pairs = ATT_WIDTH // LANES
    n_dec, n_pages = page_table.shape
    assert n_dec == bsz * n_pairs, "one decode sequence per prompt attention grid step"
    _, n_heads, head_dim, page_size = cache_kt.shape
    n_blocks = n_pages * page_size // MOBA_BLOCK
    dec = lambda b, hp: b * n_pairs + hp

    def page_spec(i):
        return pl.BlockSpec((None, n_heads, head_dim, page_size),
                            lambda b, hp, pt: (pt[dec(b, hp), i], 0, 0, 0))

    return pl.pallas_call(
        functools.partial(_prompt_attn_kernel, nb=nb, n_pages=n_pages),
        grid_spec=pltpu.PrefetchScalarGridSpec(
            num_scalar_prefetch=1,
            grid=(bsz, n_pairs),
            in_specs=[pl.BlockSpec((None, LANES, seq), lambda b, hp, pt: (b, hp, 0)),
                      pl.BlockSpec((None, seq, LANES), lambda b, hp, pt: (b, 0, hp)),
                      pl.BlockSpec((None, nb, LANES, MOBA_BLOCK), lambda b, hp, pt: (b, 0, hp, 0)),
                      pl.BlockSpec((None, nb, LANES), lambda b, hp, pt: (b, 0, hp)),
                      pl.BlockSpec((None, n_heads, head_dim, page_size),
                                   lambda b, hp, pt: (dec(b, hp), 0, 0, 0))]
                     + [page_spec(i) for i in range(n_pages)],
            out_specs=[pl.BlockSpec((None, seq, LANES), lambda b, hp, pt: (b, 0, hp)),
                       pl.BlockSpec((None, n_heads, LANES), lambda b, hp, pt: (dec(b, hp), 0, 0))],
            scratch_shapes=[pltpu.VMEM((SCORE_SLOTS, nb, MOBA_BLOCK, MOBA_BLOCK), F32),
                            pltpu.VMEM((LANES, seq), F32),
                            pltpu.VMEM((nb, HEADS_PER_VREG, V_AUG_ROWS, MOBA_BLOCK), BF16),
                            pltpu.VMEM((n_blocks, n_heads, LANES), F32)],
        ),
        out_shape=[jax.ShapeDtypeStruct((bsz, seq, ATT_WIDTH), BF16),
                   jax.ShapeDtypeStruct((n_dec, n_heads, LANES), jnp.int32)],
        compiler_params=pltpu.CompilerParams(
            dimension_semantics=("arbitrary", "arbitrary"), vmem_limit_bytes=VMEM_LIMIT),
        name="prompt_attn",
    )(page_table, qt, kb, vtb, kmeans, q_dec, *([cache_kt] * n_pages))


def _outproj_router_kernel(x_ref, att_ref, cy_ref, wo_ref, g_ref, wr_ref, br_ref, cnt_in_ref,
                           hres_ref, hn_ref, route_ref, idx_ref, cnt_ref, tri_ref, *, precise):
    step = pl.program_id(0)
    tm = x_ref.shape[0]
    hres = (x_ref[...] + _mm(att_ref[...], wo_ref[0:ATT_WIDTH, :], precise)
            + _mm(cy_ref[...], wo_ref[ATT_WIDTH:, :], precise))
    hres_ref[...] = hres
    hn = _rms(hres, g_ref[...])
    for s in range(TOKEN_ROWS):
        hn_ref[pl.ds(s, tm, stride=TOKEN_ROWS), :] = hn[:, s * LANES:(s + 1) * LANES]

    @pl.when(step == 0)
    def _():
        cnt_ref[...] = cnt_in_ref[...]
        r = lax.broadcasted_iota(jnp.int32, (tm, tm), 0)
        c = lax.broadcasted_iota(jnp.int32, (tm, tm), 1)
        tri_ref[...] = jnp.where(r < c, 1.0, 0.0).astype(BF16)

    lo = _dot_f32(wr_ref[...], hn, NT_DIMS, precise) + br_ref[...]
    sub = lax.broadcasted_iota(jnp.int32, (SUBLANES, tm), 0)
    first_max = lambda v, m: jnp.min(jnp.where(v == m, sub, SUBLANES), axis=0, keepdims=True)
    lg = jnp.where(sub < N_EXPERT_GROUPS, lo[0:SUBLANES], NEG_INF)
    mg = jnp.max(lg, axis=0, keepdims=True)
    g = first_max(lg, mg)
    pg = 1.0 / jnp.sum(jnp.exp(lg - mg), axis=0, keepdims=True)
    le = lo[SUBLANES:2 * SUBLANES]
    for gi in range(1, N_EXPERT_GROUPS):
        le = jnp.where(g == gi, lo[(gi + 1) * SUBLANES:(gi + 2) * SUBLANES], le)
    m1 = jnp.max(le, axis=0, keepdims=True)
    i1 = first_max(le, m1)
    le2 = jnp.where(sub == i1, NEG_INF, le)
    m2 = jnp.max(le2, axis=0, keepdims=True)
    i2 = first_max(le2, m2)
    e2 = jnp.exp(m2 - m1)
    w1 = pg / (1.0 + e2)
    w2 = pg * e2 / (1.0 + e2)
    ex1, ex2 = g * EXPERTS_PER_GROUP + i1, g * EXPERTS_PER_GROUP + i2

    eid = lax.broadcasted_iota(jnp.int32, (N_EXPERTS, tm), 0)
    oh1, oh2 = eid == ex1, eid == ex2
    assigned = jnp.where(oh1 | oh2, 1.0, 0.0)
    before = cnt_ref[...] + jnp.dot(assigned.astype(BF16), tri_ref[...],
                                    preferred_element_type=F32)
    r1 = jnp.sum(jnp.where(oh1, before, 0.0), axis=0, keepdims=True)
    r2 = jnp.sum(jnp.where(oh2, before, 0.0), axis=0, keepdims=True)
    cnt_ref[...] += jnp.sum(assigned, axis=1, keepdims=True)
    zrow = jnp.zeros((1, tm), jnp.int32)
    idx_ref[...] = jnp.concatenate(
        [ex1, ex2, r1.astype(jnp.int32), r2.astype(jnp.int32)] + [zrow] * (SUBLANES - 4), axis=0)
    zf = jnp.zeros((1, tm), F32)
    wrows = jnp.concatenate([zf] * 4 + [w1, w2] + [zf] * 2 + [jnp.zeros((LANES - SUBLANES, tm), F32)],
                            axis=0)
    route_ref[...] = wrows.T


def _outproj_router(x, att, cy, w_out, g_ffn, w_r, b_r, counts_in, tm, precise):
    n, d = x.shape
    rows = lambda width: pl.BlockSpec((tm, width), lambda i: (i, 0))
    full = lambda shape: pl.BlockSpec(shape, lambda i: (0,) * len(shape))
    return pl.pallas_call(
        functools.partial(_outproj_router_kernel, precise=precise),
        grid=(n // tm,),
        in_specs=[rows(d), rows(ATT_WIDTH), rows(CONV_CH), full(w_out.shape), full((1, d)),
                  full(w_r.shape), full((LANES, 1)), full((N_EXPERTS, 1))],
        out_specs=[rows(d), pl.BlockSpec((tm * TOKEN_ROWS, LANES), lambda i: (i, 0)), rows(LANES),
                   pl.BlockSpec((None, SUBLANES, tm), lambda i: (i, 0, 0)), full((N_EXPERTS, 1))],
        out_shape=[jax.ShapeDtypeStruct((n, d), F32),
                   jax.ShapeDtypeStruct((n * TOKEN_ROWS, LANES), F32),
                   jax.ShapeDtypeStruct((n, LANES), F32),
                   jax.ShapeDtypeStruct((n // tm, SUBLANES, tm), jnp.int32),
                   jax.ShapeDtypeStruct((N_EXPERTS, 1), F32)],
        scratch_shapes=[pltpu.VMEM((tm, tm), BF16)],
        compiler_params=pltpu.CompilerParams(
            dimension_semantics=("arbitrary",), vmem_limit_bytes=VMEM_LIMIT),
        name="outproj_router_precise" if precise else "outproj_router",
    )(x, att, cy, w_out, g_ffn, w_r, b_r, counts_in)


def _token_rows(ref, t):
    return ref.at[pl.ds(pl.multiple_of(t * TOKEN_ROWS, TOKEN_ROWS), TOKEN_ROWS), :]


def _start_token_copies(copies, n):
    def start(g, carry):
        for u in range(ISSUE_UNROLL):
            for c in copies(g * ISSUE_UNROLL + u):
                c.start(priority=u % DMA_QUEUES)
        return carry

    lax.fori_loop(0, n // ISSUE_UNROLL, start, 0)


def _moe_slots(idx, offsets):
    experts, inner = idx[:, :EXPERT_TOPK, :], idx[:, EXPERT_TOPK:2 * EXPERT_TOPK, :]
    onehot = experts[..., None] == jnp.arange(N_EXPERTS, dtype=jnp.int32)
    return inner + jnp.sum(jnp.where(onehot, offsets, 0), axis=-1)


def _regroup_steps(slots, tm):
    steps, k, tm0 = slots.shape
    if tm0 >= tm or (steps * tm0) % tm:
        return slots
    per = tm // tm0
    return slots.reshape(steps // per, per, k, tm0).transpose(0, 2, 1, 3).reshape(steps // per, k, tm)


def _moe_dispatch_kernel(fill_ref, na_ref, slot_ref, hn_ref, *refs, tm, n_tiles, zero_fill):
    xs_ref, zbuf, sem = refs[-3:]
    step = pl.program_id(0)

    if zero_fill:
        @pl.when(step == 0)
        def _():
            zbuf[...] = jnp.zeros_like(zbuf)

            def fill(slot):
                return pltpu.make_async_copy(
                    zbuf, xs_ref.at[pl.ds(pl.multiple_of(slot * TOKEN_ROWS, TOKEN_ROWS),
                                          MOE_TILE * TOKEN_ROWS), :], sem.at[1])

            def start_fill(i, carry):
                fill(i * MOE_TILE).start()
                return carry

            def wait_fill(i, carry):
                fill(i * MOE_TILE).wait()
                return carry

            def group_fills(act):
                for e in range(N_EXPERTS):
                    @pl.when(fill_ref[e] >= 0)
                    def _():
                        act(fill(fill_ref[e]))

            group_fills(lambda c: c.start())
            lax.fori_loop(na_ref[0], n_tiles, start_fill, 0)
            group_fills(lambda c: c.wait())
            lax.fori_loop(na_ref[0], n_tiles, wait_fill, 0)

    def copies(t):
        src = _token_rows(hn_ref, t)
        return tuple(pltpu.make_async_copy(src, _token_rows(xs_ref, slot_ref[k, t]), sem.at[0])
                     for k in range(EXPERT_TOPK))

    _start_token_copies(copies, tm)
    for _ in range(EXPERT_TOPK):
        pltpu.make_async_copy(hn_ref, xs_ref.at[pl.ds(0, tm * TOKEN_ROWS), :], sem.at[0]).wait()


def _moe_dispatch(fill_starts, n_active, slots, hn_tiles, xs_prev, n_tiles):
    n_steps, _, tm = slots.shape
    n = n_steps * tm
    zero_fill = xs_prev is None
    hbm = pl.BlockSpec(memory_space=pl.ANY)
    in_specs = [pl.BlockSpec((None, EXPERT_TOPK, tm), lambda i, *_: (i, 0, 0),
                             memory_space=pltpu.SMEM),
                pl.BlockSpec((tm * TOKEN_ROWS, LANES), lambda i, *_: (i, 0))]
    args = [fill_starts, n_active, slots, hn_tiles]
    aliases = {}
    if not zero_fill:
        in_specs.append(hbm)
        args.append(xs_prev)
        aliases = {len(args) - 1: 0}
    return pl.pallas_call(
        functools.partial(_moe_dispatch_kernel, tm=tm, n_tiles=n_tiles, zero_fill=zero_fill),
        grid_spec=pltpu.PrefetchScalarGridSpec(
            num_scalar_prefetch=2,
            grid=(n // tm,),
            in_specs=in_specs,
            out_specs=hbm,
            scratch_shapes=[pltpu.VMEM((MOE_TILE * TOKEN_ROWS, LANES), F32),
                            pltpu.SemaphoreType.DMA((2,))],
        ),
        out_shape=jax.ShapeDtypeStruct((n_tiles * MOE_TILE * TOKEN_ROWS, LANES), F32),
        input_output_aliases=aliases,
        compiler_params=pltpu.CompilerParams(
            dimension_semantics=("arbitrary",), vmem_limit_bytes=VMEM_LIMIT),
        name="moe_dispatch" if zero_fill else "moe_dispatch_more",
    )(*args)


def _moe_ffn_kernel(te_ref, na_ref, xs_ref, wg_ref, wu_ref, wd_ref, ys_ref):
    tg = MOE_TILE

    @pl.when(pl.program_id(0) < na_ref[0])
    def _():
        x = jnp.concatenate(
            [xs_ref[pl.ds(s, tg, stride=TOKEN_ROWS), :].astype(BF16) for s in range(TOKEN_ROWS)],
            axis=1)
        hg = jnp.dot(x, wg_ref[...], preferred_element_type=F32)
        hu = jnp.dot(x, wu_ref[...], preferred_element_type=F32)
        h = hg * (1.0 / (1.0 + jnp.exp(-hg))) * hu
        y = jnp.dot(h.astype(BF16), wd_ref[...], preferred_element_type=F32)
        for s in range(TOKEN_ROWS):
            ys_ref[pl.ds(s, tg, stride=TOKEN_ROWS), :] = y[:, s * LANES:(s + 1) * LANES]


def _moe_ffn(tile_expert, n_active, xs, w_gate, w_up, w_down):
    n_tiles = tile_expert.shape[0]
    tile_rows = MOE_TILE * TOKEN_ROWS
    d, ff = w_gate.shape[1:]
    last = lambda na: na[0] - 1
    w_spec = lambda shape: pl.BlockSpec(
        (None,) + shape, lambda i, te, na: (te[jnp.minimum(i, last(na))], 0, 0))
    tile_spec = pl.BlockSpec((tile_rows, LANES), lambda i, te, na: (jnp.minimum(i, last(na)), 0))
    return pl.pallas_call(
        _moe_ffn_kernel,
        grid_spec=pltpu.PrefetchScalarGridSpec(
            num_scalar_prefetch=2,
            grid=(n_tiles,),
            in_specs=[tile_spec, w_spec((d, ff)), w_spec((d, ff)), w_spec((ff, d))],
            out_specs=tile_spec,
        ),
        out_shape=jax.ShapeDtypeStruct(xs.shape, F32),
        input_output_aliases={2: 0},
        compiler_params=pltpu.CompilerParams(
            dimension_semantics=("arbitrary",), vmem_limit_bytes=VMEM_LIMIT),
        name="moe_ffn",
    )(tile_expert, n_active, xs, w_gate, w_up, w_down)


def _moe_combine_kernel(slot_ref, next_slot_ref, route_ref, hres_ref, gf_ref, ys_ref, y_ref,
                        buf, sem, *, tm):
    i = pl.program_id(0)
    cur = i % 2
    nxt = 1 - cur

    def copies(table, t, half):
        return tuple(pltpu.make_async_copy(_token_rows(ys_ref, table[k, t]),
                                           _token_rows(buf.at[half, k], t), sem.at[half, k])
                     for k in range(EXPERT_TOPK))

    def wait_rows(half):
        for k in range(EXPERT_TOPK):
            pltpu.make_async_copy(ys_ref.at[pl.ds(0, tm * TOKEN_ROWS), :], buf.at[half, k],
                                  sem.at[half, k]).wait()

    @pl.when(i == 0)
    def _():
        _start_token_copies(lambda t: copies(slot_ref, t, 0), tm)

    wait_rows(cur)
    group = min(COMBINE_GROUP, tm)
    for g in range(tm // group):
        rows = slice(g * group, (g + 1) * group)
        w1, w2 = route_ref[rows, 4:5], route_ref[rows, 5:6]
        parts = []
        for s in range(TOKEN_ROWS):
            tile_rows = pl.ds(g * group * TOKEN_ROWS + s, group, stride=TOKEN_ROWS)
            parts.append(hres_ref[rows, s * LANES:(s + 1) * LANES]
                         + w1 * buf[cur, 0, tile_rows, :] + w2 * buf[cur, 1, tile_rows, :])
        y_ref[rows, :] = _rms(jnp.concatenate(parts, axis=1), gf_ref[...])
        for u in range(group):
            for c in copies(next_slot_ref, g * group + u, nxt):
                c.start(priority=u % DMA_QUEUES)

    @pl.when(i == pl.num_programs(0) - 1)
    def _():
        wait_rows(nxt)


def _moe_combine(slots, route, hres, g_final, ys):
    n, d = hres.shape
    steps, _, tm = slots.shape
    rows = lambda width: pl.BlockSpec((tm, width), lambda i: (i, 0))
    slot_spec = lambda step_of: pl.BlockSpec((None, EXPERT_TOPK, tm), lambda i: (step_of(i), 0, 0),
                                             memory_space=pltpu.SMEM)
    return pl.pallas_call(
        functools.partial(_moe_combine_kernel, tm=tm),
        grid=(steps,),
        in_specs=[slot_spec(lambda i: i), slot_spec(lambda i: jnp.minimum(i + 1, steps - 1)),
                  rows(LANES), rows(d), pl.BlockSpec((1, d), lambda i: (0, 0)),
                  pl.BlockSpec(memory_space=pl.ANY)],
        out_specs=rows(d),
        out_shape=jax.ShapeDtypeStruct((n, d), F32),
        scratch_shapes=[pltpu.VMEM((2, EXPERT_TOPK, tm * TOKEN_ROWS, LANES), F32),
                        pltpu.SemaphoreType.DMA((2, EXPERT_TOPK))],
        compiler_params=pltpu.CompilerParams(
            dimension_semantics=("arbitrary",), vmem_limit_bytes=VMEM_LIMIT),
        name="moe_combine",
    )(slots, slots, route, hres, g_final, ys)


def _moe_plan(counts, n_tiles):
    counts = counts[:, 0].astype(jnp.int32)
    padded = (counts + MOE_TILE - 1) // MOE_TILE * MOE_TILE
    ends = jnp.cumsum(padded)
    offsets = ends - padded
    n_active = (ends[-1:] // MOE_TILE).astype(jnp.int32)
    tile_start = jnp.arange(n_tiles, dtype=jnp.int32) * MOE_TILE
    tile_expert = jnp.minimum(jnp.sum(tile_start[:, None] >= ends[None, :], axis=1), N_EXPERTS - 1)
    fill_starts = jnp.where(padded > 0, ends - MOE_TILE, -1)
    return offsets, tile_expert.astype(jnp.int32), n_active, fill_starts.astype(jnp.int32)


def _sample_inproj_kernel(x_ref, g_ref, w_ref, cos_ref, slo_ref, shi_ref, cw_ref, p0_ref, p1_ref,
                          q_ref, k_ref, v_ref, cy_ref, u_ref):
    xn = _rms(x_ref[...], g_ref[...])
    reps = ATT_WIDTH // LANES
    cos = _tile_lanes(cos_ref[...], reps)
    slo = _tile_lanes(slo_ref[...], reps)
    shi = _tile_lanes(shi_ref[...], reps)

    def proj(i):
        return _dot_f32(xn, w_ref[:, i * ATT_WIDTH:(i + 1) * ATT_WIDTH], NN_DIMS, False)

    q_ref[...] = _rope_rows(proj(0), cos, slo, shi)
    k_ref[...] = _rope_rows(proj(1), cos, slo, shi)
    v_ref[...] = proj(2)
    b_gate = proj(3)
    u = proj(4) * proj(5)
    u_ref[...] = u
    cw = cw_ref[...]
    cy_ref[...] = b_gate * (cw[0:1, :] * p0_ref[...] + cw[1:2, :] * p1_ref[...] + cw[2:3, :] * u)


def _sample_inproj(x, g_mix, w_in, tables, conv_w, prev0, prev1):
    n = x.shape[0]
    out = jax.ShapeDtypeStruct((n, ATT_WIDTH), F32)
    return pl.pallas_call(
        _sample_inproj_kernel,
        out_shape=[out] * 5,
        compiler_params=pltpu.CompilerParams(vmem_limit_bytes=VMEM_LIMIT),
        name="sample_inproj",
    )(x, g_mix, w_in, *tables, conv_w, prev0, prev1)


def _select_past_blocks(q_ref, page_refs, part, sel_ref):
    n_blocks = part.shape[0]
    pages_per_block = len(page_refs) // n_blocks
    block_rows = pages_per_block * page_refs[0].shape[-1]
    for i in range(n_blocks):
        acc = None
        for c in range(HEAD_DIM // SUBLANES):
            rows = slice(c * SUBLANES, (c + 1) * SUBLANES)
            k_rows = page_refs[i * pages_per_block][:, rows, :]
            for p in range(1, pages_per_block):
                k_rows = k_rows + page_refs[i * pages_per_block + p][:, rows, :]
            prod = k_rows * q_ref[:, rows, :]
            acc = prod if acc is None else acc + prod
        part[i] = jnp.sum(acc, axis=1)

    lane = lax.broadcasted_iota(jnp.int32, (N_HEADS, LANES), 1)
    acc = jnp.full((N_HEADS, LANES), NEG_INF, F32)
    for j in range(n_blocks):
        gate_j = jnp.sum(part[j], axis=1, keepdims=True) * (1.0 / block_rows)
        acc = jnp.where(lane == j, gate_j, acc)
    out = jnp.zeros((N_HEADS, LANES), jnp.int32)
    for r in range(MOBA_TOPK):
        best = jnp.max(acc, axis=1, keepdims=True)
        idx = jnp.min(jnp.where(acc == best, lane, LANES), axis=1, keepdims=True)
        out = jnp.where(lane == r, idx, out)
        acc = jnp.where(lane == idx, NEG_INF, acc)
    sel_ref[...] = out


def _sample_attn_kernel(sel_ref, pt_ref, q_ref, qb_ref, kn_ref, vn_ref, ck_ref, cv_ref, o_ref,
                        kbuf, vbuf, sem, *, pages_per_block, page_size):
    b = pl.program_id(0)
    nb = pl.num_programs(0)

    def copies(bb, slot):
        out = []
        for h in range(N_HEADS):
            for r in range(MOBA_TOPK):
                block = sel_ref[(bb * N_HEADS + h) * MOBA_TOPK + r]
                for p in range(pages_per_block):
                    page = pt_ref[bb, block * pages_per_block + p]
                    dst = pl.ds((r * pages_per_block + p) * page_size, page_size)
                    out.append(pltpu.make_async_copy(ck_ref.at[page, h], kbuf.at[slot, h, :, dst],
                                                     sem.at[slot, 0]))
                    out.append(pltpu.make_async_copy(cv_ref.at[page, h], vbuf.at[slot, h, :, dst],
                                                     sem.at[slot, 1]))
        return out

    @pl.when(b == 0)
    def _():
        for c in copies(0, 0):
            c.start()

    @pl.when(b + 1 < nb)
    def _():
        for c in copies(b + 1, (b + 1) % 2):
            c.start()

    slot = b % 2
    for c in copies(b, slot):
        c.wait()

    q = q_ref[...]
    qb = qb_ref[...]
    kn = kn_ref[...]
    vn = vn_ref[...]
    n_keys = kbuf.shape[-1]
    head = lax.broadcasted_iota(jnp.int32, (N_HEADS, ATT_WIDTH), 0)
    feat = lax.broadcasted_iota(jnp.int32, (N_HEADS, ATT_WIDTH), 1)
    own = (feat >= head * HEAD_DIM) & (feat < (head + 1) * HEAD_DIM)

    s = jnp.concatenate(
        [jnp.sum(kbuf[slot, :, :, c * LANES:(c + 1) * LANES] * qb, axis=1)
         for c in range(n_keys // LANES)], axis=1) * ATT_SCALE
    s_self = jnp.sum(jnp.where(own, q * kn, 0.0), axis=1, keepdims=True) * ATT_SCALE
    m = jnp.maximum(jnp.max(s, axis=1, keepdims=True), s_self)
    p = jnp.exp(s - m)
    p_self = jnp.exp(s_self - m)
    l = jnp.sum(p, axis=1, keepdims=True) + p_self
    pv = _dot_f32(p, vbuf[slot].reshape(N_HEADS * HEAD_DIM, n_keys), NT_DIMS, False)
    o = (pv + p_self * vn) / l
    o_ref[...] = jnp.sum(jnp.where(own, o, 0.0), axis=0, keepdims=True)


def _sample_attn(sel_flat, page_table, q3, q_bcast, k3, v3, cache_kt, cache_vt):
    n_dec = q3.shape[0]
    page_size = cache_kt.shape[3]
    pages_per_block = MOBA_BLOCK // page_size
    n_keys = MOBA_TOPK * MOBA_BLOCK
    row = pl.BlockSpec((None, 1, ATT_WIDTH), lambda b, sel, pt: (b, 0, 0))
    hbm = pl.BlockSpec(memory_space=pl.ANY)
    return pl.pallas_call(
        functools.partial(_sample_attn_kernel, pages_per_block=pages_per_block,
                          page_size=page_size),
        grid_spec=pltpu.PrefetchScalarGridSpec(
            num_scalar_prefetch=2,
            grid=(n_dec,),
            in_specs=[row, pl.BlockSpec((None,) + q_bcast.shape[1:],
                                        lambda b, sel, pt: (b, 0, 0, 0)), row, row, hbm, hbm],
            out_specs=row,
            scratch_shapes=[pltpu.VMEM((2, N_HEADS, HEAD_DIM, n_keys), F32),
                            pltpu.VMEM((2, N_HEADS, HEAD_DIM, n_keys), F32),
                            pltpu.SemaphoreType.DMA((2, 2))],
        ),
        out_shape=jax.ShapeDtypeStruct((n_dec, 1, ATT_WIDTH), F32),
        compiler_params=pltpu.CompilerParams(
            dimension_semantics=("arbitrary",), vmem_limit_bytes=VMEM_LIMIT),
        name="sample_attn",
    )(sel_flat, page_table, q3, q_bcast, k3, v3, cache_kt, cache_vt)


def kernel(x_prompt, x_sample, cache_k, cache_v, state_conv, page_table, g_mix, w_in, conv_w, w_out,
           g_ffn, w_router_group, b_router_group, w_router_expert, b_router_expert, w_gate, w_up,
           w_down, g_final):
    depth = g_mix.shape[0]
    assert depth == 1
    bsz, seq, d = x_prompt.shape
    n_dec, dec_len, _ = x_sample.shape
    assert dec_len == 1
    page_size = cache_k.shape[2]
    past_len = page_table.shape[1] * page_size
    assert past_len % MOBA_BLOCK == 0 and MOBA_BLOCK % page_size == 0
    assert past_len // MOBA_BLOCK >= MOBA_TOPK and seq % MOBA_BLOCK == 0

    g_mix2 = g_mix[0][None, :]
    g_ffn2 = g_ffn[0][None, :]
    g_final2 = g_final[None, :]
    w_in_f, w_out_f = w_in[0], w_out[0]
    w_out_bf = w_out_f.astype(BF16)
    gap = SUBLANES - N_EXPERT_GROUPS
    tail = LANES - SUBLANES - N_EXPERTS
    w_r = jnp.concatenate([w_router_group[0].T, jnp.zeros((gap, d), F32),
                           w_router_expert[0].T, jnp.zeros((tail, d), F32)], axis=0)
    b_r = jnp.concatenate([b_router_group[0], jnp.zeros((gap,), F32),
                           b_router_expert[0], jnp.zeros((tail,), F32)])[:, None]
    cw = conv_w[0]

    cos_s, sin_s = _rope_angles(past_len + jnp.arange(dec_len, dtype=jnp.int32))
    x_s = x_sample.reshape(n_dec, d)
    prev0, prev1 = state_conv[0, :, 0, :], state_conv[0, :, 1, :]
    q_s, k_s, v_s, cy_s, u_s = _sample_inproj(x_s, g_mix2, w_in_f, _rope_row_tables(cos_s, sin_s),
                                              cw, prev0, prev1)
    cache_kt = jnp.transpose(cache_k[0], (0, 2, 3, 1))
    cache_vt = jnp.transpose(cache_v[0], (0, 2, 3, 1))
    q_bcast = jnp.broadcast_to(q_s.reshape(n_dec, N_HEADS, HEAD_DIM, 1),
                               (n_dec, N_HEADS, HEAD_DIM, page_size))

    cos_p, sin_p = _rope_angles(jnp.arange(seq, dtype=jnp.int32))
    conv0 = jnp.zeros((bsz, CONV_K - 1, CONV_CH), F32)
    qt_p, kt_p, vt_p, kb_p, vtb_p, kmeans_p, cy_p, conv_p, wg_bf, wu_bf, wd_bf = _prompt_inproj(
        x_prompt, g_mix2, w_in_f, (cos_p.T, sin_p.T), cw, conv0, (w_gate[0], w_up[0], w_down[0]))
    att_p, sel = _prompt_attn(qt_p, kb_p, vtb_p, kmeans_p, page_table, q_bcast, cache_kt)
    n_p = bsz * seq
    hres_p, hn_p, route_p, idx_p, counts_p = _outproj_router(
        x_prompt.reshape(n_p, d), att_p.reshape(n_p, ATT_WIDTH), cy_p.reshape(n_p, CONV_CH),
        w_out_bf, g_ffn2, w_r, b_r, jnp.zeros((N_EXPERTS, 1), F32), tm=TOKEN_TILE, precise=False)

    sel_flat = sel[:, :, :MOBA_TOPK].reshape(-1)
    q3 = q_s.reshape(n_dec, 1, ATT_WIDTH)
    att_s = _sample_attn(sel_flat, page_table, q3, q_bcast, k_s.reshape(n_dec, 1, ATT_WIDTH),
                         v_s.reshape(n_dec, 1, ATT_WIDTH), cache_kt, cache_vt)
    hres_s, hn_s, route_s, idx_s, counts = _outproj_router(
        x_s, att_s.reshape(n_dec, ATT_WIDTH), cy_s, w_out_f, g_ffn2, w_r, b_r, counts_p,
        tm=n_dec, precise=True)

    n_assign = EXPERT_TOPK * (n_p + n_dec)
    n_tiles = -(-(n_assign + N_EXPERTS * (MOE_TILE - 1)) // MOE_TILE)
    offsets, tile_expert, n_active, fill_starts = _moe_plan(counts, n_tiles)
    slots_p, slots_s = _moe_slots(idx_p, offsets), _moe_slots(idx_s, offsets)
    slots_p = _regroup_steps(slots_p, COPY_STEP_TOKENS)
    xs = _moe_dispatch(fill_starts, n_active, slots_p, hn_p, None, n_tiles)
    xs = _moe_dispatch(fill_starts, n_active, slots_s, hn_s, xs, n_tiles)
    ys = _moe_ffn(tile_expert, n_active, xs, wg_bf, wu_bf, wd_bf)
    y_p = _moe_combine(slots_p, route_p, hres_p, g_final2, ys)
    y_s = _moe_combine(slots_s, route_s, hres_s, g_final2, ys)

    conv_s = jnp.stack([prev1, u_s], axis=1)
    to_bthd = lambda t: jnp.transpose(t.reshape(bsz, N_HEADS, HEAD_DIM, seq), (0, 3, 1, 2))[None]
    return (y_p.reshape(bsz, seq, d), y_s.reshape(n_dec, dec_len, d),
            to_bthd(kt_p), to_bthd(vt_p), conv_p[None],
            k_s.reshape(1, n_dec, dec_len, N_HEADS, HEAD_DIM),
            v_s.reshape(1, n_dec, dec_len, N_HEADS, HEAD_DIM),
            conv_s[None])
```

```python
import functools

import jax
import jax.numpy as jnp
from jax import lax
from jax.experimental import pallas as pl
from jax.experimental.pallas import tpu as pltpu

F32 = jnp.float32
BF16 = jnp.bfloat16
HIGHEST = lax.Precision.HIGHEST

D_MODEL = 1024
HEAD_DIM = 64
N_HEADS = 8
ATT_WIDTH = N_HEADS * HEAD_DIM
CONV_CH = D_MODEL - ATT_WIDTH
ROPE_DIM = HEAD_DIM // 4
ROPE_HALF = ROPE_DIM // 2
ROPE_THETA = 500000.0
MOBA_BLOCK = 256
MOBA_TOPK = 3
ATT_SCALE = HEAD_DIM ** -0.5
CONV_K = 3
N_EXPERT_GROUPS = 4
EXPERTS_PER_GROUP = 8
N_EXPERTS = N_EXPERT_GROUPS * EXPERTS_PER_GROUP
EXPERT_TOPK = 2
EXPERT_FF = D_MODEL // 4
RMS_EPS = 1e-6

LANES = 128
SUBLANES = 8
HEADS_PER_VREG = LANES // HEAD_DIM
TOKEN_ROWS = D_MODEL // LANES
MXU_WIDTH = 256
TOKEN_TILE = 512
MOE_TILE = 512
BF16_SUBLANES = 16
V_AUG_ROWS = HEAD_DIM + BF16_SUBLANES
LOG2_E = 1.4426950408889634
SCORE_SLOTS = 2
DMA_QUEUES = 2
ISSUE_UNROLL = 16
COPY_STEP_TOKENS = 1024
COMBINE_GROUP = 32
FILL_CHUNK = 64
VMEM_LIMIT = 56 * 1024 * 1024

NEG_INF = float("-inf")


def _rms(x, g):
    ms = jnp.mean(x * x, axis=-1, keepdims=True)
    return x * lax.rsqrt(ms + RMS_EPS) * g


def _mm(a, w, precise):
    if precise:
        return jnp.dot(a, w, precision=HIGHEST, preferred_element_type=F32)
    return jnp.dot(a.astype(BF16), w, preferred_element_type=F32)


NN_DIMS = (((1,), (0,)), ((), ()))
NT_DIMS = (((1,), (1,)), ((), ()))


def _dot_f32(a, b, dims, full):
    if full:
        return lax.dot_general(a, b, dims, precision=HIGHEST, preferred_element_type=F32)
    a_hi, b_hi = a.astype(BF16), b.astype(BF16)
    a_lo = (a - a_hi.astype(F32)).astype(BF16)
    b_lo = (b - b_hi.astype(F32)).astype(BF16)
    dot = functools.partial(lax.dot_general, dimension_numbers=dims, preferred_element_type=F32)
    n = b.shape[1]
    if dims == NN_DIMS and 2 * n <= MXU_WIDTH:
        both = dot(a_hi, jnp.concatenate([b_hi, b_lo], axis=1))
        return both[:, :n] + (both[:, n:] + dot(a_lo, b_hi))
    return dot(a_hi, b_hi) + (dot(a_hi, b_lo) + dot(a_lo, b_hi))


def _rope_rows(a, cos, sin_lo, sin_hi):
    n = a.shape[-1]
    return (a * cos + pltpu.roll(a, n - ROPE_HALF, 1) * sin_lo
            + pltpu.roll(a, ROPE_HALF, 1) * sin_hi)


def _tile_lanes(t, reps):
    return jnp.concatenate([t] * reps, axis=-1)


def _rope_angles(pos):
    inv = ROPE_THETA ** (-jnp.arange(0, ROPE_DIM, 2, dtype=F32) / ROPE_DIM)
    ang = pos.astype(F32)[:, None] * inv[None, :]
    return jnp.cos(ang), jnp.sin(ang)


def _rope_row_tables(cos, sin):
    rows = cos.shape[0]
    ones = jnp.ones((rows, HEAD_DIM - ROPE_DIM), F32)
    zeros = jnp.zeros((rows, HEAD_DIM - ROPE_HALF), F32)
    c = jnp.concatenate([cos, cos, ones], axis=1)
    s_lo = jnp.concatenate([-sin, zeros], axis=1)
    s_hi = jnp.concatenate([jnp.zeros((rows, ROPE_HALF), F32), sin,
                            jnp.zeros((rows, HEAD_DIM - ROPE_DIM), F32)], axis=1)
    rep = lambda t: jnp.concatenate([t] * HEADS_PER_VREG, axis=1)
    return rep(c), rep(s_lo), rep(s_hi)


def _rope_cols(ref, cos_t, sin_t):
    for h in range(N_HEADS):
        r = h * HEAD_DIM
        x1 = ref[r:r + ROPE_HALF, :]
        x2 = ref[r + ROPE_HALF:r + ROPE_DIM, :]
        ref[r:r + ROPE_HALF, :] = x1 * cos_t - x2 * sin_t
        ref[r + ROPE_HALF:r + ROPE_DIM, :] = x2 * cos_t + x1 * sin_t


def _prompt_inproj_kernel(x_ref, g_ref, w_ref, cost_ref, sint_ref, cw_ref, prev_ref,
                          eg_ref, eu_ref, ed_ref,
                          qt_ref, kt_ref, vt_ref, kb_ref, vtb_ref, km_ref, cy_ref, cn_ref,
                          egb_ref, eub_ref, edb_ref,
                          ubuf, halo, km_acc, wt_ref, wc_ref, *, tt):
    t = pl.program_id(1)
    blk = MOBA_BLOCK
    egb_ref[...] = eg_ref[...].astype(BF16)
    eub_ref[...] = eu_ref[...].astype(BF16)
    edb_ref[...] = ed_ref[...].astype(BF16)

    @pl.when((pl.program_id(0) == 0) & (t == 0))
    def _():
        for i in range(wt_ref.shape[0]):
            wt_ref[i] = w_ref[:, i * ATT_WIDTH:(i + 1) * ATT_WIDTH].T.astype(BF16)
        wc_ref[...] = w_ref[:, wt_ref.shape[0] * ATT_WIDTH:].astype(BF16)

    xn = _rms(x_ref[...], g_ref[...]).astype(BF16)

    qt_ref[...] = lax.dot_general(wt_ref[0], xn, NT_DIMS, preferred_element_type=F32)
    _rope_cols(qt_ref, cost_ref[...], sint_ref[...])
    kt_ref[...] = lax.dot_general(wt_ref[1], xn, NT_DIMS, preferred_element_type=F32)
    _rope_cols(kt_ref, cost_ref[...], sint_ref[...])
    vt = lax.dot_general(wt_ref[2], xn, NT_DIMS, preferred_element_type=F32)
    vt_ref[...] = vt
    for i in range(tt // blk):
        vtb_ref[i] = vt[:, i * blk:(i + 1) * blk].astype(BF16)

    k = kt_ref[...].T
    kb_ref[...] = k.astype(BF16)

    @pl.when(t == 0)
    def _():
        km_acc[...] = jnp.zeros_like(km_acc)

    rid = lax.broadcasted_iota(jnp.int32, km_acc.shape, 0)
    km = km_acc[...]
    for i in range(tt // blk):
        mean_i = jnp.mean(k[i * blk:(i + 1) * blk, :], axis=0, keepdims=True)
        km = jnp.where(rid == t * (tt // blk) + i, mean_i, km)
    km_acc[...] = km
    km_ref[...] = km

    def proj(i):
        return jnp.dot(xn, wc_ref[:, i * CONV_CH:(i + 1) * CONV_CH], preferred_element_type=F32)

    b_gate = proj(0)
    u = proj(1) * proj(2)

    @pl.when(t == 0)
    def _():
        ubuf[SUBLANES - (CONV_K - 1):SUBLANES, :] = prev_ref[...]

    @pl.when(t > 0)
    def _():
        ubuf[0:SUBLANES, :] = halo[...]

    ubuf[SUBLANES:SUBLANES + tt, :] = u
    cw = cw_ref[...]
    conv = (cw[0:1, :] * ubuf[SUBLANES - 2:SUBLANES - 2 + tt, :]
            + cw[1:2, :] * ubuf[SUBLANES - 1:SUBLANES - 1 + tt, :]
            + cw[2:3, :] * u)
    cy_ref[...] = (b_gate * conv).astype(cy_ref.dtype)
    halo[...] = ubuf[tt:tt + SUBLANES, :]
    cn_ref[...] = ubuf[tt + SUBLANES - (CONV_K - 1):tt + SUBLANES, :]


def _prompt_inproj(x, g_mix, w_in, col_tables, conv_w, conv_prev, expert_w, tt=TOKEN_TILE):
    bsz, seq, d = x.shape
    nt = seq // tt
    nb = seq // MOBA_BLOCK
    assert all(w.shape[0] == bsz * nt for w in expert_w), "one expert per projection grid step"
    expert_spec = lambda w: pl.BlockSpec((None,) + w.shape[1:], lambda b, t: (b * nt + t, 0, 0))
    n_att = 3
    conv_cols = w_in.shape[1] - n_att * ATT_WIDTH
    row_spec = lambda width: pl.BlockSpec((None, tt, width), lambda b, t: (b, t, 0))
    col_spec = pl.BlockSpec((None, ATT_WIDTH, tt), lambda b, t: (b, 0, t))
    tabt_spec = pl.BlockSpec((ROPE_HALF, tt), lambda b, t: (0, t))
    full = lambda shape: pl.BlockSpec(shape, lambda b, t: (0,) * len(shape))
    per_b = lambda shape: pl.BlockSpec((None,) + shape, lambda b, t: (b,) + (0,) * len(shape))
    vtb_spec = pl.BlockSpec((None, tt // MOBA_BLOCK, ATT_WIDTH, MOBA_BLOCK),
                            lambda b, t: (b, t, 0, 0))
    col_shape = jax.ShapeDtypeStruct((bsz, ATT_WIDTH, seq), F32)
    return pl.pallas_call(
        functools.partial(_prompt_inproj_kernel, tt=tt),
        grid=(bsz, nt),
        in_specs=[row_spec(d), full((1, d)),
                  pl.BlockSpec(w_in.shape, lambda b, t: (0, 0), pipeline_mode=pl.Buffered(1)),
                  tabt_spec, tabt_spec, full((CONV_K, CONV_CH)), per_b((CONV_K - 1, CONV_CH))]
                 + [expert_spec(w) for w in expert_w],
        out_specs=[col_spec, col_spec, col_spec, row_spec(ATT_WIDTH), vtb_spec,
                   per_b((nb, ATT_WIDTH)), row_spec(CONV_CH), per_b((CONV_K - 1, CONV_CH))]
                  + [expert_spec(w) for w in expert_w],
        out_shape=[col_shape, col_shape, col_shape,
                   jax.ShapeDtypeStruct((bsz, seq, ATT_WIDTH), BF16),
                   jax.ShapeDtypeStruct((bsz, nb, ATT_WIDTH, MOBA_BLOCK), BF16),
                   jax.ShapeDtypeStruct((bsz, nb, ATT_WIDTH), F32),
                   jax.ShapeDtypeStruct((bsz, seq, CONV_CH), BF16),
                   jax.ShapeDtypeStruct((bsz, CONV_K - 1, CONV_CH), F32)]
                  + [jax.ShapeDtypeStruct(w.shape, BF16) for w in expert_w],
        scratch_shapes=[pltpu.VMEM((tt + SUBLANES, CONV_CH), F32),
                        pltpu.VMEM((SUBLANES, CONV_CH), F32),
                        pltpu.VMEM((nb, ATT_WIDTH), F32),
                        pltpu.VMEM((n_att, ATT_WIDTH, d), BF16),
                        pltpu.VMEM((d, conv_cols), BF16)],
        compiler_params=pltpu.CompilerParams(
            dimension_semantics=("arbitrary", "arbitrary"), vmem_limit_bytes=VMEM_LIMIT),
        name="prompt_inproj",
    )(x, g_mix, w_in, *col_tables, conv_w, conv_prev, *expert_w)


def _prompt_attn_kernel(pt_ref, qt_ref, kb_ref, vtb_ref, km_ref, qdec_ref, *refs, nb, n_pages):
    page_refs = refs[:n_pages]
    o_ref, sel_ref, s_ref, ot_ref, va_ref, part_ref = refs[n_pages:]
    _select_past_blocks(qdec_ref, page_refs, part_ref, sel_ref)
    blk = MOBA_BLOCK
    means = km_ref[...]
    ones_row = lax.broadcasted_iota(jnp.int32, (V_AUG_ROWS - HEAD_DIM, blk), 0) == 0
    for j in range(nb):
        for h in range(HEADS_PER_VREG):
            va_ref[j, h, 0:HEAD_DIM, :] = vtb_ref[j, h * HEAD_DIM:(h + 1) * HEAD_DIM, :]
            va_ref[j, h, HEAD_DIM:V_AUG_ROWS, :] = jnp.where(ones_row, 1.0, 0.0).astype(BF16)
    feat = lax.broadcasted_iota(jnp.int32, (LANES, 1), 0)
    bid = lax.broadcasted_iota(jnp.int32, (nb, blk), 0)
    key = lax.broadcasted_iota(jnp.int32, (blk, blk), 0)
    qry = lax.broadcasted_iota(jnp.int32, (blk, blk), 1)

    def block_bias(qth, qi):
        if qi <= MOBA_TOPK:
            return [None] * qi
        gates = jnp.dot(means, qth, precision=HIGHEST, preferred_element_type=F32)
        gates = jnp.where(bid < qi, gates, NEG_INF)
        rows = []
        for j in range(qi):
            gj = gates[j:j + 1, :]
            beats = (gates > gj) | ((gates == gj) & (bid < j))
            cnt = jnp.sum(beats.astype(F32), axis=0, keepdims=True)
            rows.append(jnp.where(cnt < MOBA_TOPK, 0.0, NEG_INF))
        return rows

    def scores_pass(qi, h, slot, state):
        qt = qt_ref[:, qi * blk:(qi + 1) * blk]
        qth = jnp.where((feat >= h * HEAD_DIM) & (feat < (h + 1) * HEAD_DIM), qt, 0.0)
        bias = block_bias(qth, qi)
        qs = (qth * (ATT_SCALE * LOG2_E)).astype(BF16)
        m = None
        for j in range(qi + 1):
            s = jnp.dot(kb_ref[j * blk:(j + 1) * blk, :], qs,
                        preferred_element_type=F32)
            if j == qi:
                s = jnp.where(key <= qry, s, NEG_INF)
            elif bias[j] is not None:
                s = s + bias[j]
            s_ref[slot, j] = s
            m_blk = jnp.max(s, axis=0, keepdims=True)
            m = m_blk if m is None else jnp.maximum(m, m_blk)
            yield
        state["m"] = m

    def values_pass(qi, h, slot, state):
        m = state["m"]
        acc = None
        for j in range(qi + 1):
            p = jnp.exp2(s_ref[slot, j] - m).astype(BF16)
            pv = jnp.dot(va_ref[j, h], p, preferred_element_type=F32)
            acc = pv if acc is None else acc + pv
            yield
        ot_ref[h * HEAD_DIM:(h + 1) * HEAD_DIM, qi * blk:(qi + 1) * blk] = (
            acc[0:HEAD_DIM, :] / acc[HEAD_DIM:HEAD_DIM + 1, :])

    def run_interleaved(*gens):
        live = [g for g in gens if g is not None]
        while live:
            for g in list(live):
                if next(g, "done") == "done":
                    live.remove(g)

    pending = None
    for i, (qi, h) in enumerate((qi, h) for qi in range(nb) for h in range(HEADS_PER_VREG)):
        state = {}
        run_interleaved(scores_pass(qi, h, i % SCORE_SLOTS, state), pending)
        pending = values_pass(qi, h, i % SCORE_SLOTS, state)
    run_interleaved(pending)
    o_ref[...] = ot_ref[...].T.astype(o_ref.dtype)


def _prompt_attn(qt, kb, vtb, kmeans, page_table, q_dec, cache_kt):
    bsz, seq, _ = kb.shape
    nb = seq // MOBA_BLOCK
    n_pairs = ATT_WIDTH // LANES
    n_dec, n_pages = page_table.shape
    assert n_dec == bsz * n_pairs, "one decode sequence per prompt attention grid step"
    _, n_heads, head_dim, page_size = cache_kt.shape
    n_blocks = n_pages * page_size // MOBA_BLOCK
    dec = lambda b, hp: b * n_pairs + hp

    def page_spec(i):
        return pl.BlockSpec((None, n_heads, head_dim, page_size),
                            lambda b, hp, pt: (pt[dec(b, hp), i], 0, 0, 0))

    return pl.pallas_call(
        functools.partial(_prompt_attn_kernel, nb=nb, n_pages=n_pages),
        grid_spec=pltpu.PrefetchScalarGridSpec(
            num_scalar_prefetch=1,
            grid=(bsz, n_pairs),
            in_specs=[pl.BlockSpec((None, LANES, seq), lambda b, hp, pt: (b, hp, 0)),
                      pl.BlockSpec((None, seq, LANES), lambda b, hp, pt: (b, 0, hp)),
                      pl.BlockSpec((None, nb, LANES, MOBA_BLOCK), lambda b, hp, pt: (b, 0, hp, 0)),
                      pl.BlockSpec((None, nb, LANES), lambda b, hp, pt: (b, 0, hp)),
                      pl.BlockSpec((None, n_heads, head_dim, page_size),
                                   lambda b, hp, pt: (dec(b, hp), 0, 0, 0))]
                     + [page_spec(i) for i in range(n_pages)],
            out_specs=[pl.BlockSpec((None, seq, LANES), lambda b, hp, pt: (b, 0, hp)),
                       pl.BlockSpec((None, n_heads, LANES), lambda b, hp, pt: (dec(b, hp), 0, 0))],
            scratch_shapes=[pltpu.VMEM((SCORE_SLOTS, nb, MOBA_BLOCK, MOBA_BLOCK), F32),
                            pltpu.VMEM((LANES, seq), F32),
                            pltpu.VMEM((nb, HEADS_PER_VREG, V_AUG_ROWS, MOBA_BLOCK), BF16),
                            pltpu.VMEM((n_blocks, n_heads, LANES), F32)],
        ),
        out_shape=[jax.ShapeDtypeStruct((bsz, seq, ATT_WIDTH), BF16),
                   jax.ShapeDtypeStruct((n_dec, n_heads, LANES), jnp.int32)],
        compiler_params=pltpu.CompilerParams(
            dimension_semantics=("arbitrary", "arbitrary"), vmem_limit_bytes=VMEM_LIMIT),
        name="prompt_attn",
    )(page_table, qt, kb, vtb, kmeans, q_dec, *([cache_kt] * n_pages))


def _outproj_router_kernel(x_ref, att_ref, cy_ref, wo_ref, g_ref, wr_ref, br_ref, cnt_in_ref,
                           hres_ref, hn_ref, route_ref, idx_ref, cnt_ref, tri_ref, *, precise):
    step = pl.program_id(0)
    tm = x_ref.shape[0]
    hres = (x_ref[...] + _mm(att_ref[...], wo_ref[0:ATT_WIDTH, :], precise)
            + _mm(cy_ref[...], wo_ref[ATT_WIDTH:, :], precise))
    hres_ref[...] = hres
    hn = _rms(hres, g_ref[...])
    for s in range(TOKEN_ROWS):
        hn_ref[pl.ds(s, tm, stride=TOKEN_ROWS), :] = hn[:, s * LANES:(s + 1) * LANES]

    @pl.when(step == 0)
    def _():
        cnt_ref[...] = cnt_in_ref[...]
        r = lax.broadcasted_iota(jnp.int32, (tm, tm), 0)
        c = lax.broadcasted_iota(jnp.int32, (tm, tm), 1)
        tri_ref[...] = jnp.where(r < c, 1.0, 0.0).astype(BF16)

    lo = _dot_f32(wr_ref[...], hn, NT_DIMS, precise) + br_ref[...]
    sub = lax.broadcasted_iota(jnp.int32, (SUBLANES, tm), 0)
    first_max = lambda v, m: jnp.min(jnp.where(v == m, sub, SUBLANES), axis=0, keepdims=True)
    lg = jnp.where(sub < N_EXPERT_GROUPS, lo[0:SUBLANES], NEG_INF)
    mg = jnp.max(lg, axis=0, keepdims=True)
    g = first_max(lg, mg)
    pg = 1.0 / jnp.sum(jnp.exp(lg - mg), axis=0, keepdims=True)
    le = lo[SUBLANES:2 * SUBLANES]
    for gi in range(1, N_EXPERT_GROUPS):
        le = jnp.where(g == gi, lo[(gi + 1) * SUBLANES:(gi + 2) * SUBLANES], le)
    m1 = jnp.max(le, axis=0, keepdims=True)
    i1 = first_max(le, m1)
    le2 = jnp.where(sub == i1, NEG_INF, le)
    m2 = jnp.max(le2, axis=0, keepdims=True)
    i2 = first_max(le2, m2)
    e2 = jnp.exp(m2 - m1)
    w1 = pg / (1.0 + e2)
    w2 = pg * e2 / (1.0 + e2)
    ex1, ex2 = g * EXPERTS_PER_GROUP + i1, g * EXPERTS_PER_GROUP + i2

    eid = lax.broadcasted_iota(jnp.int32, (N_EXPERTS, tm), 0)
    oh1, oh2 = eid == ex1, eid == ex2
    assigned = jnp.where(oh1 | oh2, 1.0, 0.0)
    before = cnt_ref[...] + jnp.dot(assigned.astype(BF16), tri_ref[...],
                                    preferred_element_type=F32)
    r1 = jnp.sum(jnp.where(oh1, before, 0.0), axis=0, keepdims=True)
    r2 = jnp.sum(jnp.where(oh2, before, 0.0), axis=0, keepdims=True)
    cnt_ref[...] += jnp.sum(assigned, axis=1, keepdims=True)
    zrow = jnp.zeros((1, tm), jnp.int32)
    idx_ref[...] = jnp.concatenate(
        [ex1, ex2, r1.astype(jnp.int32), r2.astype(jnp.int32)] + [zrow] * (SUBLANES - 4), axis=0)
    zf = jnp.zeros((1, tm), F32)
    wrows = jnp.concatenate([zf] * 4 + [w1, w2] + [zf] * 2 + [jnp.zeros((LANES - SUBLANES, tm), F32)],
                            axis=0)
    route_ref[...] = wrows.T


def _outproj_router(x, att, cy, w_out, g_ffn, w_r, b_r, counts_in, tm, precise):
    n, d = x.shape
    rows = lambda width: pl.BlockSpec((tm, width), lambda i: (i, 0))
    full = lambda shape: pl.BlockSpec(shape, lambda i: (0,) * len(shape))
    return pl.pallas_call(
        functools.partial(_outproj_router_kernel, precise=precise),
        grid=(n // tm,),
        in_specs=[rows(d), rows(ATT_WIDTH), rows(CONV_CH), full(w_out.shape), full((1, d)),
                  full(w_r.shape), full((LANES, 1)), full((N_EXPERTS, 1))],
        out_specs=[rows(d), pl.BlockSpec((tm * TOKEN_ROWS, LANES), lambda i: (i, 0)), rows(LANES),
                   pl.BlockSpec((None, SUBLANES, tm), lambda i: (i, 0, 0)), full((N_EXPERTS, 1))],
        out_shape=[jax.ShapeDtypeStruct((n, d), F32),
                   jax.ShapeDtypeStruct((n * TOKEN_ROWS, LANES), F32),
                   jax.ShapeDtypeStruct((n, LANES), F32),
                   jax.ShapeDtypeStruct((n // tm, SUBLANES, tm), jnp.int32),
                   jax.ShapeDtypeStruct((N_EXPERTS, 1), F32)],
        scratch_shapes=[pltpu.VMEM((tm, tm), BF16)],
        compiler_params=pltpu.CompilerParams(
            dimension_semantics=("arbitrary",), vmem_limit_bytes=VMEM_LIMIT),
        name="outproj_router_precise" if precise else "outproj_router",
    )(x, att, cy, w_out, g_ffn, w_r, b_r, counts_in)


def _token_rows(ref, t):
    return ref.at[pl.ds(pl.multiple_of(t * TOKEN_ROWS, TOKEN_ROWS), TOKEN_ROWS), :]


def _start_token_copies(copies, n):
    def start(g, carry):
        for u in range(ISSUE_UNROLL):
            for c in copies(g * ISSUE_UNROLL + u):
                c.start(priority=u % DMA_QUEUES)
        return carry

    lax.fori_loop(0, n // ISSUE_UNROLL, start, 0)


def _moe_slots(idx, offsets):
    experts, inner = idx[:, :EXPERT_TOPK, :], idx[:, EXPERT_TOPK:2 * EXPERT_TOPK, :]
    onehot = experts[..., None] == jnp.arange(N_EXPERTS, dtype=jnp.int32)
    return inner + jnp.sum(jnp.where(onehot, offsets, 0), axis=-1)


def _regroup_steps(slots, tm):
    steps, k, tm0 = slots.shape
    if tm0 >= tm or (steps * tm0) % tm:
        return slots
    per = tm // tm0
    return slots.reshape(steps // per, per, k, tm0).transpose(0, 2, 1, 3).reshape(steps // per, k, tm)


def _moe_dispatch_kernel(from_ref, to_ref, slot_ref, hn_ref, *refs, tm, zero_fill):
    xs_ref, zbuf, sem = refs[-3:]
    step = pl.program_id(0)

    if zero_fill:
        @pl.when(step == 0)
        def _():
            zbuf[...] = jnp.zeros_like(zbuf)

            def fill(chunk):
                row = pl.multiple_of(chunk * (FILL_CHUNK * TOKEN_ROWS), FILL_CHUNK * TOKEN_ROWS)
                return pltpu.make_async_copy(
                    zbuf, xs_ref.at[pl.ds(row, FILL_CHUNK * TOKEN_ROWS), :], sem.at[1])

            def start_fill(c, carry):
                fill(c).start()
                return carry

            def wait_fill(c, carry):
                fill(c).wait()
                return carry

            for act in (start_fill, wait_fill):
                for r in range(from_ref.shape[0]):
                    lax.fori_loop(from_ref[r], to_ref[r], act, 0)

    def copies(t):
        src = _token_rows(hn_ref, t)
        return tuple(pltpu.make_async_copy(src, _token_rows(xs_ref, slot_ref[k, t]), sem.at[0])
                     for k in range(EXPERT_TOPK))

    _start_token_copies(copies, tm)
    for _ in range(EXPERT_TOPK):
        pltpu.make_async_copy(hn_ref, xs_ref.at[pl.ds(0, tm * TOKEN_ROWS), :], sem.at[0]).wait()


def _moe_dispatch(fill_from, fill_to, slots, hn_tiles, xs_prev, n_tiles):
    n_steps, _, tm = slots.shape
    n = n_steps * tm
    zero_fill = xs_prev is None
    hbm = pl.BlockSpec(memory_space=pl.ANY)
    in_specs = [pl.BlockSpec((None, EXPERT_TOPK, tm), lambda i, *_: (i, 0, 0),
                             memory_space=pltpu.SMEM),
                pl.BlockSpec((tm * TOKEN_ROWS, LANES), lambda i, *_: (i, 0))]
    args = [fill_from, fill_to, slots, hn_tiles]
    aliases = {}
    if not zero_fill:
        in_specs.append(hbm)
        args.append(xs_prev)
        aliases = {len(args) - 1: 0}
    return pl.pallas_call(
        functools.partial(_moe_dispatch_kernel, tm=tm, zero_fill=zero_fill),
        grid_spec=pltpu.PrefetchScalarGridSpec(
            num_scalar_prefetch=2,
            grid=(n // tm,),
            in_specs=in_specs,
            out_specs=hbm,
            scratch_shapes=[pltpu.VMEM((FILL_CHUNK * TOKEN_ROWS, LANES), F32),
                            pltpu.SemaphoreType.DMA((2,))],
        ),
        out_shape=jax.ShapeDtypeStruct((n_tiles * MOE_TILE * TOKEN_ROWS, LANES), F32),
        input_output_aliases=aliases,
        compiler_params=pltpu.CompilerParams(
            dimension_semantics=("arbitrary",), vmem_limit_bytes=VMEM_LIMIT),
        name="moe_dispatch" if zero_fill else "moe_dispatch_more",
    )(*args)


def _moe_ffn_kernel(te_ref, na_ref, xs_ref, wg_ref, wu_ref, wd_ref, ys_ref):
    tg = MOE_TILE

    @pl.when(pl.program_id(0) < na_ref[0])
    def _():
        x = jnp.concatenate(
            [xs_ref[pl.ds(s, tg, stride=TOKEN_ROWS), :].astype(BF16) for s in range(TOKEN_ROWS)],
            axis=1)
        hg = jnp.dot(x, wg_ref[...], preferred_element_type=F32)
        hu = jnp.dot(x, wu_ref[...], preferred_element_type=F32)
        h = hg * (1.0 / (1.0 + jnp.exp(-hg))) * hu
        y = jnp.dot(h.astype(BF16), wd_ref[...], preferred_element_type=F32)
        for s in range(TOKEN_ROWS):
            ys_ref[pl.ds(s, tg, stride=TOKEN_ROWS), :] = y[:, s * LANES:(s + 1) * LANES]


def _moe_ffn(tile_expert, n_active, xs, w_gate, w_up, w_down):
    n_tiles = tile_expert.shape[0]
    tile_rows = MOE_TILE * TOKEN_ROWS
    d, ff = w_gate.shape[1:]
    last = lambda na: na[0] - 1
    w_spec = lambda shape: pl.BlockSpec(
        (None,) + shape, lambda i, te, na: (te[jnp.minimum(i, last(na))], 0, 0))
    tile_spec = pl.BlockSpec((tile_rows, LANES), lambda i, te, na: (jnp.minimum(i, last(na)), 0))
    return pl.pallas_call(
        _moe_ffn_kernel,
        grid_spec=pltpu.PrefetchScalarGridSpec(
            num_scalar_prefetch=2,
            grid=(n_tiles,),
            in_specs=[tile_spec, w_spec((d, ff)), w_spec((d, ff)), w_spec((ff, d))],
            out_specs=tile_spec,
        ),
        out_shape=jax.ShapeDtypeStruct(xs.shape, F32),
        input_output_aliases={2: 0},
        compiler_params=pltpu.CompilerParams(
            dimension_semantics=("arbitrary",), vmem_limit_bytes=VMEM_LIMIT),
        name="moe_ffn",
    )(tile_expert, n_active, xs, w_gate, w_up, w_down)


def _moe_combine_kernel(slot_ref, next_slot_ref, route_ref, hres_ref, gf_ref, ys_ref, y_ref,
                        buf, sem, *, tm):
    i = pl.program_id(0)
    cur = i % 2
    nxt = 1 - cur

    def copies(table, t, half):
        return tuple(pltpu.make_async_copy(_token_rows(ys_ref, table[k, t]),
                                           _token_rows(buf.at[half, k], t), sem.at[half, k])
                     for k in range(EXPERT_TOPK))

    def wait_rows(half):
        for k in range(EXPERT_TOPK):
            pltpu.make_async_copy(ys_ref.at[pl.ds(0, tm * TOKEN_ROWS), :], buf.at[half, k],
                                  sem.at[half, k]).wait()

    @pl.when(i == 0)
    def _():
        _start_token_copies(lambda t: copies(slot_ref, t, 0), tm)

    wait_rows(cur)
    group = min(COMBINE_GROUP, tm)
    for g in range(tm // group):
        rows = slice(g * group, (g + 1) * group)
        w1, w2 = route_ref[rows, 4:5], route_ref[rows, 5:6]
        parts = []
        for s in range(TOKEN_ROWS):
            tile_rows = pl.ds(g * group * TOKEN_ROWS + s, group, stride=TOKEN_ROWS)
            parts.append(hres_ref[rows, s * LANES:(s + 1) * LANES]
                         + w1 * buf[cur, 0, tile_rows, :] + w2 * buf[cur, 1, tile_rows, :])
        y_ref[rows, :] = _rms(jnp.concatenate(parts, axis=1), gf_ref[...])
        for u in range(group):
            for c in copies(next_slot_ref, g * group + u, nxt):
                c.start(priority=u % DMA_QUEUES)

    @pl.when(i == pl.num_programs(0) - 1)
    def _():
        wait_rows(nxt)


def _moe_combine(slots, route, hres, g_final, ys):
    n, d = hres.shape
    steps, _, tm = slots.shape
    rows = lambda width: pl.BlockSpec((tm, width), lambda i: (i, 0))
    slot_spec = lambda step_of: pl.BlockSpec((None, EXPERT_TOPK, tm), lambda i: (step_of(i), 0, 0),
                                             memory_space=pltpu.SMEM)
    return pl.pallas_call(
        functools.partial(_moe_combine_kernel, tm=tm),
        grid=(steps,),
        in_specs=[slot_spec(lambda i: i), slot_spec(lambda i: jnp.minimum(i + 1, steps - 1)),
                  rows(LANES), rows(d), pl.BlockSpec((1, d), lambda i: (0, 0)),
                  pl.BlockSpec(memory_space=pl.ANY)],
        out_specs=rows(d),
        out_shape=jax.ShapeDtypeStruct((n, d), F32),
        scratch_shapes=[pltpu.VMEM((2, EXPERT_TOPK, tm * TOKEN_ROWS, LANES), F32),
                        pltpu.SemaphoreType.DMA((2, EXPERT_TOPK))],
        compiler_params=pltpu.CompilerParams(
            dimension_semantics=("arbitrary",), vmem_limit_bytes=VMEM_LIMIT),
        name="moe_combine",
    )(slots, slots, route, hres, g_final, ys)


def _moe_plan(counts, first_counts, n_tiles):
    counts = counts[:, 0].astype(jnp.int32)
    first_counts = first_counts[:, 0].astype(jnp.int32)
    padded = (counts + MOE_TILE - 1) // MOE_TILE * MOE_TILE
    ends = jnp.cumsum(padded)
    offsets = ends - padded
    n_active = (ends[-1:] // MOE_TILE).astype(jnp.int32)
    tile_start = jnp.arange(n_tiles, dtype=jnp.int32) * MOE_TILE
    tile_expert = jnp.minimum(jnp.sum(tile_start[:, None] >= ends[None, :], axis=1), N_EXPERTS - 1)
    fill_from = jnp.concatenate([(offsets + first_counts) // FILL_CHUNK, ends[-1:] // FILL_CHUNK])
    fill_to = jnp.concatenate([ends // FILL_CHUNK,
                               jnp.full((1,), n_tiles * MOE_TILE // FILL_CHUNK, jnp.int32)])
    return (offsets, tile_expert.astype(jnp.int32), n_active,
            fill_from.astype(jnp.int32), fill_to.astype(jnp.int32))


def _sample_inproj_kernel(x_ref, g_ref, w_ref, cos_ref, slo_ref, shi_ref, cw_ref, p0_ref, p1_ref,
                          q_ref, k_ref, v_ref, cy_ref, u_ref):
    xn = _rms(x_ref[...], g_ref[...])
    reps = ATT_WIDTH // LANES
    cos = _tile_lanes(cos_ref[...], reps)
    slo = _tile_lanes(slo_ref[...], reps)
    shi = _tile_lanes(shi_ref[...], reps)

    def proj(i):
        return _dot_f32(xn, w_ref[:, i * ATT_WIDTH:(i + 1) * ATT_WIDTH], NN_DIMS, False)

    q_ref[...] = _rope_rows(proj(0), cos, slo, shi)
    k_ref[...] = _rope_rows(proj(1), cos, slo, shi)
    v_ref[...] = proj(2)
    b_gate = proj(3)
    u = proj(4) * proj(5)
    u_ref[...] = u
    cw = cw_ref[...]
    cy_ref[...] = b_gate * (cw[0:1, :] * p0_ref[...] + cw[1:2, :] * p1_ref[...] + cw[2:3, :] * u)


def _sample_inproj(x, g_mix, w_in, tables, conv_w, prev0, prev1):
    n = x.shape[0]
    out = jax.ShapeDtypeStruct((n, ATT_WIDTH), F32)
    return pl.pallas_call(
        _sample_inproj_kernel,
        out_shape=[out] * 5,
        compiler_params=pltpu.CompilerParams(vmem_limit_bytes=VMEM_LIMIT),
        name="sample_inproj",
    )(x, g_mix, w_in, *tables, conv_w, prev0, prev1)


def _select_past_blocks(q_ref, page_refs, part, sel_ref):
    n_blocks = part.shape[0]
    pages_per_block = len(page_refs) // n_blocks
    block_rows = pages_per_block * page_refs[0].shape[-1]
    for i in range(n_blocks):
        acc = None
        for c in range(HEAD_DIM // SUBLANES):
            rows = slice(c * SUBLANES, (c + 1) * SUBLANES)
            k_rows = page_refs[i * pages_per_block][:, rows, :]
            for p in range(1, pages_per_block):
                k_rows = k_rows + page_refs[i * pages_per_block + p][:, rows, :]
            prod = k_rows * q_ref[:, rows, :]
            acc = prod if acc is None else acc + prod
        part[i] = jnp.sum(acc, axis=1)

    lane = lax.broadcasted_iota(jnp.int32, (N_HEADS, LANES), 1)
    acc = jnp.full((N_HEADS, LANES), NEG_INF, F32)
    for j in range(n_blocks):
        gate_j = jnp.sum(part[j], axis=1, keepdims=True) * (1.0 / block_rows)
        acc = jnp.where(lane == j, gate_j, acc)
    out = jnp.zeros((N_HEADS, LANES), jnp.int32)
    for r in range(MOBA_TOPK):
        best = jnp.max(acc, axis=1, keepdims=True)
        idx = jnp.min(jnp.where(acc == best, lane, LANES), axis=1, keepdims=True)
        out = jnp.where(lane == r, idx, out)
        acc = jnp.where(lane == idx, NEG_INF, acc)
    sel_ref[...] = out


def _sample_attn_kernel(sel_ref, pt_ref, q_ref, qb_ref, kn_ref, vn_ref, ck_ref, cv_ref, o_ref,
                        kbuf, vbuf, sem, *, pages_per_block, page_size):
    b = pl.program_id(0)
    nb = pl.num_programs(0)

    def copies(bb, slot):
        out = []
        for h in range(N_HEADS):
            for r in range(MOBA_TOPK):
                block = sel_ref[(bb * N_HEADS + h) * MOBA_TOPK + r]
                for p in range(pages_per_block):
                    page = pt_ref[bb, block * pages_per_block + p]
                    dst = pl.ds((r * pages_per_block + p) * page_size, page_size)
                    out.append(pltpu.make_async_copy(ck_ref.at[page, h], kbuf.at[slot, h, :, dst],
                                                     sem.at[slot, 0]))
                    out.append(pltpu.make_async_copy(cv_ref.at[page, h], vbuf.at[slot, h, :, dst],
                                                     sem.at[slot, 1]))
        return out

    @pl.when(b == 0)
    def _():
        for c in copies(0, 0):
            c.start()

    @pl.when(b + 1 < nb)
    def _():
        for c in copies(b + 1, (b + 1) % 2):
            c.start()

    slot = b % 2
    for c in copies(b, slot):
        c.wait()

    q = q_ref[...]
    qb = qb_ref[...]
    kn = kn_ref[...]
    vn = vn_ref[...]
    n_keys = kbuf.shape[-1]
    head = lax.broadcasted_iota(jnp.int32, (N_HEADS, ATT_WIDTH), 0)
    feat = lax.broadcasted_iota(jnp.int32, (N_HEADS, ATT_WIDTH), 1)
    own = (feat >= head * HEAD_DIM) & (feat < (head + 1) * HEAD_DIM)

    s = jnp.concatenate(
        [jnp.sum(kbuf[slot, :, :, c * LANES:(c + 1) * LANES] * qb, axis=1)
         for c in range(n_keys // LANES)], axis=1) * ATT_SCALE
    s_self = jnp.sum(jnp.where(own, q * kn, 0.0), axis=1, keepdims=True) * ATT_SCALE
    m = jnp.maximum(jnp.max(s, axis=1, keepdims=True), s_self)
    p = jnp.exp(s - m)
    p_self = jnp.exp(s_self - m)
    l = jnp.sum(p, axis=1, keepdims=True) + p_self
    pv = _dot_f32(p, vbuf[slot].reshape(N_HEADS * HEAD_DIM, n_keys), NT_DIMS, False)
    o = (pv + p_self * vn) / l
    o_ref[...] = jnp.sum(jnp.where(own, o, 0.0), axis=0, keepdims=True)


def _sample_attn(sel_flat, page_table, q3, q_bcast, k3, v3, cache_kt, cache_vt):
    n_dec = q3.shape[0]
    page_size = cache_kt.shape[3]
    pages_per_block = MOBA_BLOCK // page_size
    n_keys = MOBA_TOPK * MOBA_BLOCK
    row = pl.BlockSpec((None, 1, ATT_WIDTH), lambda b, sel, pt: (b, 0, 0))
    hbm = pl.BlockSpec(memory_space=pl.ANY)
    return pl.pallas_call(
        functools.partial(_sample_attn_kernel, pages_per_block=pages_per_block,
                          page_size=page_size),
        grid_spec=pltpu.PrefetchScalarGridSpec(
            num_scalar_prefetch=2,
            grid=(n_dec,),
            in_specs=[row, pl.BlockSpec((None,) + q_bcast.shape[1:],
                                        lambda b, sel, pt: (b, 0, 0, 0)), row, row, hbm, hbm],
            out_specs=row,
            scratch_shapes=[pltpu.VMEM((2, N_HEADS, HEAD_DIM, n_keys), F32),
                            pltpu.VMEM((2, N_HEADS, HEAD_DIM, n_keys), F32),
                            pltpu.SemaphoreType.DMA((2, 2))],
        ),
        out_shape=jax.ShapeDtypeStruct((n_dec, 1, ATT_WIDTH), F32),
        compiler_params=pltpu.CompilerParams(
            dimension_semantics=("arbitrary",), vmem_limit_bytes=VMEM_LIMIT),
        name="sample_attn",
    )(sel_flat, page_table, q3, q_bcast, k3, v3, cache_kt, cache_vt)


def kernel(x_prompt, x_sample, cache_k, cache_v, state_conv, page_table, g_mix, w_in, conv_w, w_out,
           g_ffn, w_router_group, b_router_group, w_router_expert, b_router_expert, w_gate, w_up,
           w_down, g_final):
    depth = g_mix.shape[0]
    assert depth == 1
    bsz, seq, d = x_prompt.shape
    n_dec, dec_len, _ = x_sample.shape
    assert dec_len == 1
    page_size = cache_k.shape[2]
    past_len = page_table.shape[1] * page_size
    assert past_len % MOBA_BLOCK == 0 and MOBA_BLOCK % page_size == 0
    assert past_len // MOBA_BLOCK >= MOBA_TOPK and seq % MOBA_BLOCK == 0

    g_mix2 = g_mix[0][None, :]
    g_ffn2 = g_ffn[0][None, :]
    g_final2 = g_final[None, :]
    w_in_f, w_out_f = w_in[0], w_out[0]
    w_out_bf = w_out_f.astype(BF16)
    gap = SUBLANES - N_EXPERT_GROUPS
    tail = LANES - SUBLANES - N_EXPERTS
    w_r = jnp.concatenate([w_router_group[0].T, jnp.zeros((gap, d), F32),
                           w_router_expert[0].T, jnp.zeros((tail, d), F32)], axis=0)
    b_r = jnp.concatenate([b_router_group[0], jnp.zeros((gap,), F32),
                           b_router_expert[0], jnp.zeros((tail,), F32)])[:, None]
    cw = conv_w[0]

    cos_s, sin_s = _rope_angles(past_len + jnp.arange(dec_len, dtype=jnp.int32))
    x_s = x_sample.reshape(n_dec, d)
    prev0, prev1 = state_conv[0, :, 0, :], state_conv[0, :, 1, :]
    q_s, k_s, v_s, cy_s, u_s = _sample_inproj(x_s, g_mix2, w_in_f, _rope_row_tables(cos_s, sin_s),
                                              cw, prev0, prev1)
    cache_kt = jnp.transpose(cache_k[0], (0, 2, 3, 1))
    cache_vt = jnp.transpose(cache_v[0], (0, 2, 3, 1))
    q_bcast = jnp.broadcast_to(q_s.reshape(n_dec, N_HEADS, HEAD_DIM, 1),
                               (n_dec, N_HEADS, HEAD_DIM, page_size))

    cos_p, sin_p = _rope_angles(jnp.arange(seq, dtype=jnp.int32))
    conv0 = jnp.zeros((bsz, CONV_K - 1, CONV_CH), F32)
    qt_p, kt_p, vt_p, kb_p, vtb_p, kmeans_p, cy_p, conv_p, wg_bf, wu_bf, wd_bf = _prompt_inproj(
        x_prompt, g_mix2, w_in_f, (cos_p.T, sin_p.T), cw, conv0, (w_gate[0], w_up[0], w_down[0]))
    att_p, sel = _prompt_attn(qt_p, kb_p, vtb_p, kmeans_p, page_table, q_bcast, cache_kt)
    n_p = bsz * seq
    hres_p, hn_p, route_p, idx_p, counts_p = _outproj_router(
        x_prompt.reshape(n_p, d), att_p.reshape(n_p, ATT_WIDTH), cy_p.reshape(n_p, CONV_CH),
        w_out_bf, g_ffn2, w_r, b_r, jnp.zeros((N_EXPERTS, 1), F32), tm=TOKEN_TILE, precise=False)

    sel_flat = sel[:, :, :MOBA_TOPK].reshape(-1)
    q3 = q_s.reshape(n_dec, 1, ATT_WIDTH)
    att_s = _sample_attn(sel_flat, page_table, q3, q_bcast, k_s.reshape(n_dec, 1, ATT_WIDTH),
                         v_s.reshape(n_dec, 1, ATT_WIDTH), cache_kt, cache_vt)
    hres_s, hn_s, route_s, idx_s, counts = _outproj_router(
        x_s, att_s.reshape(n_dec, ATT_WIDTH), cy_s, w_out_f, g_ffn2, w_r, b_r, counts_p,
        tm=n_dec, precise=True)

    n_assign = EXPERT_TOPK * (n_p + n_dec)
    n_tiles = -(-(n_assign + N_EXPERTS * (MOE_TILE - 1)) // MOE_TILE)
    offsets, tile_expert, n_active, fill_from, fill_to = _moe_plan(counts, counts_p, n_tiles)
    slots_p, slots_s = _moe_slots(idx_p, offsets), _moe_slots(idx_s, offsets)
    slots_p = _regroup_steps(slots_p, COPY_STEP_TOKENS)
    xs = _moe_dispatch(fill_from, fill_to, slots_p, hn_p, None, n_tiles)
    xs = _moe_dispatch(fill_from, fill_to, slots_s, hn_s, xs, n_tiles)
    ys = _moe_ffn(tile_expert, n_active, xs, wg_bf, wu_bf, wd_bf)
    y_p = _moe_combine(slots_p, route_p, hres_p, g_final2, ys)
    y_s = _moe_combine(slots_s, route_s, hres_s, g_final2, ys)

    conv_s = jnp.stack([prev1, u_s], axis=1)
    to_bthd = lambda t: jnp.transpose(t.reshape(bsz, N_HEADS, HEAD_DIM, seq), (0, 3, 1, 2))[None]
    return (y_p.reshape(bsz, seq, d), y_s.reshape(n_dec, dec_len, d),
            to_bthd(kt_p), to_bthd(vt_p), conv_p[None],
            k_s.reshape(1, n_dec, dec_len, N_HEADS, HEAD_DIM),
            v_s.reshape(1, n_dec, dec_len, N_HEADS, HEAD_DIM),
            conv_s[None])
```

```python
import functools

import jax
import jax.numpy as jnp
from jax import lax
from jax.experimental import pallas as pl
from jax.experimental.pallas import tpu as pltpu

F32 = jnp.float32
BF16 = jnp.bfloat16
HIGHEST = lax.Precision.HIGHEST

D_MODEL = 1024
HEAD_DIM = 64
N_HEADS = 8
ATT_WIDTH = N_HEADS * HEAD_DIM
CONV_CH = D_MODEL - ATT_WIDTH
ROPE_DIM = HEAD_DIM // 4
ROPE_HALF = ROPE_DIM // 2
ROPE_THETA = 500000.0
MOBA_BLOCK = 256
MOBA_TOPK = 3
ATT_SCALE = HEAD_DIM ** -0.5
CONV_K = 3
N_EXPERT_GROUPS = 4
EXPERTS_PER_GROUP = 8
N_EXPERTS = N_EXPERT_GROUPS * EXPERTS_PER_GROUP
EXPERT_TOPK = 2
EXPERT_FF = D_MODEL // 4
RMS_EPS = 1e-6

LANES = 128
SUBLANES = 8
HEADS_PER_VREG = LANES // HEAD_DIM
TOKEN_ROWS = D_MODEL // LANES
MXU_WIDTH = 256
TOKEN_TILE = 512
MOE_TILE = 512
BF16_SUBLANES = 16
V_AUG_ROWS = HEAD_DIM + BF16_SUBLANES
LOG2_E = 1.4426950408889634
SCORE_SLOTS = 2
DMA_QUEUES = 2
ISSUE_UNROLL = 16
COPY_STEP_TOKENS = 1024
COMBINE_GROUP = 32
FILL_CHUNK = 64
VMEM_LIMIT = 56 * 1024 * 1024

NEG_INF = float("-inf")


def _rms(x, g):
    ms = jnp.mean(x * x, axis=-1, keepdims=True)
    return x * lax.rsqrt(ms + RMS_EPS) * g


def _mm(a, w, precise):
    if precise:
        return jnp.dot(a, w, precision=HIGHEST, preferred_element_type=F32)
    return jnp.dot(a.astype(BF16), w, preferred_element_type=F32)


NN_DIMS = (((1,), (0,)), ((), ()))
NT_DIMS = (((1,), (1,)), ((), ()))


def _dot_f32(a, b, dims, full):
    if full:
        return lax.dot_general(a, b, dims, precision=HIGHEST, preferred_element_type=F32)
    a_hi, b_hi = a.astype(BF16), b.astype(BF16)
    a_lo = (a - a_hi.astype(F32)).astype(BF16)
    b_lo = (b - b_hi.astype(F32)).astype(BF16)
    dot = functools.partial(lax.dot_general, dimension_numbers=dims, preferred_element_type=F32)
    n = b.shape[1]
    if dims == NN_DIMS and 2 * n <= MXU_WIDTH:
        both = dot(a_hi, jnp.concatenate([b_hi, b_lo], axis=1))
        return both[:, :n] + (both[:, n:] + dot(a_lo, b_hi))
    return dot(a_hi, b_hi) + (dot(a_hi, b_lo) + dot(a_lo, b_hi))


def _rope_rows(a, cos, sin_lo, sin_hi):
    n = a.shape[-1]
    return (a * cos + pltpu.roll(a, n - ROPE_HALF, 1) * sin_lo
            + pltpu.roll(a, ROPE_HALF, 1) * sin_hi)


def _tile_lanes(t, reps):
    return jnp.concatenate([t] * reps, axis=-1)


def _rope_angles(pos):
    inv = ROPE_THETA ** (-jnp.arange(0, ROPE_DIM, 2, dtype=F32) / ROPE_DIM)
    ang = pos.astype(F32)[:, None] * inv[None, :]
    return jnp.cos(ang), jnp.sin(ang)


def _rope_row_tables(cos, sin):
    rows = cos.shape[0]
    ones = jnp.ones((rows, HEAD_DIM - ROPE_DIM), F32)
    zeros = jnp.zeros((rows, HEAD_DIM - ROPE_HALF), F32)
    c = jnp.concatenate([cos, cos, ones], axis=1)
    s_lo = jnp.concatenate([-sin, zeros], axis=1)
    s_hi = jnp.concatenate([jnp.zeros((rows, ROPE_HALF), F32), sin,
                            jnp.zeros((rows, HEAD_DIM - ROPE_DIM), F32)], axis=1)
    rep = lambda t: jnp.concatenate([t] * HEADS_PER_VREG, axis=1)
    return rep(c), rep(s_lo), rep(s_hi)


def _rope_cols(ref, cos_t, sin_t):
    for h in range(N_HEADS):
        r = h * HEAD_DIM
        x1 = ref[r:r + ROPE_HALF, :]
        x2 = ref[r + ROPE_HALF:r + ROPE_DIM, :]
        ref[r:r + ROPE_HALF, :] = x1 * cos_t - x2 * sin_t
        ref[r + ROPE_HALF:r + ROPE_DIM, :] = x2 * cos_t + x1 * sin_t


def _prompt_inproj_kernel(x_ref, g_ref, w_ref, cost_ref, sint_ref, cw_ref, prev_ref,
                          eg_ref, eu_ref, ed_ref,
                          qt_ref, kt_ref, vt_ref, kb_ref, vtb_ref, km_ref, cy_ref, cn_ref,
                          egb_ref, eub_ref, edb_ref,
                          ubuf, halo, km_acc, wt_ref, wc_ref, *, tt):
    t = pl.program_id(1)
    blk = MOBA_BLOCK
    egb_ref[...] = eg_ref[...].astype(BF16)
    eub_ref[...] = eu_ref[...].astype(BF16)
    edb_ref[...] = ed_ref[...].astype(BF16)

    @pl.when((pl.program_id(0) == 0) & (t == 0))
    def _():
        for i in range(wt_ref.shape[0]):
            wt_ref[i] = w_ref[:, i * ATT_WIDTH:(i + 1) * ATT_WIDTH].T.astype(BF16)
        wc_ref[...] = w_ref[:, wt_ref.shape[0] * ATT_WIDTH:].astype(BF16)

    xn = _rms(x_ref[...], g_ref[...]).astype(BF16)

    qt_ref[...] = lax.dot_general(wt_ref[0], xn, NT_DIMS, preferred_element_type=F32)
    _rope_cols(qt_ref, cost_ref[...], sint_ref[...])
    kt_ref[...] = lax.dot_general(wt_ref[1], xn, NT_DIMS, preferred_element_type=F32)
    _rope_cols(kt_ref, cost_ref[...], sint_ref[...])
    vt = lax.dot_general(wt_ref[2], xn, NT_DIMS, preferred_element_type=F32)
    vt_ref[...] = vt
    for i in range(tt // blk):
        vtb_ref[i] = vt[:, i * blk:(i + 1) * blk].astype(BF16)

    k = kt_ref[...].T
    kb_ref[...] = k.astype(BF16)

    @pl.when(t == 0)
    def _():
        km_acc[...] = jnp.zeros_like(km_acc)

    rid = lax.broadcasted_iota(jnp.int32, km_acc.shape, 0)
    km = km_acc[...]
    for i in range(tt // blk):
        mean_i = jnp.mean(k[i * blk:(i + 1) * blk, :], axis=0, keepdims=True)
        km = jnp.where(rid == t * (tt // blk) + i, mean_i, km)
    km_acc[...] = km
    km_ref[...] = km

    def proj(i):
        return jnp.dot(xn, wc_ref[:, i * CONV_CH:(i + 1) * CONV_CH], preferred_element_type=F32)

    b_gate = proj(0)
    u = proj(1) * proj(2)

    @pl.when(t == 0)
    def _():
        ubuf[SUBLANES - (CONV_K - 1):SUBLANES, :] = prev_ref[...]

    @pl.when(t > 0)
    def _():
        ubuf[0:SUBLANES, :] = halo[...]

    ubuf[SUBLANES:SUBLANES + tt, :] = u
    cw = cw_ref[...]
    conv = (cw[0:1, :] * ubuf[SUBLANES - 2:SUBLANES - 2 + tt, :]
            + cw[1:2, :] * ubuf[SUBLANES - 1:SUBLANES - 1 + tt, :]
            + cw[2:3, :] * u)
    cy_ref[...] = (b_gate * conv).astype(cy_ref.dtype)
    halo[...] = ubuf[tt:tt + SUBLANES, :]
    cn_ref[...] = ubuf[tt + SUBLANES - (CONV_K - 1):tt + SUBLANES, :]


def _prompt_inproj(x, g_mix, w_in, col_tables, conv_w, conv_prev, expert_w, tt=TOKEN_TILE):
    bsz, seq, d = x.shape
    nt = seq // tt
    nb = seq // MOBA_BLOCK
    assert all(w.shape[0] == bsz * nt for w in expert_w), "one expert per projection grid step"
    expert_spec = lambda w: pl.BlockSpec((None,) + w.shape[1:], lambda b, t: (b * nt + t, 0, 0))
    n_att = 3
    conv_cols = w_in.shape[1] - n_att * ATT_WIDTH
    row_spec = lambda width: pl.BlockSpec((None, tt, width), lambda b, t: (b, t, 0))
    col_spec = pl.BlockSpec((None, ATT_WIDTH, tt), lambda b, t: (b, 0, t))
    tabt_spec = pl.BlockSpec((ROPE_HALF, tt), lambda b, t: (0, t))
    full = lambda shape: pl.BlockSpec(shape, lambda b, t: (0,) * len(shape))
    per_b = lambda shape: pl.BlockSpec((None,) + shape, lambda b, t: (b,) + (0,) * len(shape))
    vtb_spec = pl.BlockSpec((None, tt // MOBA_BLOCK, ATT_WIDTH, MOBA_BLOCK),
                            lambda b, t: (b, t, 0, 0))
    col_shape = jax.ShapeDtypeStruct((bsz, ATT_WIDTH, seq), F32)
    return pl.pallas_call(
        functools.partial(_prompt_inproj_kernel, tt=tt),
        grid=(bsz, nt),
        in_specs=[row_spec(d), full((1, d)),
                  pl.BlockSpec(w_in.shape, lambda b, t: (0, 0), pipeline_mode=pl.Buffered(1)),
                  tabt_spec, tabt_spec, full((CONV_K, CONV_CH)), per_b((CONV_K - 1, CONV_CH))]
                 + [expert_spec(w) for w in expert_w],
        out_specs=[col_spec, col_spec, col_spec, row_spec(ATT_WIDTH), vtb_spec,
                   per_b((nb, ATT_WIDTH)), row_spec(CONV_CH), per_b((CONV_K - 1, CONV_CH))]
                  + [expert_spec(w) for w in expert_w],
        out_shape=[col_shape, col_shape, col_shape,
                   jax.ShapeDtypeStruct((bsz, seq, ATT_WIDTH), BF16),
                   jax.ShapeDtypeStruct((bsz, nb, ATT_WIDTH, MOBA_BLOCK), BF16),
                   jax.ShapeDtypeStruct((bsz, nb, ATT_WIDTH), F32),
                   jax.ShapeDtypeStruct((bsz, seq, CONV_CH), BF16),
                   jax.ShapeDtypeStruct((bsz, CONV_K - 1, CONV_CH), F32)]
                  + [jax.ShapeDtypeStruct(w.shape, BF16) for w in expert_w],
        scratch_shapes=[pltpu.VMEM((tt + SUBLANES, CONV_CH), F32),
                        pltpu.VMEM((SUBLANES, CONV_CH), F32),
                        pltpu.VMEM((nb, ATT_WIDTH), F32),
                        pltpu.VMEM((n_att, ATT_WIDTH, d), BF16),
                        pltpu.VMEM((d, conv_cols), BF16)],
        compiler_params=pltpu.CompilerParams(
            dimension_semantics=("arbitrary", "arbitrary"), vmem_limit_bytes=VMEM_LIMIT),
        name="prompt_inproj",
    )(x, g_mix, w_in, *col_tables, conv_w, conv_prev, *expert_w)


def _prompt_attn_kernel(pt_ref, qt_ref, kb_ref, vtb_ref, km_ref, qdec_ref, *refs, nb, n_pages):
    page_refs = refs[:n_pages]
    o_ref, sel_ref, s_ref, ot_ref, va_ref, part_ref = refs[n_pages:]
    _select_past_blocks(qdec_ref, page_refs, part_ref, sel_ref)
    blk = MOBA_BLOCK
    means = km_ref[...]
    ones_row = lax.broadcasted_iota(jnp.int32, (V_AUG_ROWS - HEAD_DIM, blk), 0) == 0
    for j in range(nb):
        for h in range(HEADS_PER_VREG):
            va_ref[j, h, 0:HEAD_DIM, :] = vtb_ref[j, h * HEAD_DIM:(h + 1) * HEAD_DIM, :]
            va_ref[j, h, HEAD_DIM:V_AUG_ROWS, :] = jnp.where(ones_row, 1.0, 0.0).astype(BF16)
    feat = lax.broadcasted_iota(jnp.int32, (LANES, 1), 0)
    bid = lax.broadcasted_iota(jnp.int32, (nb, blk), 0)
    key = lax.broadcasted_iota(jnp.int32, (blk, blk), 0)
    qry = lax.broadcasted_iota(jnp.int32, (blk, blk), 1)

    def block_bias(qth, qi):
        if qi <= MOBA_TOPK:
            return [None] * qi
        gates = jnp.dot(means, qth, precision=HIGHEST, preferred_element_type=F32)
        gates = jnp.where(bid < qi, gates, NEG_INF)
        rows = []
        for j in range(qi):
            gj = gates[j:j + 1, :]
            beats = (gates > gj) | ((gates == gj) & (bid < j))
            cnt = jnp.sum(beats.astype(F32), axis=0, keepdims=True)
            rows.append(jnp.where(cnt < MOBA_TOPK, 0.0, NEG_INF))
        return rows

    def scores_pass(qi, h, slot, state):
        qt = qt_ref[:, qi * blk:(qi + 1) * blk]
        qth = jnp.where((feat >= h * HEAD_DIM) & (feat < (h + 1) * HEAD_DIM), qt, 0.0)
        bias = block_bias(qth, qi)
        qs = (qth * (ATT_SCALE * LOG2_E)).astype(BF16)
        m = None
        for j in range(qi + 1):
            s = jnp.dot(kb_ref[j * blk:(j + 1) * blk, :], qs,
                        preferred_element_type=F32)
            if j == qi:
                s = jnp.where(key <= qry, s, NEG_INF)
            elif bias[j] is not None:
                s = s + bias[j]
            s_ref[slot, j] = s
            m_blk = jnp.max(s, axis=0, keepdims=True)
            m = m_blk if m is None else jnp.maximum(m, m_blk)
            yield
        state["m"] = m

    def values_pass(qi, h, slot, state):
        m = state["m"]
        acc = None
        for j in range(qi + 1):
            p = jnp.exp2(s_ref[slot, j] - m).astype(BF16)
            pv = jnp.dot(va_ref[j, h], p, preferred_element_type=F32)
            acc = pv if acc is None else acc + pv
            yield
        ot_ref[h * HEAD_DIM:(h + 1) * HEAD_DIM, qi * blk:(qi + 1) * blk] = (
            acc[0:HEAD_DIM, :] / acc[HEAD_DIM:HEAD_DIM + 1, :])

    def run_interleaved(*gens):
        live = [g for g in gens if g is not None]
        while live:
            for g in list(live):
                if next(g, "done") == "done":
                    live.remove(g)

    pending = None
    for i, (qi, h) in enumerate((qi, h) for qi in range(nb) for h in range(HEADS_PER_VREG)):
        state = {}
        run_interleaved(scores_pass(qi, h, i % SCORE_SLOTS, state), pending)
        pending = values_pass(qi, h, i % SCORE_SLOTS, state)
    run_interleaved(pending)
    o_ref[...] = ot_ref[...].T.astype(o_ref.dtype)


def _prompt_attn(qt, kb, vtb, kmeans, page_table, q_dec, cache_kt):
    bsz, seq, _ = kb.shape
    nb = seq // MOBA_BLOCK
    n_pairs = ATT_WIDTH // LANES
    n_dec, n_pages = page_table.shape
    assert n_dec == bsz * n_pairs, "one decode sequence per prompt attention grid step"
    _, n_heads, head_dim, page_size = cache_kt.shape
    n_blocks = n_pages * page_size // MOBA_BLOCK
    dec = lambda b, hp: b * n_pairs + hp

    def page_spec(i):
        return pl.BlockSpec((None, n_heads, head_dim, page_size),
                            lambda b, hp, pt: (pt[dec(b, hp), i], 0, 0, 0))

    return pl.pallas_call(
        functools.partial(_prompt_attn_kernel, nb=nb, n_pages=n_pages),
        grid_spec=pltpu.PrefetchScalarGridSpec(
            num_scalar_prefetch=1,
            grid=(bsz, n_pairs),
            in_specs=[pl.BlockSpec((None, LANES, seq), lambda b, hp, pt: (b, hp, 0)),
                      pl.BlockSpec((None, seq, LANES), lambda b, hp, pt: (b, 0, hp)),
                      pl.BlockSpec((None, nb, LANES, MOBA_BLOCK), lambda b, hp, pt: (b, 0, hp, 0)),
                      pl.BlockSpec((None, nb, LANES), lambda b, hp, pt: (b, 0, hp)),
                      pl.BlockSpec((None, n_heads, head_dim, page_size),
                                   lambda b, hp, pt: (dec(b, hp), 0, 0, 0))]
                     + [page_spec(i) for i in range(n_pages)],
            out_specs=[pl.BlockSpec((None, seq, LANES), lambda b, hp, pt: (b, 0, hp)),
                       pl.BlockSpec((None, n_heads, LANES), lambda b, hp, pt: (dec(b, hp), 0, 0))],
            scratch_shapes=[pltpu.VMEM((SCORE_SLOTS, nb, MOBA_BLOCK, MOBA_BLOCK), F32),
                            pltpu.VMEM((LANES, seq), F32),
                            pltpu.VMEM((nb, HEADS_PER_VREG, V_AUG_ROWS, MOBA_BLOCK), BF16),
                            pltpu.VMEM((n_blocks, n_heads, LANES), F32)],
        ),
        out_shape=[jax.ShapeDtypeStruct((bsz, seq, ATT_WIDTH), BF16),
                   jax.ShapeDtypeStruct((n_dec, n_heads, LANES), jnp.int32)],
        compiler_params=pltpu.CompilerParams(
            dimension_semantics=("arbitrary", "arbitrary"), vmem_limit_bytes=VMEM_LIMIT),
        name="prompt_attn",
    )(page_table, qt, kb, vtb, kmeans, q_dec, *([cache_kt] * n_pages))


def _outproj_router_kernel(x_ref, att_ref, cy_ref, wo_ref, g_ref, wr_ref, br_ref, cnt_in_ref,
                           hres_ref, hn_ref, route_ref, idx_ref, cnt_ref, tri_ref, wob_ref, *, precise):
    step = pl.program_id(0)
    tm = x_ref.shape[0]
    w_out = wo_ref
    if not precise:
        w_out = wob_ref

        @pl.when(step == 0)
        def _():
            wob_ref[...] = wo_ref[...].astype(BF16)

    hres = (x_ref[...] + _mm(att_ref[...], w_out[0:ATT_WIDTH, :], precise)
            + _mm(cy_ref[...], w_out[ATT_WIDTH:, :], precise))
    hres_ref[...] = hres
    hn = _rms(hres, g_ref[...])
    for s in range(TOKEN_ROWS):
        hn_ref[pl.ds(s, tm, stride=TOKEN_ROWS), :] = hn[:, s * LANES:(s + 1) * LANES]

    @pl.when(step == 0)
    def _():
        cnt_ref[...] = cnt_in_ref[...]
        r = lax.broadcasted_iota(jnp.int32, (tm, tm), 0)
        c = lax.broadcasted_iota(jnp.int32, (tm, tm), 1)
        tri_ref[...] = jnp.where(r < c, 1.0, 0.0).astype(BF16)

    lo = _dot_f32(wr_ref[...], hn, NT_DIMS, precise) + br_ref[...]
    sub = lax.broadcasted_iota(jnp.int32, (SUBLANES, tm), 0)
    first_max = lambda v, m: jnp.min(jnp.where(v == m, sub, SUBLANES), axis=0, keepdims=True)
    lg = jnp.where(sub < N_EXPERT_GROUPS, lo[0:SUBLANES], NEG_INF)
    mg = jnp.max(lg, axis=0, keepdims=True)
    g = first_max(lg, mg)
    pg = 1.0 / jnp.sum(jnp.exp(lg - mg), axis=0, keepdims=True)
    le = lo[SUBLANES:2 * SUBLANES]
    for gi in range(1, N_EXPERT_GROUPS):
        le = jnp.where(g == gi, lo[(gi + 1) * SUBLANES:(gi + 2) * SUBLANES], le)
    m1 = jnp.max(le, axis=0, keepdims=True)
    i1 = first_max(le, m1)
    le2 = jnp.where(sub == i1, NEG_INF, le)
    m2 = jnp.max(le2, axis=0, keepdims=True)
    i2 = first_max(le2, m2)
    e2 = jnp.exp(m2 - m1)
    w1 = pg / (1.0 + e2)
    w2 = pg * e2 / (1.0 + e2)
    ex1, ex2 = g * EXPERTS_PER_GROUP + i1, g * EXPERTS_PER_GROUP + i2

    eid = lax.broadcasted_iota(jnp.int32, (N_EXPERTS, tm), 0)
    oh1, oh2 = eid == ex1, eid == ex2
    assigned = jnp.where(oh1 | oh2, 1.0, 0.0)
    before = cnt_ref[...] + jnp.dot(assigned.astype(BF16), tri_ref[...],
                                    preferred_element_type=F32)
    r1 = jnp.sum(jnp.where(oh1, before, 0.0), axis=0, keepdims=True)
    r2 = jnp.sum(jnp.where(oh2, before, 0.0), axis=0, keepdims=True)
    cnt_ref[...] += jnp.sum(assigned, axis=1, keepdims=True)
    zrow = jnp.zeros((1, tm), jnp.int32)
    idx_ref[...] = jnp.concatenate(
        [ex1, ex2, r1.astype(jnp.int32), r2.astype(jnp.int32)] + [zrow] * (SUBLANES - 4), axis=0)
    zf = jnp.zeros((1, tm), F32)
    wrows = jnp.concatenate([zf] * 4 + [w1, w2] + [zf] * 2 + [jnp.zeros((LANES - SUBLANES, tm), F32)],
                            axis=0)
    route_ref[...] = wrows.T


def _outproj_router(x, att, cy, w_out, g_ffn, w_r, b_r, counts_in, tm, precise):
    n, d = x.shape
    rows = lambda width: pl.BlockSpec((tm, width), lambda i: (i, 0))
    full = lambda shape: pl.BlockSpec(shape, lambda i: (0,) * len(shape))
    return pl.pallas_call(
        functools.partial(_outproj_router_kernel, precise=precise),
        grid=(n // tm,),
        in_specs=[rows(d), rows(ATT_WIDTH), rows(CONV_CH), full(w_out.shape), full((1, d)),
                  full(w_r.shape), full((LANES, 1)), full((N_EXPERTS, 1))],
        out_specs=[rows(d), pl.BlockSpec((tm * TOKEN_ROWS, LANES), lambda i: (i, 0)), rows(LANES),
                   pl.BlockSpec((None, SUBLANES, tm), lambda i: (i, 0, 0)), full((N_EXPERTS, 1))],
        out_shape=[jax.ShapeDtypeStruct((n, d), F32),
                   jax.ShapeDtypeStruct((n * TOKEN_ROWS, LANES), F32),
                   jax.ShapeDtypeStruct((n, LANES), F32),
                   jax.ShapeDtypeStruct((n // tm, SUBLANES, tm), jnp.int32),
                   jax.ShapeDtypeStruct((N_EXPERTS, 1), F32)],
        scratch_shapes=[pltpu.VMEM((tm, tm), BF16), pltpu.VMEM(w_out.shape, BF16)],
        compiler_params=pltpu.CompilerParams(
            dimension_semantics=("arbitrary",), vmem_limit_bytes=VMEM_LIMIT),
        name="outproj_router_precise" if precise else "outproj_router",
    )(x, att, cy, w_out, g_ffn, w_r, b_r, counts_in)


def _token_rows(ref, t):
    return ref.at[pl.ds(pl.multiple_of(t * TOKEN_ROWS, TOKEN_ROWS), TOKEN_ROWS), :]


def _start_token_copies(copies, n):
    def start(g, carry):
        for u in range(ISSUE_UNROLL):
            for c in copies(g * ISSUE_UNROLL + u):
                c.start(priority=u % DMA_QUEUES)
        return carry

    lax.fori_loop(0, n // ISSUE_UNROLL, start, 0)


def _moe_slots(idx, offsets):
    experts, inner = idx[:, :EXPERT_TOPK, :], idx[:, EXPERT_TOPK:2 * EXPERT_TOPK, :]
    onehot = experts[..., None] == jnp.arange(N_EXPERTS, dtype=jnp.int32)
    return inner + jnp.sum(jnp.where(onehot, offsets, 0), axis=-1)


def _regroup_steps(slots, tm):
    steps, k, tm0 = slots.shape
    if tm0 >= tm or (steps * tm0) % tm:
        return slots
    per = tm // tm0
    return slots.reshape(steps // per, per, k, tm0).transpose(0, 2, 1, 3).reshape(steps // per, k, tm)


def _moe_dispatch_kernel(from_ref, to_ref, slot_ref, hn_ref, *refs, tm, zero_fill):
    xs_ref, zbuf, sem = refs[-3:]
    step = pl.program_id(0)

    if zero_fill:
        @pl.when(step == 0)
        def _():
            zbuf[...] = jnp.zeros_like(zbuf)

            def fill(chunk):
                row = pl.multiple_of(chunk * (FILL_CHUNK * TOKEN_ROWS), FILL_CHUNK * TOKEN_ROWS)
                return pltpu.make_async_copy(
                    zbuf, xs_ref.at[pl.ds(row, FILL_CHUNK * TOKEN_ROWS), :], sem.at[1])

            def start_fill(c, carry):
                fill(c).start()
                return carry

            def wait_fill(c, carry):
                fill(c).wait()
                return carry

            for act in (start_fill, wait_fill):
                for r in range(from_ref.shape[0]):
                    lax.fori_loop(from_ref[r], to_ref[r], act, 0)

    def copies(t):
        src = _token_rows(hn_ref, t)
        return tuple(pltpu.make_async_copy(src, _token_rows(xs_ref, slot_ref[k, t]), sem.at[0])
                     for k in range(EXPERT_TOPK))

    _start_token_copies(copies, tm)
    for _ in range(EXPERT_TOPK):
        pltpu.make_async_copy(hn_ref, xs_ref.at[pl.ds(0, tm * TOKEN_ROWS), :], sem.at[0]).wait()


def _moe_dispatch(fill_from, fill_to, slots, hn_tiles, xs_prev, n_tiles):
    n_steps, _, tm = slots.shape
    n = n_steps * tm
    zero_fill = xs_prev is None
    hbm = pl.BlockSpec(memory_space=pl.ANY)
    in_specs = [pl.BlockSpec((None, EXPERT_TOPK, tm), lambda i, *_: (i, 0, 0),
                             memory_space=pltpu.SMEM),
                pl.BlockSpec((tm * TOKEN_ROWS, LANES), lambda i, *_: (i, 0))]
    args = [fill_from, fill_to, slots, hn_tiles]
    aliases = {}
    if not zero_fill:
        in_specs.append(hbm)
        args.append(xs_prev)
        aliases = {len(args) - 1: 0}
    return pl.pallas_call(
        functools.partial(_moe_dispatch_kernel, tm=tm, zero_fill=zero_fill),
        grid_spec=pltpu.PrefetchScalarGridSpec(
            num_scalar_prefetch=2,
            grid=(n // tm,),
            in_specs=in_specs,
            out_specs=hbm,
            scratch_shapes=[pltpu.VMEM((FILL_CHUNK * TOKEN_ROWS, LANES), F32),
                            pltpu.SemaphoreType.DMA((2,))],
        ),
        out_shape=jax.ShapeDtypeStruct((n_tiles * MOE_TILE * TOKEN_ROWS, LANES), F32),
        input_output_aliases=aliases,
        compiler_params=pltpu.CompilerParams(
            dimension_semantics=("arbitrary",), vmem_limit_bytes=VMEM_LIMIT),
        name="moe_dispatch" if zero_fill else "moe_dispatch_more",
    )(*args)


def _moe_ffn_kernel(te_ref, na_ref, xs_ref, wg_ref, wu_ref, wd_ref, ys_ref):
    tg = MOE_TILE

    @pl.when(pl.program_id(0) < na_ref[0])
    def _():
        x = jnp.concatenate(
            [xs_ref[pl.ds(s, tg, stride=TOKEN_ROWS), :].astype(BF16) for s in range(TOKEN_ROWS)],
            axis=1)
        hg = jnp.dot(x, wg_ref[...], preferred_element_type=F32)
        hu = jnp.dot(x, wu_ref[...], preferred_element_type=F32)
        h = hg * (1.0 / (1.0 + jnp.exp(-hg))) * hu
        y = jnp.dot(h.astype(BF16), wd_ref[...], preferred_element_type=F32)
        for s in range(TOKEN_ROWS):
            ys_ref[pl.ds(s, tg, stride=TOKEN_ROWS), :] = y[:, s * LANES:(s + 1) * LANES]


def _moe_ffn(tile_expert, n_active, xs, w_gate, w_up, w_down):
    n_tiles = tile_expert.shape[0]
    tile_rows = MOE_TILE * TOKEN_ROWS
    d, ff = w_gate.shape[1:]
    last = lambda na: na[0] - 1
    w_spec = lambda shape: pl.BlockSpec(
        (None,) + shape, lambda i, te, na: (te[jnp.minimum(i, last(na))], 0, 0))
    tile_spec = pl.BlockSpec((tile_rows, LANES), lambda i, te, na: (jnp.minimum(i, last(na)), 0))
    return pl.pallas_call(
        _moe_ffn_kernel,
        grid_spec=pltpu.PrefetchScalarGridSpec(
            num_scalar_prefetch=2,
            grid=(n_tiles,),
            in_specs=[tile_spec, w_spec((d, ff)), w_spec((d, ff)), w_spec((ff, d))],
            out_specs=tile_spec,
        ),
        out_shape=jax.ShapeDtypeStruct(xs.shape, F32),
        input_output_aliases={2: 0},
        compiler_params=pltpu.CompilerParams(
            dimension_semantics=("arbitrary",), vmem_limit_bytes=VMEM_LIMIT),
        name="moe_ffn",
    )(tile_expert, n_active, xs, w_gate, w_up, w_down)


def _moe_combine_kernel(slot_ref, next_slot_ref, route_ref, hres_ref, gf_ref, ys_ref, y_ref,
                        buf, sem, *, tm):
    i = pl.program_id(0)
    cur = i % 2
    nxt = 1 - cur

    def copies(table, t, half):
        return tuple(pltpu.make_async_copy(_token_rows(ys_ref, table[k, t]),
                                           _token_rows(buf.at[half, k], t), sem.at[half, k])
                     for k in range(EXPERT_TOPK))

    def wait_rows(half):
        for k in range(EXPERT_TOPK):
            pltpu.make_async_copy(ys_ref.at[pl.ds(0, tm * TOKEN_ROWS), :], buf.at[half, k],
                                  sem.at[half, k]).wait()

    @pl.when(i == 0)
    def _():
        _start_token_copies(lambda t: copies(slot_ref, t, 0), tm)

    wait_rows(cur)
    group = min(COMBINE_GROUP, tm)
    for g in range(tm // group):
        rows = slice(g * group, (g + 1) * group)
        w1, w2 = route_ref[rows, 4:5], route_ref[rows, 5:6]
        parts = []
        for s in range(TOKEN_ROWS):
            tile_rows = pl.ds(g * group * TOKEN_ROWS + s, group, stride=TOKEN_ROWS)
            parts.append(hres_ref[rows, s * LANES:(s + 1) * LANES]
                         + w1 * buf[cur, 0, tile_rows, :] + w2 * buf[cur, 1, tile_rows, :])
        y_ref[rows, :] = _rms(jnp.concatenate(parts, axis=1), gf_ref[...])
        for u in range(group):
            for c in copies(next_slot_ref, g * group + u, nxt):
                c.start(priority=u % DMA_QUEUES)

    @pl.when(i == pl.num_programs(0) - 1)
    def _():
        wait_rows(nxt)


def _moe_combine(slots, route, hres, g_final, ys):
    n, d = hres.shape
    steps, _, tm = slots.shape
    rows = lambda width: pl.BlockSpec((tm, width), lambda i: (i, 0))
    slot_spec = lambda step_of: pl.BlockSpec((None, EXPERT_TOPK, tm), lambda i: (step_of(i), 0, 0),
                                             memory_space=pltpu.SMEM)
    return pl.pallas_call(
        functools.partial(_moe_combine_kernel, tm=tm),
        grid=(steps,),
        in_specs=[slot_spec(lambda i: i), slot_spec(lambda i: jnp.minimum(i + 1, steps - 1)),
                  rows(LANES), rows(d), pl.BlockSpec((1, d), lambda i: (0, 0)),
                  pl.BlockSpec(memory_space=pl.ANY)],
        out_specs=rows(d),
        out_shape=jax.ShapeDtypeStruct((n, d), F32),
        scratch_shapes=[pltpu.VMEM((2, EXPERT_TOPK, tm * TOKEN_ROWS, LANES), F32),
                        pltpu.SemaphoreType.DMA((2, EXPERT_TOPK))],
        compiler_params=pltpu.CompilerParams(
            dimension_semantics=("arbitrary",), vmem_limit_bytes=VMEM_LIMIT),
        name="moe_combine",
    )(slots, slots, route, hres, g_final, ys)


def _moe_plan(counts, first_counts, n_tiles):
    counts = counts[:, 0].astype(jnp.int32)
    first_counts = first_counts[:, 0].astype(jnp.int32)
    padded = (counts + MOE_TILE - 1) // MOE_TILE * MOE_TILE
    ends = jnp.cumsum(padded)
    offsets = ends - padded
    n_active = (ends[-1:] // MOE_TILE).astype(jnp.int32)
    tile_start = jnp.arange(n_tiles, dtype=jnp.int32) * MOE_TILE
    tile_expert = jnp.minimum(jnp.sum(tile_start[:, None] >= ends[None, :], axis=1), N_EXPERTS - 1)
    fill_from = jnp.concatenate([(offsets + first_counts) // FILL_CHUNK, ends[-1:] // FILL_CHUNK])
    fill_to = jnp.concatenate([ends // FILL_CHUNK,
                               jnp.full((1,), n_tiles * MOE_TILE // FILL_CHUNK, jnp.int32)])
    return (offsets, tile_expert.astype(jnp.int32), n_active,
            fill_from.astype(jnp.int32), fill_to.astype(jnp.int32))


def _sample_inproj_kernel(x_ref, g_ref, w_ref, cos_ref, slo_ref, shi_ref, cw_ref, p0_ref, p1_ref,
                          q_ref, k_ref, v_ref, cy_ref, u_ref):
    xn = _rms(x_ref[...], g_ref[...])
    reps = ATT_WIDTH // LANES
    cos = _tile_lanes(cos_ref[...], reps)
    slo = _tile_lanes(slo_ref[...], reps)
    shi = _tile_lanes(shi_ref[...], reps)

    def proj(i):
        return _dot_f32(xn, w_ref[:, i * ATT_WIDTH:(i + 1) * ATT_WIDTH], NN_DIMS, False)

    q_ref[...] = _rope_rows(proj(0), cos, slo, shi)
    k_ref[...] = _rope_rows(proj(1), cos, slo, shi)
    v_ref[...] = proj(2)
    b_gate = proj(3)
    u = proj(4) * proj(5)
    u_ref[...] = u
    cw = cw_ref[...]
    cy_ref[...] = b_gate * (cw[0:1, :] * p0_ref[...] + cw[1:2, :] * p1_ref[...] + cw[2:3, :] * u)


def _sample_inproj(x, g_mix, w_in, tables, conv_w, prev0, prev1):
    n = x.shape[0]
    out = jax.ShapeDtypeStruct((n, ATT_WIDTH), F32)
    return pl.pallas_call(
        _sample_inproj_kernel,
        out_shape=[out] * 5,
        compiler_params=pltpu.CompilerParams(vmem_limit_bytes=VMEM_LIMIT),
        name="sample_inproj",
    )(x, g_mix, w_in, *tables, conv_w, prev0, prev1)


def _select_past_blocks(q_ref, page_refs, part, sel_ref):
    n_blocks = part.shape[0]
    pages_per_block = len(page_refs) // n_blocks
    block_rows = pages_per_block * page_refs[0].shape[-1]
    for i in range(n_blocks):
        acc = None
        for c in range(HEAD_DIM // SUBLANES):
            rows = slice(c * SUBLANES, (c + 1) * SUBLANES)
            k_rows = page_refs[i * pages_per_block][:, rows, :]
            for p in range(1, pages_per_block):
                k_rows = k_rows + page_refs[i * pages_per_block + p][:, rows, :]
            prod = k_rows * q_ref[:, rows, :]
            acc = prod if acc is None else acc + prod
        part[i] = jnp.sum(acc, axis=1)

    lane = lax.broadcasted_iota(jnp.int32, (N_HEADS, LANES), 1)
    acc = jnp.full((N_HEADS, LANES), NEG_INF, F32)
    for j in range(n_blocks):
        gate_j = jnp.sum(part[j], axis=1, keepdims=True) * (1.0 / block_rows)
        acc = jnp.where(lane == j, gate_j, acc)
    out = jnp.zeros((N_HEADS, LANES), jnp.int32)
    for r in range(MOBA_TOPK):
        best = jnp.max(acc, axis=1, keepdims=True)
        idx = jnp.min(jnp.where(acc == best, lane, LANES), axis=1, keepdims=True)
        out = jnp.where(lane == r, idx, out)
        acc = jnp.where(lane == idx, NEG_INF, acc)
    sel_ref[...] = out


def _sample_attn_kernel(sel_ref, pt_ref, q_ref, qb_ref, kn_ref, vn_ref, ck_ref, cv_ref, o_ref,
                        kbuf, vbuf, sem, *, pages_per_block, page_size):
    b = pl.program_id(0)
    nb = pl.num_programs(0)

    def copies(bb, slot):
        out = []
        for h in range(N_HEADS):
            for r in range(MOBA_TOPK):
                block = sel_ref[(bb * N_HEADS + h) * MOBA_TOPK + r]
                for p in range(pages_per_block):
                    page = pt_ref[bb, block * pages_per_block + p]
                    dst = pl.ds((r * pages_per_block + p) * page_size, page_size)
                    out.append(pltpu.make_async_copy(ck_ref.at[page, h], kbuf.at[slot, h, :, dst],
                                                     sem.at[slot, 0]))
                    out.append(pltpu.make_async_copy(cv_ref.at[page, h], vbuf.at[slot, h, :, dst],
                                                     sem.at[slot, 1]))
        return out

    @pl.when(b == 0)
    def _():
        for c in copies(0, 0):
            c.start()

    @pl.when(b + 1 < nb)
    def _():
        for c in copies(b + 1, (b + 1) % 2):
            c.start()

    slot = b % 2
    for c in copies(b, slot):
        c.wait()

    q = q_ref[...]
    qb = qb_ref[...]
    kn = kn_ref[...]
    vn = vn_ref[...]
    n_keys = kbuf.shape[-1]
    head = lax.broadcasted_iota(jnp.int32, (N_HEADS, ATT_WIDTH), 0)
    feat = lax.broadcasted_iota(jnp.int32, (N_HEADS, ATT_WIDTH), 1)
    own = (feat >= head * HEAD_DIM) & (feat < (head + 1) * HEAD_DIM)

    s = jnp.concatenate(
        [jnp.sum(kbuf[slot, :, :, c * LANES:(c + 1) * LANES] * qb, axis=1)
         for c in range(n_keys // LANES)], axis=1) * ATT_SCALE
    s_self = jnp.sum(jnp.where(own, q * kn, 0.0), axis=1, keepdims=True) * ATT_SCALE
    m = jnp.maximum(jnp.max(s, axis=1, keepdims=True), s_self)
    p = jnp.exp(s - m)
    p_self = jnp.exp(s_self - m)
    l = jnp.sum(p, axis=1, keepdims=True) + p_self
    pv = _dot_f32(p, vbuf[slot].reshape(N_HEADS * HEAD_DIM, n_keys), NT_DIMS, False)
    o = (pv + p_self * vn) / l
    o_ref[...] = jnp.sum(jnp.where(own, o, 0.0), axis=0, keepdims=True)


def _sample_attn(sel_flat, page_table, q3, q_bcast, k3, v3, cache_kt, cache_vt):
    n_dec = q3.shape[0]
    page_size = cache_kt.shape[3]
    pages_per_block = MOBA_BLOCK // page_size
    n_keys = MOBA_TOPK * MOBA_BLOCK
    row = pl.BlockSpec((None, 1, ATT_WIDTH), lambda b, sel, pt: (b, 0, 0))
    hbm = pl.BlockSpec(memory_space=pl.ANY)
    return pl.pallas_call(
        functools.partial(_sample_attn_kernel, pages_per_block=pages_per_block,
                          page_size=page_size),
        grid_spec=pltpu.PrefetchScalarGridSpec(
            num_scalar_prefetch=2,
            grid=(n_dec,),
            in_specs=[row, pl.BlockSpec((None,) + q_bcast.shape[1:],
                                        lambda b, sel, pt: (b, 0, 0, 0)), row, row, hbm, hbm],
            out_specs=row,
            scratch_shapes=[pltpu.VMEM((2, N_HEADS, HEAD_DIM, n_keys), F32),
                            pltpu.VMEM((2, N_HEADS, HEAD_DIM, n_keys), F32),
                            pltpu.SemaphoreType.DMA((2, 2))],
        ),
        out_shape=jax.ShapeDtypeStruct((n_dec, 1, ATT_WIDTH), F32),
        compiler_params=pltpu.CompilerParams(
            dimension_semantics=("arbitrary",), vmem_limit_bytes=VMEM_LIMIT),
        name="sample_attn",
    )(sel_flat, page_table, q3, q_bcast, k3, v3, cache_kt, cache_vt)


def kernel(x_prompt, x_sample, cache_k, cache_v, state_conv, page_table, g_mix, w_in, conv_w, w_out,
           g_ffn, w_router_group, b_router_group, w_router_expert, b_router_expert, w_gate, w_up,
           w_down, g_final):
    depth = g_mix.shape[0]
    assert depth == 1
    bsz, seq, d = x_prompt.shape
    n_dec, dec_len, _ = x_sample.shape
    assert dec_len == 1
    page_size = cache_k.shape[2]
    past_len = page_table.shape[1] * page_size
    assert past_len % MOBA_BLOCK == 0 and MOBA_BLOCK % page_size == 0
    assert past_len // MOBA_BLOCK >= MOBA_TOPK and seq % MOBA_BLOCK == 0

    g_mix2 = g_mix[0][None, :]
    g_ffn2 = g_ffn[0][None, :]
    g_final2 = g_final[None, :]
    w_in_f, w_out_f = w_in[0], w_out[0]
    gap = SUBLANES - N_EXPERT_GROUPS
    tail = LANES - SUBLANES - N_EXPERTS
    w_r = jnp.concatenate([w_router_group[0].T, jnp.zeros((gap, d), F32),
                           w_router_expert[0].T, jnp.zeros((tail, d), F32)], axis=0)
    b_r = jnp.concatenate([b_router_group[0], jnp.zeros((gap,), F32),
                           b_router_expert[0], jnp.zeros((tail,), F32)])[:, None]
    cw = conv_w[0]

    cos_s, sin_s = _rope_angles(past_len + jnp.arange(dec_len, dtype=jnp.int32))
    x_s = x_sample.reshape(n_dec, d)
    prev0, prev1 = state_conv[0, :, 0, :], state_conv[0, :, 1, :]
    q_s, k_s, v_s, cy_s, u_s = _sample_inproj(x_s, g_mix2, w_in_f, _rope_row_tables(cos_s, sin_s),
                                              cw, prev0, prev1)
    cache_kt = jnp.transpose(cache_k[0], (0, 2, 3, 1))
    cache_vt = jnp.transpose(cache_v[0], (0, 2, 3, 1))
    q_bcast = jnp.broadcast_to(q_s.reshape(n_dec, N_HEADS, HEAD_DIM, 1),
                               (n_dec, N_HEADS, HEAD_DIM, page_size))

    cos_p, sin_p = _rope_angles(jnp.arange(seq, dtype=jnp.int32))
    conv0 = jnp.zeros((bsz, CONV_K - 1, CONV_CH), F32)
    qt_p, kt_p, vt_p, kb_p, vtb_p, kmeans_p, cy_p, conv_p, wg_bf, wu_bf, wd_bf = _prompt_inproj(
        x_prompt, g_mix2, w_in_f, (cos_p.T, sin_p.T), cw, conv0, (w_gate[0], w_up[0], w_down[0]))
    att_p, sel = _prompt_attn(qt_p, kb_p, vtb_p, kmeans_p, page_table, q_bcast, cache_kt)
    n_p = bsz * seq
    hres_p, hn_p, route_p, idx_p, counts_p = _outproj_router(
        x_prompt.reshape(n_p, d), att_p.reshape(n_p, ATT_WIDTH), cy_p.reshape(n_p, CONV_CH),
        w_out_f, g_ffn2, w_r, b_r, jnp.zeros((N_EXPERTS, 1), F32), tm=TOKEN_TILE, precise=False)

    sel_flat = sel[:, :, :MOBA_TOPK].reshape(-1)
    q3 = q_s.reshape(n_dec, 1, ATT_WIDTH)
    att_s = _sample_attn(sel_flat, page_table, q3, q_bcast, k_s.reshape(n_dec, 1, ATT_WIDTH),
                         v_s.reshape(n_dec, 1, ATT_WIDTH), cache_kt, cache_vt)
    hres_s, hn_s, route_s, idx_s, counts = _outproj_router(
        x_s, att_s.reshape(n_dec, ATT_WIDTH), cy_s, w_out_f, g_ffn2, w_r, b_r, counts_p,
        tm=n_dec, precise=True)

    n_assign = EXPERT_TOPK * (n_p + n_dec)
    n_tiles = -(-(n_assign + N_EXPERTS * (MOE_TILE - 1)) // MOE_TILE)
    offsets, tile_expert, n_active, fill_from, fill_to = _moe_plan(counts, counts_p, n_tiles)
    slots_p, slots_s = _moe_slots(idx_p, offsets), _moe_slots(idx_s, offsets)
    slots_p = _regroup_steps(slots_p, COPY_STEP_TOKENS)
    xs = _moe_dispatch(fill_from, fill_to, slots_p, hn_p, None, n_tiles)
    xs = _moe_dispatch(fill_from, fill_to, slots_s, hn_s, xs, n_tiles)
    ys = _moe_ffn(tile_expert, n_active, xs, wg_bf, wu_bf, wd_bf)
    y_p = _moe_combine(slots_p, route_p, hres_p, g_final2, ys)
    y_s = _moe_combine(slots_s, route_s, hres_s, g_final2, ys)

    conv_s = jnp.stack([prev1, u_s], axis=1)
    to_bthd = lambda t: jnp.transpose(t.reshape(bsz, N_HEADS, HEAD_DIM, seq), (0, 3, 1, 2))[None]
    return (y_p.reshape(bsz, seq, d), y_s.reshape(n_dec, dec_len, d),
            to_bthd(kt_p), to_bthd(vt_p), conv_p[None],
            k_s.reshape(1, n_dec, dec_len, N_HEADS, HEAD_DIM),
            v_s.reshape(1, n_dec, dec_len, N_HEADS, HEAD_DIM),
            conv_s[None])
```
